```python
import math
import jax, jax.numpy as jnp
from jax import lax
import numpy as np

D_MODEL = 1024
BATCH = 8
SEQ = 2048
DEPTH = 4
DEC_BATCH = 128
DEC_SEQ = 8
PAST_LEN = 8192
PAGE_SIZE = 128

N_MIXERS = 3
LAYER_KINDS = tuple(i % N_MIXERS for i in range(DEPTH))
N_RET = LAYER_KINDS.count(0)
N_SWA = LAYER_KINDS.count(1)
N_RWKV = LAYER_KINDS.count(2)

NORM_EPS = 1e-6
ROPE_THETA = 10000.0
NEG_INF = -1e30
D_FF = 4 * D_MODEL

RET_HEADS = 4
RET_DK = D_MODEL // RET_HEADS
RET_DV = 2 * D_MODEL // RET_HEADS
RET_CHUNK = 128

SWA_HEADS = 16
SWA_KV_HEADS = 4
SWA_GROUP = SWA_HEADS // SWA_KV_HEADS
SWA_HD = D_MODEL // SWA_HEADS
WINDOW = 128
SWA_BLOCK = 128

RWKV_HD = 64
RWKV_HEADS = D_MODEL // RWKV_HD
RWKV_DECAY_LORA = max(32, int(round(1.8 * D_MODEL ** 0.5 / 32)) * 32)
RWKV_AAA_LORA = max(32, int(round(1.8 * D_MODEL ** 0.5 / 32)) * 32)
RWKV_GATE_LORA = max(32, int(round(0.6 * D_MODEL ** 0.8 / 32)) * 32)
RWKV_GN_EPS = 64e-5

kernel_name = 'hybrid_retention_swa_rwkv7_step'


def rms_norm(x, g):
    xf = x.astype(jnp.float32)
    y = xf * lax.rsqrt(jnp.mean(xf * xf, axis=-1, keepdims=True) + NORM_EPS)
    return (y * g.astype(jnp.float32)).astype(x.dtype)


def rope(x, pos):
    half = x.shape[-1] // 2
    inv_freq = jnp.power(ROPE_THETA, -jnp.arange(half, dtype=jnp.float32) / half)
    ang = pos.astype(jnp.float32)[:, None] * inv_freq[None, :]
    cos = jnp.cos(ang)[None, :, None, :]
    sin = jnp.sin(ang)[None, :, None, :]
    xf = x.astype(jnp.float32)
    x1, x2 = xf[..., :half], xf[..., half:]
    return jnp.concatenate([x1 * cos - x2 * sin, x2 * cos + x1 * sin], axis=-1).astype(x.dtype)


def squared_relu_mlp(h, w_up, w_down):
    a = jax.nn.relu(h @ w_up)
    return (a * a) @ w_down


def chunk_retention(q, k, v, state0):
    f32 = jnp.float32
    B, T, H, DK = q.shape
    DV = v.shape[-1]
    C = math.gcd(T, RET_CHUNK)
    NC = T // C
    log_gamma = jnp.log1p(-jnp.exp2(-5.0 - jnp.arange(H, dtype=f32)))
    idx = jnp.arange(C, dtype=f32)
    diff = idx[:, None] - idx[None, :]
    inner_decay = jnp.where(diff >= 0, jnp.exp(log_gamma[:, None, None] * jnp.maximum(diff, 0.0)), 0.0)
    q_decay = jnp.exp(log_gamma[None, :] * (idx[:, None] + 1.0))[None, :, :, None]
    k_decay = jnp.exp(log_gamma[None, :] * (C - 1.0 - idx[:, None]))[None, :, :, None]
    chunk_decay = jnp.exp(log_gamma * C)[None, :, None, None]

    def to_chunks(t):
        return jnp.swapaxes(t.astype(f32).reshape(B, NC, C, H, t.shape[-1]), 0, 1)

    def step(S, xs):
        qc, kc, vc = xs
        scores = jnp.einsum('bihd,bjhd->bhij', qc, kc) * inner_decay
        o = jnp.einsum('bhij,bjhe->bihe', scores, vc)
        o = o + jnp.einsum('bihd,bhde->bihe', qc, S) * q_decay
        S = S * chunk_decay + jnp.einsum('bjhd,bjhe->bhde', kc * k_decay, vc)
        return S, o

    S, o = lax.scan(step, state0.astype(f32), (to_chunks(q), to_chunks(k), to_chunks(v)))
    o = jnp.swapaxes(o, 0, 1).reshape(B, T, H, DV)
    return o, S


def retention_mixer(h, state0, pos0, w_in, w_out):
    B, T, _ = h.shape
    nq = RET_HEADS * RET_DK
    nv = RET_HEADS * RET_DV
    proj = h @ w_in
    q = proj[..., :nq].reshape(B, T, RET_HEADS, RET_DK)
    k = proj[..., nq:2 * nq].reshape(B, T, RET_HEADS, RET_DK)
    v = proj[..., 2 * nq:2 * nq + nv].reshape(B, T, RET_HEADS, RET_DV)
    g = proj[..., 2 * nq + nv:]
    pos = pos0 + jnp.arange(T, dtype=jnp.int32)
    q = rope(q, pos)
    k = rope(k, pos) * (RET_DK ** -0.5)
    o, state = chunk_retention(q, k, v, state0)
    o = o * lax.rsqrt(jnp.mean(o * o, axis=-1, keepdims=True) + NORM_EPS)
    o = o.reshape(B, T, nv).astype(h.dtype) * jax.nn.silu(g)
    return o @ w_out, state.astype(h.dtype)


def banded_sink_attention(q, k, v, sinks, q_pos, k_pos):
    f32 = jnp.float32
    kh, grp = q.shape[3], q.shape[4]
    sink = sinks.astype(f32).reshape(kh, grp)[None, :, :, None, None]
    scale = q.shape[-1] ** -0.5

    def one_block(blk):
        qb, kb, vb, qp, kp = blk
        s = jnp.einsum('bqkgd,bskd->bkgqs', qb.astype(f32), kb.astype(f32)) * scale
        rel = qp[:, None] - kp[None, :]
        valid = (kp[None, :] >= 0) & (rel >= 0) & (rel <= WINDOW)
        s = jnp.where(valid, s, NEG_INF)
        m = jnp.maximum(jnp.max(s, axis=-1, keepdims=True), sink)
        p = jnp.exp(s - m)
        p = p / (jnp.sum(p, axis=-1, keepdims=True) + jnp.exp(sink - m))
        return jnp.einsum('bkgqs,bskd->bqkgd', p, vb.astype(f32)).astype(qb.dtype)

    return lax.map(one_block, (q, k, v, q_pos, k_pos))


def swa_mixer(h, k_buf, v_buf, pos0, w_in, b_in, sinks, w_out):
    B, T, _ = h.shape
    nq = SWA_HEADS * SWA_HD
    nk = SWA_KV_HEADS * SWA_HD
    proj = h @ w_in + b_in
    q = proj[..., :nq].reshape(B, T, SWA_HEADS, SWA_HD)
    k = proj[..., nq:nq + nk].reshape(B, T, SWA_KV_HEADS, SWA_HD)
    v = proj[..., nq + nk:].reshape(B, T, SWA_KV_HEADS, SWA_HD)
    pos = pos0 + jnp.arange(T, dtype=jnp.int32)
    q = rope(q, pos)
    k = rope(k, pos)
    if k_buf is None:
        nb = T // SWA_BLOCK
        pad = jnp.zeros((B, SWA_BLOCK, SWA_KV_HEADS, SWA_HD), k.dtype)

        def band(t):
            tp = jnp.concatenate([pad, t], axis=1).reshape(B, nb + 1, SWA_BLOCK, SWA_KV_HEADS, SWA_HD)
            return jnp.swapaxes(jnp.concatenate([tp[:, :-1], tp[:, 1:]], axis=2), 0, 1)

        kb, vb = band(k), band(v)
        qb = jnp.swapaxes(q.reshape(B, nb, SWA_BLOCK, SWA_KV_HEADS, SWA_GROUP, SWA_HD), 0, 1)
        q_pos = pos.reshape(nb, SWA_BLOCK)
        k_pos = q_pos[:, :1] - SWA_BLOCK + jnp.arange(2 * SWA_BLOCK, dtype=jnp.int32)[None, :]
        k_all, v_all = k, v
    else:
        k_all = jnp.concatenate([k_buf.astype(k.dtype), k], axis=1)
        v_all = jnp.concatenate([v_buf.astype(v.dtype), v], axis=1)
        kb, vb = k_all[None], v_all[None]
        qb = q.reshape(B, T, SWA_KV_HEADS, SWA_GROUP, SWA_HD)[None]
        q_pos = pos[None]
        k_pos = (pos0 - k_buf.shape[1] + jnp.arange(k_all.shape[1], dtype=jnp.int32))[None]
    o = banded_sink_attention(qb, kb, vb, sinks, q_pos, k_pos)
    o = jnp.swapaxes(o, 0, 1).reshape(B, T, nq)
    return o @ w_out, k_all[:, -WINDOW:], v_all[:, -WINDOW:]


def rwkv7_mixer(h, wkv0, shift0, mu, w_rkv, w0, w1, w2, a0, a1, a2, g1, g2, k_k, k_a, r_k, ln_w, ln_b, w_out):
    f32 = jnp.float32
    B, T, D = h.shape
    x_prev = jnp.concatenate([shift0[:, None, :].astype(h.dtype), h[:, :-1]], axis=1)
    xs = h[None] + (x_prev - h)[None] * mu[:, None, None, :]
    r, k, v = jnp.einsum('ibtd,ide->ibte', xs[:3], w_rkv)
    w_log = -jax.nn.softplus(-(w0 + jnp.tanh(xs[3] @ w1) @ w2)) - 0.5
    a = jax.nn.sigmoid(a0 + (xs[4] @ a1) @ a2)
    g = jax.nn.sigmoid(xs[5] @ g1) @ g2

    def heads(t):
        return t.astype(f32).reshape(B, T, RWKV_HEADS, RWKV_HD)

    kk = heads(k * k_k)
    kk = kk / jnp.maximum(jnp.sqrt(jnp.sum(kk * kk, axis=-1, keepdims=True)), 1e-12)
    k = k * (1.0 + (a - 1.0) * k_a)
    rh, kh, vh, ah = heads(r), heads(k), heads(v), heads(a)
    decay = jnp.exp(-jnp.exp(heads(w_log)))

    def step(S, inp):
        r_t, w_t, k_t, v_t, kk_t, b_t = inp
        sa = jnp.einsum('bhvk,bhk->bhv', S, -kk_t)
        S = S * w_t[:, :, None, :] + sa[..., None] * b_t[:, :, None, :] + v_t[..., None] * k_t[:, :, None, :]
        return S, jnp.einsum('bhvk,bhk->bhv', S, r_t)

    def seq_major(t):
        return jnp.swapaxes(t, 0, 1)

    S, y = lax.scan(step, wkv0.astype(f32),
                    (seq_major(rh), seq_major(decay), seq_major(kh), seq_major(vh), seq_major(kk), seq_major(kk * ah)))
    y = seq_major(y)
    yc = y - jnp.mean(y, axis=-1, keepdims=True)
    y = yc * lax.rsqrt(jnp.mean(yc * yc, axis=-1, keepdims=True) + RWKV_GN_EPS)
    y = y.reshape(B, T, D) * ln_w.astype(f32) + ln_b.astype(f32)
    bonus = jnp.sum(rh * kh * r_k.astype(f32), axis=-1, keepdims=True) * vh
    y = (y + bonus.reshape(B, T, D)).astype(h.dtype) * g
    return y @ w_out, S.astype(h.dtype), h[:, -1]


def setup_inputs(seed: int = 0) -> dict:
    key = jax.random.key(seed)
    ks = iter(jax.random.split(key, 40))
    f32 = jnp.float32
    D = D_MODEL

    def nrm(shape, scale):
        return jax.random.normal(next(ks), shape, f32) * scale

    n_ret_in = 2 * RET_HEADS * RET_DK + 2 * RET_HEADS * RET_DV
    n_swa_in = (SWA_HEADS + 2 * SWA_KV_HEADS) * SWA_HD
    return {
        'x_prompt': nrm((BATCH, SEQ, D), 1.0),
        'x_sample': nrm((DEC_BATCH, DEC_SEQ, D), 1.0),
        'state_ret': nrm((N_RET, DEC_BATCH, RET_HEADS, RET_DK, RET_DV), 0.5),
        'cache_swa_k': nrm((N_SWA, DEC_BATCH, WINDOW, SWA_KV_HEADS, SWA_HD), 1.0),
        'cache_swa_v': nrm((N_SWA, DEC_BATCH, WINDOW, SWA_KV_HEADS, SWA_HD), 1.0),
        'state_rwkv_wkv': nrm((N_RWKV, DEC_BATCH, RWKV_HEADS, RWKV_HD, RWKV_HD), 0.5),
        'state_rwkv_shift': nrm((N_RWKV, DEC_BATCH, D), 1.0),
        'norm_mix': 1.0 + nrm((DEPTH, D), 0.1),
        'norm_mlp': 1.0 + nrm((DEPTH, D), 0.1),
        'norm_final': 1.0 + nrm((D,), 0.1),
        'ret_w_in': nrm((N_RET, D, n_ret_in), D ** -0.5),
        'ret_w_out': nrm((N_RET, RET_HEADS * RET_DV, D), (RET_HEADS * RET_DV) ** -0.5),
        'swa_w_in': nrm((N_SWA, D, n_swa_in), D ** -0.5),
        'swa_b_in': nrm((N_SWA, n_swa_in), 0.02),
        'swa_sinks': nrm((N_SWA, SWA_HEADS), 1.0),
        'swa_w_out': nrm((N_SWA, SWA_HEADS * SWA_HD, D), (SWA_HEADS * SWA_HD) ** -0.5),
        'rwkv_mu': jax.random.uniform(next(ks), (N_RWKV, 6, D), f32),
        'rwkv_w_rkv': nrm((N_RWKV, 3, D, D), D ** -0.5),
        'rwkv_w0': jax.random.uniform(next(ks), (N_RWKV, D), f32, minval=-6.0, maxval=1.0),
        'rwkv_w1': nrm((N_RWKV, D, RWKV_DECAY_LORA), D ** -0.5),
        'rwkv_w2': nrm((N_RWKV, RWKV_DECAY_LORA, D), 0.1 * RWKV_DECAY_LORA ** -0.5),
        'rwkv_a0': nrm((N_RWKV, D), 0.1),
        'rwkv_a1': nrm((N_RWKV, D, RWKV_AAA_LORA), D ** -0.5),
        'rwkv_a2': nrm((N_RWKV, RWKV_AAA_LORA, D), 0.1 * RWKV_AAA_LORA ** -0.5),
        'rwkv_g1': nrm((N_RWKV, D, RWKV_GATE_LORA), D ** -0.5),
        'rwkv_g2': nrm((N_RWKV, RWKV_GATE_LORA, D), RWKV_GATE_LORA ** -0.5),
        'rwkv_k_k': 1.0 + nrm((N_RWKV, D), 0.1),
        'rwkv_k_a': 1.0 + nrm((N_RWKV, D), 0.1),
        'rwkv_r_k': nrm((N_RWKV, RWKV_HEADS, RWKV_HD), 0.1),
        'rwkv_ln_w': 1.0 + nrm((N_RWKV, D), 0.1),
        'rwkv_ln_b': nrm((N_RWKV, D), 0.02),
        'rwkv_w_out': nrm((N_RWKV, D, D), D ** -0.5),
        'mlp_w_up': nrm((DEPTH, D, D_FF), D ** -0.5),
        'mlp_w_down': nrm((DEPTH, D_FF, D), D_FF ** -0.5),
    }


def reference(x_prompt, x_sample, state_ret, cache_swa_k, cache_swa_v, state_rwkv_wkv, state_rwkv_shift,
              norm_mix, norm_mlp, norm_final, ret_w_in, ret_w_out,
              swa_w_in, swa_b_in, swa_sinks, swa_w_out,
              rwkv_mu, rwkv_w_rkv, rwkv_w0, rwkv_w1, rwkv_w2, rwkv_a0, rwkv_a1, rwkv_a2,
              rwkv_g1, rwkv_g2, rwkv_k_k, rwkv_k_a, rwkv_r_k, rwkv_ln_w, rwkv_ln_b, rwkv_w_out,
              mlp_w_up, mlp_w_down):

    def run_group(x, pos0, ret_init, swa_init, rwkv_init):
        ret_out, k_out, v_out, wkv_out, shift_out = [], [], [], [], []
        for layer in range(DEPTH):
            kind = LAYER_KINDS[layer]
            j = LAYER_KINDS[:layer].count(kind)
            h = rms_norm(x, norm_mix[layer])
            if kind == 0:
                o, s = retention_mixer(h, ret_init[j], pos0, ret_w_in[j], ret_w_out[j])
                ret_out.append(s)
            elif kind == 1:
                o, nk, nv = swa_mixer(h, swa_init[j][0], swa_init[j][1], pos0,
                                      swa_w_in[j], swa_b_in[j], swa_sinks[j], swa_w_out[j])
                k_out.append(nk)
                v_out.append(nv)
            else:
                o, s, sh = rwkv7_mixer(h, rwkv_init[j][0], rwkv_init[j][1], rwkv_mu[j], rwkv_w_rkv[j],
                                       rwkv_w0[j], rwkv_w1[j], rwkv_w2[j], rwkv_a0[j], rwkv_a1[j], rwkv_a2[j],
                                       rwkv_g1[j], rwkv_g2[j], rwkv_k_k[j], rwkv_k_a[j], rwkv_r_k[j],
                                       rwkv_ln_w[j], rwkv_ln_b[j], rwkv_w_out[j])
                wkv_out.append(s)
                shift_out.append(sh)
            x = x + o
            x = x + squared_relu_mlp(rms_norm(x, norm_mlp[layer]), mlp_w_up[layer], mlp_w_down[layer])
        return (rms_norm(x, norm_final), jnp.stack(ret_out), jnp.stack(k_out), jnp.stack(v_out),
                jnp.stack(wkv_out), jnp.stack(shift_out))

    bp, dt = x_prompt.shape[0], x_prompt.dtype
    ret0 = [jnp.zeros((bp, RET_HEADS, RET_DK, RET_DV), dt) for _ in range(N_RET)]
    swa0 = [(None, None) for _ in range(N_SWA)]
    rwkv0 = [(jnp.zeros((bp, RWKV_HEADS, RWKV_HD, RWKV_HD), dt), jnp.zeros((bp, D_MODEL), dt))
             for _ in range(N_RWKV)]
    y_prompt, ret_p, swa_k_p, swa_v_p, wkv_p, shift_p = run_group(x_prompt, 0, ret0, swa0, rwkv0)

    ret1 = [state_ret[j] for j in range(N_RET)]
    swa1 = [(cache_swa_k[j], cache_swa_v[j]) for j in range(N_SWA)]
    rwkv1 = [(state_rwkv_wkv[j], state_rwkv_shift[j]) for j in range(N_RWKV)]
    y_sample, ret_s, swa_k_s, swa_v_s, wkv_s, shift_s = run_group(x_sample, PAST_LEN, ret1, swa1, rwkv1)

    return (y_prompt, y_sample, ret_p, ret_s, swa_k_p, swa_k_s, swa_v_p, swa_v_s,
            wkv_p, wkv_s, shift_p, shift_s)
```

```python
import functools
import math

import jax
import jax.numpy as jnp
from jax import lax
from jax.experimental import pallas as pl
from jax.experimental.pallas import tpu as pltpu

F32 = jnp.float32
BF16 = jnp.bfloat16

D_MODEL = 1024
BATCH = 8
SEQ = 2048
DEPTH = 4
DEC_BATCH = 128
DEC_SEQ = 8
PAST_LEN = 8192
LAYER_KINDS = tuple(i % 3 for i in range(DEPTH))
NORM_EPS = 1e-6
ROPE_THETA = 10000.0
NEG_INF = -1e30
D_FF = 4 * D_MODEL

RET_HEADS = 4
RET_DK = D_MODEL // RET_HEADS
RET_DV = 2 * D_MODEL // RET_HEADS
RET_CHUNK = 128
RET_NQ = RET_HEADS * RET_DK
RET_NV = RET_HEADS * RET_DV

SWA_HEADS = 16
SWA_KV_HEADS = 4
SWA_GROUP = SWA_HEADS // SWA_KV_HEADS
SWA_HD = D_MODEL // SWA_HEADS
WINDOW = 128
SWA_BLOCK = 128
SWA_NQ = SWA_HEADS * SWA_HD
SWA_NK = SWA_KV_HEADS * SWA_HD

RWKV_HD = 64
RWKV_HEADS = D_MODEL // RWKV_HD
RWKV_GN_EPS = 64e-5
RWKV_CHUNK = 32

NP = BATCH * SEQ
NS = DEC_BATCH * DEC_SEQ
NT = NP + NS

VMEM_LIMIT_V7X = 48 * 1024 * 1024
LANES = 128
MXU_DIM_V7X = 256

ROW_TILE = 512


def _cparams(*sem):
    return pltpu.CompilerParams(dimension_semantics=sem, vmem_limit_bytes=VMEM_LIMIT_V7X)


def _dot(a, b):
    return jnp.dot(a.astype(BF16), b.astype(BF16), preferred_element_type=F32)


def _dot_nt(a, b):
    return lax.dot_general(a.astype(BF16), b.astype(BF16), (((1,), (1,)), ((), ())),
                           preferred_element_type=F32)


def _dot_tn(a, b):
    return lax.dot_general(a.astype(BF16), b.astype(BF16), (((0,), (0,)), ((), ())),
                           preferred_element_type=F32)


def _rms(x, g):
    ms = jnp.mean(x * x, axis=-1, keepdims=True)
    return x * lax.rsqrt(ms + NORM_EPS) * g


def _split3(x):
    hi = x.astype(BF16)
    r1 = x - hi.astype(F32)
    mid = r1.astype(BF16)
    lo = (r1 - mid.astype(F32)).astype(BF16)
    return hi, mid, lo


def _seg_sum(x, bd, passes):
    parts = _split3(x)[:passes]
    cols = []
    for j in range(D_MODEL // MXU_DIM_V7X):
        sl = slice(j * MXU_DIM_V7X, (j + 1) * MXU_DIM_V7X)
        acc = jnp.dot(parts[0][:, sl], bd, preferred_element_type=F32)
        for p in parts[1:]:
            acc = acc + jnp.dot(p[:, sl], bd, preferred_element_type=F32)
        cols.append(acc)
    return jnp.concatenate(cols, axis=-1)


def _norm_matmul_body(x_ref, g_ref, w_ref, b_ref, o_ref, xn_ref):
    @pl.when(pl.program_id(1) == 0)
    def _():
        xn_ref[...] = _rms(x_ref[...], g_ref[...]).astype(BF16)

    o_ref[...] = jnp.dot(xn_ref[...], w_ref[...], preferred_element_type=F32) + b_ref[...]


def norm_matmul(x, g, w, b, tn):
    n_out = w.shape[1]
    return pl.pallas_call(
        _norm_matmul_body,
        grid=(NT // ROW_TILE, n_out // tn),
        in_specs=[
            pl.BlockSpec((ROW_TILE, D_MODEL), lambda i, j: (i, 0)),
            pl.BlockSpec((1, D_MODEL), lambda i, j: (0, 0)),
            pl.BlockSpec((D_MODEL, tn), lambda i, j: (0, j)),
            pl.BlockSpec((1, tn), lambda i, j: (0, j)),
        ],
        out_specs=pl.BlockSpec((ROW_TILE, tn), lambda i, j: (i, j)),
        out_shape=jax.ShapeDtypeStruct((NT, n_out), F32),
        scratch_shapes=[pltpu.VMEM((ROW_TILE, D_MODEL), BF16)],
        compiler_params=_cparams("parallel", "arbitrary"),
        name="norm_matmul",
    )(x, g, w, b)


def _matmul_residual_body(x_ref, a_ref, w_ref, o_ref):
    o_ref[...] = x_ref[...] + jnp.dot(a_ref[...].astype(BF16), w_ref[...], preferred_element_type=F32)


def matmul_residual(x, a, w):
    k = a.shape[1]
    return pl.pallas_call(
        _matmul_residual_body,
        grid=(NT // ROW_TILE,),
        in_specs=[
            pl.BlockSpec((ROW_TILE, D_MODEL), lambda i: (i, 0)),
            pl.BlockSpec((ROW_TILE, k), lambda i: (i, 0)),
            pl.BlockSpec((k, D_MODEL), lambda i: (0, 0)),
        ],
        out_specs=pl.BlockSpec((ROW_TILE, D_MODEL), lambda i: (i, 0)),
        out_shape=jax.ShapeDtypeStruct((NT, D_MODEL), F32),
        compiler_params=_cparams("parallel"),
        name="matmul_residual",
    )(x, a, w)


MLP_FF_TILE = 1024


def _mlp_body(x_ref, g_ref, wu_ref, wd_ref, o_ref, xn_ref, acc_ref):
    f = pl.program_id(1)

    @pl.when(f == 0)
    def _():
        xn_ref[...] = _rms(x_ref[...], g_ref[...]).astype(BF16)
        acc_ref[...] = jnp.zeros_like(acc_ref)

    a = jnp.maximum(jnp.dot(xn_ref[...], wu_ref[...], preferred_element_type=F32), 0.0)
    acc_ref[...] += jnp.dot((a * a).astype(BF16), wd_ref[...], preferred_element_type=F32)

    @pl.when(f == pl.num_programs(1) - 1)
    def _():
        o_ref[...] = x_ref[...] + acc_ref[...]


def mlp(x, g, w_up, w_down):
    return pl.pallas_call(
        _mlp_body,
        grid=(NT // ROW_TILE, D_FF // MLP_FF_TILE),
        in_specs=[
            pl.BlockSpec((ROW_TILE, D_MODEL), lambda i, f: (i, 0)),
            pl.BlockSpec((1, D_MODEL), lambda i, f: (0, 0)),
            pl.BlockSpec((D_MODEL, MLP_FF_TILE), lambda i, f: (0, f)),
            pl.BlockSpec((MLP_FF_TILE, D_MODEL), lambda i, f: (f, 0)),
        ],
        out_specs=pl.BlockSpec((ROW_TILE, D_MODEL), lambda i, f: (i, 0)),
        out_shape=jax.ShapeDtypeStruct((NT, D_MODEL), F32),
        scratch_shapes=[pltpu.VMEM((ROW_TILE, D_MODEL), BF16), pltpu.VMEM((ROW_TILE, D_MODEL), F32)],
        compiler_params=_cparams("parallel", "arbitrary"),
        name="mlp",
    )(x, g, w_up, w_down)


def _rmsnorm_body(x_ref, g_ref, o_ref):
    o_ref[...] = _rms(x_ref[...], g_ref[...])


def rmsnorm(x, g):
    return pl.pallas_call(
        _rmsnorm_body,
        grid=(NT // ROW_TILE,),
        in_specs=[pl.BlockSpec((ROW_TILE, D_MODEL), lambda i: (i, 0)),
                  pl.BlockSpec((1, D_MODEL), lambda i: (0, 0))],
        out_specs=pl.BlockSpec((ROW_TILE, D_MODEL), lambda i: (i, 0)),
        out_shape=jax.ShapeDtypeStruct((NT, D_MODEL), F32),
        compiler_params=_cparams("parallel"),
        name="rmsnorm",
    )(x, g)


def _ret_tables(c):
    log_gamma = jnp.log1p(-jnp.exp2(-5.0 - jnp.arange(RET_HEADS, dtype=F32)))
    idx = jnp.arange(c, dtype=F32)
    diff = idx[:, None] - idx[None, :]
    inner = jnp.where(diff >= 0, jnp.exp(log_gamma[:, None, None] * jnp.maximum(diff, 0.0)), 0.0)
    q_decay = jnp.exp(log_gamma[:, None] * (idx[None, :] + 1.0))
    k_decay = jnp.exp(log_gamma[:, None] * (c - 1.0 - idx[None, :]))
    chunk_decay = jnp.exp(log_gamma * c)
    rep = lambda t: jnp.broadcast_to(t[:, :, None], (RET_HEADS, c, LANES))
    return inner, rep(q_decay), rep(k_decay), chunk_decay


def _rope_tables(pos, half):
    inv_freq = jnp.power(ROPE_THETA, -jnp.arange(half, dtype=F32) / half)
    ang = pos.astype(F32)[:, None] * inv_freq[None, :]
    return jnp.cos(ang), jnp.sin(ang)


def _ret_chunk(proj_ref, cos, sin, inner_ref, qd_ref, kd_ref, cd_ref, state, store, o_ref):
    for h in range(RET_HEADS):
        q = proj_ref[:, h * RET_DK:(h + 1) * RET_DK]
        k = proj_ref[:, RET_NQ + h * RET_DK:RET_NQ + (h + 1) * RET_DK]
        v = proj_ref[:, 2 * RET_NQ + h * RET_DV:2 * RET_NQ + (h + 1) * RET_DV]
        g = proj_ref[:, 2 * RET_NQ + RET_NV + h * RET_DV:2 * RET_NQ + RET_NV + (h + 1) * RET_DV]
        half = RET_DK // 2
        q1, q2 = q[:, :half], q[:, half:]
        k1, k2 = k[:, :half], k[:, half:]
        q = jnp.concatenate([q1 * cos - q2 * sin, q2 * cos + q1 * sin], axis=-1)
        k = jnp.concatenate([k1 * cos - k2 * sin, k2 * cos + k1 * sin], axis=-1) * (RET_DK ** -0.5)
        s0 = state(h)
        scores = _dot_nt(q, k) * inner_ref[h]
        qd = jnp.concatenate([qd_ref[h]] * (RET_DV // LANES), axis=-1)
        kd = jnp.concatenate([kd_ref[h]] * (RET_DK // LANES), axis=-1)
        o = _dot(scores, v) + _dot(q, s0) * qd
        store(h, s0 * cd_ref[h] + _dot_tn(k * kd, v))
        o = o * lax.rsqrt(jnp.mean(o * o, axis=-1, keepdims=True) + NORM_EPS)
        o_ref[:, h * RET_DV:(h + 1) * RET_DV] = o * (g * jax.nn.sigmoid(g))


def _ret_prompt_body(cd_ref, proj_ref, cos_ref, sin_ref, inner_ref, qd_ref, kd_ref, o_ref, s_ref, s_scr):
    c = pl.program_id(1)

    @pl.when(c == 0)
    def _():
        s_scr[...] = jnp.zeros_like(s_scr)

    def store(h, s):
        s_scr[h] = s

    _ret_chunk(proj_ref, cos_ref[...], sin_ref[...], inner_ref, qd_ref, kd_ref, cd_ref,
               lambda h: s_scr[h], store, o_ref)

    @pl.when(c == pl.num_programs(1) - 1)
    def _():
        s_ref[0] = s_scr[...]


def _ret_sample_body(cd_ref, proj_ref, cos_ref, sin_ref, inner_ref, qd_ref, kd_ref, s0_ref, o_in_ref,
                     o_ref, s_ref):
    del o_in_ref
    def store(h, s):
        s_ref[0, h] = s

    _ret_chunk(proj_ref, cos_ref[...], sin_ref[...], inner_ref, qd_ref, kd_ref, cd_ref,
               lambda h: s0_ref[0, h], store, o_ref)


def retention_core(proj, state_s):
    n_in = proj.shape[1]
    nc = SEQ // RET_CHUNK
    smem = pl.BlockSpec(memory_space=pltpu.SMEM)
    cos_p, sin_p = _rope_tables(jnp.arange(SEQ, dtype=jnp.int32), RET_DK // 2)
    inner, qd, kd, cd = _ret_tables(RET_CHUNK)
    full3 = lambda shape: pl.BlockSpec(shape, lambda b, c: (0, 0, 0))
    o, s_p = pl.pallas_call(
        _ret_prompt_body,
        grid=(BATCH, nc),
        in_specs=[
            smem,
            pl.BlockSpec((RET_CHUNK, n_in), lambda b, c: (b * nc + c, 0)),
            pl.BlockSpec((RET_CHUNK, LANES), lambda b, c: (c, 0)),
            pl.BlockSpec((RET_CHUNK, LANES), lambda b, c: (c, 0)),
            full3(inner.shape), full3(qd.shape), full3(kd.shape),
        ],
        out_specs=[
            pl.BlockSpec((RET_CHUNK, RET_NV), lambda b, c: (b * nc + c, 0)),
            pl.BlockSpec((1, RET_HEADS, RET_DK, RET_DV), lambda b, c: (b, 0, 0, 0)),
        ],
        out_shape=[jax.ShapeDtypeStruct((NT, RET_NV), F32),
                   jax.ShapeDtypeStruct((BATCH, RET_HEADS, RET_DK, RET_DV), F32)],
        scratch_shapes=[pltpu.VMEM((RET_HEADS, RET_DK, RET_DV), F32)],
        compiler_params=_cparams("parallel", "arbitrary"),
        name="retention_prompt",
    )(cd, proj, cos_p, sin_p, inner, qd, kd)

    cs = math.gcd(DEC_SEQ, RET_CHUNK)
    assert cs == DEC_SEQ
    cos_s, sin_s = _rope_tables(PAST_LEN + jnp.arange(DEC_SEQ, dtype=jnp.int32), RET_DK // 2)
    inner, qd, kd, cd = _ret_tables(cs)
    row0 = NP // DEC_SEQ
    full2 = lambda shape: pl.BlockSpec(shape, lambda b: (0, 0))
    full3 = lambda shape: pl.BlockSpec(shape, lambda b: (0, 0, 0))
    o, s_s = pl.pallas_call(
        _ret_sample_body,
        grid=(DEC_BATCH,),
        in_specs=[
            smem,
            pl.BlockSpec((DEC_SEQ, n_in), lambda b: (row0 + b, 0)),
            full2(cos_s.shape), full2(sin_s.shape),
            full3(inner.shape), full3(qd.shape), full3(kd.shape),
            pl.BlockSpec((1, RET_HEADS, RET_DK, RET_DV), lambda b: (b, 0, 0, 0)),
            pl.BlockSpec(memory_space=pl.ANY),
        ],
        out_specs=[
            pl.BlockSpec((DEC_SEQ, RET_NV), lambda b: (row0 + b, 0)),
            pl.BlockSpec((1, RET_HEADS, RET_DK, RET_DV), lambda b: (b, 0, 0, 0)),
        ],
        out_shape=[jax.ShapeDtypeStruct((NT, RET_NV), F32),
                   jax.ShapeDtypeStruct((DEC_BATCH, RET_HEADS, RET_DK, RET_DV), F32)],
        input_output_aliases={8: 0},
        compiler_params=_cparams("parallel"),
        name="retention_sample",
    )(cd, proj, cos_s, sin_s, inner, qd, kd, state_s, o)
    return o, s_p, s_s


def _swa_rope_tables(pos):
    cos, sin = _rope_tables(pos, SWA_HD // 2)
    cos2 = jnp.concatenate([cos, cos], axis=-1)
    sin2 = jnp.concatenate([-sin, sin], axis=-1)
    rep = LANES // SWA_HD
    return jnp.tile(cos2, (1, rep)), jnp.tile(sin2, (1, rep))


def _swa_rope(x, cos, sin):
    n = x.shape[1]
    half = SWA_HD // 2
    lane = lax.broadcasted_iota(jnp.int32, x.shape, 1)
    partner = jnp.where(lane % SWA_HD < half, pltpu.roll(x, n - half, 1), pltpu.roll(x, half, 1))
    reps = n // LANES
    return x * jnp.concatenate([cos] * reps, axis=-1) + partner * jnp.concatenate([sin] * reps, axis=-1)


def _sink_attention(q, k, v, valid, sink):
    s = _dot_nt(q, k) * (SWA_HD ** -0.5)
    s = jnp.where(valid, s, NEG_INF)
    m = jnp.maximum(jnp.max(s, axis=-1, keepdims=True), sink)
    p = jnp.exp(s - m)
    p = p / (jnp.sum(p, axis=-1, keepdims=True) + jnp.exp(sink - m))
    return _dot(p, v)


def _swa_prompt_body(sink_ref, q_ref, kc_ref, vc_ref, kp_ref, vp_ref, cosc_ref, sinc_ref, cosp_ref, sinp_ref,
                     o_ref, kr_ref):
    blk = pl.program_id(1)
    q = _swa_rope(q_ref[...], cosc_ref[...], sinc_ref[...])
    kc = _swa_rope(kc_ref[...], cosc_ref[...], sinc_ref[...])
    kp = _swa_rope(kp_ref[...], cosp_ref[...], sinp_ref[...])
    kr_ref[...] = kc
    k_all = jnp.concatenate([kp, kc], axis=0)
    v_all = jnp.concatenate([vp_ref[...], vc_ref[...]], axis=0)
    i = lax.broadcasted_iota(jnp.int32, (SWA_BLOCK, 2 * SWA_BLOCK), 0)
    j = lax.broadcasted_iota(jnp.int32, (SWA_BLOCK, 2 * SWA_BLOCK), 1)
    rel = i + SWA_BLOCK - j
    valid = (rel >= 0) & (rel <= WINDOW) & (j >= jnp.where(blk > 0, 0, SWA_BLOCK))
    for h in range(SWA_HEADS):
        kh = h // SWA_GROUP
        o_ref[:, h * SWA_HD:(h + 1) * SWA_HD] = _sink_attention(
            q[:, h * SWA_HD:(h + 1) * SWA_HD], k_all[:, kh * SWA_HD:(kh + 1) * SWA_HD],
            v_all[:, kh * SWA_HD:(kh + 1) * SWA_HD], valid, sink_ref[h])


SWA_SAMPLE_SEQS = 8


def _swa_sample_body(sink_ref, q_ref, kn_ref, vn_ref, kc_ref, vc_ref, cos_ref, sin_ref, o_in_ref,
                     o_ref, ko_ref, vo_ref):
    del o_in_ref
    s_len = WINDOW + DEC_SEQ
    i = lax.broadcasted_iota(jnp.int32, (DEC_SEQ, s_len), 0)
    j = lax.broadcasted_iota(jnp.int32, (DEC_SEQ, s_len), 1)
    rel = i + WINDOW - j
    valid = (rel >= 0) & (rel <= WINDOW)
    for b in range(SWA_SAMPLE_SEQS):
        rows = slice(b * DEC_SEQ, (b + 1) * DEC_SEQ)
        q = _swa_rope(q_ref[rows, :], cos_ref[...], sin_ref[...])
        kn = _swa_rope(kn_ref[rows, :], cos_ref[...], sin_ref[...])
        vn = vn_ref[rows, :]
        k_all = jnp.concatenate([kc_ref[b], kn], axis=0)
        v_all = jnp.concatenate([vc_ref[b], vn], axis=0)
        ko_ref[b] = k_all[DEC_SEQ:, :]
        vo_ref[b] = v_all[DEC_SEQ:, :]
        for h in range(SWA_HEADS):
            kh = h // SWA_GROUP
            o_ref[rows, h * SWA_HD:(h + 1) * SWA_HD] = _sink_attention(
                q[:, h * SWA_HD:(h + 1) * SWA_HD], k_all[:, kh * SWA_HD:(kh + 1) * SWA_HD],
                v_all[:, kh * SWA_HD:(kh + 1) * SWA_HD], valid, sink_ref[h])


def swa_core(proj, sinks, cache_k, cache_v):
    nb = SEQ // SWA_BLOCK
    smem = pl.BlockSpec(memory_space=pltpu.SMEM)
    cos_p, sin_p = _swa_rope_tables(jnp.arange(SEQ, dtype=jnp.int32))
    kcol = SWA_NQ // SWA_NK
    cur = lambda b, i: b * nb + i
    prev = lambda b, i: b * nb + jnp.maximum(i - 1, 0)
    o, k_rot = pl.pallas_call(
        _swa_prompt_body,
        grid=(BATCH, nb),
        in_specs=[
            smem,
            pl.BlockSpec((SWA_BLOCK, SWA_NQ), lambda b, i: (cur(b, i), 0)),
            pl.BlockSpec((SWA_BLOCK, SWA_NK), lambda b, i: (cur(b, i), kcol)),
            pl.BlockSpec((SWA_BLOCK, SWA_NK), lambda b, i: (cur(b, i), kcol + 1)),
            pl.BlockSpec((SWA_BLOCK, SWA_NK), lambda b, i: (prev(b, i), kcol)),
            pl.BlockSpec((SWA_BLOCK, SWA_NK), lambda b, i: (prev(b, i), kcol + 1)),
            pl.BlockSpec((SWA_BLOCK, LANES), lambda b, i: (i, 0)),
            pl.BlockSpec((SWA_BLOCK, LANES), lambda b, i: (i, 0)),
            pl.BlockSpec((SWA_BLOCK, LANES), lambda b, i: (jnp.maximum(i - 1, 0), 0)),
            pl.BlockSpec((SWA_BLOCK, LANES), lambda b, i: (jnp.maximum(i - 1, 0), 0)),
        ],
        out_specs=[
            pl.BlockSpec((SWA_BLOCK, SWA_NQ), lambda b, i: (cur(b, i), 0)),
            pl.BlockSpec((SWA_BLOCK, SWA_NK), lambda b, i: (cur(b, i), 0)),
        ],
        out_shape=[jax.ShapeDtypeStruct((NT, SWA_NQ), F32),
                   jax.ShapeDtypeStruct((NP, SWA_NK), F32)],
        compiler_params=_cparams("parallel", "arbitrary"),
        name="swa_prompt",
    )(sinks, proj, proj, proj, proj, proj, cos_p, sin_p, cos_p, sin_p)

    cos_s, sin_s = _swa_rope_tables(PAST_LEN + jnp.arange(DEC_SEQ, dtype=jnp.int32))
    rows = SWA_SAMPLE_SEQS * DEC_SEQ
    row0 = NP // rows
    o, k_s, v_s = pl.pallas_call(
        _swa_sample_body,
        grid=(DEC_BATCH // SWA_SAMPLE_SEQS,),
        in_specs=[
            smem,
            pl.BlockSpec((rows, SWA_NQ), lambda g: (row0 + g, 0)),
            pl.BlockSpec((rows, SWA_NK), lambda g: (row0 + g, kcol)),
            pl.BlockSpec((rows, SWA_NK), lambda g: (row0 + g, kcol + 1)),
            pl.BlockSpec((SWA_SAMPLE_SEQS, WINDOW, SWA_NK), lambda g: (g, 0, 0)),
            pl.BlockSpec((SWA_SAMPLE_SEQS, WINDOW, SWA_NK), lambda g: (g, 0, 0)),
            pl.BlockSpec((DEC_SEQ, LANES), lambda g: (0, 0)),
            pl.BlockSpec((DEC_SEQ, LANES), lambda g: (0, 0)),
            pl.BlockSpec(memory_space=pl.ANY),
        ],
        out_specs=[
            pl.BlockSpec((rows, SWA_NQ), lambda g: (row0 + g, 0)),
            pl.BlockSpec((SWA_SAMPLE_SEQS, WINDOW, SWA_NK), lambda g: (g, 0, 0)),
            pl.BlockSpec((SWA_SAMPLE_SEQS, WINDOW, SWA_NK), lambda g: (g, 0, 0)),
        ],
        out_shape=[jax.ShapeDtypeStruct((NT, SWA_NQ), F32),
                   jax.ShapeDtypeStruct((DEC_BATCH, WINDOW, SWA_NK), F32),
                   jax.ShapeDtypeStruct((DEC_BATCH, WINDOW, SWA_NK), F32)],
        input_output_aliases={8: 0},
        compiler_params=_cparams("parallel"),
        name="swa_sample",
    )(sinks, proj, proj, proj, cache_k, cache_v, cos_s, sin_s, o)
    return o, k_rot, k_s, v_s


RWKV_ROW_TILE = 256


def _block_diag_ones():
    r = jnp.arange(MXU_DIM_V7X) // RWKV_HD
    return (r[:, None] == r[None, :]).astype(BF16)


def _rwkv_proj_body(h_ref, xp_ref, mu_ref, wr_ref, wk_ref, wv_ref, w0_ref, w1_ref, w2_ref, a0_ref, a1_ref,
                    a2_ref, g1_ref, g2_ref, kk_ref, ka_ref, bd_ref,
                    r_out, lw_out, k_out, v_out, kk_out, b_out, g_out):
    h = h_ref[...]
    d = xp_ref[...] - h
    xs = [h + d * mu_ref[i:i + 1, :] for i in range(6)]
    r_out[...] = _dot(xs[0], wr_ref[...])
    k = _dot(xs[1], wk_ref[...])
    v_out[...] = _dot(xs[2], wv_ref[...])
    z = w0_ref[...] + _dot(jnp.tanh(_dot(xs[3], w1_ref[...])), w2_ref[...])
    softplus = jnp.maximum(-z, 0.0) + jnp.log1p(jnp.exp(-jnp.abs(z)))
    lw_out[...] = -jnp.exp(-softplus - 0.5)
    a = jax.nn.sigmoid(a0_ref[...] + _dot(_dot(xs[4], a1_ref[...]), a2_ref[...]))
    g_out[...] = _dot(jax.nn.sigmoid(_dot(xs[5], g1_ref[...])), g2_ref[...])
    kk = k * kk_ref[...]
    norm = jnp.sqrt(_seg_sum(kk * kk, bd_ref[...], 2))
    kk = kk / jnp.maximum(norm, 1e-12)
    kk_out[...] = kk
    b_out[...] = kk * a
    k_out[...] = k * (1.0 + (a - 1.0) * ka_ref[...])


def rwkv_proj(h, xp, p):
    row = pl.BlockSpec((RWKV_ROW_TILE, D_MODEL), lambda i: (i, 0))
    full = lambda a: pl.BlockSpec(a.shape, lambda i: (0,) * a.ndim)
    consts = [p["mu"], p["wr"], p["wk"], p["wv"], p["w0"], p["w1"], p["w2"], p["a0"], p["a1"], p["a2"],
              p["g1"], p["g2"], p["k_k"], p["k_a"], p["bd"]]
    return pl.pallas_call(
        _rwkv_proj_body,
        grid=(NT // RWKV_ROW_TILE,),
        in_specs=[row, row] + [full(c) for c in consts],
        out_specs=[row] * 7,
        out_shape=[jax.ShapeDtypeStruct((NT, D_MODEL), F32)] * 7,
        compiler_params=_cparams("parallel"),
        name="rwkv_proj",
    )(h, xp, *consts)


def _rwkv_chunk(c_len, r_ref, lw_ref, k_ref, v_ref, kk_ref, b_ref, state, store, y_ref):
    lw = lw_ref[...]
    row = lax.broadcasted_iota(jnp.int32, (c_len, c_len), 0)
    col = lax.broadcasted_iota(jnp.int32, (c_len, c_len), 1)
    tri = (row >= col).astype(BF16)
    lam = sum(jnp.dot(tri, part, preferred_element_type=F32) for part in _split3(lw))
    lam_end = lam[c_len - 1:c_len, :]
    e_neg = jnp.exp(-lam)
    e_end = jnp.exp(lam_end - lam)
    b = b_ref[...]
    k = k_ref[...]
    at = -kk_ref[...] * jnp.exp(lam - lw)
    rt = r_ref[...] * jnp.exp(lam)
    bh = b * e_neg
    kh = k * e_neg
    bb = b * e_end
    kb = k * e_end
    e_tot = jnp.exp(lam_end)
    v = v_ref[...]

    row2 = lax.broadcasted_iota(jnp.int32, (c_len, 2 * c_len), 0)
    col2 = lax.broadcasted_iota(jnp.int32, (c_len, 2 * c_len), 1) % c_len
    strict = row2 > col2
    incl = row2 >= col2
    n_double = int(math.log2(c_len))
    assert 2 ** n_double == c_len
    for h in range(RWKV_HEADS):
        sl = slice(h * RWKV_HD, (h + 1) * RWKV_HD)
        s0 = state(h)
        lhs2 = jnp.concatenate([at[:, sl], rt[:, sl]], axis=0)
        rhs2 = jnp.concatenate([bh[:, sl], kh[:, sl]], axis=0)
        m = _dot_nt(lhs2, rhs2)
        p = _dot_nt(lhs2, s0)
        ab = jnp.where(strict, m[:c_len, :], 0.0)
        rbk = jnp.where(incl, m[c_len:, :], 0.0)
        vh = v[:, sl]
        a_pow = ab[:, :c_len]
        u = p[:c_len, :] + _dot(ab[:, c_len:], vh)
        for it in range(n_double):
            u = u + _dot(a_pow, u)
            if it + 1 < n_double:
                a_pow = _dot(a_pow, a_pow)
        uv = jnp.concatenate([u, vh], axis=0)
        y_ref[:, sl] = p[c_len:, :] + _dot(rbk, uv)
        bkb = jnp.concatenate([bb[:, sl], kb[:, sl]], axis=0)
        store(h, s0 * e_tot[:, sl] + _dot_tn(uv, bkb))


def _rwkv_prompt_body(r_ref, lw_ref, k_ref, v_ref, kk_ref, b_ref, y_ref, s_ref, s_scr):
    c = pl.program_id(1)

    @pl.when(c == 0)
    def _():
        s_scr[...] = jnp.zeros_like(s_scr)

    def store(h, s):
        s_scr[h] = s

    _rwkv_chunk(RWKV_CHUNK, r_ref, lw_ref, k_ref, v_ref, kk_ref, b_ref, lambda h: s_scr[h], store, y_ref)

    @pl.when(c == pl.num_programs(1) - 1)
    def _():
        s_ref[0] = s_scr[...]


def _rwkv_sample_body(r_ref, lw_ref, k_ref, v_ref, kk_ref, b_ref, s0_ref, y_in_ref, y_ref, s_ref):
    del y_in_ref
    def store(h, s):
        s_ref[0, h] = s

    _rwkv_chunk(DEC_SEQ, r_ref, lw_ref, k_ref, v_ref, kk_ref, b_ref, lambda h: s0_ref[0, h], store, y_ref)


def rwkv_core(r, lw, k, v, kk, b, state_s):
    nc = SEQ // RWKV_CHUNK
    blk = pl.BlockSpec((RWKV_CHUNK, D_MODEL), lambda bi, c: (bi * nc + c, 0))
    st_shape = (1, RWKV_HEADS, RWKV_HD, RWKV_HD)
    y, s_p = pl.pallas_call(
        _rwkv_prompt_body,
        grid=(BATCH, nc),
        in_specs=[blk] * 6,
        out_specs=[blk, pl.BlockSpec(st_shape, lambda bi, c: (bi, 0, 0, 0))],
        out_shape=[jax.ShapeDtypeStruct((NT, D_MODEL), F32),
                   jax.ShapeDtypeStruct((BATCH,) + st_shape[1:], F32)],
        scratch_shapes=[pltpu.VMEM(st_shape[1:], F32)],
        compiler_params=_cparams("parallel", "arbitrary"),
        name="rwkv_prompt",
    )(r, lw, k, v, kk, b)

    row0 = NP // DEC_SEQ
    blk = pl.BlockSpec((DEC_SEQ, D_MODEL), lambda bi: (row0 + bi, 0))
    st = pl.BlockSpec(st_shape, lambda bi: (bi, 0, 0, 0))
    y, s_s = pl.pallas_call(
        _rwkv_sample_body,
        grid=(DEC_BATCH,),
        in_specs=[blk] * 6 + [st, pl.BlockSpec(memory_space=pl.ANY)],
        out_specs=[blk, st],
        out_shape=[jax.ShapeDtypeStruct((NT, D_MODEL), F32),
                   jax.ShapeDtypeStruct((DEC_BATCH,) + st_shape[1:], F32)],
        input_output_aliases={7: 0},
        compiler_params=_cparams("parallel"),
        name="rwkv_sample",
    )(r, lw, k, v, kk, b, state_s, y)
    return y, s_p, s_s


def _rwkv_out_body(x_ref, y_ref, r_ref, k_ref, v_ref, g_ref, rk_ref, lnw_ref, lnb_ref, bd_ref, wo_ref, o_ref):
    bd = bd_ref[...]
    y = y_ref[...]
    yc = y - _seg_sum(y, bd, 2) * (1.0 / RWKV_HD)
    var = _seg_sum(yc * yc, bd, 1) * (1.0 / RWKV_HD)
    yn = yc * lax.rsqrt(var + RWKV_GN_EPS) * lnw_ref[...] + lnb_ref[...]
    bonus = _seg_sum(r_ref[...] * k_ref[...] * rk_ref[...], bd, 2) * v_ref[...]
    z = (yn + bonus) * g_ref[...]
    o_ref[...] = x_ref[...] + _dot(z, wo_ref[...])


def rwkv_out(x, y, r, k, v, g, p):
    row = pl.BlockSpec((RWKV_ROW_TILE, D_MODEL), lambda i: (i, 0))
    full = lambda a: pl.BlockSpec(a.shape, lambda i: (0,) * a.ndim)
    consts = [p["r_k"], p["ln_w"], p["ln_b"], p["bd"], p["w_out"]]
    return pl.pallas_call(
        _rwkv_out_body,
        grid=(NT // RWKV_ROW_TILE,),
        in_specs=[row] * 6 + [full(c) for c in consts],
        out_specs=row,
        out_shape=jax.ShapeDtypeStruct((NT, D_MODEL), F32),
        compiler_params=_cparams("parallel"),
        name="rwkv_out",
    )(x, y, r, k, v, g, *consts)


def _pad_cols(w, n):
    return jnp.pad(w, ((0, 0), (0, n - w.shape[1])))


def _pad_rows(w, n):
    return jnp.pad(w, ((0, n - w.shape[0]), (0, 0)))


def kernel(x_prompt, x_sample, state_ret, cache_swa_k, cache_swa_v, state_rwkv_wkv, state_rwkv_shift,
           norm_mix, norm_mlp, norm_final, ret_w_in, ret_w_out,
           swa_w_in, swa_b_in, swa_sinks, swa_w_out,
           rwkv_mu, rwkv_w_rkv, rwkv_w0, rwkv_w1, rwkv_w2, rwkv_a0, rwkv_a1, rwkv_a2,
           rwkv_g1, rwkv_g2, rwkv_k_k, rwkv_k_a, rwkv_r_k, rwkv_ln_w, rwkv_ln_b, rwkv_w_out,
           mlp_w_up, mlp_w_down):
    x = jnp.concatenate([x_prompt.reshape(NP, D_MODEL), x_sample.reshape(NS, D_MODEL)], axis=0)
    row = lambda t: t.reshape(1, -1)
    ret_p, ret_s, k_p, k_s, v_p, v_s, wkv_p, wkv_s, sh_p, sh_s = ([] for _ in range(10))
    for layer in range(DEPTH):
        kind = LAYER_KINDS[layer]
        j = LAYER_KINDS[:layer].count(kind)
        g_mix = row(norm_mix[layer])
        if kind == 0:
            n_in = ret_w_in.shape[2]
            proj = norm_matmul(x, g_mix, ret_w_in[j].astype(BF16), jnp.zeros((1, n_in), F32), 2048)
            o, s_p, s_s = retention_core(proj, state_ret[j])
            x = matmul_residual(x, o, ret_w_out[j].astype(BF16))
            ret_p.append(s_p)
            ret_s.append(s_s)
        elif kind == 1:
            proj = norm_matmul(x, g_mix, swa_w_in[j].astype(BF16), row(swa_b_in[j]), SWA_NQ + 2 * SWA_NK)
            o, k_rot, kc, vc = swa_core(proj, swa_sinks[j],
                                        cache_swa_k[j].reshape(DEC_BATCH, WINDOW, SWA_NK),
                                        cache_swa_v[j].reshape(DEC_BATCH, WINDOW, SWA_NK))
            x = matmul_residual(x, o, swa_w_out[j].astype(BF16))
            kv_shape = (-1, WINDOW, SWA_KV_HEADS, SWA_HD)
            k_p.append(k_rot.reshape(BATCH, SEQ, SWA_NK)[:, -WINDOW:].reshape(kv_shape))
            v_p.append(proj[:NP, SWA_NQ + SWA_NK:].reshape(BATCH, SEQ, SWA_NK)[:, -WINDOW:].reshape(kv_shape))
            k_s.append(kc.reshape(kv_shape))
            v_s.append(vc.reshape(kv_shape))
        else:
            lora_w = LANES
            lora_g = 2 * LANES
            p = dict(
                mu=rwkv_mu[j], wr=rwkv_w_rkv[j, 0].astype(BF16), wk=rwkv_w_rkv[j, 1].astype(BF16),
                wv=rwkv_w_rkv[j, 2].astype(BF16), w0=row(rwkv_w0[j]),
                w1=_pad_cols(rwkv_w1[j], lora_w).astype(BF16), w2=_pad_rows(rwkv_w2[j], lora_w).astype(BF16),
                a0=row(rwkv_a0[j]),
                a1=_pad_cols(rwkv_a1[j], lora_w).astype(BF16), a2=_pad_rows(rwkv_a2[j], lora_w).astype(BF16),
                g1=_pad_cols(rwkv_g1[j], lora_g).astype(BF16), g2=_pad_rows(rwkv_g2[j], lora_g).astype(BF16),
                k_k=row(rwkv_k_k[j]), k_a=row(rwkv_k_a[j]), r_k=row(rwkv_r_k[j]),
                ln_w=row(rwkv_ln_w[j]), ln_b=row(rwkv_ln_b[j]), bd=_block_diag_ones(),
                w_out=rwkv_w_out[j].astype(BF16))
            h = rmsnorm(x, g_mix)
            h_p = h[:NP].reshape(BATCH, SEQ, D_MODEL)
            h_s = h[NP:].reshape(DEC_BATCH, DEC_SEQ, D_MODEL)
            xp = jnp.concatenate([
                jnp.concatenate([jnp.zeros((BATCH, 1, D_MODEL), F32), h_p[:, :-1]], axis=1).reshape(NP, D_MODEL),
                jnp.concatenate([state_rwkv_shift[j][:, None, :], h_s[:, :-1]], axis=1).reshape(NS, D_MODEL),
            ], axis=0)
            r, lw, k, v, kk, b, g = rwkv_proj(h, xp, p)
            y, s_p, s_s = rwkv_core(r, lw, k, v, kk, b, state_rwkv_wkv[j])
            x = rwkv_out(x, y, r, k, v, g, p)
            wkv_p.append(s_p)
            wkv_s.append(s_s)
            sh_p.append(h_p[:, -1])
            sh_s.append(h_s[:, -1])
        x = mlp(x, row(norm_mlp[layer]), mlp_w_up[layer].astype(BF16), mlp_w_down[layer].astype(BF16))
    y = rmsnorm(x, row(norm_final))
    return (y[:NP].reshape(BATCH, SEQ, D_MODEL), y[NP:].reshape(DEC_BATCH, DEC_SEQ, D_MODEL),
            jnp.stack(ret_p), jnp.stack(ret_s), jnp.stack(k_p), jnp.stack(k_s), jnp.stack(v_p), jnp.stack(v_s),
            jnp.stack(wkv_p), jnp.stack(wkv_s), jnp.stack(sh_p), jnp.stack(sh_s))
```

```python
import functools
import math

import jax
import jax.numpy as jnp
from jax import lax
from jax.experimental import pallas as pl
from jax.experimental.pallas import tpu as pltpu

F32 = jnp.float32
BF16 = jnp.bfloat16

D_MODEL = 1024
BATCH = 8
SEQ = 2048
DEPTH = 4
DEC_BATCH = 128
DEC_SEQ = 8
PAST_LEN = 8192
LAYER_KINDS = tuple(i % 3 for i in range(DEPTH))
NORM_EPS = 1e-6
ROPE_THETA = 10000.0
NEG_INF = -1e30
D_FF = 4 * D_MODEL

RET_HEADS = 4
RET_DK = D_MODEL // RET_HEADS
RET_DV = 2 * D_MODEL // RET_HEADS
RET_CHUNK = 128
RET_NQ = RET_HEADS * RET_DK
RET_NV = RET_HEADS * RET_DV

SWA_HEADS = 16
SWA_KV_HEADS = 4
SWA_GROUP = SWA_HEADS // SWA_KV_HEADS
SWA_HD = D_MODEL // SWA_HEADS
WINDOW = 128
SWA_BLOCK = 128
SWA_NQ = SWA_HEADS * SWA_HD
SWA_NK = SWA_KV_HEADS * SWA_HD

RWKV_HD = 64
RWKV_HEADS = D_MODEL // RWKV_HD
RWKV_GN_EPS = 64e-5
RWKV_CHUNK = 32

NP = BATCH * SEQ
NS = DEC_BATCH * DEC_SEQ
NT = NP + NS

VMEM_LIMIT_V7X = 48 * 1024 * 1024
LANES = 128
SUBLANES = 8
MXU_DIM_V7X = 256

ROW_TILE = 512


def _cparams(*sem):
    return pltpu.CompilerParams(dimension_semantics=sem, vmem_limit_bytes=VMEM_LIMIT_V7X)


def _dot(a, b):
    return jnp.dot(a.astype(BF16), b.astype(BF16), preferred_element_type=F32)


def _dot_nt(a, b):
    return lax.dot_general(a.astype(BF16), b.astype(BF16), (((1,), (1,)), ((), ())),
                           preferred_element_type=F32)


def _dot_tn(a, b):
    return lax.dot_general(a.astype(BF16), b.astype(BF16), (((0,), (0,)), ((), ())),
                           preferred_element_type=F32)


def _div_pow2(x, n):
    assert n & (n - 1) == 0
    return jnp.right_shift(x, n.bit_length() - 1)


def _mod_pow2(x, n):
    assert n & (n - 1) == 0
    return jnp.bitwise_and(x, n - 1)


def _rms(x, g):
    ms = jnp.mean(x * x, axis=-1, keepdims=True)
    return x * lax.rsqrt(ms + NORM_EPS) * g


def _split3(x):
    hi = x.astype(BF16)
    r1 = x - hi.astype(F32)
    mid = r1.astype(BF16)
    lo = (r1 - mid.astype(F32)).astype(BF16)
    return hi, mid, lo


def _seg_sum(x, bd, passes):
    parts = _split3(x)[:passes]
    cols = []
    for j in range(D_MODEL // MXU_DIM_V7X):
        sl = slice(j * MXU_DIM_V7X, (j + 1) * MXU_DIM_V7X)
        acc = jnp.dot(parts[0][:, sl], bd, preferred_element_type=F32)
        for p in parts[1:]:
            acc = acc + jnp.dot(p[:, sl], bd, preferred_element_type=F32)
        cols.append(acc)
    return jnp.concatenate(cols, axis=-1)


def _norm_matmul_body(x_ref, g_ref, w_ref, b_ref, o_ref, xn_ref):
    @pl.when(pl.program_id(1) == 0)
    def _():
        xn_ref[...] = _rms(x_ref[...], g_ref[...]).astype(BF16)

    o_ref[...] = jnp.dot(xn_ref[...], w_ref[...], preferred_element_type=F32) + b_ref[...]


def norm_matmul(x, g, w, b, tn):
    n_out = w.shape[1]
    return pl.pallas_call(
        _norm_matmul_body,
        grid=(NT // ROW_TILE, n_out // tn),
        in_specs=[
            pl.BlockSpec((ROW_TILE, D_MODEL), lambda i, j: (i, 0)),
            pl.BlockSpec((1, D_MODEL), lambda i, j: (0, 0)),
            pl.BlockSpec((D_MODEL, tn), lambda i, j: (0, j)),
            pl.BlockSpec((1, tn), lambda i, j: (0, j)),
        ],
        out_specs=pl.BlockSpec((ROW_TILE, tn), lambda i, j: (i, j)),
        out_shape=jax.ShapeDtypeStruct((NT, n_out), F32),
        scratch_shapes=[pltpu.VMEM((ROW_TILE, D_MODEL), BF16)],
        compiler_params=_cparams("parallel", "arbitrary"),
        name="norm_matmul",
    )(x, g, w, b)


def _matmul_residual_body(x_ref, a_ref, w_ref, o_ref):
    o_ref[...] = x_ref[...] + jnp.dot(a_ref[...].astype(BF16), w_ref[...], preferred_element_type=F32)


def matmul_residual(x, a, w):
    k = a.shape[1]
    return pl.pallas_call(
        _matmul_residual_body,
        grid=(NT // ROW_TILE,),
        in_specs=[
            pl.BlockSpec((ROW_TILE, D_MODEL), lambda i: (i, 0)),
            pl.BlockSpec((ROW_TILE, k), lambda i: (i, 0)),
            pl.BlockSpec((k, D_MODEL), lambda i: (0, 0)),
        ],
        out_specs=pl.BlockSpec((ROW_TILE, D_MODEL), lambda i: (i, 0)),
        out_shape=jax.ShapeDtypeStruct((NT, D_MODEL), F32),
        compiler_params=_cparams("parallel"),
        name="matmul_residual",
    )(x, a, w)


MLP_FF_TILE = 1024


def _mlp_body(x_ref, g_ref, wu_ref, wd_ref, o_ref, xn_ref, acc_ref):
    f = pl.program_id(1)

    @pl.when(f == 0)
    def _():
        xn_ref[...] = _rms(x_ref[...], g_ref[...]).astype(BF16)
        acc_ref[...] = jnp.zeros_like(acc_ref)

    a = jnp.maximum(jnp.dot(xn_ref[...], wu_ref[...], preferred_element_type=F32), 0.0)
    acc_ref[...] += jnp.dot((a * a).astype(BF16), wd_ref[...], preferred_element_type=F32)

    @pl.when(f == pl.num_programs(1) - 1)
    def _():
        o_ref[...] = x_ref[...] + acc_ref[...]


def mlp(x, g, w_up, w_down):
    return pl.pallas_call(
        _mlp_body,
        grid=(NT // ROW_TILE, D_FF // MLP_FF_TILE),
        in_specs=[
            pl.BlockSpec((ROW_TILE, D_MODEL), lambda i, f: (i, 0)),
            pl.BlockSpec((1, D_MODEL), lambda i, f: (0, 0)),
            pl.BlockSpec((D_MODEL, MLP_FF_TILE), lambda i, f: (0, f)),
            pl.BlockSpec((MLP_FF_TILE, D_MODEL), lambda i, f: (f, 0)),
        ],
        out_specs=pl.BlockSpec((ROW_TILE, D_MODEL), lambda i, f: (i, 0)),
        out_shape=jax.ShapeDtypeStruct((NT, D_MODEL), F32),
        scratch_shapes=[pltpu.VMEM((ROW_TILE, D_MODEL), BF16), pltpu.VMEM((ROW_TILE, D_MODEL), F32)],
        compiler_params=_cparams("parallel", "arbitrary"),
        name="mlp",
    )(x, g, w_up, w_down)


def _rmsnorm_body(x_ref, g_ref, o_ref):
    o_ref[...] = _rms(x_ref[...], g_ref[...])


def rmsnorm_rows(x, g, row0, n_rows):
    tile0 = row0 // ROW_TILE
    return pl.pallas_call(
        _rmsnorm_body,
        grid=(n_rows // ROW_TILE,),
        in_specs=[pl.BlockSpec((ROW_TILE, D_MODEL), lambda i: (tile0 + i, 0)),
                  pl.BlockSpec((1, D_MODEL), lambda i: (0, 0))],
        out_specs=pl.BlockSpec((ROW_TILE, D_MODEL), lambda i: (i, 0)),
        out_shape=jax.ShapeDtypeStruct((n_rows, D_MODEL), F32),
        compiler_params=_cparams("parallel"),
        name="rmsnorm",
    )(x, g)


def _ret_tables(c):
    log_gamma = jnp.log1p(-jnp.exp2(-5.0 - jnp.arange(RET_HEADS, dtype=F32)))
    idx = jnp.arange(c, dtype=F32)
    diff = idx[:, None] - idx[None, :]
    inner = jnp.where(diff >= 0, jnp.exp(log_gamma[:, None, None] * jnp.maximum(diff, 0.0)), 0.0)
    q_decay = jnp.exp(log_gamma[:, None] * (idx[None, :] + 1.0))
    k_decay = jnp.exp(log_gamma[:, None] * (c - 1.0 - idx[None, :]))
    chunk_decay = jnp.exp(log_gamma * c)
    rep = lambda t: jnp.broadcast_to(t[:, :, None], (RET_HEADS, c, LANES))
    return inner, rep(q_decay), rep(k_decay), chunk_decay


def _rope_tables(pos, half):
    inv_freq = jnp.power(ROPE_THETA, -jnp.arange(half, dtype=F32) / half)
    ang = pos.astype(F32)[:, None] * inv_freq[None, :]
    return jnp.cos(ang), jnp.sin(ang)


def _ret_chunk(proj_ref, cos, sin, inner_ref, qd_ref, kd_ref, cd_ref, state, store, o_ref):
    for h in range(RET_HEADS):
        q = proj_ref[:, h * RET_DK:(h + 1) * RET_DK]
        k = proj_ref[:, RET_NQ + h * RET_DK:RET_NQ + (h + 1) * RET_DK]
        v = proj_ref[:, 2 * RET_NQ + h * RET_DV:2 * RET_NQ + (h + 1) * RET_DV]
        g = proj_ref[:, 2 * RET_NQ + RET_NV + h * RET_DV:2 * RET_NQ + RET_NV + (h + 1) * RET_DV]
        half = RET_DK // 2
        q1, q2 = q[:, :half], q[:, half:]
        k1, k2 = k[:, :half], k[:, half:]
        q = jnp.concatenate([q1 * cos - q2 * sin, q2 * cos + q1 * sin], axis=-1)
        k = jnp.concatenate([k1 * cos - k2 * sin, k2 * cos + k1 * sin], axis=-1) * (RET_DK ** -0.5)
        s0 = state(h)
        scores = _dot_nt(q, k) * inner_ref[h]
        qd = jnp.concatenate([qd_ref[h]] * (RET_DV // LANES), axis=-1)
        kd = jnp.concatenate([kd_ref[h]] * (RET_DK // LANES), axis=-1)
        o = _dot(scores, v) + _dot(q, s0) * qd
        store(h, s0 * cd_ref[h] + _dot_tn(k * kd, v))
        o = o * lax.rsqrt(jnp.mean(o * o, axis=-1, keepdims=True) + NORM_EPS)
        o_ref[:, h * RET_DV:(h + 1) * RET_DV] = o * (g * jax.nn.sigmoid(g))


def _ret_prompt_body(cd_ref, proj_ref, cos_ref, sin_ref, inner_ref, qd_ref, kd_ref, o_ref, s_ref, s_scr):
    c = pl.program_id(1)

    @pl.when(c == 0)
    def _():
        s_scr[...] = jnp.zeros_like(s_scr)

    def store(h, s):
        s_scr[h] = s

    _ret_chunk(proj_ref, cos_ref[...], sin_ref[...], inner_ref, qd_ref, kd_ref, cd_ref,
               lambda h: s_scr[h], store, o_ref)

    @pl.when(c == pl.num_programs(1) - 1)
    def _():
        s_ref[0] = s_scr[...]


def _ret_sample_body(cd_ref, proj_ref, cos_ref, sin_ref, inner_ref, qd_ref, kd_ref, s0_ref, *rest):
    o_ref, s_ref = rest[-2:]

    def store(h, s):
        s_ref[0, 0, h] = s

    _ret_chunk(proj_ref, cos_ref[...], sin_ref[...], inner_ref, qd_ref, kd_ref, cd_ref,
               lambda h: s0_ref[0, 0, h], store, o_ref)


def retention_core(proj, state_ret, j, s_stack):
    n_in = proj.shape[1]
    nc = SEQ // RET_CHUNK
    smem = pl.BlockSpec(memory_space=pltpu.SMEM)
    cos_p, sin_p = _rope_tables(jnp.arange(SEQ, dtype=jnp.int32), RET_DK // 2)
    inner, qd, kd, cd = _ret_tables(RET_CHUNK)
    full3 = lambda shape: pl.BlockSpec(shape, lambda b, c: (0, 0, 0))
    o, s_p = pl.pallas_call(
        _ret_prompt_body,
        grid=(BATCH, nc),
        in_specs=[
            smem,
            pl.BlockSpec((RET_CHUNK, n_in), lambda b, c: (b * nc + c, 0)),
            pl.BlockSpec((RET_CHUNK, LANES), lambda b, c: (c, 0)),
            pl.BlockSpec((RET_CHUNK, LANES), lambda b, c: (c, 0)),
            full3(inner.shape), full3(qd.shape), full3(kd.shape),
        ],
        out_specs=[
            pl.BlockSpec((RET_CHUNK, RET_NV), lambda b, c: (b * nc + c, 0)),
            pl.BlockSpec((1, RET_HEADS, RET_DK, RET_DV), lambda b, c: (b, 0, 0, 0)),
        ],
        out_shape=[jax.ShapeDtypeStruct((NT, RET_NV), F32),
                   jax.ShapeDtypeStruct((BATCH, RET_HEADS, RET_DK, RET_DV), F32)],
        scratch_shapes=[pltpu.VMEM((RET_HEADS, RET_DK, RET_DV), F32)],
        compiler_params=_cparams("parallel", "arbitrary"),
        name="retention_prompt",
    )(cd, proj, cos_p, sin_p, inner, qd, kd)

    cs = math.gcd(DEC_SEQ, RET_CHUNK)
    assert cs == DEC_SEQ
    cos_s, sin_s = _rope_tables(PAST_LEN + jnp.arange(DEC_SEQ, dtype=jnp.int32), RET_DK // 2)
    inner, qd, kd, cd = _ret_tables(cs)
    row0 = NP // DEC_SEQ
    full2 = lambda shape: pl.BlockSpec(shape, lambda b: (0, 0))
    full3 = lambda shape: pl.BlockSpec(shape, lambda b: (0, 0, 0))
    st = pl.BlockSpec((1, 1, RET_HEADS, RET_DK, RET_DV), lambda b: (j, b, 0, 0, 0))
    stacked = [] if s_stack is None else [s_stack]
    o, s_s = pl.pallas_call(
        _ret_sample_body,
        grid=(DEC_BATCH,),
        in_specs=[
            smem,
            pl.BlockSpec((DEC_SEQ, n_in), lambda b: (row0 + b, 0)),
            full2(cos_s.shape), full2(sin_s.shape),
            full3(inner.shape), full3(qd.shape), full3(kd.shape),
            st,
            pl.BlockSpec(memory_space=pl.ANY),
        ] + [pl.BlockSpec(memory_space=pl.ANY)] * len(stacked),
        out_specs=[pl.BlockSpec((DEC_SEQ, RET_NV), lambda b: (row0 + b, 0)), st],
        out_shape=[jax.ShapeDtypeStruct((NT, RET_NV), F32),
                   jax.ShapeDtypeStruct(state_ret.shape, F32)],
        input_output_aliases={8: 0, **({9: 1} if stacked else {})},
        compiler_params=_cparams("parallel"),
        name="retention_sample",
    )(cd, proj, cos_s, sin_s, inner, qd, kd, state_ret, o, *stacked)
    return o, s_p, s_s


def _swa_rope_tables(pos):
    cos, sin = _rope_tables(pos, SWA_HD // 2)
    cos2 = jnp.concatenate([cos, cos], axis=-1)
    sin2 = jnp.concatenate([-sin, sin], axis=-1)
    rep = LANES // SWA_HD
    return jnp.tile(cos2, (1, rep)), jnp.tile(sin2, (1, rep))


def _swa_rope(x, cos, sin):
    n = x.shape[1]
    half = SWA_HD // 2
    lane = lax.broadcasted_iota(jnp.int32, x.shape, 1)
    partner = jnp.where(lane % SWA_HD < half, pltpu.roll(x, n - half, 1), pltpu.roll(x, half, 1))
    reps = n // LANES
    return x * jnp.concatenate([cos] * reps, axis=-1) + partner * jnp.concatenate([sin] * reps, axis=-1)


def _sink_attention(q, k, v, valid, sink_ref):
    t = q.shape[0]
    outs = [None] * SWA_HEADS
    for kh in range(SWA_KV_HEADS):
        heads = range(kh * SWA_GROUP, (kh + 1) * SWA_GROUP)
        qs = jnp.concatenate([q[:, h * SWA_HD:(h + 1) * SWA_HD] for h in heads], axis=0)
        sink = jnp.concatenate([jnp.full((t, 1), sink_ref[h], F32) for h in heads], axis=0)
        s = _dot_nt(qs, k[:, kh * SWA_HD:(kh + 1) * SWA_HD]) * (SWA_HD ** -0.5)
        s = jnp.where(valid, s, NEG_INF)
        m = jnp.maximum(jnp.max(s, axis=-1, keepdims=True), sink)
        p = jnp.exp(s - m)
        denom = jnp.sum(p, axis=-1, keepdims=True) + jnp.exp(sink - m)
        o = _dot(p, v[:, kh * SWA_HD:(kh + 1) * SWA_HD]) / denom
        for g, h in enumerate(heads):
            outs[h] = o[g * t:(g + 1) * t]
    return jnp.concatenate(outs, axis=-1)


def _swa_prompt_body(sink_ref, q_ref, kc_ref, vc_ref, kp_ref, vp_ref, cosc_ref, sinc_ref, cosp_ref, sinp_ref,
                     o_ref, kr_ref):
    blk = pl.program_id(1)
    q = _swa_rope(q_ref[...], cosc_ref[...], sinc_ref[...])
    kc = _swa_rope(kc_ref[...], cosc_ref[...], sinc_ref[...])
    kp = _swa_rope(kp_ref[...], cosp_ref[...], sinp_ref[...])
    kr_ref[...] = kc
    k_all = jnp.concatenate([kp, kc], axis=0)
    v_all = jnp.concatenate([vp_ref[...], vc_ref[...]], axis=0)
    shape = (SWA_GROUP * SWA_BLOCK, 2 * SWA_BLOCK)
    i = _mod_pow2(lax.broadcasted_iota(jnp.int32, shape, 0), SWA_BLOCK)
    j = lax.broadcasted_iota(jnp.int32, shape, 1)
    rel = i + SWA_BLOCK - j
    valid = (rel >= 0) & (rel <= WINDOW) & (j >= jnp.where(blk > 0, 0, SWA_BLOCK))
    o_ref[...] = _sink_attention(q, k_all, v_all, valid, sink_ref)


SWA_SAMPLE_SEQS = 8


def _swa_sample_body(sink_ref, q_ref, kn_ref, vn_ref, kc_ref, vc_ref, cos_ref, sin_ref, o_in_ref,
                     o_ref, ko_ref, vo_ref):
    del o_in_ref
    rows = SWA_SAMPLE_SEQS * DEC_SEQ
    n_cache = SWA_SAMPLE_SEQS * WINDOW
    q = _swa_rope(q_ref[...], cos_ref[...], sin_ref[...])
    kn = _swa_rope(kn_ref[...], cos_ref[...], sin_ref[...])
    vn = vn_ref[...]
    k_all = jnp.concatenate([kc_ref[...].reshape(n_cache, SWA_NK), kn], axis=0)
    v_all = jnp.concatenate([vc_ref[...].reshape(n_cache, SWA_NK), vn], axis=0)
    for b in range(SWA_SAMPLE_SEQS):
        new = slice(b * DEC_SEQ, (b + 1) * DEC_SEQ)
        ko_ref[b, :WINDOW - DEC_SEQ, :] = kc_ref[b, DEC_SEQ:, :]
        ko_ref[b, WINDOW - DEC_SEQ:, :] = kn[new]
        vo_ref[b, :WINDOW - DEC_SEQ, :] = vc_ref[b, DEC_SEQ:, :]
        vo_ref[b, WINDOW - DEC_SEQ:, :] = vn[new]
    shape = (SWA_GROUP * rows, n_cache + rows)
    r = _mod_pow2(lax.broadcasted_iota(jnp.int32, shape, 0), rows)
    c = lax.broadcasted_iota(jnp.int32, shape, 1)
    q_seq, q_t = _div_pow2(r, DEC_SEQ), _mod_pow2(r, DEC_SEQ)
    is_new = c >= n_cache
    k_seq = jnp.where(is_new, _div_pow2(c - n_cache, DEC_SEQ), _div_pow2(c, WINDOW))
    k_slot = jnp.where(is_new, WINDOW + _mod_pow2(c - n_cache, DEC_SEQ), _mod_pow2(c, WINDOW))
    rel = q_t + WINDOW - k_slot
    valid = (q_seq == k_seq) & (rel >= 0) & (rel <= WINDOW)
    o_ref[...] = _sink_attention(q, k_all, v_all, valid, sink_ref)


def swa_core(proj, sinks, cache_k, cache_v):
    nb = SEQ // SWA_BLOCK
    smem = pl.BlockSpec(memory_space=pltpu.SMEM)
    cos_p, sin_p = _swa_rope_tables(jnp.arange(SEQ, dtype=jnp.int32))
    kcol = SWA_NQ // SWA_NK
    cur = lambda b, i: b * nb + i
    prev = lambda b, i: b * nb + jnp.maximum(i - 1, 0)
    o, k_rot = pl.pallas_call(
        _swa_prompt_body,
        grid=(BATCH, nb),
        in_specs=[
            smem,
            pl.BlockSpec((SWA_BLOCK, SWA_NQ), lambda b, i: (cur(b, i), 0)),
            pl.BlockSpec((SWA_BLOCK, SWA_NK), lambda b, i: (cur(b, i), kcol)),
            pl.BlockSpec((SWA_BLOCK, SWA_NK), lambda b, i: (cur(b, i), kcol + 1)),
            pl.BlockSpec((SWA_BLOCK, SWA_NK), lambda b, i: (prev(b, i), kcol)),
            pl.BlockSpec((SWA_BLOCK, SWA_NK), lambda b, i: (prev(b, i), kcol + 1)),
            pl.BlockSpec((SWA_BLOCK, LANES), lambda b, i: (i, 0)),
            pl.BlockSpec((SWA_BLOCK, LANES), lambda b, i: (i, 0)),
            pl.BlockSpec((SWA_BLOCK, LANES), lambda b, i: (jnp.maximum(i - 1, 0), 0)),
            pl.BlockSpec((SWA_BLOCK, LANES), lambda b, i: (jnp.maximum(i - 1, 0), 0)),
        ],
        out_specs=[
            pl.BlockSpec((SWA_BLOCK, SWA_NQ), lambda b, i: (cur(b, i), 0)),
            pl.BlockSpec((SWA_BLOCK, SWA_NK), lambda b, i: (cur(b, i), 0)),
        ],
        out_shape=[jax.ShapeDtypeStruct((NT, SWA_NQ), F32),
                   jax.ShapeDtypeStruct((NP, SWA_NK), F32)],
        compiler_params=_cparams("parallel", "arbitrary"),
        name="swa_prompt",
    )(sinks, proj, proj, proj, proj, proj, cos_p, sin_p, cos_p, sin_p)

    cos_s, sin_s = (jnp.tile(t, (SWA_SAMPLE_SEQS, 1))
                    for t in _swa_rope_tables(PAST_LEN + jnp.arange(DEC_SEQ, dtype=jnp.int32)))
    rows = SWA_SAMPLE_SEQS * DEC_SEQ
    row0 = NP // rows
    o, k_s, v_s = pl.pallas_call(
        _swa_sample_body,
        grid=(DEC_BATCH // SWA_SAMPLE_SEQS,),
        in_specs=[
            smem,
            pl.BlockSpec((rows, SWA_NQ), lambda g: (row0 + g, 0)),
            pl.BlockSpec((rows, SWA_NK), lambda g: (row0 + g, kcol)),
            pl.BlockSpec((rows, SWA_NK), lambda g: (row0 + g, kcol + 1)),
            pl.BlockSpec((SWA_SAMPLE_SEQS, WINDOW, SWA_NK), lambda g: (g, 0, 0)),
            pl.BlockSpec((SWA_SAMPLE_SEQS, WINDOW, SWA_NK), lambda g: (g, 0, 0)),
            pl.BlockSpec((rows, LANES), lambda g: (0, 0)),
            pl.BlockSpec((rows, LANES), lambda g: (0, 0)),
            pl.BlockSpec(memory_space=pl.ANY),
        ],
        out_specs=[
            pl.BlockSpec((rows, SWA_NQ), lambda g: (row0 + g, 0)),
            pl.BlockSpec((SWA_SAMPLE_SEQS, WINDOW, SWA_NK), lambda g: (g, 0, 0)),
            pl.BlockSpec((SWA_SAMPLE_SEQS, WINDOW, SWA_NK), lambda g: (g, 0, 0)),
        ],
        out_shape=[jax.ShapeDtypeStruct((NT, SWA_NQ), F32),
                   jax.ShapeDtypeStruct((DEC_BATCH, WINDOW, SWA_NK), F32),
                   jax.ShapeDtypeStruct((DEC_BATCH, WINDOW, SWA_NK), F32)],
        input_output_aliases={8: 0},
        compiler_params=_cparams("parallel"),
        name="swa_sample",
    )(sinks, proj, proj, proj, cache_k, cache_v, cos_s, sin_s, o)
    return o, k_rot, k_s, v_s


RWKV_ROW_TILE = 256
RWKV_PAIRS = RWKV_HEADS // 2
RWKV_STACK = MXU_DIM_V7X
RWKV_SAMPLE_SEQS = RWKV_STACK // (RWKV_PAIRS * DEC_SEQ)


def _block_diag_ones():
    r = jnp.arange(MXU_DIM_V7X) // RWKV_HD
    return (r[:, None] == r[None, :]).astype(BF16)


def _store_pairs(ref, val):
    for p in range(RWKV_PAIRS):
        ref[p] = val[:, p * LANES:(p + 1) * LANES]


def _load_pairs(ref):
    return jnp.concatenate([ref[p] for p in range(RWKV_PAIRS)], axis=-1)


def _rwkv_proj_body(x_ref, xprev_ref, first_ref, gn_ref, mu_ref, wr_ref, wk_ref, wv_ref, w0_ref, w1_ref, w2_ref,
                    a0_ref, a1_ref, a2_ref, g1_ref, g2_ref, kk_ref, ka_ref, bd_ref,
                    r_out, lw_out, k_out, v_out, kk_out, b_out, g_out, hlast_out, h_scr):
    i = pl.program_id(0)
    h = _rms(x_ref[...], gn_ref[...])
    prev_tile_last = _rms(xprev_ref[...], gn_ref[...])[SUBLANES - 1:SUBLANES, :]
    rowid = lax.broadcasted_iota(jnp.int32, h.shape, 0)
    xp = jnp.where(rowid == 0, prev_tile_last, pltpu.roll(h, 1, 0))
    seq_len_mask = jnp.where(i >= NP // RWKV_ROW_TILE, DEC_SEQ - 1, SEQ - 1)
    is_first = jnp.bitwise_and(i * RWKV_ROW_TILE + rowid, seq_len_mask) == 0
    xp = jnp.where(is_first, first_ref[...], xp)
    _store_pairs(h_scr, h)
    every8th = pl.ds(SUBLANES - 1, RWKV_ROW_TILE // SUBLANES, stride=SUBLANES)
    hlast_out[...] = jnp.concatenate([h_scr[p, every8th, :] for p in range(RWKV_PAIRS)], axis=-1)
    d = xp - h
    xs = [h + d * mu_ref[i:i + 1, :] for i in range(6)]
    _store_pairs(r_out, _dot(xs[0], wr_ref[...]))
    k = _dot(xs[1], wk_ref[...])
    _store_pairs(v_out, _dot(xs[2], wv_ref[...]))
    z = w0_ref[...] + _dot(jnp.tanh(_dot(xs[3], w1_ref[...])), w2_ref[...])
    softplus = jnp.maximum(-z, 0.0) + jnp.log1p(jnp.exp(-jnp.abs(z)))
    _store_pairs(lw_out, -jnp.exp(-softplus - 0.5))
    a = jax.nn.sigmoid(a0_ref[...] + _dot(_dot(xs[4], a1_ref[...]), a2_ref[...]))
    g_out[...] = _dot(jax.nn.sigmoid(_dot(xs[5], g1_ref[...])), g2_ref[...])
    kk = k * kk_ref[...]
    norm = jnp.sqrt(_seg_sum(kk * kk, bd_ref[...], 2))
    kk = kk / jnp.maximum(norm, 1e-12)
    _store_pairs(kk_out, kk)
    _store_pairs(b_out, kk * a)
    _store_pairs(k_out, k * (1.0 + (a - 1.0) * ka_ref[...]))


def rwkv_proj(x, shift_s, p):
    first = jnp.concatenate([
        jnp.zeros((RWKV_ROW_TILE, D_MODEL), F32),
        jnp.pad(shift_s[:, None, :], ((0, 0), (0, DEC_SEQ - 1), (0, 0))).reshape(NS, D_MODEL)], axis=0)
    prompt_tiles = NP // RWKV_ROW_TILE
    row = pl.BlockSpec((RWKV_ROW_TILE, D_MODEL), lambda i: (i, 0))
    prev = pl.BlockSpec((SUBLANES, D_MODEL), lambda i: (jnp.maximum(i * (RWKV_ROW_TILE // SUBLANES) - 1, 0), 0))
    first_spec = pl.BlockSpec((RWKV_ROW_TILE, D_MODEL), lambda i: (jnp.maximum(i - prompt_tiles + 1, 0), 0))
    pair = pl.BlockSpec((RWKV_PAIRS, RWKV_ROW_TILE, LANES), lambda i: (0, i, 0))
    full = lambda a: pl.BlockSpec(a.shape, lambda i: (0,) * a.ndim)
    consts = [p["g_mix"], p["mu"], p["wr"], p["wk"], p["wv"], p["w0"], p["w1"], p["w2"], p["a0"], p["a1"],
              p["a2"], p["g1"], p["g2"], p["k_k"], p["k_a"], p["bd"]]
    pair_shape = jax.ShapeDtypeStruct((RWKV_PAIRS, NT, LANES), F32)
    return pl.pallas_call(
        _rwkv_proj_body,
        grid=(NT // RWKV_ROW_TILE,),
        in_specs=[row, prev, first_spec] + [full(c) for c in consts],
        out_specs=[pair] * 6 + [row, pl.BlockSpec((RWKV_ROW_TILE // SUBLANES, D_MODEL), lambda i: (i, 0))],
        out_shape=[pair_shape] * 6 + [jax.ShapeDtypeStruct((NT, D_MODEL), F32),
                                      jax.ShapeDtypeStruct((NT // SUBLANES, D_MODEL), F32)],
        scratch_shapes=[pltpu.VMEM((RWKV_PAIRS, RWKV_ROW_TILE, LANES), F32)],
        compiler_params=_cparams("parallel"),
        name="rwkv_proj",
    )(x, x, first, *consts)


def _rwkv_chunk(c_len, r_ref, lw_ref, k_ref, v_ref, kk_ref, b_ref, state, store, y_ref):
    n_grp = RWKV_STACK // c_len
    shift = int(math.log2(c_len))
    assert 2 ** shift == c_len
    flat = lambda ref: ref[...].reshape(RWKV_STACK, LANES)
    lw = flat(lw_ref)
    row = lax.broadcasted_iota(jnp.int32, (RWKV_STACK, RWKV_STACK), 0)
    col = lax.broadcasted_iota(jnp.int32, (RWKV_STACK, RWKV_STACK), 1)
    same = jnp.right_shift(row, shift) == jnp.right_shift(col, shift)
    strict = same & (row > col)
    incl = same & (row >= col)
    tri = jnp.where(incl, 1.0, 0.0).astype(BF16)
    lam = sum(jnp.dot(tri, part, preferred_element_type=F32) for part in _split3(lw))
    lam3 = lam.reshape(n_grp, c_len, LANES)
    lam_end3 = lam3[:, c_len - 1:c_len, :]
    e_end = jnp.exp(lam_end3 - lam3).reshape(RWKV_STACK, LANES)
    e_tot3 = jnp.exp(lam_end3)
    e_neg = jnp.exp(-lam)
    b = flat(b_ref)
    k = flat(k_ref)
    v = flat(v_ref)
    at = -flat(kk_ref) * jnp.exp(lam - lw)
    rt = flat(r_ref) * jnp.exp(lam)
    bh = b * e_neg
    kh = k * e_neg
    bb = b * e_end
    kb = k * e_end

    lane = lax.broadcasted_iota(jnp.int32, (RWKV_STACK, LANES), 1)
    halves = (lane < RWKV_HD, lane >= RWKV_HD)
    rhs = jnp.concatenate([bh, kh], axis=0)
    a_pow, b_low, r_low = [], [], []
    for sel in halves:
        lhs = jnp.concatenate([jnp.where(sel, at, 0.0), jnp.where(sel, rt, 0.0)], axis=0)
        m = _dot_nt(lhs, rhs)
        a_pow.append(jnp.where(strict, m[:RWKV_STACK, :RWKV_STACK], 0.0))
        b_low.append(jnp.where(strict, m[:RWKV_STACK, RWKV_STACK:], 0.0))
        r_low.append(jnp.concatenate([jnp.where(incl, m[RWKV_STACK:, :RWKV_STACK], 0.0),
                                      jnp.where(incl, m[RWKV_STACK:, RWKV_STACK:], 0.0)], axis=1))

    rows = lambda t, g: t[g * c_len:(g + 1) * c_len]
    states = [state(g) for g in range(n_grp)]
    a_s, r_s = [], []
    for g in range(n_grp):
        p = _dot_nt(jnp.concatenate([rows(at, g), rows(rt, g)], axis=0), states[g])
        a_s.append(p[:c_len])
        r_s.append(p[c_len:])
    u = jnp.concatenate(a_s, axis=0)
    for e, sel in enumerate(halves):
        u = u + jnp.where(sel, _dot(b_low[e], v), 0.0)
    for it in range(shift):
        u = u + sum(jnp.where(sel, _dot(a_pow[e], u), 0.0) for e, sel in enumerate(halves))
        if it + 1 < shift:
            a_pow = [_dot(a, a) for a in a_pow]
    uv = jnp.concatenate([u, v], axis=0)
    y = jnp.concatenate(r_s, axis=0)
    for e, sel in enumerate(halves):
        y = y + jnp.where(sel, _dot(r_low[e], uv), 0.0)
    y_ref[...] = y.reshape(y_ref.shape)

    r128 = lax.broadcasted_iota(jnp.int32, (LANES, LANES), 0)
    c128 = lax.broadcasted_iota(jnp.int32, (LANES, LANES), 1)
    diag = (r128 < RWKV_HD) == (c128 < RWKV_HD)
    for g in range(n_grp):
        upd = _dot_tn(jnp.concatenate([rows(u, g), rows(v, g)], axis=0),
                      jnp.concatenate([rows(bb, g), rows(kb, g)], axis=0))
        store(g, states[g] * e_tot3[g] + jnp.where(diag, upd, 0.0))


def _rwkv_prompt_body(r_ref, lw_ref, k_ref, v_ref, kk_ref, b_ref, y_ref, s_ref, s_scr):
    c = pl.program_id(1)

    @pl.when(c == 0)
    def _():
        s_scr[...] = jnp.zeros_like(s_scr)

    def store(g, s):
        s_scr[g] = s

    _rwkv_chunk(RWKV_CHUNK, r_ref, lw_ref, k_ref, v_ref, kk_ref, b_ref, lambda g: s_scr[g], store, y_ref)

    @pl.when(c == pl.num_programs(1) - 1)
    def _():
        for p in range(RWKV_PAIRS):
            s_ref[0, 2 * p] = s_scr[p, :RWKV_HD, :RWKV_HD]
            s_ref[0, 2 * p + 1] = s_scr[p, RWKV_HD:, RWKV_HD:]


def _rwkv_sample_body(r_ref, lw_ref, k_ref, v_ref, kk_ref, b_ref, s0_ref, y_in_ref, y_ref, s_ref):
    del y_in_ref
    zero = jnp.zeros((RWKV_HD, RWKV_HD), F32)

    def state(g):
        p, seq = divmod(g, RWKV_SAMPLE_SEQS)
        return jnp.concatenate([jnp.concatenate([s0_ref[seq, 2 * p], zero], axis=1),
                                jnp.concatenate([zero, s0_ref[seq, 2 * p + 1]], axis=1)], axis=0)

    def store(g, s):
        p, seq = divmod(g, RWKV_SAMPLE_SEQS)
        s_ref[seq, 2 * p] = s[:RWKV_HD, :RWKV_HD]
        s_ref[seq, 2 * p + 1] = s[RWKV_HD:, RWKV_HD:]

    _rwkv_chunk(DEC_SEQ, r_ref, lw_ref, k_ref, v_ref, kk_ref, b_ref, state, store, y_ref)


def rwkv_core(r, lw, k, v, kk, b, state_s):
    assert RWKV_PAIRS * RWKV_CHUNK == RWKV_STACK
    nc = SEQ // RWKV_CHUNK
    blk = pl.BlockSpec((RWKV_PAIRS, RWKV_CHUNK, LANES), lambda bi, c: (0, bi * nc + c, 0))
    st_shape = (RWKV_HEADS, RWKV_HD, RWKV_HD)
    pair_shape = jax.ShapeDtypeStruct((RWKV_PAIRS, NT, LANES), F32)
    y, s_p = pl.pallas_call(
        _rwkv_prompt_body,
        grid=(BATCH, nc),
        in_specs=[blk] * 6,
        out_specs=[blk, pl.BlockSpec((1,) + st_shape, lambda bi, c: (bi, 0, 0, 0))],
        out_shape=[pair_shape, jax.ShapeDtypeStruct((BATCH,) + st_shape, F32)],
        scratch_shapes=[pltpu.VMEM((RWKV_PAIRS, LANES, LANES), F32)],
        compiler_params=_cparams("parallel", "arbitrary"),
        name="rwkv_prompt",
    )(r, lw, k, v, kk, b)

    rows = RWKV_SAMPLE_SEQS * DEC_SEQ
    row0 = NP // rows
    blk = pl.BlockSpec((RWKV_PAIRS, rows, LANES), lambda i: (0, row0 + i, 0))
    st = pl.BlockSpec((RWKV_SAMPLE_SEQS,) + st_shape, lambda i: (i, 0, 0, 0))
    y, s_s = pl.pallas_call(
        _rwkv_sample_body,
        grid=(DEC_BATCH // RWKV_SAMPLE_SEQS,),
        in_specs=[blk] * 6 + [st, pl.BlockSpec(memory_space=pl.ANY)],
        out_specs=[blk, st],
        out_shape=[pair_shape, jax.ShapeDtypeStruct((DEC_BATCH,) + st_shape, F32)],
        input_output_aliases={7: 0},
        compiler_params=_cparams("parallel"),
        name="rwkv_sample",
    )(r, lw, k, v, kk, b, state_s, y)
    return y, s_p, s_s


def _rwkv_out_body(x_ref, y_ref, r_ref, k_ref, v_ref, g_ref, rk_ref, lnw_ref, lnb_ref, bd_ref, wo_ref, o_ref):
    bd = bd_ref[...]
    y = _load_pairs(y_ref)
    yc = y - _seg_sum(y, bd, 2) * (1.0 / RWKV_HD)
    var = _seg_sum(yc * yc, bd, 1) * (1.0 / RWKV_HD)
    yn = yc * lax.rsqrt(var + RWKV_GN_EPS) * lnw_ref[...] + lnb_ref[...]
    bonus = _seg_sum(_load_pairs(r_ref) * _load_pairs(k_ref) * rk_ref[...], bd, 2) * _load_pairs(v_ref)
    z = (yn + bonus) * g_ref[...]
    o_ref[...] = x_ref[...] + _dot(z, wo_ref[...])


def rwkv_out(x, y, r, k, v, g, p):
    row = pl.BlockSpec((RWKV_ROW_TILE, D_MODEL), lambda i: (i, 0))
    pair = pl.BlockSpec((RWKV_PAIRS, RWKV_ROW_TILE, LANES), lambda i: (0, i, 0))
    full = lambda a: pl.BlockSpec(a.shape, lambda i: (0,) * a.ndim)
    consts = [p["r_k"], p["ln_w"], p["ln_b"], p["bd"], p["w_out"]]
    return pl.pallas_call(
        _rwkv_out_body,
        grid=(NT // RWKV_ROW_TILE,),
        in_specs=[row] + [pair] * 4 + [row] + [full(c) for c in consts],
        out_specs=row,
        out_shape=jax.ShapeDtypeStruct((NT, D_MODEL), F32),
        compiler_params=_cparams("parallel"),
        name="rwkv_out",
    )(x, y, r, k, v, g, *consts)


def _pad_cols(w, n):
    return jnp.pad(w, ((0, 0), (0, n - w.shape[1])))


def _pad_rows(w, n):
    return jnp.pad(w, ((0, n - w.shape[0]), (0, 0)))


def kernel(x_prompt, x_sample, state_ret, cache_swa_k, cache_swa_v, state_rwkv_wkv, state_rwkv_shift,
           norm_mix, norm_mlp, norm_final, ret_w_in, ret_w_out,
           swa_w_in, swa_b_in, swa_sinks, swa_w_out,
           rwkv_mu, rwkv_w_rkv, rwkv_w0, rwkv_w1, rwkv_w2, rwkv_a0, rwkv_a1, rwkv_a2,
           rwkv_g1, rwkv_g2, rwkv_k_k, rwkv_k_a, rwkv_r_k, rwkv_ln_w, rwkv_ln_b, rwkv_w_out,
           mlp_w_up, mlp_w_down):
    x = jnp.concatenate([x_prompt.reshape(NP, D_MODEL), x_sample.reshape(NS, D_MODEL)], axis=0)
    row = lambda t: t.reshape(1, -1)
    ret_p, k_p, k_s, v_p, v_s, wkv_p, wkv_s, sh_p, sh_s = ([] for _ in range(9))
    ret_s = None
    for layer in range(DEPTH):
        kind = LAYER_KINDS[layer]
        j = LAYER_KINDS[:layer].count(kind)
        g_mix = row(norm_mix[layer])
        if kind == 0:
            n_in = ret_w_in.shape[2]
            proj = norm_matmul(x, g_mix, ret_w_in[j].astype(BF16), jnp.zeros((1, n_in), F32), 2048)
            o, s_p, ret_s = retention_core(proj, state_ret, j, ret_s)
            x = matmul_residual(x, o, ret_w_out[j].astype(BF16))
            ret_p.append(s_p)
        elif kind == 1:
            proj = norm_matmul(x, g_mix, swa_w_in[j].astype(BF16), row(swa_b_in[j]), SWA_NQ + 2 * SWA_NK)
            o, k_rot, kc, vc = swa_core(proj, swa_sinks[j],
                                        cache_swa_k[j].reshape(DEC_BATCH, WINDOW, SWA_NK),
                                        cache_swa_v[j].reshape(DEC_BATCH, WINDOW, SWA_NK))
            x = matmul_residual(x, o, swa_w_out[j].astype(BF16))
            kv_shape = (-1, WINDOW, SWA_KV_HEADS, SWA_HD)
            k_p.append(k_rot.reshape(BATCH, SEQ, SWA_NK)[:, -WINDOW:].reshape(kv_shape))
            v_p.append(proj[:NP, SWA_NQ + SWA_NK:].reshape(BATCH, SEQ, SWA_NK)[:, -WINDOW:].reshape(kv_shape))
            k_s.append(kc.reshape(kv_shape))
            v_s.append(vc.reshape(kv_shape))
        else:
            lora_w = LANES
            lora_g = 2 * LANES
            p = dict(
                mu=rwkv_mu[j], wr=rwkv_w_rkv[j, 0].astype(BF16), wk=rwkv_w_rkv[j, 1].astype(BF16),
                wv=rwkv_w_rkv[j, 2].astype(BF16), w0=row(rwkv_w0[j]),
                w1=_pad_cols(rwkv_w1[j], lora_w).astype(BF16), w2=_pad_rows(rwkv_w2[j], lora_w).astype(BF16),
                a0=row(rwkv_a0[j]),
                a1=_pad_cols(rwkv_a1[j], lora_w).astype(BF16), a2=_pad_rows(rwkv_a2[j], lora_w).astype(BF16),
                g1=_pad_cols(rwkv_g1[j], lora_g).astype(BF16), g2=_pad_rows(rwkv_g2[j], lora_g).astype(BF16),
                k_k=row(rwkv_k_k[j]), k_a=row(rwkv_k_a[j]), r_k=row(rwkv_r_k[j]),
                ln_w=row(rwkv_ln_w[j]), ln_b=row(rwkv_ln_b[j]), bd=_block_diag_ones(),
                w_out=rwkv_w_out[j].astype(BF16), g_mix=g_mix)
            r, lw, k, v, kk, b, g, h_last = rwkv_proj(x, state_rwkv_shift[j], p)
            y, s_p, s_s = rwkv_core(r, lw, k, v, kk, b, state_rwkv_wkv[j])
            x = rwkv_out(x, y, r, k, v, g, p)
            wkv_p.append(s_p)
            wkv_s.append(s_s)
            sh_p.append(h_last[SEQ // SUBLANES - 1:NP // SUBLANES:SEQ // SUBLANES])
            sh_s.append(h_last[NP // SUBLANES:])
        x = mlp(x, row(norm_mlp[layer]), mlp_w_up[layer].astype(BF16), mlp_w_down[layer].astype(BF16))
    y_p = rmsnorm_rows(x, row(norm_final), 0, NP)
    y_s = rmsnorm_rows(x, row(norm_final), NP, NS)
    return (y_p.reshape(BATCH, SEQ, D_MODEL), y_s.reshape(DEC_BATCH, DEC_SEQ, D_MODEL),
            jnp.stack(ret_p), ret_s, jnp.stack(k_p), jnp.stack(k_s), jnp.stack(v_p), jnp.stack(v_s),
            jnp.stack(wkv_p), jnp.stack(wkv_s), jnp.stack(sh_p), jnp.stack(sh_s))
```

```python
import functools
import math

import jax
import jax.numpy as jnp
from jax import lax
from jax.experimental import pallas as pl
from jax.experimental.pallas import tpu as pltpu

F32 = jnp.float32
BF16 = jnp.bfloat16

D_MODEL = 1024
BATCH = 8
SEQ = 2048
DEPTH = 4
DEC_BATCH = 128
DEC_SEQ = 8
PAST_LEN = 8192
LAYER_KINDS = tuple(i % 3 for i in range(DEPTH))
NORM_EPS = 1e-6
ROPE_THETA = 10000.0
NEG_INF = -1e30
D_FF = 4 * D_MODEL

RET_HEADS = 4
RET_DK = D_MODEL // RET_HEADS
RET_DV = 2 * D_MODEL // RET_HEADS
RET_CHUNK = 128
RET_NQ = RET_HEADS * RET_DK
RET_NV = RET_HEADS * RET_DV

SWA_HEADS = 16
SWA_KV_HEADS = 4
SWA_GROUP = SWA_HEADS // SWA_KV_HEADS
SWA_HD = D_MODEL // SWA_HEADS
WINDOW = 128
SWA_BLOCK = 128
SWA_NQ = SWA_HEADS * SWA_HD
SWA_NK = SWA_KV_HEADS * SWA_HD

RWKV_HD = 64
RWKV_HEADS = D_MODEL // RWKV_HD
RWKV_GN_EPS = 64e-5
RWKV_CHUNK = 32

NP = BATCH * SEQ
NS = DEC_BATCH * DEC_SEQ
NT = NP + NS

VMEM_LIMIT_V7X = 48 * 1024 * 1024
LANES = 128
SUBLANES = 8
MXU_DIM_V7X = 256

ROW_TILE = 1024


def _cparams(*sem):
    return pltpu.CompilerParams(dimension_semantics=sem, vmem_limit_bytes=VMEM_LIMIT_V7X)


def _dot(a, b):
    return jnp.dot(a.astype(BF16), b.astype(BF16), preferred_element_type=F32)


def _dot_nt(a, b):
    return lax.dot_general(a.astype(BF16), b.astype(BF16), (((1,), (1,)), ((), ())),
                           preferred_element_type=F32)


def _dot_tn(a, b):
    return lax.dot_general(a.astype(BF16), b.astype(BF16), (((0,), (0,)), ((), ())),
                           preferred_element_type=F32)


def _div_pow2(x, n):
    assert n & (n - 1) == 0
    return jnp.right_shift(x, n.bit_length() - 1)


def _mod_pow2(x, n):
    assert n & (n - 1) == 0
    return jnp.bitwise_and(x, n - 1)


def _rms(x, g):
    ms = jnp.mean(x * x, axis=-1, keepdims=True)
    return x * lax.rsqrt(ms + NORM_EPS) * g


def _split3(x):
    hi = x.astype(BF16)
    r1 = x - hi.astype(F32)
    mid = r1.astype(BF16)
    lo = (r1 - mid.astype(F32)).astype(BF16)
    return hi, mid, lo


def _seg_sum(x, bd, passes):
    parts = _split3(x)[:passes]
    cols = []
    for j in range(D_MODEL // MXU_DIM_V7X):
        sl = slice(j * MXU_DIM_V7X, (j + 1) * MXU_DIM_V7X)
        acc = jnp.dot(parts[0][:, sl], bd, preferred_element_type=F32)
        for p in parts[1:]:
            acc = acc + jnp.dot(p[:, sl], bd, preferred_element_type=F32)
        cols.append(acc)
    return jnp.concatenate(cols, axis=-1)


def _norm_matmul_body(x_ref, g_ref, w_ref, b_ref, o_ref, xn_ref):
    @pl.when(pl.program_id(1) == 0)
    def _():
        xn_ref[...] = _rms(x_ref[...], g_ref[...]).astype(BF16)

    o_ref[...] = (jnp.dot(xn_ref[...], w_ref[...], preferred_element_type=F32) + b_ref[...]).astype(o_ref.dtype)


def norm_matmul(x, g, w, b, tn):
    n_out = w.shape[1]
    return pl.pallas_call(
        _norm_matmul_body,
        grid=(NT // ROW_TILE, n_out // tn),
        in_specs=[
            pl.BlockSpec((ROW_TILE, D_MODEL), lambda i, j: (i, 0)),
            pl.BlockSpec((1, D_MODEL), lambda i, j: (0, 0)),
            pl.BlockSpec((D_MODEL, tn), lambda i, j: (0, j)),
            pl.BlockSpec((1, tn), lambda i, j: (0, j)),
        ],
        out_specs=pl.BlockSpec((ROW_TILE, tn), lambda i, j: (i, j)),
        out_shape=jax.ShapeDtypeStruct((NT, n_out), BF16),
        scratch_shapes=[pltpu.VMEM((ROW_TILE, D_MODEL), BF16)],
        compiler_params=_cparams("parallel", "arbitrary"),
        name="norm_matmul",
    )(x, g, w, b)


def _group_specs(tile, block_of):
    n_prompt = NP // tile
    return (block_of(lambda i: jnp.minimum(i, n_prompt - 1)), block_of(lambda i: jnp.maximum(i - n_prompt, 0)))


def _pick_group(tile, p_val, s_val):
    return jnp.where(pl.program_id(0) < NP // tile, p_val, s_val)


def _matmul_residual_body(x_ref, ap_ref, as_ref, w_ref, o_ref):
    a = _pick_group(ROW_TILE, ap_ref[...], as_ref[...])
    o_ref[...] = x_ref[...] + jnp.dot(a, w_ref[...], preferred_element_type=F32)


def matmul_residual(x, a_p, a_s, w):
    k = a_p.shape[1]
    return pl.pallas_call(
        _matmul_residual_body,
        grid=(NT // ROW_TILE,),
        in_specs=[
            pl.BlockSpec((ROW_TILE, D_MODEL), lambda i: (i, 0)),
            *_group_specs(ROW_TILE, lambda f: pl.BlockSpec((ROW_TILE, k), lambda i: (f(i), 0))),
            pl.BlockSpec((k, D_MODEL), lambda i: (0, 0)),
        ],
        out_specs=pl.BlockSpec((ROW_TILE, D_MODEL), lambda i: (i, 0)),
        out_shape=jax.ShapeDtypeStruct((NT, D_MODEL), F32),
        compiler_params=_cparams("parallel"),
        name="matmul_residual",
    )(x, a_p, a_s, w)


MLP_FF_TILE = 1024


def _mlp_body(x_ref, g_ref, wu_ref, wd_ref, o_ref, xn_ref, acc_ref):
    f = pl.program_id(1)

    @pl.when(f == 0)
    def _():
        xn_ref[...] = _rms(x_ref[...], g_ref[...]).astype(BF16)
        acc_ref[...] = jnp.zeros_like(acc_ref)

    a = jnp.maximum(jnp.dot(xn_ref[...], wu_ref[...], preferred_element_type=F32), 0.0)
    acc_ref[...] += jnp.dot((a * a).astype(BF16), wd_ref[...], preferred_element_type=F32)

    @pl.when(f == pl.num_programs(1) - 1)
    def _():
        o_ref[...] = x_ref[...] + acc_ref[...]


def mlp(x, g, w_up, w_down):
    return pl.pallas_call(
        _mlp_body,
        grid=(NT // ROW_TILE, D_FF // MLP_FF_TILE),
        in_specs=[
            pl.BlockSpec((ROW_TILE, D_MODEL), lambda i, f: (i, 0)),
            pl.BlockSpec((1, D_MODEL), lambda i, f: (0, 0)),
            pl.BlockSpec((D_MODEL, MLP_FF_TILE), lambda i, f: (0, f)),
            pl.BlockSpec((MLP_FF_TILE, D_MODEL), lambda i, f: (f, 0)),
        ],
        out_specs=pl.BlockSpec((ROW_TILE, D_MODEL), lambda i, f: (i, 0)),
        out_shape=jax.ShapeDtypeStruct((NT, D_MODEL), F32),
        scratch_shapes=[pltpu.VMEM((ROW_TILE, D_MODEL), BF16), pltpu.VMEM((ROW_TILE, D_MODEL), F32)],
        compiler_params=_cparams("parallel", "arbitrary"),
        name="mlp",
    )(x, g, w_up, w_down)


def _rmsnorm_body(x_ref, g_ref, o_ref):
    o_ref[...] = _rms(x_ref[...], g_ref[...])


def rmsnorm_rows(x, g, row0, n_rows):
    tile0 = row0 // ROW_TILE
    return pl.pallas_call(
        _rmsnorm_body,
        grid=(n_rows // ROW_TILE,),
        in_specs=[pl.BlockSpec((ROW_TILE, D_MODEL), lambda i: (tile0 + i, 0)),
                  pl.BlockSpec((1, D_MODEL), lambda i: (0, 0))],
        out_specs=pl.BlockSpec((ROW_TILE, D_MODEL), lambda i: (i, 0)),
        out_shape=jax.ShapeDtypeStruct((n_rows, D_MODEL), F32),
        compiler_params=_cparams("parallel"),
        name="rmsnorm",
    )(x, g)


def _ret_tables(c):
    log_gamma = jnp.log1p(-jnp.exp2(-5.0 - jnp.arange(RET_HEADS, dtype=F32)))
    idx = jnp.arange(c, dtype=F32)
    diff = idx[:, None] - idx[None, :]
    inner = jnp.where(diff >= 0, jnp.exp(log_gamma[:, None, None] * jnp.maximum(diff, 0.0)), 0.0)
    q_decay = jnp.exp(log_gamma[:, None] * (idx[None, :] + 1.0))
    k_decay = jnp.exp(log_gamma[:, None] * (c - 1.0 - idx[None, :]))
    chunk_decay = jnp.exp(log_gamma * c)
    rep = lambda t: jnp.broadcast_to(t[:, :, None], (RET_HEADS, c, LANES))
    return inner, rep(q_decay), rep(k_decay), chunk_decay


def _rope_tables(pos, half):
    inv_freq = jnp.power(ROPE_THETA, -jnp.arange(half, dtype=F32) / half)
    ang = pos.astype(F32)[:, None] * inv_freq[None, :]
    return jnp.cos(ang), jnp.sin(ang)


def _ret_chunk(cols, cos, sin, inner_ref, qd_ref, kd_ref, cd_ref, state, store, put):
    for h in range(RET_HEADS):
        q = cols(h * RET_DK, (h + 1) * RET_DK).astype(F32)
        k = cols(RET_NQ + h * RET_DK, RET_NQ + (h + 1) * RET_DK).astype(F32)
        v = cols(2 * RET_NQ + h * RET_DV, 2 * RET_NQ + (h + 1) * RET_DV)
        g = cols(2 * RET_NQ + RET_NV + h * RET_DV, 2 * RET_NQ + RET_NV + (h + 1) * RET_DV).astype(F32)
        half = RET_DK // 2
        q1, q2 = q[:, :half], q[:, half:]
        k1, k2 = k[:, :half], k[:, half:]
        q = jnp.concatenate([q1 * cos - q2 * sin, q2 * cos + q1 * sin], axis=-1)
        k = jnp.concatenate([k1 * cos - k2 * sin, k2 * cos + k1 * sin], axis=-1) * (RET_DK ** -0.5)
        s0 = state(h)
        scores = _dot_nt(q, k) * inner_ref[h]
        qd = jnp.concatenate([qd_ref[h]] * (RET_DV // LANES), axis=-1)
        kd = jnp.concatenate([kd_ref[h]] * (RET_DK // LANES), axis=-1)
        o = _dot(scores, v) + _dot(q, s0) * qd
        store(h, s0 * cd_ref[h] + _dot_tn(k * kd, v))
        o = o * lax.rsqrt(jnp.mean(o * o, axis=-1, keepdims=True) + NORM_EPS)
        put(h, (o * (g * jax.nn.sigmoid(g))).astype(BF16))


def _ret_prompt_body(cd_ref, proj_ref, cos_ref, sin_ref, inner_ref, qd_ref, kd_ref, o_ref, s_ref, s_scr):
    c = pl.program_id(1)

    @pl.when(c == 0)
    def _():
        s_scr[...] = jnp.zeros_like(s_scr)

    def store(h, s):
        s_scr[h] = s

    def put(h, o):
        o_ref[:, h * RET_DV:(h + 1) * RET_DV] = o

    _ret_chunk(lambda lo, hi: proj_ref[:, lo:hi], cos_ref[...], sin_ref[...], inner_ref, qd_ref, kd_ref, cd_ref,
               lambda h: s_scr[h], store, put)

    @pl.when(c == pl.num_programs(1) - 1)
    def _():
        s_ref[0] = s_scr[...]


RET_SAMPLE_SEQS = 2


def _ret_sample_body(cd_ref, proj_ref, cos_ref, sin_ref, inner_ref, qd_ref, kd_ref, s0_ref, *rest):
    o_ref, s_ref = rest[-2:]
    outs = []
    for seq in range(RET_SAMPLE_SEQS):
        rows = slice(seq * DEC_SEQ, (seq + 1) * DEC_SEQ)

        def store(h, s):
            s_ref[0, seq, h] = s

        pieces = [None] * RET_HEADS
        _ret_chunk(lambda lo, hi: proj_ref[:, lo:hi].astype(F32)[rows], cos_ref[...], sin_ref[...],
                   inner_ref, qd_ref, kd_ref, cd_ref, lambda h: s0_ref[0, seq, h], store,
                   lambda h, o: pieces.__setitem__(h, o.astype(F32)))
        outs.append(jnp.concatenate(pieces, axis=-1))
    o_ref[...] = jnp.concatenate(outs, axis=0).astype(BF16)


def retention_core(proj, state_ret, j, s_stack):
    n_in = proj.shape[1]
    nc = SEQ // RET_CHUNK
    smem = pl.BlockSpec(memory_space=pltpu.SMEM)
    cos_p, sin_p = _rope_tables(jnp.arange(SEQ, dtype=jnp.int32), RET_DK // 2)
    inner, qd, kd, cd = _ret_tables(RET_CHUNK)
    full3 = lambda shape: pl.BlockSpec(shape, lambda b, c: (0, 0, 0))
    o, s_p = pl.pallas_call(
        _ret_prompt_body,
        grid=(BATCH, nc),
        in_specs=[
            smem,
            pl.BlockSpec((RET_CHUNK, n_in), lambda b, c: (b * nc + c, 0)),
            pl.BlockSpec((RET_CHUNK, LANES), lambda b, c: (c, 0)),
            pl.BlockSpec((RET_CHUNK, LANES), lambda b, c: (c, 0)),
            full3(inner.shape), full3(qd.shape), full3(kd.shape),
        ],
        out_specs=[
            pl.BlockSpec((RET_CHUNK, RET_NV), lambda b, c: (b * nc + c, 0)),
            pl.BlockSpec((1, RET_HEADS, RET_DK, RET_DV), lambda b, c: (b, 0, 0, 0)),
        ],
        out_shape=[jax.ShapeDtypeStruct((NP, RET_NV), BF16),
                   jax.ShapeDtypeStruct((BATCH, RET_HEADS, RET_DK, RET_DV), F32)],
        scratch_shapes=[pltpu.VMEM((RET_HEADS, RET_DK, RET_DV), F32)],
        compiler_params=_cparams("parallel", "arbitrary"),
        name="retention_prompt",
    )(cd, proj, cos_p, sin_p, inner, qd, kd)

    cs = math.gcd(DEC_SEQ, RET_CHUNK)
    assert cs == DEC_SEQ
    cos_s, sin_s = _rope_tables(PAST_LEN + jnp.arange(DEC_SEQ, dtype=jnp.int32), RET_DK // 2)
    inner, qd, kd, cd = _ret_tables(cs)
    rows = RET_SAMPLE_SEQS * DEC_SEQ
    row0 = NP // rows
    full2 = lambda shape: pl.BlockSpec(shape, lambda b: (0, 0))
    full3 = lambda shape: pl.BlockSpec(shape, lambda b: (0, 0, 0))
    st = pl.BlockSpec((1, RET_SAMPLE_SEQS, RET_HEADS, RET_DK, RET_DV), lambda b: (j, b, 0, 0, 0))
    stacked = [] if s_stack is None else [s_stack]
    o_s, s_s = pl.pallas_call(
        _ret_sample_body,
        grid=(DEC_BATCH // RET_SAMPLE_SEQS,),
        in_specs=[
            smem,
            pl.BlockSpec((rows, n_in), lambda b: (row0 + b, 0)),
            full2(cos_s.shape), full2(sin_s.shape),
            full3(inner.shape), full3(qd.shape), full3(kd.shape),
            st,
        ] + [pl.BlockSpec(memory_space=pl.ANY)] * len(stacked),
        out_specs=[pl.BlockSpec((rows, RET_NV), lambda b: (b, 0)), st],
        out_shape=[jax.ShapeDtypeStruct((NS, RET_NV), BF16),
                   jax.ShapeDtypeStruct(state_ret.shape, F32)],
        input_output_aliases={8: 1} if stacked else {},
        compiler_params=_cparams("parallel"),
        name="retention_sample",
    )(cd, proj, cos_s, sin_s, inner, qd, kd, state_ret, *stacked)
    return o, o_s, s_p, s_s


def _swa_rope_tables(pos):
    cos, sin = _rope_tables(pos, SWA_HD // 2)
    cos2 = jnp.concatenate([cos, cos], axis=-1)
    sin2 = jnp.concatenate([-sin, sin], axis=-1)
    rep = LANES // SWA_HD
    return jnp.tile(cos2, (1, rep)), jnp.tile(sin2, (1, rep))


def _swa_rope(x, cos, sin):
    n = x.shape[1]
    half = SWA_HD // 2
    lane = lax.broadcasted_iota(jnp.int32, x.shape, 1)
    partner = jnp.where(lane % SWA_HD < half, pltpu.roll(x, n - half, 1), pltpu.roll(x, half, 1))
    reps = n // LANES
    return x * jnp.concatenate([cos] * reps, axis=-1) + partner * jnp.concatenate([sin] * reps, axis=-1)


def _sink_attention(q, k, v, valid, sink_ref):
    t = q.shape[0]
    outs = [None] * SWA_HEADS
    for kh in range(SWA_KV_HEADS):
        heads = range(kh * SWA_GROUP, (kh + 1) * SWA_GROUP)
        qs = jnp.concatenate([q[:, h * SWA_HD:(h + 1) * SWA_HD] for h in heads], axis=0)
        sink = jnp.concatenate([jnp.full((t, 1), sink_ref[h], F32) for h in heads], axis=0)
        s = _dot_nt(qs, k[:, kh * SWA_HD:(kh + 1) * SWA_HD]) * (SWA_HD ** -0.5)
        s = jnp.where(valid, s, NEG_INF)
        m = jnp.maximum(jnp.max(s, axis=-1, keepdims=True), sink)
        p = jnp.exp(s - m)
        denom = jnp.sum(p, axis=-1, keepdims=True) + jnp.exp(sink - m)
        o = _dot(p, v[:, kh * SWA_HD:(kh + 1) * SWA_HD]) / denom
        for g, h in enumerate(heads):
            outs[h] = o[g * t:(g + 1) * t]
    return jnp.concatenate(outs, axis=-1)


def _swa_prompt_body(sink_ref, q_ref, kc_ref, vc_ref, kp_ref, vp_ref, cosc_ref, sinc_ref, cosp_ref, sinp_ref,
                     o_ref, kr_ref):
    blk = pl.program_id(1)
    q = _swa_rope(q_ref[...].astype(F32), cosc_ref[...], sinc_ref[...])
    kc = _swa_rope(kc_ref[...].astype(F32), cosc_ref[...], sinc_ref[...])
    kp = _swa_rope(kp_ref[...].astype(F32), cosp_ref[...], sinp_ref[...])
    kr_ref[...] = kc
    k_all = jnp.concatenate([kp, kc], axis=0)
    v_all = jnp.concatenate([vp_ref[...], vc_ref[...]], axis=0)
    shape = (SWA_GROUP * SWA_BLOCK, 2 * SWA_BLOCK)
    i = _mod_pow2(lax.broadcasted_iota(jnp.int32, shape, 0), SWA_BLOCK)
    j = lax.broadcasted_iota(jnp.int32, shape, 1)
    rel = i + SWA_BLOCK - j
    valid = (rel >= 0) & (rel <= WINDOW) & (j >= jnp.where(blk > 0, 0, SWA_BLOCK))
    o_ref[...] = _sink_attention(q, k_all, v_all, valid, sink_ref).astype(BF16)


SWA_SAMPLE_SEQS = 8


def _swa_sample_body(sink_ref, q_ref, kn_ref, vn_ref, kc_ref, vc_ref, cos_ref, sin_ref,
                     o_ref, ko_ref, vo_ref):
    rows = SWA_SAMPLE_SEQS * DEC_SEQ
    n_cache = SWA_SAMPLE_SEQS * WINDOW
    q = _swa_rope(q_ref[...].astype(F32), cos_ref[...], sin_ref[...])
    kn = _swa_rope(kn_ref[...].astype(F32), cos_ref[...], sin_ref[...])
    vn = vn_ref[...].astype(F32)
    k_all = jnp.concatenate([kc_ref[...].reshape(n_cache, SWA_NK), kn], axis=0)
    v_all = jnp.concatenate([vc_ref[...].reshape(n_cache, SWA_NK), vn], axis=0)
    for b in range(SWA_SAMPLE_SEQS):
        new = slice(b * DEC_SEQ, (b + 1) * DEC_SEQ)
        ko_ref[b, :WINDOW - DEC_SEQ, :] = kc_ref[b, DEC_SEQ:, :]
        ko_ref[b, WINDOW - DEC_SEQ:, :] = kn[new]
        vo_ref[b, :WINDOW - DEC_SEQ, :] = vc_ref[b, DEC_SEQ:, :]
        vo_ref[b, WINDOW - DEC_SEQ:, :] = vn[new]
    shape = (SWA_GROUP * rows, n_cache + rows)
    r = _mod_pow2(lax.broadcasted_iota(jnp.int32, shape, 0), rows)
    c = lax.broadcasted_iota(jnp.int32, shape, 1)
    q_seq, q_t = _div_pow2(r, DEC_SEQ), _mod_pow2(r, DEC_SEQ)
    is_new = c >= n_cache
    k_seq = jnp.where(is_new, _div_pow2(c - n_cache, DEC_SEQ), _div_pow2(c, WINDOW))
    k_slot = jnp.where(is_new, WINDOW + _mod_pow2(c - n_cache, DEC_SEQ), _mod_pow2(c, WINDOW))
    rel = q_t + WINDOW - k_slot
    valid = (q_seq == k_seq) & (rel >= 0) & (rel <= WINDOW)
    o_ref[...] = _sink_attention(q, k_all, v_all, valid, sink_ref).astype(BF16)


def swa_core(proj, sinks, cache_k, cache_v):
    nb = SEQ // SWA_BLOCK
    smem = pl.BlockSpec(memory_space=pltpu.SMEM)
    cos_p, sin_p = _swa_rope_tables(jnp.arange(SEQ, dtype=jnp.int32))
    kcol = SWA_NQ // SWA_NK
    cur = lambda b, i: b * nb + i
    prev = lambda b, i: b * nb + jnp.maximum(i - 1, 0)
    o, k_rot = pl.pallas_call(
        _swa_prompt_body,
        grid=(BATCH, nb),
        in_specs=[
            smem,
            pl.BlockSpec((SWA_BLOCK, SWA_NQ), lambda b, i: (cur(b, i), 0)),
            pl.BlockSpec((SWA_BLOCK, SWA_NK), lambda b, i: (cur(b, i), kcol)),
            pl.BlockSpec((SWA_BLOCK, SWA_NK), lambda b, i: (cur(b, i), kcol + 1)),
            pl.BlockSpec((SWA_BLOCK, SWA_NK), lambda b, i: (prev(b, i), kcol)),
            pl.BlockSpec((SWA_BLOCK, SWA_NK), lambda b, i: (prev(b, i), kcol + 1)),
            pl.BlockSpec((SWA_BLOCK, LANES), lambda b, i: (i, 0)),
            pl.BlockSpec((SWA_BLOCK, LANES), lambda b, i: (i, 0)),
            pl.BlockSpec((SWA_BLOCK, LANES), lambda b, i: (jnp.maximum(i - 1, 0), 0)),
            pl.BlockSpec((SWA_BLOCK, LANES), lambda b, i: (jnp.maximum(i - 1, 0), 0)),
        ],
        out_specs=[
            pl.BlockSpec((SWA_BLOCK, SWA_NQ), lambda b, i: (cur(b, i), 0)),
            pl.BlockSpec((SWA_BLOCK, SWA_NK), lambda b, i: (cur(b, i), 0)),
        ],
        out_shape=[jax.ShapeDtypeStruct((NP, SWA_NQ), BF16),
                   jax.ShapeDtypeStruct((NP, SWA_NK), F32)],
        compiler_params=_cparams("parallel", "arbitrary"),
        name="swa_prompt",
    )(sinks, proj, proj, proj, proj, proj, cos_p, sin_p, cos_p, sin_p)

    cos_s, sin_s = (jnp.tile(t, (SWA_SAMPLE_SEQS, 1))
                    for t in _swa_rope_tables(PAST_LEN + jnp.arange(DEC_SEQ, dtype=jnp.int32)))
    rows = SWA_SAMPLE_SEQS * DEC_SEQ
    row0 = NP // rows
    o_s, k_s, v_s = pl.pallas_call(
        _swa_sample_body,
        grid=(DEC_BATCH // SWA_SAMPLE_SEQS,),
        in_specs=[
            smem,
            pl.BlockSpec((rows, SWA_NQ), lambda g: (row0 + g, 0)),
            pl.BlockSpec((rows, SWA_NK), lambda g: (row0 + g, kcol)),
            pl.BlockSpec((rows, SWA_NK), lambda g: (row0 + g, kcol + 1)),
            pl.BlockSpec((SWA_SAMPLE_SEQS, WINDOW, SWA_NK), lambda g: (g, 0, 0)),
            pl.BlockSpec((SWA_SAMPLE_SEQS, WINDOW, SWA_NK), lambda g: (g, 0, 0)),
            pl.BlockSpec((rows, LANES), lambda g: (0, 0)),
            pl.BlockSpec((rows, LANES), lambda g: (0, 0)),
        ],
        out_specs=[
            pl.BlockSpec((rows, SWA_NQ), lambda g: (g, 0)),
            pl.BlockSpec((SWA_SAMPLE_SEQS, WINDOW, SWA_NK), lambda g: (g, 0, 0)),
            pl.BlockSpec((SWA_SAMPLE_SEQS, WINDOW, SWA_NK), lambda g: (g, 0, 0)),
        ],
        out_shape=[jax.ShapeDtypeStruct((NS, SWA_NQ), BF16),
                   jax.ShapeDtypeStruct((DEC_BATCH, WINDOW, SWA_NK), F32),
                   jax.ShapeDtypeStruct((DEC_BATCH, WINDOW, SWA_NK), F32)],
        compiler_params=_cparams("parallel"),
        name="swa_sample",
    )(sinks, proj, proj, proj, cache_k, cache_v, cos_s, sin_s)
    return o, o_s, k_rot, k_s, v_s


RWKV_ROW_TILE = 256
RWKV_PAIRS = RWKV_HEADS // 2
RWKV_STACK = MXU_DIM_V7X
RWKV_SAMPLE_SEQS = RWKV_STACK // (RWKV_PAIRS * DEC_SEQ)


def _block_diag_ones():
    r = jnp.arange(MXU_DIM_V7X) // RWKV_HD
    return (r[:, None] == r[None, :]).astype(BF16)


def _store_pairs(ref, val):
    for p in range(RWKV_PAIRS):
        ref[p] = val[:, p * LANES:(p + 1) * LANES].astype(ref.dtype)


def _load_pairs(ref):
    return jnp.concatenate([ref[p] for p in range(RWKV_PAIRS)], axis=-1)


def _rwkv_proj_body(x_ref, xprev_ref, first_ref, gn_ref, mu_ref, wr_ref, wk_ref, wv_ref, w0_ref, w1_ref, w2_ref,
                    a0_ref, a1_ref, a2_ref, g1_ref, g2_ref, kk_ref, ka_ref, bd_ref,
                    r_out, lw_out, k_out, v_out, kk_out, b_out, g_out, hlast_out, h_scr):
    i = pl.program_id(0)
    h = _rms(x_ref[...], gn_ref[...])
    prev_tile_last = _rms(xprev_ref[...], gn_ref[...])[SUBLANES - 1:SUBLANES, :]
    rowid = lax.broadcasted_iota(jnp.int32, h.shape, 0)
    xp = jnp.where(rowid == 0, prev_tile_last, pltpu.roll(h, 1, 0))
    seq_len_mask = jnp.where(i >= NP // RWKV_ROW_TILE, DEC_SEQ - 1, SEQ - 1)
    is_first = jnp.bitwise_and(i * RWKV_ROW_TILE + rowid, seq_len_mask) == 0
    xp = jnp.where(is_first, first_ref[...], xp)
    _store_pairs(h_scr, h)
    every8th = pl.ds(SUBLANES - 1, RWKV_ROW_TILE // SUBLANES, stride=SUBLANES)
    hlast_out[...] = jnp.concatenate([h_scr[p, every8th, :] for p in range(RWKV_PAIRS)], axis=-1)
    d = xp - h
    xs = [h + d * mu_ref[i:i + 1, :] for i in range(6)]
    _store_pairs(r_out, _dot(xs[0], wr_ref[...]))
    k = _dot(xs[1], wk_ref[...])
    _store_pairs(v_out, _dot(xs[2], wv_ref[...]))
    z = w0_ref[...] + _dot(jnp.tanh(_dot(xs[3], w1_ref[...])), w2_ref[...])
    softplus = jnp.maximum(-z, 0.0) + jnp.log1p(jnp.exp(-jnp.abs(z)))
    _store_pairs(lw_out, -jnp.exp(-softplus - 0.5))
    a = jax.nn.sigmoid(a0_ref[...] + _dot(_dot(xs[4], a1_ref[...]), a2_ref[...]))
    g_out[...] = _dot(jax.nn.sigmoid(_dot(xs[5], g1_ref[...])), g2_ref[...]).astype(BF16)
    kk = k * kk_ref[...]
    norm = jnp.sqrt(_seg_sum(kk * kk, bd_ref[...], 2))
    kk = kk / jnp.maximum(norm, 1e-12)
    _store_pairs(kk_out, kk)
    _store_pairs(b_out, kk * a)
    _store_pairs(k_out, k * (1.0 + (a - 1.0) * ka_ref[...]))


def rwkv_proj(x, shift_s, p):
    first = jnp.concatenate([
        jnp.zeros((RWKV_ROW_TILE, D_MODEL), F32),
        jnp.pad(shift_s[:, None, :], ((0, 0), (0, DEC_SEQ - 1), (0, 0))).reshape(NS, D_MODEL)], axis=0)
    prompt_tiles = NP // RWKV_ROW_TILE
    row = pl.BlockSpec((RWKV_ROW_TILE, D_MODEL), lambda i: (i, 0))
    prev = pl.BlockSpec((SUBLANES, D_MODEL), lambda i: (jnp.maximum(i * (RWKV_ROW_TILE // SUBLANES) - 1, 0), 0))
    first_spec = pl.BlockSpec((RWKV_ROW_TILE, D_MODEL), lambda i: (jnp.maximum(i - prompt_tiles + 1, 0), 0))
    pair = pl.BlockSpec((RWKV_PAIRS, RWKV_ROW_TILE, LANES), lambda i: (0, i, 0))
    full = lambda a: pl.BlockSpec(a.shape, lambda i: (0,) * a.ndim)
    consts = [p["g_mix"], p["mu"], p["wr"], p["wk"], p["wv"], p["w0"], p["w1"], p["w2"], p["a0"], p["a1"],
              p["a2"], p["g1"], p["g2"], p["k_k"], p["k_a"], p["bd"]]
    pair_shape = lambda dt: jax.ShapeDtypeStruct((RWKV_PAIRS, NT, LANES), dt)
    return pl.pallas_call(
        _rwkv_proj_body,
        grid=(NT // RWKV_ROW_TILE,),
        in_specs=[row, prev, first_spec] + [full(c) for c in consts],
        out_specs=[pair] * 6 + [row, pl.BlockSpec((RWKV_ROW_TILE // SUBLANES, D_MODEL), lambda i: (i, 0))],
        out_shape=[pair_shape(BF16), pair_shape(F32)] + [pair_shape(BF16)] * 4
        + [jax.ShapeDtypeStruct((NT, D_MODEL), BF16), jax.ShapeDtypeStruct((NT // SUBLANES, D_MODEL), F32)],
        scratch_shapes=[pltpu.VMEM((RWKV_PAIRS, RWKV_ROW_TILE, LANES), F32)],
        compiler_params=_cparams("parallel"),
        name="rwkv_proj",
    )(x, x, first, *consts)


def _rwkv_chunk(c_len, r, lw, k, v, kk, b, state, store):
    n_grp = RWKV_STACK // c_len
    shift = int(math.log2(c_len))
    assert 2 ** shift == c_len
    row = lax.broadcasted_iota(jnp.int32, (RWKV_STACK, RWKV_STACK), 0)
    col = lax.broadcasted_iota(jnp.int32, (RWKV_STACK, RWKV_STACK), 1)
    same = jnp.right_shift(row, shift) == jnp.right_shift(col, shift)
    strict = same & (row > col)
    incl = same & (row >= col)
    tri = jnp.where(incl, 1.0, 0.0).astype(BF16)
    lam = sum(jnp.dot(tri, part, preferred_element_type=F32) for part in _split3(lw))
    lam3 = lam.reshape(n_grp, c_len, LANES)
    lam_end3 = lam3[:, c_len - 1:c_len, :]
    e_end = jnp.exp(lam_end3 - lam3).reshape(RWKV_STACK, LANES)
    e_tot3 = jnp.exp(lam_end3)
    e_neg = jnp.exp(-lam)
    at = -kk * jnp.exp(lam - lw)
    rt = r * jnp.exp(lam)
    bh = b * e_neg
    kh = k * e_neg
    bb = b * e_end
    kb = k * e_end

    lane = lax.broadcasted_iota(jnp.int32, (RWKV_STACK, LANES), 1)
    halves = (lane < RWKV_HD, lane >= RWKV_HD)
    rhs = jnp.concatenate([bh, kh], axis=0)
    a_pow, b_low, r_low = [], [], []
    for sel in halves:
        lhs = jnp.concatenate([jnp.where(sel, at, 0.0), jnp.where(sel, rt, 0.0)], axis=0)
        m = _dot_nt(lhs, rhs)
        a_pow.append(jnp.where(strict, m[:RWKV_STACK, :RWKV_STACK], 0.0))
        b_low.append(jnp.where(strict, m[:RWKV_STACK, RWKV_STACK:], 0.0))
        r_low.append(jnp.concatenate([jnp.where(incl, m[RWKV_STACK:, :RWKV_STACK], 0.0),
                                      jnp.where(incl, m[RWKV_STACK:, RWKV_STACK:], 0.0)], axis=1))

    rows = lambda t, g: t[g * c_len:(g + 1) * c_len]
    states = [state(g) for g in range(n_grp)]
    a_s, r_s = [], []
    for g in range(n_grp):
        p = _dot_nt(jnp.concatenate([rows(at, g), rows(rt, g)], axis=0), states[g])
        a_s.append(p[:c_len])
        r_s.append(p[c_len:])
    u = jnp.concatenate(a_s, axis=0)
    for e, sel in enumerate(halves):
        u = u + jnp.where(sel, _dot(b_low[e], v), 0.0)
    for it in range(shift):
        u = u + sum(jnp.where(sel, _dot(a_pow[e], u), 0.0) for e, sel in enumerate(halves))
        if it + 1 < shift:
            a_pow = [_dot(a, a) for a in a_pow]
    uv = jnp.concatenate([u, v], axis=0)
    y = jnp.concatenate(r_s, axis=0)
    for e, sel in enumerate(halves):
        y = y + jnp.where(sel, _dot(r_low[e], uv), 0.0)

    r128 = lax.broadcasted_iota(jnp.int32, (LANES, LANES), 0)
    c128 = lax.broadcasted_iota(jnp.int32, (LANES, LANES), 1)
    diag = (r128 < RWKV_HD) == (c128 < RWKV_HD)
    for g in range(n_grp):
        upd = _dot_tn(jnp.concatenate([rows(u, g), rows(v, g)], axis=0),
                      jnp.concatenate([rows(bb, g), rows(kb, g)], axis=0))
        store(g, states[g] * e_tot3[g] + jnp.where(diag, upd, 0.0))
    return y


RWKV_PROMPT_SEQS = 2


def _stacked(ref):
    return ref[...].reshape(RWKV_STACK, LANES).astype(F32)


def _rwkv_prompt_body(*refs):
    ins, (y_ref, s_ref, s_scr) = refs[:6 * RWKV_PROMPT_SEQS], refs[6 * RWKV_PROMPT_SEQS:]
    c = pl.program_id(1)

    @pl.when(c == 0)
    def _():
        s_scr[...] = jnp.zeros_like(s_scr)

    for seq in range(RWKV_PROMPT_SEQS):
        def store(g, s):
            s_scr[seq, g] = s

        args = [_stacked(ins[a * RWKV_PROMPT_SEQS + seq]) for a in range(6)]
        y = _rwkv_chunk(RWKV_CHUNK, *args, lambda g: s_scr[seq, g], store)
        y_ref[:, seq] = y.reshape(RWKV_PAIRS, RWKV_CHUNK, LANES)

    @pl.when(c == pl.num_programs(1) - 1)
    def _():
        for seq in range(RWKV_PROMPT_SEQS):
            for p in range(RWKV_PAIRS):
                s_ref[seq, 2 * p] = s_scr[seq, p, :RWKV_HD, :RWKV_HD]
                s_ref[seq, 2 * p + 1] = s_scr[seq, p, RWKV_HD:, RWKV_HD:]


def _rwkv_sample_body(r_ref, lw_ref, k_ref, v_ref, kk_ref, b_ref, s0_ref, y_ref, s_ref):
    zero = jnp.zeros((RWKV_HD, RWKV_HD), F32)

    def state(g):
        p, seq = divmod(g, RWKV_SAMPLE_SEQS)
        return jnp.concatenate([jnp.concatenate([s0_ref[seq, 2 * p], zero], axis=1),
                                jnp.concatenate([zero, s0_ref[seq, 2 * p + 1]], axis=1)], axis=0)

    def store(g, s):
        p, seq = divmod(g, RWKV_SAMPLE_SEQS)
        s_ref[seq, 2 * p] = s[:RWKV_HD, :RWKV_HD]
        s_ref[seq, 2 * p + 1] = s[RWKV_HD:, RWKV_HD:]

    args = [_stacked(ref) for ref in (r_ref, lw_ref, k_ref, v_ref, kk_ref, b_ref)]
    y_ref[...] = _rwkv_chunk(DEC_SEQ, *args, state, store).reshape(y_ref.shape)


def rwkv_core(r, lw, k, v, kk, b, state_s):
    assert RWKV_PAIRS * RWKV_CHUNK == RWKV_STACK
    nc = SEQ // RWKV_CHUNK
    st_shape = (RWKV_HEADS, RWKV_HD, RWKV_HD)
    seq_blk = lambda seq: pl.BlockSpec((RWKV_PAIRS, RWKV_CHUNK, LANES),
                                       lambda bi, c: (0, (bi * RWKV_PROMPT_SEQS + seq) * nc + c, 0))
    inputs = (r, lw, k, v, kk, b)
    y_p, s_p = pl.pallas_call(
        _rwkv_prompt_body,
        grid=(BATCH // RWKV_PROMPT_SEQS, nc),
        in_specs=[seq_blk(seq) for _ in inputs for seq in range(RWKV_PROMPT_SEQS)],
        out_specs=[pl.BlockSpec((RWKV_PAIRS, RWKV_PROMPT_SEQS, RWKV_CHUNK, LANES), lambda bi, c: (0, bi, c, 0)),
                   pl.BlockSpec((RWKV_PROMPT_SEQS,) + st_shape, lambda bi, c: (bi, 0, 0, 0))],
        out_shape=[jax.ShapeDtypeStruct((RWKV_PAIRS, BATCH, SEQ, LANES), F32),
                   jax.ShapeDtypeStruct((BATCH,) + st_shape, F32)],
        scratch_shapes=[pltpu.VMEM((RWKV_PROMPT_SEQS, RWKV_PAIRS, LANES, LANES), F32)],
        compiler_params=_cparams("parallel", "arbitrary"),
        name="rwkv_prompt",
    )(*[a for a in inputs for _ in range(RWKV_PROMPT_SEQS)])

    rows = RWKV_SAMPLE_SEQS * DEC_SEQ
    row0 = NP // rows
    blk = pl.BlockSpec((RWKV_PAIRS, rows, LANES), lambda i: (0, row0 + i, 0))
    st = pl.BlockSpec((RWKV_SAMPLE_SEQS,) + st_shape, lambda i: (i, 0, 0, 0))
    y_s, s_s = pl.pallas_call(
        _rwkv_sample_body,
        grid=(DEC_BATCH // RWKV_SAMPLE_SEQS,),
        in_specs=[blk] * 6 + [st],
        out_specs=[pl.BlockSpec((RWKV_PAIRS, rows, LANES), lambda i: (0, i, 0)), st],
        out_shape=[jax.ShapeDtypeStruct((RWKV_PAIRS, NS, LANES), F32),
                   jax.ShapeDtypeStruct((DEC_BATCH,) + st_shape, F32)],
        compiler_params=_cparams("parallel"),
        name="rwkv_sample",
    )(*inputs, state_s)
    return y_p.reshape(RWKV_PAIRS, NP, LANES), y_s, s_p, s_s


def _rwkv_out_body(x_ref, yp_ref, ys_ref, r_ref, k_ref, v_ref, g_ref, rk_ref, lnw_ref, lnb_ref, bd_ref, wo_ref,
                   o_ref):
    bd = bd_ref[...]
    y = _pick_group(RWKV_ROW_TILE, _load_pairs(yp_ref), _load_pairs(ys_ref))
    yc = y - _seg_sum(y, bd, 2) * (1.0 / RWKV_HD)
    var = _seg_sum(yc * yc, bd, 1) * (1.0 / RWKV_HD)
    yn = yc * lax.rsqrt(var + RWKV_GN_EPS) * lnw_ref[...] + lnb_ref[...]
    rk = _load_pairs(r_ref).astype(F32) * _load_pairs(k_ref).astype(F32)
    bonus = _seg_sum(rk * rk_ref[...], bd, 2) * _load_pairs(v_ref).astype(F32)
    z = (yn + bonus) * g_ref[...].astype(F32)
    o_ref[...] = x_ref[...] + _dot(z, wo_ref[...])


def rwkv_out(x, y_p, y_s, r, k, v, g, p):
    row = pl.BlockSpec((RWKV_ROW_TILE, D_MODEL), lambda i: (i, 0))
    pair_of = lambda f: pl.BlockSpec((RWKV_PAIRS, RWKV_ROW_TILE, LANES), lambda i: (0, f(i), 0))
    full = lambda a: pl.BlockSpec(a.shape, lambda i: (0,) * a.ndim)
    consts = [p["r_k"], p["ln_w"], p["ln_b"], p["bd"], p["w_out"]]
    return pl.pallas_call(
        _rwkv_out_body,
        grid=(NT // RWKV_ROW_TILE,),
        in_specs=[row, *_group_specs(RWKV_ROW_TILE, pair_of)] + [pair_of(lambda i: i)] * 3 + [row]
        + [full(c) for c in consts],
        out_specs=row,
        out_shape=jax.ShapeDtypeStruct((NT, D_MODEL), F32),
        compiler_params=_cparams("parallel"),
        name="rwkv_out",
    )(x, y_p, y_s, r, k, v, g, *consts)


def _pad_cols(w, n):
    return jnp.pad(w, ((0, 0), (0, n - w.shape[1])))


def _pad_rows(w, n):
    return jnp.pad(w, ((0, n - w.shape[0]), (0, 0)))


def kernel(x_prompt, x_sample, state_ret, cache_swa_k, cache_swa_v, state_rwkv_wkv, state_rwkv_shift,
           norm_mix, norm_mlp, norm_final, ret_w_in, ret_w_out,
           swa_w_in, swa_b_in, swa_sinks, swa_w_out,
           rwkv_mu, rwkv_w_rkv, rwkv_w0, rwkv_w1, rwkv_w2, rwkv_a0, rwkv_a1, rwkv_a2,
           rwkv_g1, rwkv_g2, rwkv_k_k, rwkv_k_a, rwkv_r_k, rwkv_ln_w, rwkv_ln_b, rwkv_w_out,
           mlp_w_up, mlp_w_down):
    x = jnp.concatenate([x_prompt.reshape(NP, D_MODEL), x_sample.reshape(NS, D_MODEL)], axis=0)
    row = lambda t: t.reshape(1, -1)
    ret_p, k_p, k_s, v_p, v_s, wkv_p, wkv_s, sh_p, sh_s = ([] for _ in range(9))
    ret_s = None
    for layer in range(DEPTH):
        kind = LAYER_KINDS[layer]
        j = LAYER_KINDS[:layer].count(kind)
        g_mix = row(norm_mix[layer])
        if kind == 0:
            n_in = ret_w_in.shape[2]
            proj = norm_matmul(x, g_mix, ret_w_in[j].astype(BF16), jnp.zeros((1, n_in), F32), 2048)
            o_p, o_s, s_p, ret_s = retention_core(proj, state_ret, j, ret_s)
            x = matmul_residual(x, o_p, o_s, ret_w_out[j].astype(BF16))
            ret_p.append(s_p)
        elif kind == 1:
            proj = norm_matmul(x, g_mix, swa_w_in[j].astype(BF16), row(swa_b_in[j]), SWA_NQ + 2 * SWA_NK)
            o_p, o_s, k_rot, kc, vc = swa_core(proj, swa_sinks[j],
                                        cache_swa_k[j].reshape(DEC_BATCH, WINDOW, SWA_NK),
                                        cache_swa_v[j].reshape(DEC_BATCH, WINDOW, SWA_NK))
            x = matmul_residual(x, o_p, o_s, swa_w_out[j].astype(BF16))
            kv_shape = (-1, WINDOW, SWA_KV_HEADS, SWA_HD)
            k_p.append(k_rot.reshape(BATCH, SEQ, SWA_NK)[:, -WINDOW:].reshape(kv_shape))
            v_p.append(proj[:NP, SWA_NQ + SWA_NK:].reshape(BATCH, SEQ, SWA_NK)[:, -WINDOW:]
                       .astype(F32).reshape(kv_shape))
            k_s.append(kc.reshape(kv_shape))
            v_s.append(vc.reshape(kv_shape))
        else:
            lora_w = LANES
            lora_g = 2 * LANES
            p = dict(
                mu=rwkv_mu[j], wr=rwkv_w_rkv[j, 0].astype(BF16), wk=rwkv_w_rkv[j, 1].astype(BF16),
                wv=rwkv_w_rkv[j, 2].astype(BF16), w0=row(rwkv_w0[j]),
                w1=_pad_cols(rwkv_w1[j], lora_w).astype(BF16), w2=_pad_rows(rwkv_w2[j], lora_w).astype(BF16),
                a0=row(rwkv_a0[j]),
                a1=_pad_cols(rwkv_a1[j], lora_w).astype(BF16), a2=_pad_rows(rwkv_a2[j], lora_w).astype(BF16),
                g1=_pad_cols(rwkv_g1[j], lora_g).astype(BF16), g2=_pad_rows(rwkv_g2[j], lora_g).astype(BF16),
                k_k=row(rwkv_k_k[j]), k_a=row(rwkv_k_a[j]), r_k=row(rwkv_r_k[j]),
                ln_w=row(rwkv_ln_w[j]), ln_b=row(rwkv_ln_b[j]), bd=_block_diag_ones(),
                w_out=rwkv_w_out[j].astype(BF16), g_mix=g_mix)
            r, lw, k, v, kk, b, g, h_last = rwkv_proj(x, state_rwkv_shift[j], p)
            y_p, y_s, s_p, s_s = rwkv_core(r, lw, k, v, kk, b, state_rwkv_wkv[j])
            x = rwkv_out(x, y_p, y_s, r, k, v, g, p)
            wkv_p.append(s_p)
            wkv_s.append(s_s)
            sh_p.append(h_last[SEQ // SUBLANES - 1:NP // SUBLANES:SEQ // SUBLANES])
            sh_s.append(h_last[NP // SUBLANES:])
        x = mlp(x, row(norm_mlp[layer]), mlp_w_up[layer].astype(BF16), mlp_w_down[layer].astype(BF16))
    y_p = rmsnorm_rows(x, row(norm_final), 0, NP)
    y_s = rmsnorm_rows(x, row(norm_final), NP, NS)
    return (y_p.reshape(BATCH, SEQ, D_MODEL), y_s.reshape(DEC_BATCH, DEC_SEQ, D_MODEL),
            jnp.stack(ret_p), ret_s, jnp.stack(k_p), jnp.stack(k_s), jnp.stack(v_p), jnp.stack(v_s),
            jnp.stack(wkv_p), jnp.stack(wkv_s), jnp.stack(sh_p), jnp.stack(sh_s))
```

```python
import functools
import math

import jax
import jax.numpy as jnp
from jax import lax
from jax.experimental import pallas as pl
from jax.experimental.pallas import tpu as pltpu

F32 = jnp.float32
BF16 = jnp.bfloat16

D_MODEL = 1024
BATCH = 8
SEQ = 2048
DEPTH = 4
DEC_BATCH = 128
DEC_SEQ = 8
PAST_LEN = 8192
LAYER_KINDS = tuple(i % 3 for i in range(DEPTH))
NORM_EPS = 1e-6
ROPE_THETA = 10000.0
NEG_INF = -1e30
D_FF = 4 * D_MODEL

RET_HEADS = 4
RET_DK = D_MODEL // RET_HEADS
RET_DV = 2 * D_MODEL // RET_HEADS
RET_CHUNK = 128
RET_PROMPT_CHUNK = 256
RET_NQ = RET_HEADS * RET_DK
RET_NV = RET_HEADS * RET_DV

SWA_HEADS = 16
SWA_KV_HEADS = 4
SWA_GROUP = SWA_HEADS // SWA_KV_HEADS
SWA_HD = D_MODEL // SWA_HEADS
WINDOW = 128
SWA_BLOCK = 128
SWA_NQ = SWA_HEADS * SWA_HD
SWA_NK = SWA_KV_HEADS * SWA_HD

RWKV_HD = 64
RWKV_HEADS = D_MODEL // RWKV_HD
RWKV_GN_EPS = 64e-5
RWKV_CHUNK = 32

NP = BATCH * SEQ
NS = DEC_BATCH * DEC_SEQ
NT = NP + NS

VMEM_LIMIT_V7X = 48 * 1024 * 1024
LANES = 128
SUBLANES = 8
MXU_DIM_V7X = 256

ROW_TILE = 1024


def _cparams(*sem):
    return pltpu.CompilerParams(dimension_semantics=sem, vmem_limit_bytes=VMEM_LIMIT_V7X)


def _dot(a, b):
    return jnp.dot(a.astype(BF16), b.astype(BF16), preferred_element_type=F32)


def _dot_nt(a, b):
    return lax.dot_general(a.astype(BF16), b.astype(BF16), (((1,), (1,)), ((), ())),
                           preferred_element_type=F32)


def _dot_tn(a, b):
    return lax.dot_general(a.astype(BF16), b.astype(BF16), (((0,), (0,)), ((), ())),
                           preferred_element_type=F32)


def _div_pow2(x, n):
    assert n & (n - 1) == 0
    return jnp.right_shift(x, n.bit_length() - 1)


def _mod_pow2(x, n):
    assert n & (n - 1) == 0
    return jnp.bitwise_and(x, n - 1)


def _rms(x, g):
    ms = jnp.mean(x * x, axis=-1, keepdims=True)
    return x * lax.rsqrt(ms + NORM_EPS) * g


def _split3(x):
    hi = x.astype(BF16)
    r1 = x - hi.astype(F32)
    mid = r1.astype(BF16)
    lo = (r1 - mid.astype(F32)).astype(BF16)
    return hi, mid, lo


def _seg_sum(x, bd, passes):
    parts = _split3(x)[:passes]
    cols = []
    for j in range(D_MODEL // MXU_DIM_V7X):
        sl = slice(j * MXU_DIM_V7X, (j + 1) * MXU_DIM_V7X)
        acc = jnp.dot(parts[0][:, sl], bd, preferred_element_type=F32)
        for p in parts[1:]:
            acc = acc + jnp.dot(p[:, sl], bd, preferred_element_type=F32)
        cols.append(acc)
    return jnp.concatenate(cols, axis=-1)


def _rope_tables(pos, half):
    inv_freq = jnp.power(ROPE_THETA, -jnp.arange(half, dtype=F32) / half)
    ang = pos.astype(F32)[:, None] * inv_freq[None, :]
    return jnp.cos(ang), jnp.sin(ang)


def _stacked_rope_tables(half, widen):
    tp = widen(*_rope_tables(jnp.arange(SEQ, dtype=jnp.int32), half))
    ts = widen(*_rope_tables(PAST_LEN + jnp.arange(DEC_SEQ, dtype=jnp.int32), half))
    return tuple(jnp.concatenate([jnp.tile(p, (BATCH, 1)), jnp.tile(s, (DEC_BATCH, 1))], axis=0)
                 for p, s in zip(tp, ts))


RET_PROJ_TILE = RET_NQ + RET_NQ


def _ret_proj_body(x_ref, g_ref, w_ref, cos_ref, sin_ref, o_ref, xn_ref):
    j = pl.program_id(1)

    @pl.when(j == 0)
    def _():
        xn_ref[...] = _rms(x_ref[...], g_ref[...]).astype(BF16)

    acc = jnp.dot(xn_ref[...], w_ref[...], preferred_element_type=F32)

    @pl.when(j == 0)
    def _():
        cos, sin = cos_ref[...], sin_ref[...]
        half = RET_DK // 2
        for blk in range(RET_PROJ_TILE // RET_DK):
            lo, mid, hi = blk * RET_DK, blk * RET_DK + half, (blk + 1) * RET_DK
            x1, x2 = acc[:, lo:mid], acc[:, mid:hi]
            scale = 1.0 if blk < RET_HEADS else RET_DK ** -0.5
            o_ref[:, lo:mid] = ((x1 * cos - x2 * sin) * scale).astype(BF16)
            o_ref[:, mid:hi] = ((x2 * cos + x1 * sin) * scale).astype(BF16)

    @pl.when(j > 0)
    def _():
        o_ref[...] = acc.astype(BF16)


def ret_proj(x, g, w):
    n_out = w.shape[1]
    cos, sin = _stacked_rope_tables(RET_DK // 2, lambda c, s: (c, s))
    return pl.pallas_call(
        _ret_proj_body,
        grid=(NT // ROW_TILE, n_out // RET_PROJ_TILE),
        in_specs=[
            pl.BlockSpec((ROW_TILE, D_MODEL), lambda i, j: (i, 0)),
            pl.BlockSpec((1, D_MODEL), lambda i, j: (0, 0)),
            pl.BlockSpec((D_MODEL, RET_PROJ_TILE), lambda i, j: (0, j)),
            pl.BlockSpec((ROW_TILE, LANES), lambda i, j: (i, 0)),
            pl.BlockSpec((ROW_TILE, LANES), lambda i, j: (i, 0)),
        ],
        out_specs=pl.BlockSpec((ROW_TILE, RET_PROJ_TILE), lambda i, j: (i, j)),
        out_shape=jax.ShapeDtypeStruct((NT, n_out), BF16),
        scratch_shapes=[pltpu.VMEM((ROW_TILE, D_MODEL), BF16)],
        compiler_params=_cparams("parallel", "arbitrary"),
        name="ret_proj",
    )(x, g, w, cos, sin)


SWA_PROJ_ROW_TILE = 512


def _swa_rope(x, cos, sin):
    n = x.shape[1]
    half = SWA_HD // 2
    lane = lax.broadcasted_iota(jnp.int32, x.shape, 1)
    partner = jnp.where(_mod_pow2(lane, SWA_HD) < half, pltpu.roll(x, n - half, 1), pltpu.roll(x, half, 1))
    reps = n // LANES
    return x * jnp.concatenate([cos] * reps, axis=-1) + partner * jnp.concatenate([sin] * reps, axis=-1)


def _swa_proj_body(x_ref, g_ref, w_ref, b_ref, cos_ref, sin_ref, o_ref, kr_ref):
    xn = _rms(x_ref[...], g_ref[...]).astype(BF16)
    acc = jnp.dot(xn, w_ref[...], preferred_element_type=F32) + b_ref[...]
    n_qk = SWA_NQ + SWA_NK
    qk = _swa_rope(acc[:, :n_qk], cos_ref[...], sin_ref[...])
    o_ref[:, :n_qk] = qk.astype(BF16)
    o_ref[:, n_qk:] = acc[:, n_qk:].astype(BF16)
    kr_ref[...] = qk[:, SWA_NQ:]


def swa_proj(x, g, w, b):
    n_out = w.shape[1]

    def widen(c, s):
        rep = LANES // SWA_HD
        return jnp.tile(jnp.concatenate([c, c], axis=-1), (1, rep)), jnp.tile(jnp.concatenate([-s, s], axis=-1), (1, rep))

    cos, sin = _stacked_rope_tables(SWA_HD // 2, widen)
    tile = SWA_PROJ_ROW_TILE
    return pl.pallas_call(
        _swa_proj_body,
        grid=(NT // tile,),
        in_specs=[
            pl.BlockSpec((tile, D_MODEL), lambda i: (i, 0)),
            pl.BlockSpec((1, D_MODEL), lambda i: (0, 0)),
            pl.BlockSpec((D_MODEL, n_out), lambda i: (0, 0)),
            pl.BlockSpec((1, n_out), lambda i: (0, 0)),
            pl.BlockSpec((tile, LANES), lambda i: (i, 0)),
            pl.BlockSpec((tile, LANES), lambda i: (i, 0)),
        ],
        out_specs=[pl.BlockSpec((tile, n_out), lambda i: (i, 0)), pl.BlockSpec((tile, SWA_NK), lambda i: (i, 0))],
        out_shape=[jax.ShapeDtypeStruct((NT, n_out), BF16), jax.ShapeDtypeStruct((NT, SWA_NK), F32)],
        compiler_params=_cparams("parallel"),
        name="swa_proj",
    )(x, g, w, b, cos, sin)


def _group_specs(tile, block_of):
    n_prompt = NP // tile
    return (block_of(lambda i: jnp.minimum(i, n_prompt - 1)), block_of(lambda i: jnp.maximum(i - n_prompt, 0)))


def _pick_group(tile, p_val, s_val):
    return jnp.where(pl.program_id(0) < NP // tile, p_val, s_val)


def _matmul_residual_body(x_ref, ap_ref, as_ref, w_ref, o_ref):
    a = _pick_group(ROW_TILE, ap_ref[...], as_ref[...])
    o_ref[...] = x_ref[...] + jnp.dot(a, w_ref[...], preferred_element_type=F32)


def matmul_residual(x, a_p, a_s, w):
    k = a_p.shape[1]
    return pl.pallas_call(
        _matmul_residual_body,
        grid=(NT // ROW_TILE,),
        in_specs=[
            pl.BlockSpec((ROW_TILE, D_MODEL), lambda i: (i, 0)),
            *_group_specs(ROW_TILE, lambda f: pl.BlockSpec((ROW_TILE, k), lambda i: (f(i), 0))),
            pl.BlockSpec((k, D_MODEL), lambda i: (0, 0)),
        ],
        out_specs=pl.BlockSpec((ROW_TILE, D_MODEL), lambda i: (i, 0)),
        out_shape=jax.ShapeDtypeStruct((NT, D_MODEL), F32),
        compiler_params=_cparams("parallel"),
        name="matmul_residual",
    )(x, a_p, a_s, w)


MLP_FF_TILE = 1024


def _mlp_body(x_ref, g_ref, wu_ref, wd_ref, o_ref, xn_ref, acc_ref):
    f = pl.program_id(1)

    @pl.when(f == 0)
    def _():
        xn_ref[...] = _rms(x_ref[...], g_ref[...]).astype(BF16)
        acc_ref[...] = jnp.zeros_like(acc_ref)

    a = jnp.maximum(jnp.dot(xn_ref[...], wu_ref[...], preferred_element_type=F32), 0.0)
    acc_ref[...] += jnp.dot((a * a).astype(BF16), wd_ref[...], preferred_element_type=F32)

    @pl.when(f == pl.num_programs(1) - 1)
    def _():
        o_ref[...] = x_ref[...] + acc_ref[...]


def mlp(x, g, w_up, w_down):
    return pl.pallas_call(
        _mlp_body,
        grid=(NT // ROW_TILE, D_FF // MLP_FF_TILE),
        in_specs=[
            pl.BlockSpec((ROW_TILE, D_MODEL), lambda i, f: (i, 0)),
            pl.BlockSpec((1, D_MODEL), lambda i, f: (0, 0)),
            pl.BlockSpec((D_MODEL, MLP_FF_TILE), lambda i, f: (0, f)),
            pl.BlockSpec((MLP_FF_TILE, D_MODEL), lambda i, f: (f, 0)),
        ],
        out_specs=pl.BlockSpec((ROW_TILE, D_MODEL), lambda i, f: (i, 0)),
        out_shape=jax.ShapeDtypeStruct((NT, D_MODEL), F32),
        scratch_shapes=[pltpu.VMEM((ROW_TILE, D_MODEL), BF16), pltpu.VMEM((ROW_TILE, D_MODEL), F32)],
        compiler_params=_cparams("parallel", "arbitrary"),
        name="mlp",
    )(x, g, w_up, w_down)


def _rmsnorm_body(x_ref, g_ref, o_ref):
    o_ref[...] = _rms(x_ref[...], g_ref[...])


def rmsnorm_rows(x, g, row0, n_rows):
    tile0 = row0 // ROW_TILE
    return pl.pallas_call(
        _rmsnorm_body,
        grid=(n_rows // ROW_TILE,),
        in_specs=[pl.BlockSpec((ROW_TILE, D_MODEL), lambda i: (tile0 + i, 0)),
                  pl.BlockSpec((1, D_MODEL), lambda i: (0, 0))],
        out_specs=pl.BlockSpec((ROW_TILE, D_MODEL), lambda i: (i, 0)),
        out_shape=jax.ShapeDtypeStruct((n_rows, D_MODEL), F32),
        compiler_params=_cparams("parallel"),
        name="rmsnorm",
    )(x, g)


def _ret_tables(c):
    log_gamma = jnp.log1p(-jnp.exp2(-5.0 - jnp.arange(RET_HEADS, dtype=F32)))
    idx = jnp.arange(c, dtype=F32)
    diff = idx[:, None] - idx[None, :]
    inner = jnp.where(diff >= 0, jnp.exp(log_gamma[:, None, None] * jnp.maximum(diff, 0.0)), 0.0)
    q_decay = jnp.exp(log_gamma[:, None] * (idx[None, :] + 1.0))
    k_decay = jnp.exp(log_gamma[:, None] * (c - 1.0 - idx[None, :]))
    chunk_decay = jnp.exp(log_gamma * c)
    rep = lambda t: jnp.broadcast_to(t[:, :, None], (RET_HEADS, c, LANES))
    return inner, rep(q_decay), rep(k_decay), chunk_decay


def _ret_heads(rows, lo, width):
    return jnp.stack([rows[:, lo + h * width:lo + (h + 1) * width] for h in range(RET_HEADS)])


def _ret_chunk(rows_list, s0, inner, qd, kd, cd_ref):
    n_seq = len(rows_list)
    cat = lambda f: jnp.concatenate([f(r) for r in rows_list], axis=0)
    q = cat(lambda r: _ret_heads(r, 0, RET_DK)).astype(BF16)
    k = cat(lambda r: _ret_heads(r, RET_NQ, RET_DK)).astype(F32)
    v = cat(lambda r: _ret_heads(r, 2 * RET_NQ, RET_DV)).astype(BF16)
    g = cat(lambda r: _ret_heads(r, 2 * RET_NQ + RET_NV, RET_DV)).astype(F32)
    per_seq = lambda t: jnp.concatenate([t] * n_seq, axis=0)
    inner, qd, kd = per_seq(inner), per_seq(qd), per_seq(kd)
    cd = per_seq(jnp.stack([jnp.full((1, 1), cd_ref[h], F32) for h in range(RET_HEADS)]))
    qd = jnp.concatenate([qd] * (RET_DV // LANES), axis=-1)
    kd = jnp.concatenate([kd] * (RET_DK // LANES), axis=-1)
    scores = jnp.einsum('bqd,bkd->bqk', q, k.astype(BF16), preferred_element_type=F32) * inner
    o = (jnp.einsum('bqk,bke->bqe', scores.astype(BF16), v, preferred_element_type=F32)
         + jnp.einsum('bqd,bde->bqe', q, s0.astype(BF16), preferred_element_type=F32) * qd)
    s_new = s0 * cd + jnp.einsum('bkd,bke->bde', (k * kd).astype(BF16), v, preferred_element_type=F32)
    o = o * lax.rsqrt(jnp.mean(o * o, axis=-1, keepdims=True) + NORM_EPS)
    return o * (g * jax.nn.sigmoid(g)), s_new


def _ret_prompt_body(cd_ref, proj_ref, inner_ref, qd_ref, kd_ref, o_ref, s_ref, s_scr):
    c = pl.program_id(1)

    @pl.when(c == 0)
    def _():
        s_scr[...] = jnp.zeros_like(s_scr)

    o, s_new = _ret_chunk([proj_ref[...]], s_scr[...], inner_ref[...], qd_ref[...], kd_ref[...], cd_ref)
    s_scr[...] = s_new
    o_ref[...] = jnp.concatenate([o[h] for h in range(RET_HEADS)], axis=-1).astype(BF16)

    @pl.when(c == pl.num_programs(1) - 1)
    def _():
        s_ref[0] = s_scr[...]


RET_SAMPLE_SEQS = 2


def _ret_sample_body(cd_ref, proj_ref, inner_ref, qd_ref, kd_ref, s0_ref, *rest):
    o_ref, s_ref = rest[-2:]
    rows = proj_ref[...].astype(F32)
    batch = RET_SAMPLE_SEQS * RET_HEADS
    o, s_new = _ret_chunk([rows[seq * DEC_SEQ:(seq + 1) * DEC_SEQ] for seq in range(RET_SAMPLE_SEQS)],
                          s0_ref[0].reshape(batch, RET_DK, RET_DV), inner_ref[...], qd_ref[...], kd_ref[...],
                          cd_ref)
    s_ref[0] = s_new.reshape(RET_SAMPLE_SEQS, RET_HEADS, RET_DK, RET_DV)
    o_ref[...] = jnp.concatenate(
        [jnp.concatenate([o[seq * RET_HEADS + h] for h in range(RET_HEADS)], axis=-1)
         for seq in range(RET_SAMPLE_SEQS)], axis=0).astype(BF16)


def retention_core(proj, state_ret, j, s_stack):
    n_in = proj.shape[1]
    nc = SEQ // RET_PROMPT_CHUNK
    smem = pl.BlockSpec(memory_space=pltpu.SMEM)
    inner, qd, kd, cd = _ret_tables(RET_PROMPT_CHUNK)
    full3 = lambda shape: pl.BlockSpec(shape, lambda b, c: (0, 0, 0))
    o, s_p = pl.pallas_call(
        _ret_prompt_body,
        grid=(BATCH, nc),
        in_specs=[
            smem,
            pl.BlockSpec((RET_PROMPT_CHUNK, n_in), lambda b, c: (b * nc + c, 0)),
            full3(inner.shape), full3(qd.shape), full3(kd.shape),
        ],
        out_specs=[
            pl.BlockSpec((RET_PROMPT_CHUNK, RET_NV), lambda b, c: (b * nc + c, 0)),
            pl.BlockSpec((1, RET_HEADS, RET_DK, RET_DV), lambda b, c: (b, 0, 0, 0)),
        ],
        out_shape=[jax.ShapeDtypeStruct((NP, RET_NV), BF16),
                   jax.ShapeDtypeStruct((BATCH, RET_HEADS, RET_DK, RET_DV), F32)],
        scratch_shapes=[pltpu.VMEM((RET_HEADS, RET_DK, RET_DV), F32)],
        compiler_params=_cparams("parallel", "arbitrary"),
        name="retention_prompt",
    )(cd, proj, inner, qd, kd)

    cs = math.gcd(DEC_SEQ, RET_CHUNK)
    assert cs == DEC_SEQ
    inner, qd, kd, cd = _ret_tables(cs)
    rows = RET_SAMPLE_SEQS * DEC_SEQ
    row0 = NP // rows
    full2 = lambda shape: pl.BlockSpec(shape, lambda b: (0, 0))
    full3 = lambda shape: pl.BlockSpec(shape, lambda b: (0, 0, 0))
    st = pl.BlockSpec((1, RET_SAMPLE_SEQS, RET_HEADS, RET_DK, RET_DV), lambda b: (j, b, 0, 0, 0))
    stacked = [] if s_stack is None else [s_stack]
    o_s, s_s = pl.pallas_call(
        _ret_sample_body,
        grid=(DEC_BATCH // RET_SAMPLE_SEQS,),
        in_specs=[
            smem,
            pl.BlockSpec((rows, n_in), lambda b: (row0 + b, 0)),
            full3(inner.shape), full3(qd.shape), full3(kd.shape),
            st,
        ] + [pl.BlockSpec(memory_space=pl.ANY)] * len(stacked),
        out_specs=[pl.BlockSpec((rows, RET_NV), lambda b: (b, 0)), st],
        out_shape=[jax.ShapeDtypeStruct((NS, RET_NV), BF16),
                   jax.ShapeDtypeStruct(state_ret.shape, F32)],
        input_output_aliases={6: 1} if stacked else {},
        compiler_params=_cparams("parallel"),
        name="retention_sample",
    )(cd, proj, inner, qd, kd, state_ret, *stacked)
    return o, o_s, s_p, s_s


def _sink_attention(q, k, v, valid, sink_ref):
    t = q.shape[0]
    head_cols = lambda x, h: x[:, h * SWA_HD:(h + 1) * SWA_HD]
    qs = jnp.stack([jnp.concatenate([head_cols(q, kh * SWA_GROUP + g) for g in range(SWA_GROUP)], axis=0)
                    for kh in range(SWA_KV_HEADS)]).astype(BF16)
    ks = jnp.stack([head_cols(k, kh) for kh in range(SWA_KV_HEADS)]).astype(BF16)
    vs = jnp.stack([head_cols(v, kh) for kh in range(SWA_KV_HEADS)]).astype(BF16)
    sink = jnp.stack([jnp.concatenate([jnp.full((t, 1), sink_ref[kh * SWA_GROUP + g], F32)
                                       for g in range(SWA_GROUP)], axis=0) for kh in range(SWA_KV_HEADS)])
    s = jnp.einsum('hqd,hkd->hqk', qs, ks, preferred_element_type=F32) * (SWA_HD ** -0.5)
    s = jnp.where(valid[None], s, NEG_INF)
    m = jnp.maximum(jnp.max(s, axis=-1, keepdims=True), sink)
    p = jnp.exp(s - m)
    denom = jnp.sum(p, axis=-1, keepdims=True) + jnp.exp(sink - m)
    o = jnp.einsum('hqk,hkd->hqd', p.astype(BF16), vs, preferred_element_type=F32) / denom
    return jnp.concatenate([o[h // SWA_GROUP, (h % SWA_GROUP) * t:(h % SWA_GROUP + 1) * t]
                            for h in range(SWA_HEADS)], axis=-1)


def _swa_prompt_body(sink_ref, q_ref, kc_ref, vc_ref, kp_ref, vp_ref, o_ref):
    blk = pl.program_id(1)
    q = q_ref[...]
    k_all = jnp.concatenate([kp_ref[...], kc_ref[...]], axis=0)
    v_all = jnp.concatenate([vp_ref[...], vc_ref[...]], axis=0)
    shape = (SWA_GROUP * SWA_BLOCK, 2 * SWA_BLOCK)
    i = _mod_pow2(lax.broadcasted_iota(jnp.int32, shape, 0), SWA_BLOCK)
    j = lax.broadcasted_iota(jnp.int32, shape, 1)
    rel = i + SWA_BLOCK - j
    valid = (rel >= 0) & (rel <= WINDOW) & (j >= jnp.where(blk > 0, 0, SWA_BLOCK))
    o_ref[...] = _sink_attention(q, k_all, v_all, valid, sink_ref).astype(BF16)


SWA_SAMPLE_SEQS = 8


def _swa_sample_body(sink_ref, q_ref, kn_ref, vn_ref, kc_ref, vc_ref, o_ref, ko_ref, vo_ref):
    rows = SWA_SAMPLE_SEQS * DEC_SEQ
    n_cache = SWA_SAMPLE_SEQS * WINDOW
    q = q_ref[...]
    kn = kn_ref[...]
    vn = vn_ref[...].astype(F32)
    k_all = jnp.concatenate([kc_ref[...].reshape(n_cache, SWA_NK), kn], axis=0)
    v_all = jnp.concatenate([vc_ref[...].reshape(n_cache, SWA_NK), vn], axis=0)
    for b in range(SWA_SAMPLE_SEQS):
        new = slice(b * DEC_SEQ, (b + 1) * DEC_SEQ)
        ko_ref[b, :WINDOW - DEC_SEQ, :] = kc_ref[b, DEC_SEQ:, :]
        ko_ref[b, WINDOW - DEC_SEQ:, :] = kn[new]
        vo_ref[b, :WINDOW - DEC_SEQ, :] = vc_ref[b, DEC_SEQ:, :]
        vo_ref[b, WINDOW - DEC_SEQ:, :] = vn[new]
    shape = (SWA_GROUP * rows, n_cache + rows)
    r = _mod_pow2(lax.broadcasted_iota(jnp.int32, shape, 0), rows)
    c = lax.broadcasted_iota(jnp.int32, shape, 1)
    q_seq, q_t = _div_pow2(r, DEC_SEQ), _mod_pow2(r, DEC_SEQ)
    is_new = c >= n_cache
    k_seq = jnp.where(is_new, _div_pow2(c - n_cache, DEC_SEQ), _div_pow2(c, WINDOW))
    k_slot = jnp.where(is_new, WINDOW + _mod_pow2(c - n_cache, DEC_SEQ), _mod_pow2(c, WINDOW))
    rel = q_t + WINDOW - k_slot
    valid = (q_seq == k_seq) & (rel >= 0) & (rel <= WINDOW)
    o_ref[...] = _sink_attention(q, k_all, v_all, valid, sink_ref).astype(BF16)


def swa_core(proj, k_rot, sinks, cache_k, cache_v):
    nb = SEQ // SWA_BLOCK
    smem = pl.BlockSpec(memory_space=pltpu.SMEM)
    kcol = SWA_NQ // SWA_NK
    cur = lambda b, i: b * nb + i
    prev = lambda b, i: b * nb + jnp.maximum(i - 1, 0)
    o = pl.pallas_call(
        _swa_prompt_body,
        grid=(BATCH, nb),
        in_specs=[
            smem,
            pl.BlockSpec((SWA_BLOCK, SWA_NQ), lambda b, i: (cur(b, i), 0)),
            pl.BlockSpec((SWA_BLOCK, SWA_NK), lambda b, i: (cur(b, i), kcol)),
            pl.BlockSpec((SWA_BLOCK, SWA_NK), lambda b, i: (cur(b, i), kcol + 1)),
            pl.BlockSpec((SWA_BLOCK, SWA_NK), lambda b, i: (prev(b, i), kcol)),
            pl.BlockSpec((SWA_BLOCK, SWA_NK), lambda b, i: (prev(b, i), kcol + 1)),
        ],
        out_specs=pl.BlockSpec((SWA_BLOCK, SWA_NQ), lambda b, i: (cur(b, i), 0)),
        out_shape=jax.ShapeDtypeStruct((NP, SWA_NQ), BF16),
        compiler_params=_cparams("parallel", "arbitrary"),
        name="swa_prompt",
    )(sinks, proj, proj, proj, proj, proj)

    rows = SWA_SAMPLE_SEQS * DEC_SEQ
    row0 = NP // rows
    o_s, k_s, v_s = pl.pallas_call(
        _swa_sample_body,
        grid=(DEC_BATCH // SWA_SAMPLE_SEQS,),
        in_specs=[
            smem,
            pl.BlockSpec((rows, SWA_NQ), lambda g: (row0 + g, 0)),
            pl.BlockSpec((rows, SWA_NK), lambda g: (row0 + g, 0)),
            pl.BlockSpec((rows, SWA_NK), lambda g: (row0 + g, kcol + 1)),
            pl.BlockSpec((SWA_SAMPLE_SEQS, WINDOW, SWA_NK), lambda g: (g, 0, 0)),
            pl.BlockSpec((SWA_SAMPLE_SEQS, WINDOW, SWA_NK), lambda g: (g, 0, 0)),
        ],
        out_specs=[
            pl.BlockSpec((rows, SWA_NQ), lambda g: (g, 0)),
            pl.BlockSpec((SWA_SAMPLE_SEQS, WINDOW, SWA_NK), lambda g: (g, 0, 0)),
            pl.BlockSpec((SWA_SAMPLE_SEQS, WINDOW, SWA_NK), lambda g: (g, 0, 0)),
        ],
        out_shape=[jax.ShapeDtypeStruct((NS, SWA_NQ), BF16),
                   jax.ShapeDtypeStruct((DEC_BATCH, WINDOW, SWA_NK), F32),
                   jax.ShapeDtypeStruct((DEC_BATCH, WINDOW, SWA_NK), F32)],
        compiler_params=_cparams("parallel"),
        name="swa_sample",
    )(sinks, proj, k_rot, proj, cache_k, cache_v)
    return o, o_s, k_s, v_s


RWKV_ROW_TILE = 256
RWKV_PAIRS = RWKV_HEADS // 2
RWKV_STACK = MXU_DIM_V7X
RWKV_SAMPLE_SEQS = RWKV_STACK // (RWKV_PAIRS * DEC_SEQ)


def _block_diag_ones():
    r = jnp.arange(MXU_DIM_V7X) // RWKV_HD
    return (r[:, None] == r[None, :]).astype(BF16)


def _store_pairs(ref, val):
    for p in range(RWKV_PAIRS):
        ref[p] = val[:, p * LANES:(p + 1) * LANES].astype(ref.dtype)


def _load_pairs(ref):
    return jnp.concatenate([ref[p] for p in range(RWKV_PAIRS)], axis=-1)


def _rwkv_proj_body(x_ref, xprev_ref, first_ref, gn_ref, mu_ref, wr_ref, wk_ref, wv_ref, w0_ref, w1_ref, w2_ref,
                    a0_ref, a1_ref, a2_ref, g1_ref, g2_ref, kk_ref, ka_ref, bd_ref,
                    r_out, lw_out, k_out, v_out, kk_out, b_out, g_out, hlast_out, h_scr):
    i = pl.program_id(0)
    h = _rms(x_ref[...], gn_ref[...])
    prev_tile_last = _rms(xprev_ref[...], gn_ref[...])[SUBLANES - 1:SUBLANES, :]
    rowid = lax.broadcasted_iota(jnp.int32, h.shape, 0)
    xp = jnp.where(rowid == 0, prev_tile_last, pltpu.roll(h, 1, 0))
    seq_len_mask = jnp.where(i >= NP // RWKV_ROW_TILE, DEC_SEQ - 1, SEQ - 1)
    is_first = jnp.bitwise_and(i * RWKV_ROW_TILE + rowid, seq_len_mask) == 0
    xp = jnp.where(is_first, first_ref[...], xp)
    _store_pairs(h_scr, h)
    every8th = pl.ds(SUBLANES - 1, RWKV_ROW_TILE // SUBLANES, stride=SUBLANES)
    hlast_out[...] = jnp.concatenate([h_scr[p, every8th, :] for p in range(RWKV_PAIRS)], axis=-1)
    d = xp - h
    xs = [h + d * mu_ref[i:i + 1, :] for i in range(6)]
    _store_pairs(r_out, _dot(xs[0], wr_ref[...]))
    k = _dot(xs[1], wk_ref[...])
    _store_pairs(v_out, _dot(xs[2], wv_ref[...]))
    z = w0_ref[...] + _dot(jnp.tanh(_dot(xs[3], w1_ref[...])), w2_ref[...])
    softplus = jnp.maximum(-z, 0.0) + jnp.log1p(jnp.exp(-jnp.abs(z)))
    _store_pairs(lw_out, -jnp.exp(-softplus - 0.5))
    a = jax.nn.sigmoid(a0_ref[...] + _dot(_dot(xs[4], a1_ref[...]), a2_ref[...]))
    g_out[...] = _dot(jax.nn.sigmoid(_dot(xs[5], g1_ref[...])), g2_ref[...]).astype(BF16)
    kk = k * kk_ref[...]
    norm = jnp.sqrt(_seg_sum(kk * kk, bd_ref[...], 2))
    kk = kk / jnp.maximum(norm, 1e-12)
    _store_pairs(kk_out, kk)
    _store_pairs(b_out, kk * a)
    _store_pairs(k_out, k * (1.0 + (a - 1.0) * ka_ref[...]))


def rwkv_proj(x, shift_s, p):
    first = jnp.concatenate([
        jnp.zeros((RWKV_ROW_TILE, D_MODEL), F32),
        jnp.pad(shift_s[:, None, :], ((0, 0), (0, DEC_SEQ - 1), (0, 0))).reshape(NS, D_MODEL)], axis=0)
    prompt_tiles = NP // RWKV_ROW_TILE
    row = pl.BlockSpec((RWKV_ROW_TILE, D_MODEL), lambda i: (i, 0))
    prev = pl.BlockSpec((SUBLANES, D_MODEL), lambda i: (jnp.maximum(i * (RWKV_ROW_TILE // SUBLANES) - 1, 0), 0))
    first_spec = pl.BlockSpec((RWKV_ROW_TILE, D_MODEL), lambda i: (jnp.maximum(i - prompt_tiles + 1, 0), 0))
    pair = pl.BlockSpec((RWKV_PAIRS, RWKV_ROW_TILE, LANES), lambda i: (0, i, 0))
    full = lambda a: pl.BlockSpec(a.shape, lambda i: (0,) * a.ndim)
    consts = [p["g_mix"], p["mu"], p["wr"], p["wk"], p["wv"], p["w0"], p["w1"], p["w2"], p["a0"], p["a1"],
              p["a2"], p["g1"], p["g2"], p["k_k"], p["k_a"], p["bd"]]
    pair_shape = lambda dt: jax.ShapeDtypeStruct((RWKV_PAIRS, NT, LANES), dt)
    return pl.pallas_call(
        _rwkv_proj_body,
        grid=(NT // RWKV_ROW_TILE,),
        in_specs=[row, prev, first_spec] + [full(c) for c in consts],
        out_specs=[pair] * 6 + [row, pl.BlockSpec((RWKV_ROW_TILE // SUBLANES, D_MODEL), lambda i: (i, 0))],
        out_shape=[pair_shape(BF16), pair_shape(F32)] + [pair_shape(BF16)] * 4
        + [jax.ShapeDtypeStruct((NT, D_MODEL), BF16), jax.ShapeDtypeStruct((NT // SUBLANES, D_MODEL), F32)],
        scratch_shapes=[pltpu.VMEM((RWKV_PAIRS, RWKV_ROW_TILE, LANES), F32)],
        compiler_params=_cparams("parallel"),
        name="rwkv_proj",
    )(x, x, first, *consts)


def _rwkv_chunk(c_len, r, lw, k, v, kk, b, state, store):
    n_grp = RWKV_STACK // c_len
    shift = int(math.log2(c_len))
    assert 2 ** shift == c_len
    row = lax.broadcasted_iota(jnp.int32, (RWKV_STACK, RWKV_STACK), 0)
    col = lax.broadcasted_iota(jnp.int32, (RWKV_STACK, RWKV_STACK), 1)
    same = jnp.right_shift(row, shift) == jnp.right_shift(col, shift)
    strict = same & (row > col)
    incl = same & (row >= col)
    tri = jnp.where(incl, 1.0, 0.0).astype(BF16)
    lam = sum(jnp.dot(tri, part, preferred_element_type=F32) for part in _split3(lw))
    lam3 = lam.reshape(n_grp, c_len, LANES)
    lam_end3 = lam3[:, c_len - 1:c_len, :]
    e_end = jnp.exp(lam_end3 - lam3).reshape(RWKV_STACK, LANES)
    e_tot3 = jnp.exp(lam_end3)
    e_neg = jnp.exp(-lam)
    at = -kk * jnp.exp(lam - lw)
    rt = r * jnp.exp(lam)
    bh = b * e_neg
    kh = k * e_neg
    bb = b * e_end
    kb = k * e_end

    lane = lax.broadcasted_iota(jnp.int32, (RWKV_STACK, LANES), 1)
    halves = (lane < RWKV_HD, lane >= RWKV_HD)
    rhs = jnp.concatenate([bh, kh], axis=0)
    a_pow, b_low, r_low = [], [], []
    for sel in halves:
        lhs = jnp.concatenate([jnp.where(sel, at, 0.0), jnp.where(sel, rt, 0.0)], axis=0)
        m = _dot_nt(lhs, rhs)
        a_pow.append(jnp.where(strict, m[:RWKV_STACK, :RWKV_STACK], 0.0))
        b_low.append(jnp.where(strict, m[:RWKV_STACK, RWKV_STACK:], 0.0))
        r_low.append(jnp.concatenate([jnp.where(incl, m[RWKV_STACK:, :RWKV_STACK], 0.0),
                                      jnp.where(incl, m[RWKV_STACK:, RWKV_STACK:], 0.0)], axis=1))

    rows = lambda t, g: t[g * c_len:(g + 1) * c_len]
    states = [state(g) for g in range(n_grp)]
    a_s, r_s = [], []
    for g in range(n_grp):
        p = _dot_nt(jnp.concatenate([rows(at, g), rows(rt, g)], axis=0), states[g])
        a_s.append(p[:c_len])
        r_s.append(p[c_len:])
    u = jnp.concatenate(a_s, axis=0)
    for e, sel in enumerate(halves):
        u = u + jnp.where(sel, _dot(b_low[e], v), 0.0)
    for it in range(shift):
        u = u + sum(jnp.where(sel, _dot(a_pow[e], u), 0.0) for e, sel in enumerate(halves))
        if it + 1 < shift:
            a_pow = [_dot(a, a) for a in a_pow]
    uv = jnp.concatenate([u, v], axis=0)
    y = jnp.concatenate(r_s, axis=0)
    for e, sel in enumerate(halves):
        y = y + jnp.where(sel, _dot(r_low[e], uv), 0.0)

    r128 = lax.broadcasted_iota(jnp.int32, (LANES, LANES), 0)
    c128 = lax.broadcasted_iota(jnp.int32, (LANES, LANES), 1)
    diag = (r128 < RWKV_HD) == (c128 < RWKV_HD)
    for g in range(n_grp):
        upd = _dot_tn(jnp.concatenate([rows(u, g), rows(v, g)], axis=0),
                      jnp.concatenate([rows(bb, g), rows(kb, g)], axis=0))
        store(g, states[g] * e_tot3[g] + jnp.where(diag, upd, 0.0))
    return y


RWKV_PROMPT_SEQS = 2
RWKV_STEP_TOKENS = 128


def _stacked(ref):
    return ref[...].reshape(RWKV_STACK, LANES).astype(F32)


def _rwkv_prompt_body(*refs):
    ins, (y_ref, s_ref, s_scr) = refs[:6 * RWKV_PROMPT_SEQS], refs[6 * RWKV_PROMPT_SEQS:]
    c = pl.program_id(1)

    @pl.when(c == 0)
    def _():
        s_scr[...] = jnp.zeros_like(s_scr)

    def one_chunk(ci, carry):
        rows = pl.ds(pl.multiple_of(ci * RWKV_CHUNK, RWKV_CHUNK), RWKV_CHUNK)
        for seq in range(RWKV_PROMPT_SEQS):
            def store(g, s):
                s_scr[seq, g] = s

            args = [ins[a * RWKV_PROMPT_SEQS + seq][:, rows, :].reshape(RWKV_STACK, LANES).astype(F32)
                    for a in range(6)]
            y = _rwkv_chunk(RWKV_CHUNK, *args, lambda g: s_scr[seq, g], store)
            y_ref[:, seq, rows, :] = y.reshape(RWKV_PAIRS, RWKV_CHUNK, LANES)
        return carry

    lax.fori_loop(0, RWKV_STEP_TOKENS // RWKV_CHUNK, one_chunk, 0)

    @pl.when(c == pl.num_programs(1) - 1)
    def _():
        for seq in range(RWKV_PROMPT_SEQS):
            for p in range(RWKV_PAIRS):
                s_ref[seq, 2 * p] = s_scr[seq, p, :RWKV_HD, :RWKV_HD]
                s_ref[seq, 2 * p + 1] = s_scr[seq, p, RWKV_HD:, RWKV_HD:]


def _rwkv_sample_body(r_ref, lw_ref, k_ref, v_ref, kk_ref, b_ref, s0_ref, y_ref, s_ref):
    zero = jnp.zeros((RWKV_HD, RWKV_HD), F32)

    def state(g):
        p, seq = divmod(g, RWKV_SAMPLE_SEQS)
        return jnp.concatenate([jnp.concatenate([s0_ref[seq, 2 * p], zero], axis=1),
                                jnp.concatenate([zero, s0_ref[seq, 2 * p + 1]], axis=1)], axis=0)

    def store(g, s):
        p, seq = divmod(g, RWKV_SAMPLE_SEQS)
        s_ref[seq, 2 * p] = s[:RWKV_HD, :RWKV_HD]
        s_ref[seq, 2 * p + 1] = s[RWKV_HD:, RWKV_HD:]

    args = [_stacked(ref) for ref in (r_ref, lw_ref, k_ref, v_ref, kk_ref, b_ref)]
    y_ref[...] = _rwkv_chunk(DEC_SEQ, *args, state, store).reshape(y_ref.shape)


def rwkv_core(r, lw, k, v, kk, b, state_s):
    assert RWKV_PAIRS * RWKV_CHUNK == RWKV_STACK
    nc = SEQ // RWKV_STEP_TOKENS
    st_shape = (RWKV_HEADS, RWKV_HD, RWKV_HD)
    seq_blk = lambda seq: pl.BlockSpec((RWKV_PAIRS, RWKV_STEP_TOKENS, LANES),
                                       lambda bi, c: (0, (bi * RWKV_PROMPT_SEQS + seq) * nc + c, 0))
    inputs = (r, lw, k, v, kk, b)
    y_p, s_p = pl.pallas_call(
        _rwkv_prompt_body,
        grid=(BATCH // RWKV_PROMPT_SEQS, nc),
        in_specs=[seq_blk(seq) for _ in inputs for seq in range(RWKV_PROMPT_SEQS)],
        out_specs=[pl.BlockSpec((RWKV_PAIRS, RWKV_PROMPT_SEQS, RWKV_STEP_TOKENS, LANES), lambda bi, c: (0, bi, c, 0)),
                   pl.BlockSpec((RWKV_PROMPT_SEQS,) + st_shape, lambda bi, c: (bi, 0, 0, 0))],
        out_shape=[jax.ShapeDtypeStruct((RWKV_PAIRS, BATCH, SEQ, LANES), F32),
                   jax.ShapeDtypeStruct((BATCH,) + st_shape, F32)],
        scratch_shapes=[pltpu.VMEM((RWKV_PROMPT_SEQS, RWKV_PAIRS, LANES, LANES), F32)],
        compiler_params=_cparams("parallel", "arbitrary"),
        name="rwkv_prompt",
    )(*[a for a in inputs for _ in range(RWKV_PROMPT_SEQS)])

    rows = RWKV_SAMPLE_SEQS * DEC_SEQ
    row0 = NP // rows
    blk = pl.BlockSpec((RWKV_PAIRS, rows, LANES), lambda i: (0, row0 + i, 0))
    st = pl.BlockSpec((RWKV_SAMPLE_SEQS,) + st_shape, lambda i: (i, 0, 0, 0))
    y_s, s_s = pl.pallas_call(
        _rwkv_sample_body,
        grid=(DEC_BATCH // RWKV_SAMPLE_SEQS,),
        in_specs=[blk] * 6 + [st],
        out_specs=[pl.BlockSpec((RWKV_PAIRS, rows, LANES), lambda i: (0, i, 0)), st],
        out_shape=[jax.ShapeDtypeStruct((RWKV_PAIRS, NS, LANES), F32),
                   jax.ShapeDtypeStruct((DEC_BATCH,) + st_shape, F32)],
        compiler_params=_cparams("parallel"),
        name="rwkv_sample",
    )(*inputs, state_s)
    return y_p.reshape(RWKV_PAIRS, NP, LANES), y_s, s_p, s_s


def _rwkv_out_body(x_ref, yp_ref, ys_ref, r_ref, k_ref, v_ref, g_ref, rk_ref, lnw_ref, lnb_ref, bd_ref, wo_ref,
                   o_ref):
    bd = bd_ref[...]
    y = _pick_group(RWKV_ROW_TILE, _load_pairs(yp_ref), _load_pairs(ys_ref))
    yc = y - _seg_sum(y, bd, 2) * (1.0 / RWKV_HD)
    var = _seg_sum(yc * yc, bd, 1) * (1.0 / RWKV_HD)
    yn = yc * lax.rsqrt(var + RWKV_GN_EPS) * lnw_ref[...] + lnb_ref[...]
    rk = _load_pairs(r_ref).astype(F32) * _load_pairs(k_ref).astype(F32)
    bonus = _seg_sum(rk * rk_ref[...], bd, 2) * _load_pairs(v_ref).astype(F32)
    z = (yn + bonus) * g_ref[...].astype(F32)
    o_ref[...] = x_ref[...] + _dot(z, wo_ref[...])


def rwkv_out(x, y_p, y_s, r, k, v, g, p):
    row = pl.BlockSpec((RWKV_ROW_TILE, D_MODEL), lambda i: (i, 0))
    pair_of = lambda f: pl.BlockSpec((RWKV_PAIRS, RWKV_ROW_TILE, LANES), lambda i: (0, f(i), 0))
    full = lambda a: pl.BlockSpec(a.shape, lambda i: (0,) * a.ndim)
    consts = [p["r_k"], p["ln_w"], p["ln_b"], p["bd"], p["w_out"]]
    return pl.pallas_call(
        _rwkv_out_body,
        grid=(NT // RWKV_ROW_TILE,),
        in_specs=[row, *_group_specs(RWKV_ROW_TILE, pair_of)] + [pair_of(lambda i: i)] * 3 + [row]
        + [full(c) for c in consts],
        out_specs=row,
        out_shape=jax.ShapeDtypeStruct((NT, D_MODEL), F32),
        compiler_params=_cparams("parallel"),
        name="rwkv_out",
    )(x, y_p, y_s, r, k, v, g, *consts)


def _pad_cols(w, n):
    return jnp.pad(w, ((0, 0), (0, n - w.shape[1])))


def _pad_rows(w, n):
    return jnp.pad(w, ((0, n - w.shape[0]), (0, 0)))


def kernel(x_prompt, x_sample, state_ret, cache_swa_k, cache_swa_v, state_rwkv_wkv, state_rwkv_shift,
           norm_mix, norm_mlp, norm_final, ret_w_in, ret_w_out,
           swa_w_in, swa_b_in, swa_sinks, swa_w_out,
           rwkv_mu, rwkv_w_rkv, rwkv_w0, rwkv_w1, rwkv_w2, rwkv_a0, rwkv_a1, rwkv_a2,
           rwkv_g1, rwkv_g2, rwkv_k_k, rwkv_k_a, rwkv_r_k, rwkv_ln_w, rwkv_ln_b, rwkv_w_out,
           mlp_w_up, mlp_w_down):
    x = jnp.concatenate([x_prompt.reshape(NP, D_MODEL), x_sample.reshape(NS, D_MODEL)], axis=0)
    row = lambda t: t.reshape(1, -1)
    ret_p, k_p, k_s, v_p, v_s, wkv_p, wkv_s, sh_p, sh_s = ([] for _ in range(9))
    ret_s = None
    for layer in range(DEPTH):
        kind = LAYER_KINDS[layer]
        j = LAYER_KINDS[:layer].count(kind)
        g_mix = row(norm_mix[layer])
        if kind == 0:
            proj = ret_proj(x, g_mix, ret_w_in[j].astype(BF16))
            o_p, o_s, s_p, ret_s = retention_core(proj, state_ret, j, ret_s)
            x = matmul_residual(x, o_p, o_s, ret_w_out[j].astype(BF16))
            ret_p.append(s_p)
        elif kind == 1:
            proj, k_rot = swa_proj(x, g_mix, swa_w_in[j].astype(BF16), row(swa_b_in[j]))
            o_p, o_s, kc, vc = swa_core(proj, k_rot, swa_sinks[j],
                                        cache_swa_k[j].reshape(DEC_BATCH, WINDOW, SWA_NK),
                                        cache_swa_v[j].reshape(DEC_BATCH, WINDOW, SWA_NK))
            x = matmul_residual(x, o_p, o_s, swa_w_out[j].astype(BF16))
            kv_shape = (-1, WINDOW, SWA_KV_HEADS, SWA_HD)
            k_p.append(k_rot[:NP].reshape(BATCH, SEQ, SWA_NK)[:, -WINDOW:].reshape(kv_shape))
            v_p.append(proj[:NP, SWA_NQ + SWA_NK:].reshape(BATCH, SEQ, SWA_NK)[:, -WINDOW:]
                       .astype(F32).reshape(kv_shape))
            k_s.append(kc.reshape(kv_shape))
            v_s.append(vc.reshape(kv_shape))
        else:
            lora_w = LANES
            lora_g = 2 * LANES
            p = dict(
                mu=rwkv_mu[j], wr=rwkv_w_rkv[j, 0].astype(BF16), wk=rwkv_w_rkv[j, 1].astype(BF16),
                wv=rwkv_w_rkv[j, 2].astype(BF16), w0=row(rwkv_w0[j]),
                w1=_pad_cols(rwkv_w1[j], lora_w).astype(BF16), w2=_pad_rows(rwkv_w2[j], lora_w).astype(BF16),
                a0=row(rwkv_a0[j]),
                a1=_pad_cols(rwkv_a1[j], lora_w).astype(BF16), a2=_pad_rows(rwkv_a2[j], lora_w).astype(BF16),
                g1=_pad_cols(rwkv_g1[j], lora_g).astype(BF16), g2=_pad_rows(rwkv_g2[j], lora_g).astype(BF16),
                k_k=row(rwkv_k_k[j]), k_a=row(rwkv_k_a[j]), r_k=row(rwkv_r_k[j]),
                ln_w=row(rwkv_ln_w[j]), ln_b=row(rwkv_ln_b[j]), bd=_block_diag_ones(),
                w_out=rwkv_w_out[j].astype(BF16), g_mix=g_mix)
            r, lw, k, v, kk, b, g, h_last = rwkv_proj(x, state_rwkv_shift[j], p)
            y_p, y_s, s_p, s_s = rwkv_core(r, lw, k, v, kk, b, state_rwkv_wkv[j])
            x = rwkv_out(x, y_p, y_s, r, k, v, g, p)
            wkv_p.append(s_p)
            wkv_s.append(s_s)
            sh_p.append(h_last[SEQ // SUBLANES - 1:NP // SUBLANES:SEQ // SUBLANES])
            sh_s.append(h_last[NP // SUBLANES:])
        x = mlp(x, row(norm_mlp[layer]), mlp_w_up[layer].astype(BF16), mlp_w_down[layer].astype(BF16))
    y_p = rmsnorm_rows(x, row(norm_final), 0, NP)
    y_s = rmsnorm_rows(x, row(norm_final), NP, NS)
    return (y_p.reshape(BATCH, SEQ, D_MODEL), y_s.reshape(DEC_BATCH, DEC_SEQ, D_MODEL),
            jnp.stack(ret_p), ret_s, jnp.stack(k_p), jnp.stack(k_s), jnp.stack(v_p), jnp.stack(v_s),
            jnp.stack(wkv_p), jnp.stack(wkv_s), jnp.stack(sh_p), jnp.stack(sh_s))
```

```python
import functools
import math

import jax
import jax.numpy as jnp
from jax import lax
from jax.experimental import pallas as pl
from jax.experimental.pallas import tpu as pltpu

F32 = jnp.float32
BF16 = jnp.bfloat16

D_MODEL = 1024
BATCH = 8
SEQ = 2048
DEPTH = 4
DEC_BATCH = 128
DEC_SEQ = 8
PAST_LEN = 8192
LAYER_KINDS = tuple(i % 3 for i in range(DEPTH))
NORM_EPS = 1e-6
ROPE_THETA = 10000.0
NEG_INF = -1e30
D_FF = 4 * D_MODEL

RET_HEADS = 4
RET_DK = D_MODEL // RET_HEADS
RET_DV = 2 * D_MODEL // RET_HEADS
RET_CHUNK = 128
RET_PROMPT_CHUNK = 256
RET_NQ = RET_HEADS * RET_DK
RET_NV = RET_HEADS * RET_DV

SWA_HEADS = 16
SWA_KV_HEADS = 4
SWA_GROUP = SWA_HEADS // SWA_KV_HEADS
SWA_HD = D_MODEL // SWA_HEADS
WINDOW = 128
SWA_BLOCK = 128
SWA_NQ = SWA_HEADS * SWA_HD
SWA_NK = SWA_KV_HEADS * SWA_HD

RWKV_HD = 64
RWKV_HEADS = D_MODEL // RWKV_HD
RWKV_GN_EPS = 64e-5
RWKV_CHUNK = 32

NP = BATCH * SEQ
NS = DEC_BATCH * DEC_SEQ
NT = NP + NS

VMEM_LIMIT_V7X = 48 * 1024 * 1024
LANES = 128
SUBLANES = 8
MXU_DIM_V7X = 256

ROW_TILE = 1024


def _cparams(*sem):
    return pltpu.CompilerParams(dimension_semantics=sem, vmem_limit_bytes=VMEM_LIMIT_V7X)


def _dot(a, b):
    return jnp.dot(a.astype(BF16), b.astype(BF16), preferred_element_type=F32)


def _dot_nt(a, b):
    return lax.dot_general(a.astype(BF16), b.astype(BF16), (((1,), (1,)), ((), ())),
                           preferred_element_type=F32)


def _dot_tn(a, b):
    return lax.dot_general(a.astype(BF16), b.astype(BF16), (((0,), (0,)), ((), ())),
                           preferred_element_type=F32)


def _div_pow2(x, n):
    assert n & (n - 1) == 0
    return jnp.right_shift(x, n.bit_length() - 1)


def _mod_pow2(x, n):
    assert n & (n - 1) == 0
    return jnp.bitwise_and(x, n - 1)


def _rms(x, g):
    ms = jnp.mean(x * x, axis=-1, keepdims=True)
    return x * lax.rsqrt(ms + NORM_EPS) * g


def _split3(x):
    hi = x.astype(BF16)
    r1 = x - hi.astype(F32)
    mid = r1.astype(BF16)
    lo = (r1 - mid.astype(F32)).astype(BF16)
    return hi, mid, lo


def _seg_sum(x, bd, passes):
    parts = _split3(x)[:passes]
    cols = []
    for j in range(D_MODEL // MXU_DIM_V7X):
        sl = slice(j * MXU_DIM_V7X, (j + 1) * MXU_DIM_V7X)
        acc = jnp.dot(parts[0][:, sl], bd, preferred_element_type=F32)
        for p in parts[1:]:
            acc = acc + jnp.dot(p[:, sl], bd, preferred_element_type=F32)
        cols.append(acc)
    return jnp.concatenate(cols, axis=-1)


def _rope_tables(pos, half):
    inv_freq = jnp.power(ROPE_THETA, -jnp.arange(half, dtype=F32) / half)
    ang = pos.astype(F32)[:, None] * inv_freq[None, :]
    return jnp.cos(ang), jnp.sin(ang)


def _group_specs(tile, block_of):
    n_prompt = NP // tile
    return (block_of(lambda i: jnp.minimum(i, n_prompt - 1)), block_of(lambda i: jnp.maximum(i - n_prompt, 0)))


def _pick_group(tile, p_val, s_val):
    return jnp.where(pl.program_id(0) < NP // tile, p_val, s_val)


def _token_rows(x, tile):
    if isinstance(x, tuple):
        return list(x), list(_group_specs(tile, lambda f: pl.BlockSpec((tile, D_MODEL), lambda *g: (f(g[0]), 0))))
    return [x], [pl.BlockSpec((tile, D_MODEL), lambda *g: (g[0], 0))]


def _read_rows(refs, tile):
    return refs[0][...] if len(refs) == 1 else _pick_group(tile, refs[0][...], refs[1][...])


def _rope_table_inputs(half, widen, tile):
    tp = widen(*_rope_tables(jnp.arange(SEQ, dtype=jnp.int32), half))
    ts = [jnp.tile(t, (DEC_BATCH, 1))
          for t in widen(*_rope_tables(PAST_LEN + jnp.arange(DEC_SEQ, dtype=jnp.int32), half))]
    n_prompt = NP // tile
    per_seq = SEQ // tile
    p_spec = pl.BlockSpec((tile, LANES), lambda *g: (g[0] % per_seq, 0))
    s_spec = pl.BlockSpec((tile, LANES), lambda *g: (jnp.maximum(g[0] - n_prompt, 0), 0))
    return [tp[0], tp[1], ts[0], ts[1]], [p_spec, p_spec, s_spec, s_spec]


RET_PROJ_TILE = RET_NQ + RET_NQ


def _ret_proj_body(n_x, *refs):
    x_refs, (g_ref, w_ref, cosp_ref, sinp_ref, coss_ref, sins_ref, o_ref, xn_ref) = refs[:n_x], refs[n_x:]
    j = pl.program_id(1)

    @pl.when(j == 0)
    def _():
        xn_ref[...] = _rms(_read_rows(x_refs, ROW_TILE), g_ref[...]).astype(BF16)

    acc = jnp.dot(xn_ref[...], w_ref[...], preferred_element_type=F32)

    @pl.when(j == 0)
    def _():
        cos = _pick_group(ROW_TILE, cosp_ref[...], coss_ref[...])
        sin = _pick_group(ROW_TILE, sinp_ref[...], sins_ref[...])
        half = RET_DK // 2
        for blk in range(RET_PROJ_TILE // RET_DK):
            lo, mid, hi = blk * RET_DK, blk * RET_DK + half, (blk + 1) * RET_DK
            x1, x2 = acc[:, lo:mid], acc[:, mid:hi]
            scale = 1.0 if blk < RET_HEADS else RET_DK ** -0.5
            o_ref[:, lo:mid] = ((x1 * cos - x2 * sin) * scale).astype(BF16)
            o_ref[:, mid:hi] = ((x2 * cos + x1 * sin) * scale).astype(BF16)

    @pl.when(j > 0)
    def _():
        o_ref[...] = acc.astype(BF16)


def ret_proj(x, g, w):
    n_out = w.shape[1]
    x_arrays, x_specs = _token_rows(x, ROW_TILE)
    tables, table_specs = _rope_table_inputs(RET_DK // 2, lambda c, s: (c, s), ROW_TILE)
    return pl.pallas_call(
        functools.partial(_ret_proj_body, len(x_arrays)),
        grid=(NT // ROW_TILE, n_out // RET_PROJ_TILE),
        in_specs=x_specs + [
            pl.BlockSpec((1, D_MODEL), lambda i, j: (0, 0)),
            pl.BlockSpec((D_MODEL, RET_PROJ_TILE), lambda i, j: (0, j)),
        ] + table_specs,
        out_specs=pl.BlockSpec((ROW_TILE, RET_PROJ_TILE), lambda i, j: (i, j)),
        out_shape=jax.ShapeDtypeStruct((NT, n_out), BF16),
        scratch_shapes=[pltpu.VMEM((ROW_TILE, D_MODEL), BF16)],
        compiler_params=_cparams("parallel", "arbitrary"),
        name="ret_proj",
    )(*x_arrays, g, w, *tables)


SWA_PROJ_ROW_TILE = 512


def _swa_rope(x, cos, sin):
    n = x.shape[1]
    half = SWA_HD // 2
    lane = lax.broadcasted_iota(jnp.int32, x.shape, 1)
    partner = jnp.where(_mod_pow2(lane, SWA_HD) < half, pltpu.roll(x, n - half, 1), pltpu.roll(x, half, 1))
    reps = n // LANES
    return x * jnp.concatenate([cos] * reps, axis=-1) + partner * jnp.concatenate([sin] * reps, axis=-1)


def _swa_proj_body(x_ref, g_ref, w_ref, b_ref, cosp_ref, sinp_ref, coss_ref, sins_ref, o_ref, kr_ref):
    xn = _rms(x_ref[...], g_ref[...]).astype(BF16)
    acc = jnp.dot(xn, w_ref[...], preferred_element_type=F32) + b_ref[...]
    n_qk = SWA_NQ + SWA_NK
    cos = _pick_group(SWA_PROJ_ROW_TILE, cosp_ref[...], coss_ref[...])
    sin = _pick_group(SWA_PROJ_ROW_TILE, sinp_ref[...], sins_ref[...])
    qk = _swa_rope(acc[:, :n_qk], cos, sin)
    o_ref[:, :n_qk] = qk.astype(BF16)
    o_ref[:, n_qk:] = acc[:, n_qk:].astype(BF16)
    kr_ref[...] = qk[:, SWA_NQ:]


def swa_proj(x, g, w, b):
    n_out = w.shape[1]

    def widen(c, s):
        rep = LANES // SWA_HD
        return jnp.tile(jnp.concatenate([c, c], axis=-1), (1, rep)), jnp.tile(jnp.concatenate([-s, s], axis=-1), (1, rep))

    tile = SWA_PROJ_ROW_TILE
    tables, table_specs = _rope_table_inputs(SWA_HD // 2, widen, tile)
    return pl.pallas_call(
        _swa_proj_body,
        grid=(NT // tile,),
        in_specs=[
            pl.BlockSpec((tile, D_MODEL), lambda i: (i, 0)),
            pl.BlockSpec((1, D_MODEL), lambda i: (0, 0)),
            pl.BlockSpec((D_MODEL, n_out), lambda i: (0, 0)),
            pl.BlockSpec((1, n_out), lambda i: (0, 0)),
        ] + table_specs,
        out_specs=[pl.BlockSpec((tile, n_out), lambda i: (i, 0)), pl.BlockSpec((tile, SWA_NK), lambda i: (i, 0))],
        out_shape=[jax.ShapeDtypeStruct((NT, n_out), BF16), jax.ShapeDtypeStruct((NT, SWA_NK), F32)],
        compiler_params=_cparams("parallel"),
        name="swa_proj",
    )(x, g, w, b, *tables)


def _matmul_residual_body(n_x, *refs):
    x_refs, (ap_ref, as_ref, w_ref, o_ref) = refs[:n_x], refs[n_x:]
    a = _pick_group(ROW_TILE, ap_ref[...], as_ref[...])
    o_ref[...] = _read_rows(x_refs, ROW_TILE) + jnp.dot(a, w_ref[...], preferred_element_type=F32)


def matmul_residual(x, a_p, a_s, w):
    k = a_p.shape[1]
    x_arrays, x_specs = _token_rows(x, ROW_TILE)
    return pl.pallas_call(
        functools.partial(_matmul_residual_body, len(x_arrays)),
        grid=(NT // ROW_TILE,),
        in_specs=x_specs + [
            *_group_specs(ROW_TILE, lambda f: pl.BlockSpec((ROW_TILE, k), lambda i: (f(i), 0))),
            pl.BlockSpec((k, D_MODEL), lambda i: (0, 0)),
        ],
        out_specs=pl.BlockSpec((ROW_TILE, D_MODEL), lambda i: (i, 0)),
        out_shape=jax.ShapeDtypeStruct((NT, D_MODEL), F32),
        compiler_params=_cparams("parallel"),
        name="matmul_residual",
    )(*x_arrays, a_p, a_s, w)


MLP_FF_TILE = 1024


def _mlp_body(x_ref, g_ref, wu_ref, wd_ref, o_ref, xn_ref, acc_ref):
    f = pl.program_id(1)

    @pl.when(f == 0)
    def _():
        xn_ref[...] = _rms(x_ref[...], g_ref[...]).astype(BF16)
        acc_ref[...] = jnp.zeros_like(acc_ref)

    a = jnp.maximum(jnp.dot(xn_ref[...], wu_ref[...], preferred_element_type=F32), 0.0)
    acc_ref[...] += jnp.dot((a * a).astype(BF16), wd_ref[...], preferred_element_type=F32)

    @pl.when(f == pl.num_programs(1) - 1)
    def _():
        o_ref[...] = x_ref[...] + acc_ref[...]


def mlp(x, g, w_up, w_down):
    return pl.pallas_call(
        _mlp_body,
        grid=(NT // ROW_TILE, D_FF // MLP_FF_TILE),
        in_specs=[
            pl.BlockSpec((ROW_TILE, D_MODEL), lambda i, f: (i, 0)),
            pl.BlockSpec((1, D_MODEL), lambda i, f: (0, 0)),
            pl.BlockSpec((D_MODEL, MLP_FF_TILE), lambda i, f: (0, f)),
            pl.BlockSpec((MLP_FF_TILE, D_MODEL), lambda i, f: (f, 0)),
        ],
        out_specs=pl.BlockSpec((ROW_TILE, D_MODEL), lambda i, f: (i, 0)),
        out_shape=jax.ShapeDtypeStruct((NT, D_MODEL), F32),
        scratch_shapes=[pltpu.VMEM((ROW_TILE, D_MODEL), BF16), pltpu.VMEM((ROW_TILE, D_MODEL), F32)],
        compiler_params=_cparams("parallel", "arbitrary"),
        name="mlp",
    )(x, g, w_up, w_down)


def _rmsnorm_body(x_ref, g_ref, o_ref):
    o_ref[...] = _rms(x_ref[...], g_ref[...])


def rmsnorm_rows(x, g, row0, n_rows):
    tile0 = row0 // ROW_TILE
    return pl.pallas_call(
        _rmsnorm_body,
        grid=(n_rows // ROW_TILE,),
        in_specs=[pl.BlockSpec((ROW_TILE, D_MODEL), lambda i: (tile0 + i, 0)),
                  pl.BlockSpec((1, D_MODEL), lambda i: (0, 0))],
        out_specs=pl.BlockSpec((ROW_TILE, D_MODEL), lambda i: (i, 0)),
        out_shape=jax.ShapeDtypeStruct((n_rows, D_MODEL), F32),
        compiler_params=_cparams("parallel"),
        name="rmsnorm",
    )(x, g)


def _ret_tables(c):
    log_gamma = jnp.log1p(-jnp.exp2(-5.0 - jnp.arange(RET_HEADS, dtype=F32)))
    idx = jnp.arange(c, dtype=F32)
    diff = idx[:, None] - idx[None, :]
    inner = jnp.where(diff >= 0, jnp.exp(log_gamma[:, None, None] * jnp.maximum(diff, 0.0)), 0.0)
    q_decay = jnp.exp(log_gamma[:, None] * (idx[None, :] + 1.0))
    k_decay = jnp.exp(log_gamma[:, None] * (c - 1.0 - idx[None, :]))
    chunk_decay = jnp.exp(log_gamma * c)
    rep = lambda t: jnp.broadcast_to(t[:, :, None], (RET_HEADS, c, LANES))
    return inner, rep(q_decay), rep(k_decay), chunk_decay


def _ret_heads(rows, lo, width):
    return jnp.stack([rows[:, lo + h * width:lo + (h + 1) * width] for h in range(RET_HEADS)])


def _ret_chunk(rows_list, s0, inner, qd, kd, cd_ref):
    n_seq = len(rows_list)
    cat = lambda f: jnp.concatenate([f(r) for r in rows_list], axis=0)
    q = cat(lambda r: _ret_heads(r, 0, RET_DK)).astype(BF16)
    k = cat(lambda r: _ret_heads(r, RET_NQ, RET_DK)).astype(F32)
    v = cat(lambda r: _ret_heads(r, 2 * RET_NQ, RET_DV)).astype(BF16)
    g = cat(lambda r: _ret_heads(r, 2 * RET_NQ + RET_NV, RET_DV)).astype(F32)
    per_seq = lambda t: jnp.concatenate([t] * n_seq, axis=0)
    inner, qd, kd = per_seq(inner), per_seq(qd), per_seq(kd)
    cd = per_seq(jnp.stack([jnp.full((1, 1), cd_ref[h], F32) for h in range(RET_HEADS)]))
    qd = jnp.concatenate([qd] * (RET_DV // LANES), axis=-1)
    kd = jnp.concatenate([kd] * (RET_DK // LANES), axis=-1)
    scores = jnp.einsum('bqd,bkd->bqk', q, k.astype(BF16), preferred_element_type=F32) * inner
    o = (jnp.einsum('bqk,bke->bqe', scores.astype(BF16), v, preferred_element_type=F32)
         + jnp.einsum('bqd,bde->bqe', q, s0.astype(BF16), preferred_element_type=F32) * qd)
    s_new = s0 * cd + jnp.einsum('bkd,bke->bde', (k * kd).astype(BF16), v, preferred_element_type=F32)
    o = o * lax.rsqrt(jnp.mean(o * o, axis=-1, keepdims=True) + NORM_EPS)
    return o * (g * jax.nn.sigmoid(g)), s_new


def _ret_prompt_body(cd_ref, proj_ref, inner_ref, qd_ref, kd_ref, o_ref, s_ref, s_scr):
    c = pl.program_id(1)

    @pl.when(c == 0)
    def _():
        s_scr[...] = jnp.zeros_like(s_scr)

    o, s_new = _ret_chunk([proj_ref[...]], s_scr[...], inner_ref[...], qd_ref[...], kd_ref[...], cd_ref)
    s_scr[...] = s_new
    o_ref[...] = jnp.concatenate([o[h] for h in range(RET_HEADS)], axis=-1).astype(BF16)

    @pl.when(c == pl.num_programs(1) - 1)
    def _():
        s_ref[0] = s_scr[...]


RET_SAMPLE_SEQS = 2


def _ret_sample_body(cd_ref, proj_ref, inner_ref, qd_ref, kd_ref, s0_ref, *rest):
    o_ref, s_ref = rest[-2:]
    rows = proj_ref[...].astype(F32)
    batch = RET_SAMPLE_SEQS * RET_HEADS
    o, s_new = _ret_chunk([rows[seq * DEC_SEQ:(seq + 1) * DEC_SEQ] for seq in range(RET_SAMPLE_SEQS)],
                          s0_ref[0].reshape(batch, RET_DK, RET_DV), inner_ref[...], qd_ref[...], kd_ref[...],
                          cd_ref)
    s_ref[0] = s_new.reshape(RET_SAMPLE_SEQS, RET_HEADS, RET_DK, RET_DV)
    o_ref[...] = jnp.concatenate(
        [jnp.concatenate([o[seq * RET_HEADS + h] for h in range(RET_HEADS)], axis=-1)
         for seq in range(RET_SAMPLE_SEQS)], axis=0).astype(BF16)


def retention_core(proj, state_ret, j, s_stack):
    n_in = proj.shape[1]
    nc = SEQ // RET_PROMPT_CHUNK
    smem = pl.BlockSpec(memory_space=pltpu.SMEM)
    inner, qd, kd, cd = _ret_tables(RET_PROMPT_CHUNK)
    full3 = lambda shape: pl.BlockSpec(shape, lambda b, c: (0, 0, 0))
    o, s_p = pl.pallas_call(
        _ret_prompt_body,
        grid=(BATCH, nc),
        in_specs=[
            smem,
            pl.BlockSpec((RET_PROMPT_CHUNK, n_in), lambda b, c: (b * nc + c, 0)),
            full3(inner.shape), full3(qd.shape), full3(kd.shape),
        ],
        out_specs=[
            pl.BlockSpec((RET_PROMPT_CHUNK, RET_NV), lambda b, c: (b * nc + c, 0)),
            pl.BlockSpec((1, RET_HEADS, RET_DK, RET_DV), lambda b, c: (b, 0, 0, 0)),
        ],
        out_shape=[jax.ShapeDtypeStruct((NP, RET_NV), BF16),
                   jax.ShapeDtypeStruct((BATCH, RET_HEADS, RET_DK, RET_DV), F32)],
        scratch_shapes=[pltpu.VMEM((RET_HEADS, RET_DK, RET_DV), F32)],
        compiler_params=_cparams("parallel", "arbitrary"),
        name="retention_prompt",
    )(cd, proj, inner, qd, kd)

    cs = math.gcd(DEC_SEQ, RET_CHUNK)
    assert cs == DEC_SEQ
    inner, qd, kd, cd = _ret_tables(cs)
    rows = RET_SAMPLE_SEQS * DEC_SEQ
    row0 = NP // rows
    full2 = lambda shape: pl.BlockSpec(shape, lambda b: (0, 0))
    full3 = lambda shape: pl.BlockSpec(shape, lambda b: (0, 0, 0))
    st = pl.BlockSpec((1, RET_SAMPLE_SEQS, RET_HEADS, RET_DK, RET_DV), lambda b: (j, b, 0, 0, 0))
    stacked = [] if s_stack is None else [s_stack]
    o_s, s_s = pl.pallas_call(
        _ret_sample_body,
        grid=(DEC_BATCH // RET_SAMPLE_SEQS,),
        in_specs=[
            smem,
            pl.BlockSpec((rows, n_in), lambda b: (row0 + b, 0)),
            full3(inner.shape), full3(qd.shape), full3(kd.shape),
            st,
        ] + [pl.BlockSpec(memory_space=pl.ANY)] * len(stacked),
        out_specs=[pl.BlockSpec((rows, RET_NV), lambda b: (b, 0)), st],
        out_shape=[jax.ShapeDtypeStruct((NS, RET_NV), BF16),
                   jax.ShapeDtypeStruct(state_ret.shape, F32)],
        input_output_aliases={6: 1} if stacked else {},
        compiler_params=_cparams("parallel"),
        name="retention_sample",
    )(cd, proj, inner, qd, kd, state_ret, *stacked)
    return o, o_s, s_p, s_s


def _sink_attention(q, k, v, valid, sink_ref):
    t = q.shape[0]
    head_cols = lambda x, h: x[:, h * SWA_HD:(h + 1) * SWA_HD]
    qs = jnp.stack([jnp.concatenate([head_cols(q, kh * SWA_GROUP + g) for g in range(SWA_GROUP)], axis=0)
                    for kh in range(SWA_KV_HEADS)]).astype(BF16)
    ks = jnp.stack([head_cols(k, kh) for kh in range(SWA_KV_HEADS)]).astype(BF16)
    vs = jnp.stack([head_cols(v, kh) for kh in range(SWA_KV_HEADS)]).astype(BF16)
    sink = jnp.stack([jnp.concatenate([jnp.full((t, 1), sink_ref[kh * SWA_GROUP + g], F32)
                                       for g in range(SWA_GROUP)], axis=0) for kh in range(SWA_KV_HEADS)])
    s = jnp.einsum('hqd,hkd->hqk', qs, ks, preferred_element_type=F32) * (SWA_HD ** -0.5)
    s = jnp.where(valid[None], s, NEG_INF)
    m = jnp.maximum(jnp.max(s, axis=-1, keepdims=True), sink)
    p = jnp.exp(s - m)
    denom = jnp.sum(p, axis=-1, keepdims=True) + jnp.exp(sink - m)
    o = jnp.einsum('hqk,hkd->hqd', p.astype(BF16), vs, preferred_element_type=F32) / denom
    return jnp.concatenate([o[h // SWA_GROUP, (h % SWA_GROUP) * t:(h % SWA_GROUP + 1) * t]
                            for h in range(SWA_HEADS)], axis=-1)


def _swa_prompt_body(sink_ref, q_ref, kc_ref, vc_ref, kp_ref, vp_ref, o_ref):
    blk = pl.program_id(1)
    q = q_ref[...]
    k_all = jnp.concatenate([kp_ref[...], kc_ref[...]], axis=0)
    v_all = jnp.concatenate([vp_ref[...], vc_ref[...]], axis=0)
    shape = (SWA_GROUP * SWA_BLOCK, 2 * SWA_BLOCK)
    i = _mod_pow2(lax.broadcasted_iota(jnp.int32, shape, 0), SWA_BLOCK)
    j = lax.broadcasted_iota(jnp.int32, shape, 1)
    rel = i + SWA_BLOCK - j
    valid = (rel >= 0) & (rel <= WINDOW) & (j >= jnp.where(blk > 0, 0, SWA_BLOCK))
    o_ref[...] = _sink_attention(q, k_all, v_all, valid, sink_ref).astype(BF16)


SWA_SAMPLE_SEQS = 8


def _swa_sample_body(sink_ref, q_ref, kn_ref, vn_ref, kc_ref, vc_ref, o_ref, ko_ref, vo_ref):
    rows = SWA_SAMPLE_SEQS * DEC_SEQ
    n_cache = SWA_SAMPLE_SEQS * WINDOW
    q = q_ref[...]
    kn = kn_ref[...]
    vn = vn_ref[...].astype(F32)
    k_all = jnp.concatenate([kc_ref[...].reshape(n_cache, SWA_NK), kn], axis=0)
    v_all = jnp.concatenate([vc_ref[...].reshape(n_cache, SWA_NK), vn], axis=0)
    for b in range(SWA_SAMPLE_SEQS):
        new = slice(b * DEC_SEQ, (b + 1) * DEC_SEQ)
        ko_ref[b, :WINDOW - DEC_SEQ, :] = kc_ref[b, DEC_SEQ:, :]
        ko_ref[b, WINDOW - DEC_SEQ:, :] = kn[new]
        vo_ref[b, :WINDOW - DEC_SEQ, :] = vc_ref[b, DEC_SEQ:, :]
        vo_ref[b, WINDOW - DEC_SEQ:, :] = vn[new]
    shape = (SWA_GROUP * rows, n_cache + rows)
    r = _mod_pow2(lax.broadcasted_iota(jnp.int32, shape, 0), rows)
    c = lax.broadcasted_iota(jnp.int32, shape, 1)
    q_seq, q_t = _div_pow2(r, DEC_SEQ), _mod_pow2(r, DEC_SEQ)
    is_new = c >= n_cache
    k_seq = jnp.where(is_new, _div_pow2(c - n_cache, DEC_SEQ), _div_pow2(c, WINDOW))
    k_slot = jnp.where(is_new, WINDOW + _mod_pow2(c - n_cache, DEC_SEQ), _mod_pow2(c, WINDOW))
    rel = q_t + WINDOW - k_slot
    valid = (q_seq == k_seq) & (rel >= 0) & (rel <= WINDOW)
    o_ref[...] = _sink_attention(q, k_all, v_all, valid, sink_ref).astype(BF16)


def swa_core(proj, k_rot, sinks, cache_k, cache_v):
    nb = SEQ // SWA_BLOCK
    smem = pl.BlockSpec(memory_space=pltpu.SMEM)
    kcol = SWA_NQ // SWA_NK
    cur = lambda b, i: b * nb + i
    prev = lambda b, i: b * nb + jnp.maximum(i - 1, 0)
    o = pl.pallas_call(
        _swa_prompt_body,
        grid=(BATCH, nb),
        in_specs=[
            smem,
            pl.BlockSpec((SWA_BLOCK, SWA_NQ), lambda b, i: (cur(b, i), 0)),
            pl.BlockSpec((SWA_BLOCK, SWA_NK), lambda b, i: (cur(b, i), kcol)),
            pl.BlockSpec((SWA_BLOCK, SWA_NK), lambda b, i: (cur(b, i), kcol + 1)),
            pl.BlockSpec((SWA_BLOCK, SWA_NK), lambda b, i: (prev(b, i), kcol)),
            pl.BlockSpec((SWA_BLOCK, SWA_NK), lambda b, i: (prev(b, i), kcol + 1)),
        ],
        out_specs=pl.BlockSpec((SWA_BLOCK, SWA_NQ), lambda b, i: (cur(b, i), 0)),
        out_shape=jax.ShapeDtypeStruct((NP, SWA_NQ), BF16),
        compiler_params=_cparams("parallel", "arbitrary"),
        name="swa_prompt",
    )(sinks, proj, proj, proj, proj, proj)

    rows = SWA_SAMPLE_SEQS * DEC_SEQ
    row0 = NP // rows
    o_s, k_s, v_s = pl.pallas_call(
        _swa_sample_body,
        grid=(DEC_BATCH // SWA_SAMPLE_SEQS,),
        in_specs=[
            smem,
            pl.BlockSpec((rows, SWA_NQ), lambda g: (row0 + g, 0)),
            pl.BlockSpec((rows, SWA_NK), lambda g: (row0 + g, 0)),
            pl.BlockSpec((rows, SWA_NK), lambda g: (row0 + g, kcol + 1)),
            pl.BlockSpec((SWA_SAMPLE_SEQS, WINDOW, SWA_NK), lambda g: (g, 0, 0)),
            pl.BlockSpec((SWA_SAMPLE_SEQS, WINDOW, SWA_NK), lambda g: (g, 0, 0)),
        ],
        out_specs=[
            pl.BlockSpec((rows, SWA_NQ), lambda g: (g, 0)),
            pl.BlockSpec((SWA_SAMPLE_SEQS, WINDOW, SWA_NK), lambda g: (g, 0, 0)),
            pl.BlockSpec((SWA_SAMPLE_SEQS, WINDOW, SWA_NK), lambda g: (g, 0, 0)),
        ],
        out_shape=[jax.ShapeDtypeStruct((NS, SWA_NQ), BF16),
                   jax.ShapeDtypeStruct((DEC_BATCH, WINDOW, SWA_NK), F32),
                   jax.ShapeDtypeStruct((DEC_BATCH, WINDOW, SWA_NK), F32)],
        compiler_params=_cparams("parallel"),
        name="swa_sample",
    )(sinks, proj, k_rot, proj, cache_k, cache_v)
    return o, o_s, k_s, v_s


RWKV_ROW_TILE = 256
RWKV_PAIRS = RWKV_HEADS // 2
RWKV_STACK = MXU_DIM_V7X
RWKV_SAMPLE_SEQS = RWKV_STACK // (RWKV_PAIRS * DEC_SEQ)


def _block_diag_ones():
    r = jnp.arange(MXU_DIM_V7X) // RWKV_HD
    return (r[:, None] == r[None, :]).astype(BF16)


def _store_pairs(ref, val):
    for p in range(RWKV_PAIRS):
        ref[p] = val[:, p * LANES:(p + 1) * LANES].astype(ref.dtype)


def _load_pairs(ref):
    return jnp.concatenate([ref[p] for p in range(RWKV_PAIRS)], axis=-1)


def _rwkv_proj_body(x_ref, xprev_ref, first_ref, gn_ref, mu_ref, wr_ref, wk_ref, wv_ref, w0_ref, w1_ref, w2_ref,
                    a0_ref, a1_ref, a2_ref, g1_ref, g2_ref, kk_ref, ka_ref, bd_ref,
                    r_out, lw_out, k_out, v_out, kk_out, b_out, g_out, hlast_out, h_scr):
    i = pl.program_id(0)
    h = _rms(x_ref[...], gn_ref[...])
    prev_tile_last = _rms(xprev_ref[...], gn_ref[...])[SUBLANES - 1:SUBLANES, :]
    rowid = lax.broadcasted_iota(jnp.int32, h.shape, 0)
    xp = jnp.where(rowid == 0, prev_tile_last, pltpu.roll(h, 1, 0))
    seq_len_mask = jnp.where(i >= NP // RWKV_ROW_TILE, DEC_SEQ - 1, SEQ - 1)
    is_first = jnp.bitwise_and(i * RWKV_ROW_TILE + rowid, seq_len_mask) == 0
    xp = jnp.where(is_first, first_ref[...], xp)
    _store_pairs(h_scr, h)
    every8th = pl.ds(SUBLANES - 1, RWKV_ROW_TILE // SUBLANES, stride=SUBLANES)
    hlast_out[...] = jnp.concatenate([h_scr[p, every8th, :] for p in range(RWKV_PAIRS)], axis=-1)
    d = xp - h
    xs = [h + d * mu_ref[i:i + 1, :] for i in range(6)]
    _store_pairs(r_out, _dot(xs[0], wr_ref[...]))
    k = _dot(xs[1], wk_ref[...])
    _store_pairs(v_out, _dot(xs[2], wv_ref[...]))
    z = w0_ref[...] + _dot(jnp.tanh(_dot(xs[3], w1_ref[...])), w2_ref[...])
    softplus = jnp.maximum(-z, 0.0) + jnp.log1p(jnp.exp(-jnp.abs(z)))
    _store_pairs(lw_out, -jnp.exp(-softplus - 0.5))
    a = jax.nn.sigmoid(a0_ref[...] + _dot(_dot(xs[4], a1_ref[...]), a2_ref[...]))
    g_out[...] = _dot(jax.nn.sigmoid(_dot(xs[5], g1_ref[...])), g2_ref[...]).astype(BF16)
    kk = k * kk_ref[...]
    norm = jnp.sqrt(_seg_sum(kk * kk, bd_ref[...], 2))
    kk = kk / jnp.maximum(norm, 1e-12)
    _store_pairs(kk_out, kk)
    _store_pairs(b_out, kk * a)
    _store_pairs(k_out, k * (1.0 + (a - 1.0) * ka_ref[...]))


def rwkv_proj(x, shift_s, p):
    first = jnp.concatenate([
        jnp.zeros((RWKV_ROW_TILE, D_MODEL), F32),
        jnp.pad(shift_s[:, None, :], ((0, 0), (0, DEC_SEQ - 1), (0, 0))).reshape(NS, D_MODEL)], axis=0)
    prompt_tiles = NP // RWKV_ROW_TILE
    row = pl.BlockSpec((RWKV_ROW_TILE, D_MODEL), lambda i: (i, 0))
    prev = pl.BlockSpec((SUBLANES, D_MODEL), lambda i: (jnp.maximum(i * (RWKV_ROW_TILE // SUBLANES) - 1, 0), 0))
    first_spec = pl.BlockSpec((RWKV_ROW_TILE, D_MODEL), lambda i: (jnp.maximum(i - prompt_tiles + 1, 0), 0))
    pair = pl.BlockSpec((RWKV_PAIRS, RWKV_ROW_TILE, LANES), lambda i: (0, i, 0))
    full = lambda a: pl.BlockSpec(a.shape, lambda i: (0,) * a.ndim)
    consts = [p["g_mix"], p["mu"], p["wr"], p["wk"], p["wv"], p["w0"], p["w1"], p["w2"], p["a0"], p["a1"],
              p["a2"], p["g1"], p["g2"], p["k_k"], p["k_a"], p["bd"]]
    pair_shape = lambda dt: jax.ShapeDtypeStruct((RWKV_PAIRS, NT, LANES), dt)
    return pl.pallas_call(
        _rwkv_proj_body,
        grid=(NT // RWKV_ROW_TILE,),
        in_specs=[row, prev, first_spec] + [full(c) for c in consts],
        out_specs=[pair] * 6 + [row, pl.BlockSpec((RWKV_ROW_TILE // SUBLANES, D_MODEL), lambda i: (i, 0))],
        out_shape=[pair_shape(BF16), pair_shape(F32)] + [pair_shape(BF16)] * 4
        + [jax.ShapeDtypeStruct((NT, D_MODEL), BF16), jax.ShapeDtypeStruct((NT // SUBLANES, D_MODEL), F32)],
        scratch_shapes=[pltpu.VMEM((RWKV_PAIRS, RWKV_ROW_TILE, LANES), F32)],
        compiler_params=_cparams("parallel"),
        name="rwkv_proj",
    )(x, x, first, *consts)


def _rwkv_chunks(c_len, problems):
    n_grp = RWKV_STACK // c_len
    shift = int(math.log2(c_len))
    assert 2 ** shift == c_len
    row = lax.broadcasted_iota(jnp.int32, (RWKV_STACK, RWKV_STACK), 0)
    col = lax.broadcasted_iota(jnp.int32, (RWKV_STACK, RWKV_STACK), 1)
    same = jnp.right_shift(row, shift) == jnp.right_shift(col, shift)
    strict = same & (row > col)
    incl = same & (row >= col)
    tri = jnp.where(incl, 1.0, 0.0).astype(BF16)
    each = lambda f, *cols: [f(*xs) for xs in zip(*cols)]
    r, lw, k, v, kk, b, state, store = zip(*problems)
    lam = each(lambda x: sum(jnp.dot(tri, part, preferred_element_type=F32) for part in _split3(x)), lw)
    lam3 = each(lambda x: x.reshape(n_grp, c_len, LANES), lam)
    lam_end3 = each(lambda x: x[:, c_len - 1:c_len, :], lam3)
    e_end = each(lambda e, x: jnp.exp(e - x).reshape(RWKV_STACK, LANES), lam_end3, lam3)
    e_tot3 = each(jnp.exp, lam_end3)
    e_neg = each(lambda x: jnp.exp(-x), lam)
    at = each(lambda kk_, l, w: -kk_ * jnp.exp(l - w), kk, lam, lw)
    rt = each(lambda r_, l: r_ * jnp.exp(l), r, lam)
    mul = lambda x, y: x * y
    bh, kh, bb, kb = each(mul, b, e_neg), each(mul, k, e_neg), each(mul, b, e_end), each(mul, k, e_end)

    lane = lax.broadcasted_iota(jnp.int32, (RWKV_STACK, LANES), 1)
    halves = (lane < RWKV_HD, lane >= RWKV_HD)
    rhs = each(lambda x, y: jnp.concatenate([x, y], axis=0), bh, kh)
    a_pow, b_low, r_low = [], [], []
    for sel in halves:
        lhs = each(lambda x, y: jnp.concatenate([jnp.where(sel, x, 0.0), jnp.where(sel, y, 0.0)], axis=0), at, rt)
        m = each(_dot_nt, lhs, rhs)
        a_pow.append(each(lambda x: jnp.where(strict, x[:RWKV_STACK, :RWKV_STACK], 0.0), m))
        b_low.append(each(lambda x: jnp.where(strict, x[:RWKV_STACK, RWKV_STACK:], 0.0), m))
        r_low.append(each(lambda x: jnp.concatenate([jnp.where(incl, x[RWKV_STACK:, :RWKV_STACK], 0.0),
                                                     jnp.where(incl, x[RWKV_STACK:, RWKV_STACK:], 0.0)], axis=1), m))

    rows = lambda t, g: t[g * c_len:(g + 1) * c_len]
    states = each(lambda st: [st(g) for g in range(n_grp)], state)
    p = [each(lambda x, y, st: _dot_nt(jnp.concatenate([rows(x, g), rows(y, g)], axis=0), st[g]), at, rt, states)
         for g in range(n_grp)]
    n_prob = len(problems)
    u = [jnp.concatenate([p[g][i][:c_len] for g in range(n_grp)], axis=0) for i in range(n_prob)]
    y = [jnp.concatenate([p[g][i][c_len:] for g in range(n_grp)], axis=0) for i in range(n_prob)]
    for e, sel in enumerate(halves):
        u = each(lambda u_, bl, v_: u_ + jnp.where(sel, _dot(bl, v_), 0.0), u, b_low[e], v)
    for it in range(shift):
        upd = [each(lambda a, u_: jnp.where(sel, _dot(a, u_), 0.0), a_pow[e], u) for e, sel in enumerate(halves)]
        u = each(lambda u_, x0, x1: u_ + x0 + x1, u, upd[0], upd[1])
        if it + 1 < shift:
            a_pow = [each(lambda a: _dot(a, a), a_pow[e]) for e in range(len(halves))]
    uv = each(lambda u_, v_: jnp.concatenate([u_, v_], axis=0), u, v)
    for e, sel in enumerate(halves):
        y = each(lambda y_, rl, uv_: y_ + jnp.where(sel, _dot(rl, uv_), 0.0), y, r_low[e], uv)

    r128 = lax.broadcasted_iota(jnp.int32, (LANES, LANES), 0)
    c128 = lax.broadcasted_iota(jnp.int32, (LANES, LANES), 1)
    diag = (r128 < RWKV_HD) == (c128 < RWKV_HD)
    for g in range(n_grp):
        upd = each(lambda u_, v_, bb_, kb_: _dot_tn(jnp.concatenate([rows(u_, g), rows(v_, g)], axis=0),
                                                    jnp.concatenate([rows(bb_, g), rows(kb_, g)], axis=0)),
                   u, v, bb, kb)
        for i in range(n_prob):
            store[i](g, states[i][g] * e_tot3[i][g] + jnp.where(diag, upd[i], 0.0))
    return y


RWKV_PROMPT_SEQS = 4
RWKV_STEP_TOKENS = 128


def _stacked(ref):
    return ref[...].reshape(RWKV_STACK, LANES).astype(F32)


def _rwkv_prompt_body(*refs):
    ins, (y_ref, s_ref, s_scr) = refs[:6 * RWKV_PROMPT_SEQS], refs[6 * RWKV_PROMPT_SEQS:]
    c = pl.program_id(1)

    @pl.when(c == 0)
    def _():
        s_scr[...] = jnp.zeros_like(s_scr)

    def one_chunk(ci, carry):
        rows = pl.ds(pl.multiple_of(ci * RWKV_CHUNK, RWKV_CHUNK), RWKV_CHUNK)
        def problem(seq):
            def store(g, s):
                s_scr[seq, g] = s

            args = [ins[a * RWKV_PROMPT_SEQS + seq][:, rows, :].reshape(RWKV_STACK, LANES).astype(F32)
                    for a in range(6)]
            return (*args, lambda g: s_scr[seq, g], store)

        ys = _rwkv_chunks(RWKV_CHUNK, [problem(seq) for seq in range(RWKV_PROMPT_SEQS)])
        for seq, y in enumerate(ys):
            y_ref[:, seq, rows, :] = y.reshape(RWKV_PAIRS, RWKV_CHUNK, LANES)
        return carry

    lax.fori_loop(0, RWKV_STEP_TOKENS // RWKV_CHUNK, one_chunk, 0)

    @pl.when(c == pl.num_programs(1) - 1)
    def _():
        for seq in range(RWKV_PROMPT_SEQS):
            for p in range(RWKV_PAIRS):
                s_ref[seq, 2 * p] = s_scr[seq, p, :RWKV_HD, :RWKV_HD]
                s_ref[seq, 2 * p + 1] = s_scr[seq, p, RWKV_HD:, RWKV_HD:]


def _rwkv_sample_body(r_ref, lw_ref, k_ref, v_ref, kk_ref, b_ref, s0_ref, y_ref, s_ref):
    zero = jnp.zeros((RWKV_HD, RWKV_HD), F32)

    def state(g):
        p, seq = divmod(g, RWKV_SAMPLE_SEQS)
        return jnp.concatenate([jnp.concatenate([s0_ref[seq, 2 * p], zero], axis=1),
                                jnp.concatenate([zero, s0_ref[seq, 2 * p + 1]], axis=1)], axis=0)

    def store(g, s):
        p, seq = divmod(g, RWKV_SAMPLE_SEQS)
        s_ref[seq, 2 * p] = s[:RWKV_HD, :RWKV_HD]
        s_ref[seq, 2 * p + 1] = s[RWKV_HD:, RWKV_HD:]

    args = [_stacked(ref) for ref in (r_ref, lw_ref, k_ref, v_ref, kk_ref, b_ref)]
    y_ref[...] = _rwkv_chunks(DEC_SEQ, [(*args, state, store)])[0].reshape(y_ref.shape)


def rwkv_core(r, lw, k, v, kk, b, state_s):
    assert RWKV_PAIRS * RWKV_CHUNK == RWKV_STACK
    nc = SEQ // RWKV_STEP_TOKENS
    st_shape = (RWKV_HEADS, RWKV_HD, RWKV_HD)
    seq_blk = lambda seq: pl.BlockSpec((RWKV_PAIRS, RWKV_STEP_TOKENS, LANES),
                                       lambda bi, c: (0, (bi * RWKV_PROMPT_SEQS + seq) * nc + c, 0))
    inputs = (r, lw, k, v, kk, b)
    y_p, s_p = pl.pallas_call(
        _rwkv_prompt_body,
        grid=(BATCH // RWKV_PROMPT_SEQS, nc),
        in_specs=[seq_blk(seq) for _ in inputs for seq in range(RWKV_PROMPT_SEQS)],
        out_specs=[pl.BlockSpec((RWKV_PAIRS, RWKV_PROMPT_SEQS, RWKV_STEP_TOKENS, LANES), lambda bi, c: (0, bi, c, 0)),
                   pl.BlockSpec((RWKV_PROMPT_SEQS,) + st_shape, lambda bi, c: (bi, 0, 0, 0))],
        out_shape=[jax.ShapeDtypeStruct((RWKV_PAIRS, BATCH, SEQ, LANES), F32),
                   jax.ShapeDtypeStruct((BATCH,) + st_shape, F32)],
        scratch_shapes=[pltpu.VMEM((RWKV_PROMPT_SEQS, RWKV_PAIRS, LANES, LANES), F32)],
        compiler_params=_cparams("parallel", "arbitrary"),
        name="rwkv_prompt",
    )(*[a for a in inputs for _ in range(RWKV_PROMPT_SEQS)])

    rows = RWKV_SAMPLE_SEQS * DEC_SEQ
    row0 = NP // rows
    blk = pl.BlockSpec((RWKV_PAIRS, rows, LANES), lambda i: (0, row0 + i, 0))
    st = pl.BlockSpec((RWKV_SAMPLE_SEQS,) + st_shape, lambda i: (i, 0, 0, 0))
    y_s, s_s = pl.pallas_call(
        _rwkv_sample_body,
        grid=(DEC_BATCH // RWKV_SAMPLE_SEQS,),
        in_specs=[blk] * 6 + [st],
        out_specs=[pl.BlockSpec((RWKV_PAIRS, rows, LANES), lambda i: (0, i, 0)), st],
        out_shape=[jax.ShapeDtypeStruct((RWKV_PAIRS, NS, LANES), F32),
                   jax.ShapeDtypeStruct((DEC_BATCH,) + st_shape, F32)],
        compiler_params=_cparams("parallel"),
        name="rwkv_sample",
    )(*inputs, state_s)
    return y_p.reshape(RWKV_PAIRS, NP, LANES), y_s, s_p, s_s


def _rwkv_out_body(x_ref, yp_ref, ys_ref, r_ref, k_ref, v_ref, g_ref, rk_ref, lnw_ref, lnb_ref, bd_ref, wo_ref,
                   o_ref):
    bd = bd_ref[...]
    y = _pick_group(RWKV_ROW_TILE, _load_pairs(yp_ref), _load_pairs(ys_ref))
    yc = y - _seg_sum(y, bd, 2) * (1.0 / RWKV_HD)
    var = _seg_sum(yc * yc, bd, 1) * (1.0 / RWKV_HD)
    yn = yc * lax.rsqrt(var + RWKV_GN_EPS) * lnw_ref[...] + lnb_ref[...]
    rk = _load_pairs(r_ref).astype(F32) * _load_pairs(k_ref).astype(F32)
    bonus = _seg_sum(rk * rk_ref[...], bd, 2) * _load_pairs(v_ref).astype(F32)
    z = (yn + bonus) * g_ref[...].astype(F32)
    o_ref[...] = x_ref[...] + _dot(z, wo_ref[...])


def rwkv_out(x, y_p, y_s, r, k, v, g, p):
    row = pl.BlockSpec((RWKV_ROW_TILE, D_MODEL), lambda i: (i, 0))
    pair_of = lambda f: pl.BlockSpec((RWKV_PAIRS, RWKV_ROW_TILE, LANES), lambda i: (0, f(i), 0))
    full = lambda a: pl.BlockSpec(a.shape, lambda i: (0,) * a.ndim)
    consts = [p["r_k"], p["ln_w"], p["ln_b"], p["bd"], p["w_out"]]
    return pl.pallas_call(
        _rwkv_out_body,
        grid=(NT // RWKV_ROW_TILE,),
        in_specs=[row, *_group_specs(RWKV_ROW_TILE, pair_of)] + [pair_of(lambda i: i)] * 3 + [row]
        + [full(c) for c in consts],
        out_specs=row,
        out_shape=jax.ShapeDtypeStruct((NT, D_MODEL), F32),
        compiler_params=_cparams("parallel"),
        name="rwkv_out",
    )(x, y_p, y_s, r, k, v, g, *consts)


def _pad_cols(w, n):
    return jnp.pad(w, ((0, 0), (0, n - w.shape[1])))


def _pad_rows(w, n):
    return jnp.pad(w, ((0, n - w.shape[0]), (0, 0)))


def kernel(x_prompt, x_sample, state_ret, cache_swa_k, cache_swa_v, state_rwkv_wkv, state_rwkv_shift,
           norm_mix, norm_mlp, norm_final, ret_w_in, ret_w_out,
           swa_w_in, swa_b_in, swa_sinks, swa_w_out,
           rwkv_mu, rwkv_w_rkv, rwkv_w0, rwkv_w1, rwkv_w2, rwkv_a0, rwkv_a1, rwkv_a2,
           rwkv_g1, rwkv_g2, rwkv_k_k, rwkv_k_a, rwkv_r_k, rwkv_ln_w, rwkv_ln_b, rwkv_w_out,
           mlp_w_up, mlp_w_down):
    x = (x_prompt.reshape(NP, D_MODEL), x_sample.reshape(NS, D_MODEL))
    if LAYER_KINDS[0] != 0:
        x = jnp.concatenate(x, axis=0)
    row = lambda t: t.reshape(1, -1)
    ret_p, k_p, k_s, v_p, v_s, wkv_p, wkv_s, sh_p, sh_s = ([] for _ in range(9))
    ret_s = None
    for layer in range(DEPTH):
        kind = LAYER_KINDS[layer]
        j = LAYER_KINDS[:layer].count(kind)
        g_mix = row(norm_mix[layer])
        if kind == 0:
            proj = ret_proj(x, g_mix, ret_w_in[j].astype(BF16))
            o_p, o_s, s_p, ret_s = retention_core(proj, state_ret, j, ret_s)
            x = matmul_residual(x, o_p, o_s, ret_w_out[j].astype(BF16))
            ret_p.append(s_p)
        elif kind == 1:
            proj, k_rot = swa_proj(x, g_mix, swa_w_in[j].astype(BF16), row(swa_b_in[j]))
            o_p, o_s, kc, vc = swa_core(proj, k_rot, swa_sinks[j],
                                        cache_swa_k[j].reshape(DEC_BATCH, WINDOW, SWA_NK),
                                        cache_swa_v[j].reshape(DEC_BATCH, WINDOW, SWA_NK))
            x = matmul_residual(x, o_p, o_s, swa_w_out[j].astype(BF16))
            kv_shape = (-1, WINDOW, SWA_KV_HEADS, SWA_HD)
            k_p.append(k_rot[:NP].reshape(BATCH, SEQ, SWA_NK)[:, -WINDOW:].reshape(kv_shape))
            v_p.append(proj[:NP, SWA_NQ + SWA_NK:].reshape(BATCH, SEQ, SWA_NK)[:, -WINDOW:]
                       .astype(F32).reshape(kv_shape))
            k_s.append(kc.reshape(kv_shape))
            v_s.append(vc.reshape(kv_shape))
        else:
            lora_w = LANES
            lora_g = 2 * LANES
            p = dict(
                mu=rwkv_mu[j], wr=rwkv_w_rkv[j, 0].astype(BF16), wk=rwkv_w_rkv[j, 1].astype(BF16),
                wv=rwkv_w_rkv[j, 2].astype(BF16), w0=row(rwkv_w0[j]),
                w1=_pad_cols(rwkv_w1[j], lora_w).astype(BF16), w2=_pad_rows(rwkv_w2[j], lora_w).astype(BF16),
                a0=row(rwkv_a0[j]),
                a1=_pad_cols(rwkv_a1[j], lora_w).astype(BF16), a2=_pad_rows(rwkv_a2[j], lora_w).astype(BF16),
                g1=_pad_cols(rwkv_g1[j], lora_g).astype(BF16), g2=_pad_rows(rwkv_g2[j], lora_g).astype(BF16),
                k_k=row(rwkv_k_k[j]), k_a=row(rwkv_k_a[j]), r_k=row(rwkv_r_k[j]),
                ln_w=row(rwkv_ln_w[j]), ln_b=row(rwkv_ln_b[j]), bd=_block_diag_ones(),
                w_out=rwkv_w_out[j].astype(BF16), g_mix=g_mix)
            r, lw, k, v, kk, b, g, h_last = rwkv_proj(x, state_rwkv_shift[j], p)
            y_p, y_s, s_p, s_s = rwkv_core(r, lw, k, v, kk, b, state_rwkv_wkv[j])
            x = rwkv_out(x, y_p, y_s, r, k, v, g, p)
            wkv_p.append(s_p)
            wkv_s.append(s_s)
            sh_p.append(h_last[SEQ // SUBLANES - 1:NP // SUBLANES:SEQ // SUBLANES])
            sh_s.append(h_last[NP // SUBLANES:])
        x = mlp(x, row(norm_mlp[layer]), mlp_w_up[layer].astype(BF16), mlp_w_down[layer].astype(BF16))
    y_p = rmsnorm_rows(x, row(norm_final), 0, NP)
    y_s = rmsnorm_rows(x, row(norm_final), NP, NS)
    return (y_p.reshape(BATCH, SEQ, D_MODEL), y_s.reshape(DEC_BATCH, DEC_SEQ, D_MODEL),
            jnp.stack(ret_p), ret_s, jnp.stack(k_p), jnp.stack(k_s), jnp.stack(v_p), jnp.stack(v_s),
            jnp.stack(wkv_p), jnp.stack(wkv_s), jnp.stack(sh_p), jnp.stack(sh_s))
```

```python
import functools
import math

import jax
import jax.numpy as jnp
from jax import lax
from jax.experimental import pallas as pl
from jax.experimental.pallas import tpu as pltpu

F32 = jnp.float32
BF16 = jnp.bfloat16

D_MODEL = 1024
BATCH = 8
SEQ = 2048
DEPTH = 4
DEC_BATCH = 128
DEC_SEQ = 8
PAST_LEN = 8192
LAYER_KINDS = tuple(i % 3 for i in range(DEPTH))
NORM_EPS = 1e-6
ROPE_THETA = 10000.0
NEG_INF = -1e30
D_FF = 4 * D_MODEL

RET_HEADS = 4
RET_DK = D_MODEL // RET_HEADS
RET_DV = 2 * D_MODEL // RET_HEADS
RET_CHUNK = 128
RET_PROMPT_CHUNK = 256
RET_NQ = RET_HEADS * RET_DK
RET_NV = RET_HEADS * RET_DV

SWA_HEADS = 16
SWA_KV_HEADS = 4
SWA_GROUP = SWA_HEADS // SWA_KV_HEADS
SWA_HD = D_MODEL // SWA_HEADS
WINDOW = 128
SWA_BLOCK = 128
SWA_NQ = SWA_HEADS * SWA_HD
SWA_NK = SWA_KV_HEADS * SWA_HD

RWKV_HD = 64
RWKV_HEADS = D_MODEL // RWKV_HD
RWKV_GN_EPS = 64e-5
RWKV_CHUNK = 32

NP = BATCH * SEQ
NS = DEC_BATCH * DEC_SEQ
NT = NP + NS

VMEM_LIMIT_V7X = 48 * 1024 * 1024
LANES = 128
SUBLANES = 8
MXU_DIM_V7X = 256

ROW_TILE = 1024


def _cparams(*sem):
    return pltpu.CompilerParams(dimension_semantics=sem, vmem_limit_bytes=VMEM_LIMIT_V7X)


def _dot(a, b):
    return jnp.dot(a.astype(BF16), b.astype(BF16), preferred_element_type=F32)


def _dot_nt(a, b):
    return lax.dot_general(a.astype(BF16), b.astype(BF16), (((1,), (1,)), ((), ())),
                           preferred_element_type=F32)


def _dot_tn(a, b):
    return lax.dot_general(a.astype(BF16), b.astype(BF16), (((0,), (0,)), ((), ())),
                           preferred_element_type=F32)


def _div_pow2(x, n):
    assert n & (n - 1) == 0
    return jnp.right_shift(x, n.bit_length() - 1)


def _mod_pow2(x, n):
    assert n & (n - 1) == 0
    return jnp.bitwise_and(x, n - 1)


def _rms(x, g):
    ms = jnp.mean(x * x, axis=-1, keepdims=True)
    return x * lax.rsqrt(ms + NORM_EPS) * g


def _split3(x):
    hi = x.astype(BF16)
    r1 = x - hi.astype(F32)
    mid = r1.astype(BF16)
    lo = (r1 - mid.astype(F32)).astype(BF16)
    return hi, mid, lo


def _seg_sum(x, bd, passes):
    parts = _split3(x)[:passes]
    cols = []
    for j in range(D_MODEL // MXU_DIM_V7X):
        sl = slice(j * MXU_DIM_V7X, (j + 1) * MXU_DIM_V7X)
        acc = jnp.dot(parts[0][:, sl], bd, preferred_element_type=F32)
        for p in parts[1:]:
            acc = acc + jnp.dot(p[:, sl], bd, preferred_element_type=F32)
        cols.append(acc)
    return jnp.concatenate(cols, axis=-1)


def _rope_tables(pos, half):
    inv_freq = jnp.power(ROPE_THETA, -jnp.arange(half, dtype=F32) / half)
    ang = pos.astype(F32)[:, None] * inv_freq[None, :]
    return jnp.cos(ang), jnp.sin(ang)


def _group_specs(tile, block_of):
    n_prompt = NP // tile
    return (block_of(lambda i: jnp.minimum(i, n_prompt - 1)), block_of(lambda i: jnp.maximum(i - n_prompt, 0)))


def _pick_group(tile, p_val, s_val):
    return jnp.where(pl.program_id(0) < NP // tile, p_val, s_val)


def _token_rows(x, tile):
    if isinstance(x, tuple):
        return list(x), list(_group_specs(tile, lambda f: pl.BlockSpec((tile, D_MODEL), lambda *g: (f(g[0]), 0))))
    return [x], [pl.BlockSpec((tile, D_MODEL), lambda *g: (g[0], 0))]


def _read_rows(refs, tile):
    return refs[0][...] if len(refs) == 1 else _pick_group(tile, refs[0][...], refs[1][...])


def _rope_table_inputs(half, widen, tile):
    tp = widen(*_rope_tables(jnp.arange(SEQ, dtype=jnp.int32), half))
    ts = [jnp.tile(t, (DEC_BATCH, 1))
          for t in widen(*_rope_tables(PAST_LEN + jnp.arange(DEC_SEQ, dtype=jnp.int32), half))]
    n_prompt = NP // tile
    per_seq = SEQ // tile
    p_spec = pl.BlockSpec((tile, LANES), lambda *g: (g[0] % per_seq, 0))
    s_spec = pl.BlockSpec((tile, LANES), lambda *g: (jnp.maximum(g[0] - n_prompt, 0), 0))
    return [tp[0], tp[1], ts[0], ts[1]], [p_spec, p_spec, s_spec, s_spec]


RET_PROJ_TILE = RET_NQ + RET_NQ


def _ret_proj_body(n_x, *refs):
    x_refs, (g_ref, w_ref, cosp_ref, sinp_ref, coss_ref, sins_ref, o_ref, xn_ref) = refs[:n_x], refs[n_x:]
    j = pl.program_id(1)

    @pl.when(j == 0)
    def _():
        xn_ref[...] = _rms(_read_rows(x_refs, ROW_TILE), g_ref[...]).astype(BF16)

    acc = jnp.dot(xn_ref[...], w_ref[...], preferred_element_type=F32)

    @pl.when(j == 0)
    def _():
        cos = _pick_group(ROW_TILE, cosp_ref[...], coss_ref[...])
        sin = _pick_group(ROW_TILE, sinp_ref[...], sins_ref[...])
        half = RET_DK // 2
        for blk in range(RET_PROJ_TILE // RET_DK):
            lo, mid, hi = blk * RET_DK, blk * RET_DK + half, (blk + 1) * RET_DK
            x1, x2 = acc[:, lo:mid], acc[:, mid:hi]
            scale = 1.0 if blk < RET_HEADS else RET_DK ** -0.5
            o_ref[:, lo:mid] = ((x1 * cos - x2 * sin) * scale).astype(BF16)
            o_ref[:, mid:hi] = ((x2 * cos + x1 * sin) * scale).astype(BF16)

    @pl.when(j > 0)
    def _():
        o_ref[...] = acc.astype(BF16)


def ret_proj(x, g, w):
    n_out = w.shape[1]
    x_arrays, x_specs = _token_rows(x, ROW_TILE)
    tables, table_specs = _rope_table_inputs(RET_DK // 2, lambda c, s: (c, s), ROW_TILE)
    return pl.pallas_call(
        functools.partial(_ret_proj_body, len(x_arrays)),
        grid=(NT // ROW_TILE, n_out // RET_PROJ_TILE),
        in_specs=x_specs + [
            pl.BlockSpec((1, D_MODEL), lambda i, j: (0, 0)),
            pl.BlockSpec((D_MODEL, RET_PROJ_TILE), lambda i, j: (0, j)),
        ] + table_specs,
        out_specs=pl.BlockSpec((ROW_TILE, RET_PROJ_TILE), lambda i, j: (i, j)),
        out_shape=jax.ShapeDtypeStruct((NT, n_out), BF16),
        scratch_shapes=[pltpu.VMEM((ROW_TILE, D_MODEL), BF16)],
        compiler_params=_cparams("parallel", "arbitrary"),
        name="ret_proj",
    )(*x_arrays, g, w, *tables)


SWA_PROJ_ROW_TILE = 512


def _swa_rope(x, cos, sin):
    n = x.shape[1]
    half = SWA_HD // 2
    lane = lax.broadcasted_iota(jnp.int32, x.shape, 1)
    partner = jnp.where(_mod_pow2(lane, SWA_HD) < half, pltpu.roll(x, n - half, 1), pltpu.roll(x, half, 1))
    reps = n // LANES
    return x * jnp.concatenate([cos] * reps, axis=-1) + partner * jnp.concatenate([sin] * reps, axis=-1)


def _swa_proj_body(x_ref, g_ref, w_ref, b_ref, cosp_ref, sinp_ref, coss_ref, sins_ref, o_ref, kr_ref):
    xn = _rms(x_ref[...], g_ref[...]).astype(BF16)
    acc = jnp.dot(xn, w_ref[...], preferred_element_type=F32) + b_ref[...]
    n_qk = SWA_NQ + SWA_NK
    cos = _pick_group(SWA_PROJ_ROW_TILE, cosp_ref[...], coss_ref[...])
    sin = _pick_group(SWA_PROJ_ROW_TILE, sinp_ref[...], sins_ref[...])
    qk = _swa_rope(acc[:, :n_qk], cos, sin)
    assert math.log2(SWA_HD ** 0.5).is_integer()
    o_ref[:, :SWA_NQ] = (qk[:, :SWA_NQ] * (SWA_HD ** -0.5)).astype(BF16)
    o_ref[:, SWA_NQ:n_qk] = qk[:, SWA_NQ:].astype(BF16)
    o_ref[:, n_qk:] = acc[:, n_qk:].astype(BF16)
    kr_ref[...] = qk[:, SWA_NQ:]


def swa_proj(x, g, w, b):
    n_out = w.shape[1]

    def widen(c, s):
        rep = LANES // SWA_HD
        return jnp.tile(jnp.concatenate([c, c], axis=-1), (1, rep)), jnp.tile(jnp.concatenate([-s, s], axis=-1), (1, rep))

    tile = SWA_PROJ_ROW_TILE
    tables, table_specs = _rope_table_inputs(SWA_HD // 2, widen, tile)
    return pl.pallas_call(
        _swa_proj_body,
        grid=(NT // tile,),
        in_specs=[
            pl.BlockSpec((tile, D_MODEL), lambda i: (i, 0)),
            pl.BlockSpec((1, D_MODEL), lambda i: (0, 0)),
            pl.BlockSpec((D_MODEL, n_out), lambda i: (0, 0)),
            pl.BlockSpec((1, n_out), lambda i: (0, 0)),
        ] + table_specs,
        out_specs=[pl.BlockSpec((tile, n_out), lambda i: (i, 0)), pl.BlockSpec((tile, SWA_NK), lambda i: (i, 0))],
        out_shape=[jax.ShapeDtypeStruct((NT, n_out), BF16), jax.ShapeDtypeStruct((NT, SWA_NK), F32)],
        compiler_params=_cparams("parallel"),
        name="swa_proj",
    )(x, g, w, b, *tables)


def _matmul_residual_body(n_x, *refs):
    x_refs, (ap_ref, as_ref, w_ref, o_ref) = refs[:n_x], refs[n_x:]
    a = _pick_group(ROW_TILE, ap_ref[...], as_ref[...])
    o_ref[...] = _read_rows(x_refs, ROW_TILE) + jnp.dot(a, w_ref[...], preferred_element_type=F32)


def matmul_residual(x, a_p, a_s, w):
    k = a_p.shape[1]
    x_arrays, x_specs = _token_rows(x, ROW_TILE)
    return pl.pallas_call(
        functools.partial(_matmul_residual_body, len(x_arrays)),
        grid=(NT // ROW_TILE,),
        in_specs=x_specs + [
            *_group_specs(ROW_TILE, lambda f: pl.BlockSpec((ROW_TILE, k), lambda i: (f(i), 0))),
            pl.BlockSpec((k, D_MODEL), lambda i: (0, 0)),
        ],
        out_specs=pl.BlockSpec((ROW_TILE, D_MODEL), lambda i: (i, 0)),
        out_shape=jax.ShapeDtypeStruct((NT, D_MODEL), F32),
        compiler_params=_cparams("parallel"),
        name="matmul_residual",
    )(*x_arrays, a_p, a_s, w)


MLP_FF_TILE = 1024


def _mlp_body(has_final_norm, x_ref, g_ref, wu_ref, wd_ref, *rest):
    xn_ref, acc_ref = rest[-2:]
    f = pl.program_id(1)

    @pl.when(f == 0)
    def _():
        xn_ref[...] = _rms(x_ref[...], g_ref[...]).astype(BF16)
        acc_ref[...] = jnp.zeros_like(acc_ref)

    a = jnp.maximum(jnp.dot(xn_ref[...], wu_ref[...], preferred_element_type=F32), 0.0)
    acc_ref[...] += jnp.dot((a * a).astype(BF16), wd_ref[...], preferred_element_type=F32)

    @pl.when(f == pl.num_programs(1) - 1)
    def _():
        out = x_ref[...] + acc_ref[...]
        if not has_final_norm:
            o_ref, = rest[:-2]
            o_ref[...] = out
        else:
            gf_ref, yp_ref, ys_ref = rest[:-2]
            y = _rms(out, gf_ref[...])
            is_prompt = pl.program_id(0) < NP // ROW_TILE

            @pl.when(is_prompt)
            def _():
                yp_ref[...] = y

            @pl.when(jnp.logical_not(is_prompt))
            def _():
                ys_ref[...] = y


def mlp(x, g, w_up, w_down, final_g=None):
    row = pl.BlockSpec((ROW_TILE, D_MODEL), lambda i, f: (i, 0))
    vec = pl.BlockSpec((1, D_MODEL), lambda i, f: (0, 0))
    if final_g is None:
        extra_in, extra_specs = [], []
        out_specs = row
        out_shape = jax.ShapeDtypeStruct((NT, D_MODEL), F32)
    else:
        extra_in, extra_specs = [final_g], [vec]
        out_specs = list(_group_specs(ROW_TILE, lambda g_: pl.BlockSpec((ROW_TILE, D_MODEL), lambda i, f: (g_(i), 0))))
        out_shape = [jax.ShapeDtypeStruct((NP, D_MODEL), F32), jax.ShapeDtypeStruct((NS, D_MODEL), F32)]
    return pl.pallas_call(
        functools.partial(_mlp_body, final_g is not None),
        grid=(NT // ROW_TILE, D_FF // MLP_FF_TILE),
        in_specs=[
            row, vec,
            pl.BlockSpec((D_MODEL, MLP_FF_TILE), lambda i, f: (0, f)),
            pl.BlockSpec((MLP_FF_TILE, D_MODEL), lambda i, f: (f, 0)),
        ] + extra_specs,
        out_specs=out_specs,
        out_shape=out_shape,
        scratch_shapes=[pltpu.VMEM((ROW_TILE, D_MODEL), BF16), pltpu.VMEM((ROW_TILE, D_MODEL), F32)],
        compiler_params=_cparams("parallel" if final_g is None else "arbitrary", "arbitrary"),
        name="mlp",
    )(x, g, w_up, w_down, *extra_in)


def _ret_tables(c):
    log_gamma = jnp.log1p(-jnp.exp2(-5.0 - jnp.arange(RET_HEADS, dtype=F32)))
    idx = jnp.arange(c, dtype=F32)
    diff = idx[:, None] - idx[None, :]
    inner = jnp.where(diff >= 0, jnp.exp(log_gamma[:, None, None] * jnp.maximum(diff, 0.0)), 0.0)
    q_decay = jnp.exp(log_gamma[:, None] * (idx[None, :] + 1.0))
    k_decay = jnp.exp(log_gamma[:, None] * (c - 1.0 - idx[None, :]))
    chunk_decay = jnp.exp(log_gamma * c)
    rep = lambda t: jnp.broadcast_to(t[:, :, None], (RET_HEADS, c, LANES))
    return inner, rep(q_decay), rep(k_decay), chunk_decay


def _ret_heads(rows, lo, width):
    return jnp.stack([rows[:, lo + h * width:lo + (h + 1) * width] for h in range(RET_HEADS)])


def _ret_chunk(rows_list, s0, inner, qd, kd, cd_ref):
    n_seq = len(rows_list)
    cat = lambda f: jnp.concatenate([f(r) for r in rows_list], axis=0)
    q = cat(lambda r: _ret_heads(r, 0, RET_DK)).astype(BF16)
    k = cat(lambda r: _ret_heads(r, RET_NQ, RET_DK)).astype(F32)
    v = cat(lambda r: _ret_heads(r, 2 * RET_NQ, RET_DV)).astype(BF16)
    g = cat(lambda r: _ret_heads(r, 2 * RET_NQ + RET_NV, RET_DV)).astype(F32)
    per_seq = lambda t: jnp.concatenate([t] * n_seq, axis=0)
    inner, qd, kd = per_seq(inner), per_seq(qd), per_seq(kd)
    cd = per_seq(jnp.stack([jnp.full((1, 1), cd_ref[h], F32) for h in range(RET_HEADS)]))
    qd = jnp.concatenate([qd] * (RET_DV // LANES), axis=-1)
    kd = jnp.concatenate([kd] * (RET_DK // LANES), axis=-1)
    scores = jnp.einsum('bqd,bkd->bqk', q, k.astype(BF16), preferred_element_type=F32) * inner
    o = (jnp.einsum('bqk,bke->bqe', scores.astype(BF16), v, preferred_element_type=F32)
         + jnp.einsum('bqd,bde->bqe', q, s0.astype(BF16), preferred_element_type=F32) * qd)
    s_new = s0 * cd + jnp.einsum('bkd,bke->bde', (k * kd).astype(BF16), v, preferred_element_type=F32)
    o = o * lax.rsqrt(jnp.mean(o * o, axis=-1, keepdims=True) + NORM_EPS)
    return o * (g * jax.nn.sigmoid(g)), s_new


def _ret_prompt_body(cd_ref, proj_ref, inner_ref, qd_ref, kd_ref, o_ref, s_ref, s_scr):
    c = pl.program_id(1)

    @pl.when(c == 0)
    def _():
        s_scr[...] = jnp.zeros_like(s_scr)

    o, s_new = _ret_chunk([proj_ref[...]], s_scr[...], inner_ref[...], qd_ref[...], kd_ref[...], cd_ref)
    s_scr[...] = s_new
    o_ref[...] = jnp.concatenate([o[h] for h in range(RET_HEADS)], axis=-1).astype(BF16)

    @pl.when(c == pl.num_programs(1) - 1)
    def _():
        s_ref[0] = s_scr[...]


RET_SAMPLE_SEQS = 2


def _ret_sample_body(cd_ref, proj_ref, inner_ref, qd_ref, kd_ref, s0_ref, *rest):
    o_ref, s_ref = rest[-2:]
    rows = proj_ref[...].astype(F32)
    batch = RET_SAMPLE_SEQS * RET_HEADS
    o, s_new = _ret_chunk([rows[seq * DEC_SEQ:(seq + 1) * DEC_SEQ] for seq in range(RET_SAMPLE_SEQS)],
                          s0_ref[0].reshape(batch, RET_DK, RET_DV), inner_ref[...], qd_ref[...], kd_ref[...],
                          cd_ref)
    s_ref[0] = s_new.reshape(RET_SAMPLE_SEQS, RET_HEADS, RET_DK, RET_DV)
    o_ref[...] = jnp.concatenate(
        [jnp.concatenate([o[seq * RET_HEADS + h] for h in range(RET_HEADS)], axis=-1)
         for seq in range(RET_SAMPLE_SEQS)], axis=0).astype(BF16)


def retention_core(proj, state_ret, j, s_stack):
    n_in = proj.shape[1]
    nc = SEQ // RET_PROMPT_CHUNK
    smem = pl.BlockSpec(memory_space=pltpu.SMEM)
    inner, qd, kd, cd = _ret_tables(RET_PROMPT_CHUNK)
    full3 = lambda shape: pl.BlockSpec(shape, lambda b, c: (0, 0, 0))
    o, s_p = pl.pallas_call(
        _ret_prompt_body,
        grid=(BATCH, nc),
        in_specs=[
            smem,
            pl.BlockSpec((RET_PROMPT_CHUNK, n_in), lambda b, c: (b * nc + c, 0)),
            full3(inner.shape), full3(qd.shape), full3(kd.shape),
        ],
        out_specs=[
            pl.BlockSpec((RET_PROMPT_CHUNK, RET_NV), lambda b, c: (b * nc + c, 0)),
            pl.BlockSpec((1, RET_HEADS, RET_DK, RET_DV), lambda b, c: (b, 0, 0, 0)),
        ],
        out_shape=[jax.ShapeDtypeStruct((NP, RET_NV), BF16),
                   jax.ShapeDtypeStruct((BATCH, RET_HEADS, RET_DK, RET_DV), F32)],
        scratch_shapes=[pltpu.VMEM((RET_HEADS, RET_DK, RET_DV), F32)],
        compiler_params=_cparams("parallel", "arbitrary"),
        name="retention_prompt",
    )(cd, proj, inner, qd, kd)

    cs = math.gcd(DEC_SEQ, RET_CHUNK)
    assert cs == DEC_SEQ
    inner, qd, kd, cd = _ret_tables(cs)
    rows = RET_SAMPLE_SEQS * DEC_SEQ
    row0 = NP // rows
    full2 = lambda shape: pl.BlockSpec(shape, lambda b: (0, 0))
    full3 = lambda shape: pl.BlockSpec(shape, lambda b: (0, 0, 0))
    st = pl.BlockSpec((1, RET_SAMPLE_SEQS, RET_HEADS, RET_DK, RET_DV), lambda b: (j, b, 0, 0, 0))
    stacked = [] if s_stack is None else [s_stack]
    o_s, s_s = pl.pallas_call(
        _ret_sample_body,
        grid=(DEC_BATCH // RET_SAMPLE_SEQS,),
        in_specs=[
            smem,
            pl.BlockSpec((rows, n_in), lambda b: (row0 + b, 0)),
            full3(inner.shape), full3(qd.shape), full3(kd.shape),
            st,
        ] + [pl.BlockSpec(memory_space=pl.ANY)] * len(stacked),
        out_specs=[pl.BlockSpec((rows, RET_NV), lambda b: (b, 0)), st],
        out_shape=[jax.ShapeDtypeStruct((NS, RET_NV), BF16),
                   jax.ShapeDtypeStruct(state_ret.shape, F32)],
        input_output_aliases={6: 1} if stacked else {},
        compiler_params=_cparams("parallel"),
        name="retention_sample",
    )(cd, proj, inner, qd, kd, state_ret, *stacked)
    return o, o_s, s_p, s_s


def _sink_attention(q, k, v, valid, sink_ref):
    t = q.shape[0]
    head_cols = lambda x, h: x[:, h * SWA_HD:(h + 1) * SWA_HD]
    qs = jnp.stack([jnp.concatenate([head_cols(q, kh * SWA_GROUP + g) for g in range(SWA_GROUP)], axis=0)
                    for kh in range(SWA_KV_HEADS)]).astype(BF16)
    ks = jnp.stack([head_cols(k, kh) for kh in range(SWA_KV_HEADS)]).astype(BF16)
    vs = jnp.stack([head_cols(v, kh) for kh in range(SWA_KV_HEADS)]).astype(BF16)
    sink = jnp.stack([jnp.concatenate([jnp.full((t, 1), sink_ref[kh * SWA_GROUP + g], F32)
                                       for g in range(SWA_GROUP)], axis=0) for kh in range(SWA_KV_HEADS)])
    s = jnp.einsum('hqd,hkd->hqk', qs, ks, preferred_element_type=F32)
    s = jnp.where(valid[None], s, NEG_INF)
    m = jnp.maximum(jnp.max(s, axis=-1, keepdims=True), sink)
    p = jnp.exp(s - m)
    denom = jnp.sum(p, axis=-1, keepdims=True) + jnp.exp(sink - m)
    o = jnp.einsum('hqk,hkd->hqd', p.astype(BF16), vs, preferred_element_type=F32) / denom
    return jnp.concatenate([o[h // SWA_GROUP, (h % SWA_GROUP) * t:(h % SWA_GROUP + 1) * t]
                            for h in range(SWA_HEADS)], axis=-1)


def _swa_prompt_body(sink_ref, q_ref, kc_ref, vc_ref, kp_ref, vp_ref, o_ref):
    blk = pl.program_id(1)
    q = q_ref[...]
    k_all = jnp.concatenate([kp_ref[...], kc_ref[...]], axis=0)
    v_all = jnp.concatenate([vp_ref[...], vc_ref[...]], axis=0)
    shape = (SWA_GROUP * SWA_BLOCK, 2 * SWA_BLOCK)
    i = _mod_pow2(lax.broadcasted_iota(jnp.int32, shape, 0), SWA_BLOCK)
    j = lax.broadcasted_iota(jnp.int32, shape, 1)
    rel = i + SWA_BLOCK - j
    valid = (rel >= 0) & (rel <= WINDOW) & (j >= jnp.where(blk > 0, 0, SWA_BLOCK))
    o_ref[...] = _sink_attention(q, k_all, v_all, valid, sink_ref).astype(BF16)


SWA_SAMPLE_SEQS = 8


def _swa_sample_body(sink_ref, q_ref, kn_ref, vn_ref, kc_ref, vc_ref, o_ref, ko_ref, vo_ref):
    rows = SWA_SAMPLE_SEQS * DEC_SEQ
    n_cache = SWA_SAMPLE_SEQS * WINDOW
    q = q_ref[...]
    kn = kn_ref[...]
    vn = vn_ref[...].astype(F32)
    k_all = jnp.concatenate([kc_ref[...].reshape(n_cache, SWA_NK), kn], axis=0)
    v_all = jnp.concatenate([vc_ref[...].reshape(n_cache, SWA_NK), vn], axis=0)
    for b in range(SWA_SAMPLE_SEQS):
        new = slice(b * DEC_SEQ, (b + 1) * DEC_SEQ)
        ko_ref[b, :WINDOW - DEC_SEQ, :] = kc_ref[b, DEC_SEQ:, :]
        ko_ref[b, WINDOW - DEC_SEQ:, :] = kn[new]
        vo_ref[b, :WINDOW - DEC_SEQ, :] = vc_ref[b, DEC_SEQ:, :]
        vo_ref[b, WINDOW - DEC_SEQ:, :] = vn[new]
    shape = (SWA_GROUP * rows, n_cache + rows)
    r = _mod_pow2(lax.broadcasted_iota(jnp.int32, shape, 0), rows)
    c = lax.broadcasted_iota(jnp.int32, shape, 1)
    q_seq, q_t = _div_pow2(r, DEC_SEQ), _mod_pow2(r, DEC_SEQ)
    is_new = c >= n_cache
    k_seq = jnp.where(is_new, _div_pow2(c - n_cache, DEC_SEQ), _div_pow2(c, WINDOW))
    k_slot = jnp.where(is_new, WINDOW + _mod_pow2(c - n_cache, DEC_SEQ), _mod_pow2(c, WINDOW))
    rel = q_t + WINDOW - k_slot
    valid = (q_seq == k_seq) & (rel >= 0) & (rel <= WINDOW)
    o_ref[...] = _sink_attention(q, k_all, v_all, valid, sink_ref).astype(BF16)


def swa_core(proj, k_rot, sinks, cache_k, cache_v):
    nb = SEQ // SWA_BLOCK
    smem = pl.BlockSpec(memory_space=pltpu.SMEM)
    kcol = SWA_NQ // SWA_NK
    cur = lambda b, i: b * nb + i
    prev = lambda b, i: b * nb + jnp.maximum(i - 1, 0)
    o = pl.pallas_call(
        _swa_prompt_body,
        grid=(BATCH, nb),
        in_specs=[
            smem,
            pl.BlockSpec((SWA_BLOCK, SWA_NQ), lambda b, i: (cur(b, i), 0)),
            pl.BlockSpec((SWA_BLOCK, SWA_NK), lambda b, i: (cur(b, i), kcol)),
            pl.BlockSpec((SWA_BLOCK, SWA_NK), lambda b, i: (cur(b, i), kcol + 1)),
            pl.BlockSpec((SWA_BLOCK, SWA_NK), lambda b, i: (prev(b, i), kcol)),
            pl.BlockSpec((SWA_BLOCK, SWA_NK), lambda b, i: (prev(b, i), kcol + 1)),
        ],
        out_specs=pl.BlockSpec((SWA_BLOCK, SWA_NQ), lambda b, i: (cur(b, i), 0)),
        out_shape=jax.ShapeDtypeStruct((NP, SWA_NQ), BF16),
        compiler_params=_cparams("parallel", "arbitrary"),
        name="swa_prompt",
    )(sinks, proj, proj, proj, proj, proj)

    rows = SWA_SAMPLE_SEQS * DEC_SEQ
    row0 = NP // rows
    o_s, k_s, v_s = pl.pallas_call(
        _swa_sample_body,
        grid=(DEC_BATCH // SWA_SAMPLE_SEQS,),
        in_specs=[
            smem,
            pl.BlockSpec((rows, SWA_NQ), lambda g: (row0 + g, 0)),
            pl.BlockSpec((rows, SWA_NK), lambda g: (row0 + g, 0)),
            pl.BlockSpec((rows, SWA_NK), lambda g: (row0 + g, kcol + 1)),
            pl.BlockSpec((SWA_SAMPLE_SEQS, WINDOW, SWA_NK), lambda g: (g, 0, 0)),
            pl.BlockSpec((SWA_SAMPLE_SEQS, WINDOW, SWA_NK), lambda g: (g, 0, 0)),
        ],
        out_specs=[
            pl.BlockSpec((rows, SWA_NQ), lambda g: (g, 0)),
            pl.BlockSpec((SWA_SAMPLE_SEQS, WINDOW, SWA_NK), lambda g: (g, 0, 0)),
            pl.BlockSpec((SWA_SAMPLE_SEQS, WINDOW, SWA_NK), lambda g: (g, 0, 0)),
        ],
        out_shape=[jax.ShapeDtypeStruct((NS, SWA_NQ), BF16),
                   jax.ShapeDtypeStruct((DEC_BATCH, WINDOW, SWA_NK), F32),
                   jax.ShapeDtypeStruct((DEC_BATCH, WINDOW, SWA_NK), F32)],
        compiler_params=_cparams("parallel"),
        name="swa_sample",
    )(sinks, proj, k_rot, proj, cache_k, cache_v)
    return o, o_s, k_s, v_s


RWKV_ROW_TILE = 256
RWKV_PAIRS = RWKV_HEADS // 2
RWKV_STACK = MXU_DIM_V7X
RWKV_SAMPLE_SEQS = RWKV_STACK // (RWKV_PAIRS * DEC_SEQ)


def _block_diag_ones():
    r = jnp.arange(MXU_DIM_V7X) // RWKV_HD
    return (r[:, None] == r[None, :]).astype(BF16)


def _store_pairs(ref, val):
    for p in range(RWKV_PAIRS):
        ref[p] = val[:, p * LANES:(p + 1) * LANES].astype(ref.dtype)


def _load_pairs(ref):
    return jnp.concatenate([ref[p] for p in range(RWKV_PAIRS)], axis=-1)


def _rwkv_proj_body(x_ref, xprev_ref, first_ref, gn_ref, mu_ref, wr_ref, wk_ref, wv_ref, w0_ref, w1_ref, w2_ref,
                    a0_ref, a1_ref, a2_ref, g1_ref, g2_ref, kk_ref, ka_ref, bd_ref,
                    r_out, lw_out, k_out, v_out, kk_out, b_out, g_out, hlast_out, h_scr):
    i = pl.program_id(0)
    h = _rms(x_ref[...], gn_ref[...])
    prev_tile_last = _rms(xprev_ref[...], gn_ref[...])[SUBLANES - 1:SUBLANES, :]
    rowid = lax.broadcasted_iota(jnp.int32, h.shape, 0)
    xp = jnp.where(rowid == 0, prev_tile_last, pltpu.roll(h, 1, 0))
    seq_len_mask = jnp.where(i >= NP // RWKV_ROW_TILE, DEC_SEQ - 1, SEQ - 1)
    is_first = jnp.bitwise_and(i * RWKV_ROW_TILE + rowid, seq_len_mask) == 0
    xp = jnp.where(is_first, first_ref[...], xp)
    _store_pairs(h_scr, h)
    every8th = pl.ds(SUBLANES - 1, RWKV_ROW_TILE // SUBLANES, stride=SUBLANES)
    hlast_out[...] = jnp.concatenate([h_scr[p, every8th, :] for p in range(RWKV_PAIRS)], axis=-1)
    d = xp - h
    xs = [h + d * mu_ref[i:i + 1, :] for i in range(6)]
    _store_pairs(r_out, _dot(xs[0], wr_ref[...]))
    k = _dot(xs[1], wk_ref[...])
    _store_pairs(v_out, _dot(xs[2], wv_ref[...]))
    z = w0_ref[...] + _dot(jnp.tanh(_dot(xs[3], w1_ref[...])), w2_ref[...])
    softplus = jnp.maximum(-z, 0.0) + jnp.log1p(jnp.exp(-jnp.abs(z)))
    _store_pairs(lw_out, -jnp.exp(-softplus - 0.5))
    a = jax.nn.sigmoid(a0_ref[...] + _dot(_dot(xs[4], a1_ref[...]), a2_ref[...]))
    g_out[...] = _dot(jax.nn.sigmoid(_dot(xs[5], g1_ref[...])), g2_ref[...]).astype(BF16)
    kk = k * kk_ref[...]
    norm = jnp.sqrt(_seg_sum(kk * kk, bd_ref[...], 2))
    kk = kk / jnp.maximum(norm, 1e-12)
    _store_pairs(kk_out, kk)
    _store_pairs(b_out, kk * a)
    _store_pairs(k_out, k * (1.0 + (a - 1.0) * ka_ref[...]))


def rwkv_proj(x, shift_s, p):
    first = jnp.concatenate([
        jnp.zeros((RWKV_ROW_TILE, D_MODEL), F32),
        jnp.pad(shift_s[:, None, :], ((0, 0), (0, DEC_SEQ - 1), (0, 0))).reshape(NS, D_MODEL)], axis=0)
    prompt_tiles = NP // RWKV_ROW_TILE
    row = pl.BlockSpec((RWKV_ROW_TILE, D_MODEL), lambda i: (i, 0))
    prev = pl.BlockSpec((SUBLANES, D_MODEL), lambda i: (jnp.maximum(i * (RWKV_ROW_TILE // SUBLANES) - 1, 0), 0))
    first_spec = pl.BlockSpec((RWKV_ROW_TILE, D_MODEL), lambda i: (jnp.maximum(i - prompt_tiles + 1, 0), 0))
    pair = pl.BlockSpec((RWKV_PAIRS, RWKV_ROW_TILE, LANES), lambda i: (0, i, 0))
    full = lambda a: pl.BlockSpec(a.shape, lambda i: (0,) * a.ndim)
    consts = [p["g_mix"], p["mu"], p["wr"], p["wk"], p["wv"], p["w0"], p["w1"], p["w2"], p["a0"], p["a1"],
              p["a2"], p["g1"], p["g2"], p["k_k"], p["k_a"], p["bd"]]
    pair_shape = lambda dt: jax.ShapeDtypeStruct((RWKV_PAIRS, NT, LANES), dt)
    return pl.pallas_call(
        _rwkv_proj_body,
        grid=(NT // RWKV_ROW_TILE,),
        in_specs=[row, prev, first_spec] + [full(c) for c in consts],
        out_specs=[pair] * 6 + [row, pl.BlockSpec((RWKV_ROW_TILE // SUBLANES, D_MODEL), lambda i: (i, 0))],
        out_shape=[pair_shape(BF16), pair_shape(F32)] + [pair_shape(BF16)] * 4
        + [jax.ShapeDtypeStruct((NT, D_MODEL), BF16), jax.ShapeDtypeStruct((NT // SUBLANES, D_MODEL), F32)],
        scratch_shapes=[pltpu.VMEM((RWKV_PAIRS, RWKV_ROW_TILE, LANES), F32)],
        compiler_params=_cparams("parallel"),
        name="rwkv_proj",
    )(x, x, first, *consts)


def _rwkv_chunks(c_len, problems):
    n_grp = RWKV_STACK // c_len
    shift = int(math.log2(c_len))
    assert 2 ** shift == c_len
    row = lax.broadcasted_iota(jnp.int32, (RWKV_STACK, RWKV_STACK), 0)
    col = lax.broadcasted_iota(jnp.int32, (RWKV_STACK, RWKV_STACK), 1)
    same = jnp.right_shift(row, shift) == jnp.right_shift(col, shift)
    strict_t = same & (row < col)
    incl_t = same & (row <= col)
    tri = jnp.where(same & (row >= col), 1.0, 0.0).astype(BF16)
    each = lambda f, *cols: [f(*xs) for xs in zip(*cols)]
    r, lw, k, v, kk, b, state, store = zip(*problems)
    lam = each(lambda x: sum(jnp.dot(tri, part, preferred_element_type=F32) for part in _split3(x)), lw)
    lam3 = each(lambda x: x.reshape(n_grp, c_len, LANES), lam)
    lam_end3 = each(lambda x: x[:, c_len - 1:c_len, :], lam3)
    e_end = each(lambda e, x: jnp.exp(e - x).reshape(RWKV_STACK, LANES), lam_end3, lam3)
    e_tot3 = each(jnp.exp, lam_end3)
    e_neg = each(lambda x: jnp.exp(-x), lam)
    at = each(lambda kk_, l, w: -kk_ * jnp.exp(l - w), kk, lam, lw)
    rt = each(lambda r_, l: r_ * jnp.exp(l), r, lam)
    mul = lambda x, y: x * y
    bh, kh, bb, kb = each(mul, b, e_neg), each(mul, k, e_neg), each(mul, b, e_end), each(mul, k, e_end)

    lane = lax.broadcasted_iota(jnp.int32, (RWKV_STACK, LANES), 1)
    halves = (lane < RWKV_HD, lane >= RWKV_HD)
    rhs = each(lambda x, y: jnp.concatenate([x, y], axis=0), bh, kh)
    at_pow, bt_up, rt_up = [], [], []
    for sel in halves:
        lhs = each(lambda x, y: jnp.concatenate([jnp.where(sel, x, 0.0), jnp.where(sel, y, 0.0)], axis=0), at, rt)
        mt = each(_dot_nt, rhs, lhs)
        at_pow.append(each(lambda x: jnp.where(strict_t, x[:RWKV_STACK, :RWKV_STACK], 0.0), mt))
        bt_up.append(each(lambda x: jnp.where(strict_t, x[RWKV_STACK:, :RWKV_STACK], 0.0), mt))
        rt_up.append(each(lambda x: jnp.concatenate([jnp.where(incl_t, x[:RWKV_STACK, RWKV_STACK:], 0.0),
                                                     jnp.where(incl_t, x[RWKV_STACK:, RWKV_STACK:], 0.0)], axis=0), mt))

    rows = lambda t, g: t[g * c_len:(g + 1) * c_len]
    states = each(lambda st: [st(g) for g in range(n_grp)], state)
    p = [each(lambda x, y, st: _dot_nt(jnp.concatenate([rows(x, g), rows(y, g)], axis=0), st[g]), at, rt, states)
         for g in range(n_grp)]
    n_prob = len(problems)
    a_s_t = [jnp.concatenate([p[g][i][:c_len] for g in range(n_grp)], axis=0).T for i in range(n_prob)]
    r_s_t = [jnp.concatenate([p[g][i][c_len:] for g in range(n_grp)], axis=0).T for i in range(n_prob)]
    v_t = each(lambda x: x.T, v)
    half_rows = lambda x, e: x[e * RWKV_HD:(e + 1) * RWKV_HD]
    n_half = len(halves)
    u_t = [each(lambda a_, v_, b_: half_rows(a_, e) + _dot(half_rows(v_, e), b_), a_s_t, v_t, bt_up[e])
           for e in range(n_half)]
    for it in range(shift):
        u_t = [each(lambda u_, a_: u_ + _dot(u_, a_), u_t[e], at_pow[e]) for e in range(n_half)]
        if it + 1 < shift:
            at_pow = [each(lambda a_: _dot(a_, a_), at_pow[e]) for e in range(n_half)]
    y_t = [each(lambda r_, u_, v_, m_: half_rows(r_, e) + _dot(jnp.concatenate([u_, half_rows(v_, e)], axis=1), m_),
                r_s_t, u_t[e], v_t, rt_up[e]) for e in range(n_half)]
    stack_halves = lambda parts: [jnp.concatenate([parts[e][i] for e in range(n_half)], axis=0).T
                                  for i in range(n_prob)]
    u = stack_halves(u_t)
    y = stack_halves(y_t)

    r128 = lax.broadcasted_iota(jnp.int32, (LANES, LANES), 0)
    c128 = lax.broadcasted_iota(jnp.int32, (LANES, LANES), 1)
    diag = (r128 < RWKV_HD) == (c128 < RWKV_HD)
    for g in range(n_grp):
        upd = each(lambda u_, v_, bb_, kb_: _dot_tn(jnp.concatenate([rows(u_, g), rows(v_, g)], axis=0),
                                                    jnp.concatenate([rows(bb_, g), rows(kb_, g)], axis=0)),
                   u, v, bb, kb)
        for i in range(n_prob):
            store[i](g, states[i][g] * e_tot3[i][g] + jnp.where(diag, upd[i], 0.0))
    return y


RWKV_PROMPT_SEQS = 4
RWKV_STEP_TOKENS = 128


def _rwkv_prompt_body(*refs):
    ins, (y_ref, s_ref, s_scr) = refs[:6 * RWKV_PROMPT_SEQS], refs[6 * RWKV_PROMPT_SEQS:]
    c = pl.program_id(1)

    @pl.when(c == 0)
    def _():
        s_scr[...] = jnp.zeros_like(s_scr)

    def one_chunk(ci, carry):
        rows = pl.ds(pl.multiple_of(ci * RWKV_CHUNK, RWKV_CHUNK), RWKV_CHUNK)
        def problem(seq):
            def store(g, s):
                s_scr[seq, g] = s

            args = [ins[a * RWKV_PROMPT_SEQS + seq][:, rows, :].reshape(RWKV_STACK, LANES).astype(F32)
                    for a in range(6)]
            return (*args, lambda g: s_scr[seq, g], store)

        ys = _rwkv_chunks(RWKV_CHUNK, [problem(seq) for seq in range(RWKV_PROMPT_SEQS)])
        for seq, y in enumerate(ys):
            y_ref[:, seq, rows, :] = y.reshape(RWKV_PAIRS, RWKV_CHUNK, LANES)
        return carry

    lax.fori_loop(0, RWKV_STEP_TOKENS // RWKV_CHUNK, one_chunk, 0)

    @pl.when(c == pl.num_programs(1) - 1)
    def _():
        for seq in range(RWKV_PROMPT_SEQS):
            for p in range(RWKV_PAIRS):
                s_ref[seq, 2 * p] = s_scr[seq, p, :RWKV_HD, :RWKV_HD]
                s_ref[seq, 2 * p + 1] = s_scr[seq, p, RWKV_HD:, RWKV_HD:]


RWKV_SAMPLE_PROBLEMS = 2


def _rwkv_sample_body(r_ref, lw_ref, k_ref, v_ref, kk_ref, b_ref, s0_ref, y_ref, s_ref):
    zero = jnp.zeros((RWKV_HD, RWKV_HD), F32)
    rows_per = RWKV_SAMPLE_SEQS * DEC_SEQ

    def problem(q):
        def state(g):
            p, seq = divmod(g, RWKV_SAMPLE_SEQS)
            seq += q * RWKV_SAMPLE_SEQS
            return jnp.concatenate([jnp.concatenate([s0_ref[seq, 2 * p], zero], axis=1),
                                    jnp.concatenate([zero, s0_ref[seq, 2 * p + 1]], axis=1)], axis=0)

        def store(g, s):
            p, seq = divmod(g, RWKV_SAMPLE_SEQS)
            seq += q * RWKV_SAMPLE_SEQS
            s_ref[seq, 2 * p] = s[:RWKV_HD, :RWKV_HD]
            s_ref[seq, 2 * p + 1] = s[RWKV_HD:, RWKV_HD:]

        args = [ref[:, q * rows_per:(q + 1) * rows_per, :].reshape(RWKV_STACK, LANES).astype(F32)
                for ref in (r_ref, lw_ref, k_ref, v_ref, kk_ref, b_ref)]
        return (*args, state, store)

    ys = _rwkv_chunks(DEC_SEQ, [problem(q) for q in range(RWKV_SAMPLE_PROBLEMS)])
    for q, y in enumerate(ys):
        y_ref[:, q * rows_per:(q + 1) * rows_per, :] = y.reshape(RWKV_PAIRS, rows_per, LANES)


def rwkv_core(r, lw, k, v, kk, b, state_s):
    assert RWKV_PAIRS * RWKV_CHUNK == RWKV_STACK
    nc = SEQ // RWKV_STEP_TOKENS
    st_shape = (RWKV_HEADS, RWKV_HD, RWKV_HD)
    seq_blk = lambda seq: pl.BlockSpec((RWKV_PAIRS, RWKV_STEP_TOKENS, LANES),
                                       lambda bi, c: (0, (bi * RWKV_PROMPT_SEQS + seq) * nc + c, 0))
    inputs = (r, lw, k, v, kk, b)
    y_p, s_p = pl.pallas_call(
        _rwkv_prompt_body,
        grid=(BATCH // RWKV_PROMPT_SEQS, nc),
        in_specs=[seq_blk(seq) for _ in inputs for seq in range(RWKV_PROMPT_SEQS)],
        out_specs=[pl.BlockSpec((RWKV_PAIRS, RWKV_PROMPT_SEQS, RWKV_STEP_TOKENS, LANES), lambda bi, c: (0, bi, c, 0)),
                   pl.BlockSpec((RWKV_PROMPT_SEQS,) + st_shape, lambda bi, c: (bi, 0, 0, 0))],
        out_shape=[jax.ShapeDtypeStruct((RWKV_PAIRS, BATCH, SEQ, LANES), F32),
                   jax.ShapeDtypeStruct((BATCH,) + st_shape, F32)],
        scratch_shapes=[pltpu.VMEM((RWKV_PROMPT_SEQS, RWKV_PAIRS, LANES, LANES), F32)],
        compiler_params=_cparams("parallel", "arbitrary"),
        name="rwkv_prompt",
    )(*[a for a in inputs for _ in range(RWKV_PROMPT_SEQS)])

    step_seqs = RWKV_SAMPLE_PROBLEMS * RWKV_SAMPLE_SEQS
    rows = step_seqs * DEC_SEQ
    row0 = NP // rows
    blk = pl.BlockSpec((RWKV_PAIRS, rows, LANES), lambda i: (0, row0 + i, 0))
    st = pl.BlockSpec((step_seqs,) + st_shape, lambda i: (i, 0, 0, 0))
    y_s, s_s = pl.pallas_call(
        _rwkv_sample_body,
        grid=(DEC_BATCH // step_seqs,),
        in_specs=[blk] * 6 + [st],
        out_specs=[pl.BlockSpec((RWKV_PAIRS, rows, LANES), lambda i: (0, i, 0)), st],
        out_shape=[jax.ShapeDtypeStruct((RWKV_PAIRS, NS, LANES), F32),
                   jax.ShapeDtypeStruct((DEC_BATCH,) + st_shape, F32)],
        compiler_params=_cparams("parallel"),
        name="rwkv_sample",
    )(*inputs, state_s)
    return y_p.reshape(RWKV_PAIRS, NP, LANES), y_s, s_p, s_s


def _rwkv_out_body(x_ref, yp_ref, ys_ref, r_ref, k_ref, v_ref, g_ref, rk_ref, lnw_ref, lnb_ref, bd_ref, wo_ref,
                   o_ref):
    bd = bd_ref[...]
    y = _pick_group(RWKV_ROW_TILE, _load_pairs(yp_ref), _load_pairs(ys_ref))
    yc = y - _seg_sum(y, bd, 2) * (1.0 / RWKV_HD)
    var = _seg_sum(yc * yc, bd, 1) * (1.0 / RWKV_HD)
    yn = yc * lax.rsqrt(var + RWKV_GN_EPS) * lnw_ref[...] + lnb_ref[...]
    rk = _load_pairs(r_ref).astype(F32) * _load_pairs(k_ref).astype(F32)
    bonus = _seg_sum(rk * rk_ref[...], bd, 2) * _load_pairs(v_ref).astype(F32)
    z = (yn + bonus) * g_ref[...].astype(F32)
    o_ref[...] = x_ref[...] + _dot(z, wo_ref[...])


def rwkv_out(x, y_p, y_s, r, k, v, g, p):
    row = pl.BlockSpec((RWKV_ROW_TILE, D_MODEL), lambda i: (i, 0))
    pair_of = lambda f: pl.BlockSpec((RWKV_PAIRS, RWKV_ROW_TILE, LANES), lambda i: (0, f(i), 0))
    full = lambda a: pl.BlockSpec(a.shape, lambda i: (0,) * a.ndim)
    consts = [p["r_k"], p["ln_w"], p["ln_b"], p["bd"], p["w_out"]]
    return pl.pallas_call(
        _rwkv_out_body,
        grid=(NT // RWKV_ROW_TILE,),
        in_specs=[row, *_group_specs(RWKV_ROW_TILE, pair_of)] + [pair_of(lambda i: i)] * 3 + [row]
        + [full(c) for c in consts],
        out_specs=row,
        out_shape=jax.ShapeDtypeStruct((NT, D_MODEL), F32),
        compiler_params=_cparams("parallel"),
        name="rwkv_out",
    )(x, y_p, y_s, r, k, v, g, *consts)


def _pad_cols(w, n):
    return jnp.pad(w, ((0, 0), (0, n - w.shape[1])))


def _pad_rows(w, n):
    return jnp.pad(w, ((0, n - w.shape[0]), (0, 0)))


def kernel(x_prompt, x_sample, state_ret, cache_swa_k, cache_swa_v, state_rwkv_wkv, state_rwkv_shift,
           norm_mix, norm_mlp, norm_final, ret_w_in, ret_w_out,
           swa_w_in, swa_b_in, swa_sinks, swa_w_out,
           rwkv_mu, rwkv_w_rkv, rwkv_w0, rwkv_w1, rwkv_w2, rwkv_a0, rwkv_a1, rwkv_a2,
           rwkv_g1, rwkv_g2, rwkv_k_k, rwkv_k_a, rwkv_r_k, rwkv_ln_w, rwkv_ln_b, rwkv_w_out,
           mlp_w_up, mlp_w_down):
    x = (x_prompt.reshape(NP, D_MODEL), x_sample.reshape(NS, D_MODEL))
    if LAYER_KINDS[0] != 0:
        x = jnp.concatenate(x, axis=0)
    row = lambda t: t.reshape(1, -1)
    ret_p, k_p, k_s, v_p, v_s, wkv_p, wkv_s, sh_p, sh_s = ([] for _ in range(9))
    ret_s = None
    for layer in range(DEPTH):
        kind = LAYER_KINDS[layer]
        j = LAYER_KINDS[:layer].count(kind)
        g_mix = row(norm_mix[layer])
        if kind == 0:
            proj = ret_proj(x, g_mix, ret_w_in[j].astype(BF16))
            o_p, o_s, s_p, ret_s = retention_core(proj, state_ret, j, ret_s)
            x = matmul_residual(x, o_p, o_s, ret_w_out[j].astype(BF16))
            ret_p.append(s_p)
        elif kind == 1:
            proj, k_rot = swa_proj(x, g_mix, swa_w_in[j].astype(BF16), row(swa_b_in[j]))
            o_p, o_s, kc, vc = swa_core(proj, k_rot, swa_sinks[j],
                                        cache_swa_k[j].reshape(DEC_BATCH, WINDOW, SWA_NK),
                                        cache_swa_v[j].reshape(DEC_BATCH, WINDOW, SWA_NK))
            x = matmul_residual(x, o_p, o_s, swa_w_out[j].astype(BF16))
            kv_shape = (-1, WINDOW, SWA_KV_HEADS, SWA_HD)
            k_p.append(k_rot[:NP].reshape(BATCH, SEQ, SWA_NK)[:, -WINDOW:].reshape(kv_shape))
            v_p.append(proj[:NP, SWA_NQ + SWA_NK:].reshape(BATCH, SEQ, SWA_NK)[:, -WINDOW:]
                       .astype(F32).reshape(kv_shape))
            k_s.append(kc.reshape(kv_shape))
            v_s.append(vc.reshape(kv_shape))
        else:
            lora_w = LANES
            lora_g = 2 * LANES
            p = dict(
                mu=rwkv_mu[j], wr=rwkv_w_rkv[j, 0].astype(BF16), wk=rwkv_w_rkv[j, 1].astype(BF16),
                wv=rwkv_w_rkv[j, 2].astype(BF16), w0=row(rwkv_w0[j]),
                w1=_pad_cols(rwkv_w1[j], lora_w).astype(BF16), w2=_pad_rows(rwkv_w2[j], lora_w).astype(BF16),
                a0=row(rwkv_a0[j]),
                a1=_pad_cols(rwkv_a1[j], lora_w).astype(BF16), a2=_pad_rows(rwkv_a2[j], lora_w).astype(BF16),
                g1=_pad_cols(rwkv_g1[j], lora_g).astype(BF16), g2=_pad_rows(rwkv_g2[j], lora_g).astype(BF16),
                k_k=row(rwkv_k_k[j]), k_a=row(rwkv_k_a[j]), r_k=row(rwkv_r_k[j]),
                ln_w=row(rwkv_ln_w[j]), ln_b=row(rwkv_ln_b[j]), bd=_block_diag_ones(),
                w_out=rwkv_w_out[j].astype(BF16), g_mix=g_mix)
            r, lw, k, v, kk, b, g, h_last = rwkv_proj(x, state_rwkv_shift[j], p)
            y_p, y_s, s_p, s_s = rwkv_core(r, lw, k, v, kk, b, state_rwkv_wkv[j])
            x = rwkv_out(x, y_p, y_s, r, k, v, g, p)
            wkv_p.append(s_p)
            wkv_s.append(s_s)
            sh_p.append(h_last[SEQ // SUBLANES - 1:NP // SUBLANES:SEQ // SUBLANES])
            sh_s.append(h_last[NP // SUBLANES:])
        x = mlp(x, row(norm_mlp[layer]), mlp_w_up[layer].astype(BF16), mlp_w_down[layer].astype(BF16),
                final_g=row(norm_final) if layer == DEPTH - 1 else None)
    y_p, y_s = x
    return (y_p.reshape(BATCH, SEQ, D_MODEL), y_s.reshape(DEC_BATCH, DEC_SEQ, D_MODEL),
            jnp.stack(ret_p), ret_s, jnp.stack(k_p), jnp.stack(k_s), jnp.stack(v_p), jnp.stack(v_s),
            jnp.stack(wkv_p), jnp.stack(wkv_s), jnp.stack(sh_p), jnp.stack(sh_s))
```

```python
import functools
import math

import jax
import jax.numpy as jnp
from jax import lax
from jax.experimental import pallas as pl
from jax.experimental.pallas import tpu as pltpu

F32 = jnp.float32
BF16 = jnp.bfloat16

D_MODEL = 1024
BATCH = 8
SEQ = 2048
DEPTH = 4
DEC_BATCH = 128
DEC_SEQ = 8
PAST_LEN = 8192
LAYER_KINDS = tuple(i % 3 for i in range(DEPTH))
NORM_EPS = 1e-6
ROPE_THETA = 10000.0
NEG_INF = -1e30
D_FF = 4 * D_MODEL

RET_HEADS = 4
RET_DK = D_MODEL // RET_HEADS
RET_DV = 2 * D_MODEL // RET_HEADS
RET_CHUNK = 128
RET_PROMPT_CHUNK = 256
RET_NQ = RET_HEADS * RET_DK
RET_NV = RET_HEADS * RET_DV

SWA_HEADS = 16
SWA_KV_HEADS = 4
SWA_GROUP = SWA_HEADS // SWA_KV_HEADS
SWA_HD = D_MODEL // SWA_HEADS
WINDOW = 128
SWA_BLOCK = 128
SWA_NQ = SWA_HEADS * SWA_HD
SWA_NK = SWA_KV_HEADS * SWA_HD

RWKV_HD = 64
RWKV_HEADS = D_MODEL // RWKV_HD
RWKV_GN_EPS = 64e-5
RWKV_CHUNK = 32

NP = BATCH * SEQ
NS = DEC_BATCH * DEC_SEQ
NT = NP + NS

VMEM_LIMIT_V7X = 48 * 1024 * 1024
LANES = 128
SUBLANES = 8
MXU_DIM_V7X = 256

ROW_TILE = 1024


def _cparams(*sem):
    return pltpu.CompilerParams(dimension_semantics=sem, vmem_limit_bytes=VMEM_LIMIT_V7X)


def _dot(a, b):
    return jnp.dot(a.astype(BF16), b.astype(BF16), preferred_element_type=F32)


def _dot_nt(a, b):
    return lax.dot_general(a.astype(BF16), b.astype(BF16), (((1,), (1,)), ((), ())),
                           preferred_element_type=F32)


def _dot_tn(a, b):
    return lax.dot_general(a.astype(BF16), b.astype(BF16), (((0,), (0,)), ((), ())),
                           preferred_element_type=F32)


def _div_pow2(x, n):
    assert n & (n - 1) == 0
    return jnp.right_shift(x, n.bit_length() - 1)


def _mod_pow2(x, n):
    assert n & (n - 1) == 0
    return jnp.bitwise_and(x, n - 1)


def _rms(x, g):
    ms = jnp.mean(x * x, axis=-1, keepdims=True)
    return x * lax.rsqrt(ms + NORM_EPS) * g


def _split3(x):
    hi = x.astype(BF16)
    r1 = x - hi.astype(F32)
    mid = r1.astype(BF16)
    lo = (r1 - mid.astype(F32)).astype(BF16)
    return hi, mid, lo


def _seg_sum(x, bd):
    xb = x.astype(BF16)
    return jnp.concatenate(
        [jnp.dot(xb[:, j * MXU_DIM_V7X:(j + 1) * MXU_DIM_V7X], bd, preferred_element_type=F32)
         for j in range(D_MODEL // MXU_DIM_V7X)], axis=-1)


def _rope_tables(pos, half):
    inv_freq = jnp.power(ROPE_THETA, -jnp.arange(half, dtype=F32) / half)
    ang = pos.astype(F32)[:, None] * inv_freq[None, :]
    return jnp.cos(ang), jnp.sin(ang)


def _group_specs(tile, block_of):
    n_prompt = NP // tile
    return (block_of(lambda i: jnp.minimum(i, n_prompt - 1)), block_of(lambda i: jnp.maximum(i - n_prompt, 0)))


def _pick_group(tile, p_val, s_val):
    return jnp.where(pl.program_id(0) < NP // tile, p_val, s_val)


def _token_rows(x, tile):
    if isinstance(x, tuple):
        return list(x), list(_group_specs(tile, lambda f: pl.BlockSpec((tile, D_MODEL), lambda *g: (f(g[0]), 0))))
    return [x], [pl.BlockSpec((tile, D_MODEL), lambda *g: (g[0], 0))]


def _read_rows(refs, tile):
    return refs[0][...] if len(refs) == 1 else _pick_group(tile, refs[0][...], refs[1][...])


def _rope_table_inputs(half, widen, tile):
    tp = widen(*_rope_tables(jnp.arange(SEQ, dtype=jnp.int32), half))
    ts = [jnp.tile(t, (DEC_BATCH, 1))
          for t in widen(*_rope_tables(PAST_LEN + jnp.arange(DEC_SEQ, dtype=jnp.int32), half))]
    n_prompt = NP // tile
    per_seq = SEQ // tile
    p_spec = pl.BlockSpec((tile, LANES), lambda *g: (g[0] % per_seq, 0))
    s_spec = pl.BlockSpec((tile, LANES), lambda *g: (jnp.maximum(g[0] - n_prompt, 0), 0))
    return [tp[0], tp[1], ts[0], ts[1]], [p_spec, p_spec, s_spec, s_spec]


RET_PROJ_TILE = RET_NQ + RET_NQ


def _ret_proj_body(n_x, *refs):
    x_refs, (g_ref, w_ref, cosp_ref, sinp_ref, coss_ref, sins_ref, o_ref, xn_ref) = refs[:n_x], refs[n_x:]
    j = pl.program_id(1)

    @pl.when(j == 0)
    def _():
        xn_ref[...] = _rms(_read_rows(x_refs, ROW_TILE), g_ref[...]).astype(BF16)

    acc = jnp.dot(xn_ref[...], w_ref[...], preferred_element_type=F32)

    @pl.when(j == 0)
    def _():
        cos = _pick_group(ROW_TILE, cosp_ref[...], coss_ref[...])
        sin = _pick_group(ROW_TILE, sinp_ref[...], sins_ref[...])
        half = RET_DK // 2
        for blk in range(RET_PROJ_TILE // RET_DK):
            lo, mid, hi = blk * RET_DK, blk * RET_DK + half, (blk + 1) * RET_DK
            x1, x2 = acc[:, lo:mid], acc[:, mid:hi]
            scale = 1.0 if blk < RET_HEADS else RET_DK ** -0.5
            o_ref[:, lo:mid] = ((x1 * cos - x2 * sin) * scale).astype(BF16)
            o_ref[:, mid:hi] = ((x2 * cos + x1 * sin) * scale).astype(BF16)

    @pl.when(j > 0)
    def _():
        o_ref[...] = acc.astype(BF16)


def ret_proj(x, g, w):
    n_out = w.shape[1]
    x_arrays, x_specs = _token_rows(x, ROW_TILE)
    tables, table_specs = _rope_table_inputs(RET_DK // 2, lambda c, s: (c, s), ROW_TILE)
    return pl.pallas_call(
        functools.partial(_ret_proj_body, len(x_arrays)),
        grid=(NT // ROW_TILE, n_out // RET_PROJ_TILE),
        in_specs=x_specs + [
            pl.BlockSpec((1, D_MODEL), lambda i, j: (0, 0)),
            pl.BlockSpec((D_MODEL, RET_PROJ_TILE), lambda i, j: (0, j)),
        ] + table_specs,
        out_specs=pl.BlockSpec((ROW_TILE, RET_PROJ_TILE), lambda i, j: (i, j)),
        out_shape=jax.ShapeDtypeStruct((NT, n_out), BF16),
        scratch_shapes=[pltpu.VMEM((ROW_TILE, D_MODEL), BF16)],
        compiler_params=_cparams("parallel", "arbitrary"),
        name="ret_proj",
    )(*x_arrays, g, w, *tables)


SWA_PROJ_ROW_TILE = 512


def _swa_rope(x, cos, sin):
    n = x.shape[1]
    half = SWA_HD // 2
    lane = lax.broadcasted_iota(jnp.int32, x.shape, 1)
    partner = jnp.where(_mod_pow2(lane, SWA_HD) < half, pltpu.roll(x, n - half, 1), pltpu.roll(x, half, 1))
    reps = n // LANES
    return x * jnp.concatenate([cos] * reps, axis=-1) + partner * jnp.concatenate([sin] * reps, axis=-1)


def _swa_proj_body(x_ref, g_ref, w_ref, b_ref, cosp_ref, sinp_ref, coss_ref, sins_ref, o_ref, kr_ref):
    xn = _rms(x_ref[...], g_ref[...]).astype(BF16)
    acc = jnp.dot(xn, w_ref[...], preferred_element_type=F32) + b_ref[...]
    n_qk = SWA_NQ + SWA_NK
    cos = _pick_group(SWA_PROJ_ROW_TILE, cosp_ref[...], coss_ref[...])
    sin = _pick_group(SWA_PROJ_ROW_TILE, sinp_ref[...], sins_ref[...])
    qk = _swa_rope(acc[:, :n_qk], cos, sin)
    assert math.log2(SWA_HD ** 0.5).is_integer()
    o_ref[:, :SWA_NQ] = (qk[:, :SWA_NQ] * (SWA_HD ** -0.5)).astype(BF16)
    o_ref[:, SWA_NQ:n_qk] = qk[:, SWA_NQ:].astype(BF16)
    o_ref[:, n_qk:] = acc[:, n_qk:].astype(BF16)
    kr_ref[...] = qk[:, SWA_NQ:]


def swa_proj(x, g, w, b):
    n_out = w.shape[1]

    def widen(c, s):
        rep = LANES // SWA_HD
        return jnp.tile(jnp.concatenate([c, c], axis=-1), (1, rep)), jnp.tile(jnp.concatenate([-s, s], axis=-1), (1, rep))

    tile = SWA_PROJ_ROW_TILE
    tables, table_specs = _rope_table_inputs(SWA_HD // 2, widen, tile)
    return pl.pallas_call(
        _swa_proj_body,
        grid=(NT // tile,),
        in_specs=[
            pl.BlockSpec((tile, D_MODEL), lambda i: (i, 0)),
            pl.BlockSpec((1, D_MODEL), lambda i: (0, 0)),
            pl.BlockSpec((D_MODEL, n_out), lambda i: (0, 0)),
            pl.BlockSpec((1, n_out), lambda i: (0, 0)),
        ] + table_specs,
        out_specs=[pl.BlockSpec((tile, n_out), lambda i: (i, 0)), pl.BlockSpec((tile, SWA_NK), lambda i: (i, 0))],
        out_shape=[jax.ShapeDtypeStruct((NT, n_out), BF16), jax.ShapeDtypeStruct((NT, SWA_NK), F32)],
        compiler_params=_cparams("parallel"),
        name="swa_proj",
    )(x, g, w, b, *tables)


def _matmul_residual_body(n_x, *refs):
    x_refs, (ap_ref, as_ref, w_ref, o_ref) = refs[:n_x], refs[n_x:]
    a = _pick_group(ROW_TILE, ap_ref[...], as_ref[...])
    o_ref[...] = _read_rows(x_refs, ROW_TILE) + jnp.dot(a, w_ref[...], preferred_element_type=F32)


def matmul_residual(x, a_p, a_s, w):
    k = a_p.shape[1]
    x_arrays, x_specs = _token_rows(x, ROW_TILE)
    return pl.pallas_call(
        functools.partial(_matmul_residual_body, len(x_arrays)),
        grid=(NT // ROW_TILE,),
        in_specs=x_specs + [
            *_group_specs(ROW_TILE, lambda f: pl.BlockSpec((ROW_TILE, k), lambda i: (f(i), 0))),
            pl.BlockSpec((k, D_MODEL), lambda i: (0, 0)),
        ],
        out_specs=pl.BlockSpec((ROW_TILE, D_MODEL), lambda i: (i, 0)),
        out_shape=jax.ShapeDtypeStruct((NT, D_MODEL), F32),
        compiler_params=_cparams("parallel"),
        name="matmul_residual",
    )(*x_arrays, a_p, a_s, w)


MLP_FF_TILE = 1024


def _mlp_body(has_final_norm, x_ref, g_ref, wu_ref, wd_ref, *rest):
    xn_ref, acc_ref = rest[-2:]
    f = pl.program_id(1)

    @pl.when(f == 0)
    def _():
        xn_ref[...] = _rms(x_ref[...], g_ref[...]).astype(BF16)
        acc_ref[...] = jnp.zeros_like(acc_ref)

    a = jnp.maximum(jnp.dot(xn_ref[...], wu_ref[...], preferred_element_type=F32), 0.0)
    acc_ref[...] += jnp.dot((a * a).astype(BF16), wd_ref[...], preferred_element_type=F32)

    @pl.when(f == pl.num_programs(1) - 1)
    def _():
        out = x_ref[...] + acc_ref[...]
        if not has_final_norm:
            o_ref, = rest[:-2]
            o_ref[...] = out
        else:
            gf_ref, yp_ref, ys_ref = rest[:-2]
            y = _rms(out, gf_ref[...])
            is_prompt = pl.program_id(0) < NP // ROW_TILE

            @pl.when(is_prompt)
            def _():
                yp_ref[...] = y

            @pl.when(jnp.logical_not(is_prompt))
            def _():
                ys_ref[...] = y


def mlp(x, g, w_up, w_down, final_g=None):
    row = pl.BlockSpec((ROW_TILE, D_MODEL), lambda i, f: (i, 0))
    vec = pl.BlockSpec((1, D_MODEL), lambda i, f: (0, 0))
    if final_g is None:
        extra_in, extra_specs = [], []
        out_specs = row
        out_shape = jax.ShapeDtypeStruct((NT, D_MODEL), F32)
    else:
        extra_in, extra_specs = [final_g], [vec]
        out_specs = list(_group_specs(ROW_TILE, lambda g_: pl.BlockSpec((ROW_TILE, D_MODEL), lambda i, f: (g_(i), 0))))
        out_shape = [jax.ShapeDtypeStruct((NP, D_MODEL), F32), jax.ShapeDtypeStruct((NS, D_MODEL), F32)]
    return pl.pallas_call(
        functools.partial(_mlp_body, final_g is not None),
        grid=(NT // ROW_TILE, D_FF // MLP_FF_TILE),
        in_specs=[
            row, vec,
            pl.BlockSpec((D_MODEL, MLP_FF_TILE), lambda i, f: (0, f)),
            pl.BlockSpec((MLP_FF_TILE, D_MODEL), lambda i, f: (f, 0)),
        ] + extra_specs,
        out_specs=out_specs,
        out_shape=out_shape,
        scratch_shapes=[pltpu.VMEM((ROW_TILE, D_MODEL), BF16), pltpu.VMEM((ROW_TILE, D_MODEL), F32)],
        compiler_params=_cparams("parallel" if final_g is None else "arbitrary", "arbitrary"),
        name="mlp",
    )(x, g, w_up, w_down, *extra_in)


def _ret_tables(c):
    log_gamma = jnp.log1p(-jnp.exp2(-5.0 - jnp.arange(RET_HEADS, dtype=F32)))
    idx = jnp.arange(c, dtype=F32)
    diff = idx[:, None] - idx[None, :]
    inner = jnp.where(diff >= 0, jnp.exp(log_gamma[:, None, None] * jnp.maximum(diff, 0.0)), 0.0)
    q_decay = jnp.exp(log_gamma[:, None] * (idx[None, :] + 1.0))
    k_decay = jnp.exp(log_gamma[:, None] * (c - 1.0 - idx[None, :]))
    chunk_decay = jnp.exp(log_gamma * c)
    rep = lambda t: jnp.broadcast_to(t[:, :, None], (RET_HEADS, c, LANES))
    return inner, rep(q_decay), rep(k_decay), chunk_decay


def _ret_heads(rows, lo, width):
    return jnp.stack([rows[:, lo + h * width:lo + (h + 1) * width] for h in range(RET_HEADS)])


def _ret_chunk(rows_list, s0, inner, qd, kd, cd_ref):
    n_seq = len(rows_list)
    cat = lambda f: jnp.concatenate([f(r) for r in rows_list], axis=0)
    q = cat(lambda r: _ret_heads(r, 0, RET_DK)).astype(BF16)
    k = cat(lambda r: _ret_heads(r, RET_NQ, RET_DK)).astype(F32)
    v = cat(lambda r: _ret_heads(r, 2 * RET_NQ, RET_DV)).astype(BF16)
    g = cat(lambda r: _ret_heads(r, 2 * RET_NQ + RET_NV, RET_DV)).astype(F32)
    per_seq = lambda t: jnp.concatenate([t] * n_seq, axis=0)
    inner, qd, kd = per_seq(inner), per_seq(qd), per_seq(kd)
    cd = per_seq(jnp.stack([jnp.full((1, 1), cd_ref[h], F32) for h in range(RET_HEADS)]))
    qd = jnp.concatenate([qd] * (RET_DV // LANES), axis=-1)
    kd = jnp.concatenate([kd] * (RET_DK // LANES), axis=-1)
    scores = jnp.einsum('bqd,bkd->bqk', q, k.astype(BF16), preferred_element_type=F32) * inner
    o = (jnp.einsum('bqk,bke->bqe', scores.astype(BF16), v, preferred_element_type=F32)
         + jnp.einsum('bqd,bde->bqe', q, s0.astype(BF16), preferred_element_type=F32) * qd)
    s_new = s0 * cd + jnp.einsum('bkd,bke->bde', (k * kd).astype(BF16), v, preferred_element_type=F32)
    o = o * lax.rsqrt(jnp.mean(o * o, axis=-1, keepdims=True) + NORM_EPS)
    return o * (g * jax.nn.sigmoid(g)), s_new


def _ret_prompt_body(cd_ref, proj_ref, inner_ref, qd_ref, kd_ref, o_ref, s_ref, s_scr):
    c = pl.program_id(1)

    @pl.when(c == 0)
    def _():
        s_scr[...] = jnp.zeros_like(s_scr)

    o, s_new = _ret_chunk([proj_ref[...]], s_scr[...], inner_ref[...], qd_ref[...], kd_ref[...], cd_ref)
    s_scr[...] = s_new
    o_ref[...] = jnp.concatenate([o[h] for h in range(RET_HEADS)], axis=-1).astype(BF16)

    @pl.when(c == pl.num_programs(1) - 1)
    def _():
        s_ref[0] = s_scr[...]


RET_SAMPLE_SEQS = 2


def _ret_sample_body(cd_ref, proj_ref, inner_ref, qd_ref, kd_ref, s0_ref, *rest):
    o_ref, s_ref = rest[-2:]
    rows = proj_ref[...].astype(F32)
    batch = RET_SAMPLE_SEQS * RET_HEADS
    o, s_new = _ret_chunk([rows[seq * DEC_SEQ:(seq + 1) * DEC_SEQ] for seq in range(RET_SAMPLE_SEQS)],
                          s0_ref[0].reshape(batch, RET_DK, RET_DV), inner_ref[...], qd_ref[...], kd_ref[...],
                          cd_ref)
    s_ref[0] = s_new.reshape(RET_SAMPLE_SEQS, RET_HEADS, RET_DK, RET_DV)
    o_ref[...] = jnp.concatenate(
        [jnp.concatenate([o[seq * RET_HEADS + h] for h in range(RET_HEADS)], axis=-1)
         for seq in range(RET_SAMPLE_SEQS)], axis=0).astype(BF16)


def retention_core(proj, state_ret, j, s_stack):
    n_in = proj.shape[1]
    nc = SEQ // RET_PROMPT_CHUNK
    smem = pl.BlockSpec(memory_space=pltpu.SMEM)
    inner, qd, kd, cd = _ret_tables(RET_PROMPT_CHUNK)
    full3 = lambda shape: pl.BlockSpec(shape, lambda b, c: (0, 0, 0))
    o, s_p = pl.pallas_call(
        _ret_prompt_body,
        grid=(BATCH, nc),
        in_specs=[
            smem,
            pl.BlockSpec((RET_PROMPT_CHUNK, n_in), lambda b, c: (b * nc + c, 0)),
            full3(inner.shape), full3(qd.shape), full3(kd.shape),
        ],
        out_specs=[
            pl.BlockSpec((RET_PROMPT_CHUNK, RET_NV), lambda b, c: (b * nc + c, 0)),
            pl.BlockSpec((1, RET_HEADS, RET_DK, RET_DV), lambda b, c: (b, 0, 0, 0)),
        ],
        out_shape=[jax.ShapeDtypeStruct((NP, RET_NV), BF16),
                   jax.ShapeDtypeStruct((BATCH, RET_HEADS, RET_DK, RET_DV), F32)],
        scratch_shapes=[pltpu.VMEM((RET_HEADS, RET_DK, RET_DV), F32)],
        compiler_params=_cparams("parallel", "arbitrary"),
        name="retention_prompt",
    )(cd, proj, inner, qd, kd)

    cs = math.gcd(DEC_SEQ, RET_CHUNK)
    assert cs == DEC_SEQ
    inner, qd, kd, cd = _ret_tables(cs)
    rows = RET_SAMPLE_SEQS * DEC_SEQ
    row0 = NP // rows
    full2 = lambda shape: pl.BlockSpec(shape, lambda b: (0, 0))
    full3 = lambda shape: pl.BlockSpec(shape, lambda b: (0, 0, 0))
    st = pl.BlockSpec((1, RET_SAMPLE_SEQS, RET_HEADS, RET_DK, RET_DV), lambda b: (j, b, 0, 0, 0))
    stacked = [] if s_stack is None else [s_stack]
    o_s, s_s = pl.pallas_call(
        _ret_sample_body,
        grid=(DEC_BATCH // RET_SAMPLE_SEQS,),
        in_specs=[
            smem,
            pl.BlockSpec((rows, n_in), lambda b: (row0 + b, 0)),
            full3(inner.shape), full3(qd.shape), full3(kd.shape),
            st,
        ] + [pl.BlockSpec(memory_space=pl.ANY)] * len(stacked),
        out_specs=[pl.BlockSpec((rows, RET_NV), lambda b: (b, 0)), st],
        out_shape=[jax.ShapeDtypeStruct((NS, RET_NV), BF16),
                   jax.ShapeDtypeStruct(state_ret.shape, F32)],
        input_output_aliases={6: 1} if stacked else {},
        compiler_params=_cparams("parallel"),
        name="retention_sample",
    )(cd, proj, inner, qd, kd, state_ret, *stacked)
    return o, o_s, s_p, s_s


def _sink_attention(q, k, v, valid_t, sink_ref):
    t = q.shape[0]
    head_cols = lambda x, h: x[:, h * SWA_HD:(h + 1) * SWA_HD]
    qs = jnp.stack([jnp.concatenate([head_cols(q, kh * SWA_GROUP + g) for g in range(SWA_GROUP)], axis=0)
                    for kh in range(SWA_KV_HEADS)]).astype(BF16)
    ks = jnp.stack([head_cols(k, kh) for kh in range(SWA_KV_HEADS)]).astype(BF16)
    vs = jnp.stack([head_cols(v, kh) for kh in range(SWA_KV_HEADS)]).astype(BF16)
    sink = jnp.stack([jnp.concatenate([jnp.full((1, t), sink_ref[kh * SWA_GROUP + g], F32)
                                       for g in range(SWA_GROUP)], axis=1) for kh in range(SWA_KV_HEADS)])
    s = jnp.einsum('hkd,hqd->hkq', ks, qs, preferred_element_type=F32)
    s = jnp.where(valid_t[None], s, NEG_INF)
    m = jnp.maximum(jnp.max(s, axis=1, keepdims=True), sink)
    p = jnp.exp(s - m)
    denom = jnp.sum(p, axis=1, keepdims=True) + jnp.exp(sink - m)
    o_t = jnp.einsum('hkd,hkq->hdq', vs, p.astype(BF16), preferred_element_type=F32) / denom
    o = [o_t[kh].T for kh in range(SWA_KV_HEADS)]
    return jnp.concatenate([o[h // SWA_GROUP][(h % SWA_GROUP) * t:(h % SWA_GROUP + 1) * t]
                            for h in range(SWA_HEADS)], axis=-1)


def _swa_prompt_body(sink_ref, q_ref, kc_ref, vc_ref, kp_ref, vp_ref, o_ref):
    blk = pl.program_id(1)
    q = q_ref[...]
    k_all = jnp.concatenate([kp_ref[...], kc_ref[...]], axis=0)
    v_all = jnp.concatenate([vp_ref[...], vc_ref[...]], axis=0)
    shape = (2 * SWA_BLOCK, SWA_GROUP * SWA_BLOCK)
    i = _mod_pow2(lax.broadcasted_iota(jnp.int32, shape, 1), SWA_BLOCK)
    j = lax.broadcasted_iota(jnp.int32, shape, 0)
    rel = i + SWA_BLOCK - j
    valid = (rel >= 0) & (rel <= WINDOW) & (j >= jnp.where(blk > 0, 0, SWA_BLOCK))
    o_ref[...] = _sink_attention(q, k_all, v_all, valid, sink_ref).astype(BF16)


SWA_SAMPLE_SEQS = 8


def _swa_sample_body(sink_ref, q_ref, kn_ref, vn_ref, kc_ref, vc_ref, o_ref, ko_ref, vo_ref):
    rows = SWA_SAMPLE_SEQS * DEC_SEQ
    n_cache = SWA_SAMPLE_SEQS * WINDOW
    q = q_ref[...]
    kn = kn_ref[...]
    vn = vn_ref[...].astype(F32)
    k_all = jnp.concatenate([kc_ref[...].reshape(n_cache, SWA_NK), kn], axis=0)
    v_all = jnp.concatenate([vc_ref[...].reshape(n_cache, SWA_NK), vn], axis=0)
    for b in range(SWA_SAMPLE_SEQS):
        new = slice(b * DEC_SEQ, (b + 1) * DEC_SEQ)
        ko_ref[b, :WINDOW - DEC_SEQ, :] = kc_ref[b, DEC_SEQ:, :]
        ko_ref[b, WINDOW - DEC_SEQ:, :] = kn[new]
        vo_ref[b, :WINDOW - DEC_SEQ, :] = vc_ref[b, DEC_SEQ:, :]
        vo_ref[b, WINDOW - DEC_SEQ:, :] = vn[new]
    shape = (n_cache + rows, SWA_GROUP * rows)
    r = _mod_pow2(lax.broadcasted_iota(jnp.int32, shape, 1), rows)
    c = lax.broadcasted_iota(jnp.int32, shape, 0)
    q_seq, q_t = _div_pow2(r, DEC_SEQ), _mod_pow2(r, DEC_SEQ)
    is_new = c >= n_cache
    k_seq = jnp.where(is_new, _div_pow2(c - n_cache, DEC_SEQ), _div_pow2(c, WINDOW))
    k_slot = jnp.where(is_new, WINDOW + _mod_pow2(c - n_cache, DEC_SEQ), _mod_pow2(c, WINDOW))
    rel = q_t + WINDOW - k_slot
    valid = (q_seq == k_seq) & (rel >= 0) & (rel <= WINDOW)
    o_ref[...] = _sink_attention(q, k_all, v_all, valid, sink_ref).astype(BF16)


def swa_core(proj, k_rot, sinks, cache_k, cache_v):
    nb = SEQ // SWA_BLOCK
    smem = pl.BlockSpec(memory_space=pltpu.SMEM)
    kcol = SWA_NQ // SWA_NK
    cur = lambda b, i: b * nb + i
    prev = lambda b, i: b * nb + jnp.maximum(i - 1, 0)
    o = pl.pallas_call(
        _swa_prompt_body,
        grid=(BATCH, nb),
        in_specs=[
            smem,
            pl.BlockSpec((SWA_BLOCK, SWA_NQ), lambda b, i: (cur(b, i), 0)),
            pl.BlockSpec((SWA_BLOCK, SWA_NK), lambda b, i: (cur(b, i), kcol)),
            pl.BlockSpec((SWA_BLOCK, SWA_NK), lambda b, i: (cur(b, i), kcol + 1)),
            pl.BlockSpec((SWA_BLOCK, SWA_NK), lambda b, i: (prev(b, i), kcol)),
            pl.BlockSpec((SWA_BLOCK, SWA_NK), lambda b, i: (prev(b, i), kcol + 1)),
        ],
        out_specs=pl.BlockSpec((SWA_BLOCK, SWA_NQ), lambda b, i: (cur(b, i), 0)),
        out_shape=jax.ShapeDtypeStruct((NP, SWA_NQ), BF16),
        compiler_params=_cparams("parallel", "arbitrary"),
        name="swa_prompt",
    )(sinks, proj, proj, proj, proj, proj)

    rows = SWA_SAMPLE_SEQS * DEC_SEQ
    row0 = NP // rows
    o_s, k_s, v_s = pl.pallas_call(
        _swa_sample_body,
        grid=(DEC_BATCH // SWA_SAMPLE_SEQS,),
        in_specs=[
            smem,
            pl.BlockSpec((rows, SWA_NQ), lambda g: (row0 + g, 0)),
            pl.BlockSpec((rows, SWA_NK), lambda g: (row0 + g, 0)),
            pl.BlockSpec((rows, SWA_NK), lambda g: (row0 + g, kcol + 1)),
            pl.BlockSpec((SWA_SAMPLE_SEQS, WINDOW, SWA_NK), lambda g: (g, 0, 0)),
            pl.BlockSpec((SWA_SAMPLE_SEQS, WINDOW, SWA_NK), lambda g: (g, 0, 0)),
        ],
        out_specs=[
            pl.BlockSpec((rows, SWA_NQ), lambda g: (g, 0)),
            pl.BlockSpec((SWA_SAMPLE_SEQS, WINDOW, SWA_NK), lambda g: (g, 0, 0)),
            pl.BlockSpec((SWA_SAMPLE_SEQS, WINDOW, SWA_NK), lambda g: (g, 0, 0)),
        ],
        out_shape=[jax.ShapeDtypeStruct((NS, SWA_NQ), BF16),
                   jax.ShapeDtypeStruct((DEC_BATCH, WINDOW, SWA_NK), F32),
                   jax.ShapeDtypeStruct((DEC_BATCH, WINDOW, SWA_NK), F32)],
        compiler_params=_cparams("parallel"),
        name="swa_sample",
    )(sinks, proj, k_rot, proj, cache_k, cache_v)
    return o, o_s, k_s, v_s


RWKV_ROW_TILE = 256
RWKV_PAIRS = RWKV_HEADS // 2
RWKV_STACK = MXU_DIM_V7X
RWKV_SAMPLE_SEQS = RWKV_STACK // (RWKV_PAIRS * DEC_SEQ)


def _block_diag_ones():
    r = jnp.arange(MXU_DIM_V7X) // RWKV_HD
    return (r[:, None] == r[None, :]).astype(BF16)


def _store_pairs(ref, val):
    for p in range(RWKV_PAIRS):
        ref[p] = val[:, p * LANES:(p + 1) * LANES].astype(ref.dtype)


def _load_pairs(ref):
    return jnp.concatenate([ref[p] for p in range(RWKV_PAIRS)], axis=-1)


def _rwkv_proj_body(x_ref, xprev_ref, first_ref, gn_ref, mu_ref, wr_ref, wk_ref, wv_ref, w0_ref, w1_ref, w2_ref,
                    a0_ref, a1_ref, a2_ref, g1_ref, g2_ref, kk_ref, ka_ref, bd_ref,
                    r_out, lw_out, k_out, v_out, kk_out, b_out, g_out, hlast_out, h_scr):
    i = pl.program_id(0)
    h = _rms(x_ref[...], gn_ref[...])
    prev_tile_last = _rms(xprev_ref[...], gn_ref[...])[SUBLANES - 1:SUBLANES, :]
    rowid = lax.broadcasted_iota(jnp.int32, h.shape, 0)
    xp = jnp.where(rowid == 0, prev_tile_last, pltpu.roll(h, 1, 0))
    seq_len_mask = jnp.where(i >= NP // RWKV_ROW_TILE, DEC_SEQ - 1, SEQ - 1)
    is_first = jnp.bitwise_and(i * RWKV_ROW_TILE + rowid, seq_len_mask) == 0
    xp = jnp.where(is_first, first_ref[...], xp)
    _store_pairs(h_scr, h)
    every8th = pl.ds(SUBLANES - 1, RWKV_ROW_TILE // SUBLANES, stride=SUBLANES)
    hlast_out[...] = jnp.concatenate([h_scr[p, every8th, :] for p in range(RWKV_PAIRS)], axis=-1)
    d = xp - h
    xs = [h + d * mu_ref[i:i + 1, :] for i in range(6)]
    _store_pairs(r_out, _dot(xs[0], wr_ref[...]))
    k = _dot(xs[1], wk_ref[...])
    _store_pairs(v_out, _dot(xs[2], wv_ref[...]))
    z = w0_ref[...] + _dot(jnp.tanh(_dot(xs[3], w1_ref[...])), w2_ref[...])
    _store_pairs(lw_out, -math.exp(-0.5) * jax.nn.sigmoid(z))
    a = jax.nn.sigmoid(a0_ref[...] + _dot(_dot(xs[4], a1_ref[...]), a2_ref[...]))
    g_out[...] = _dot(jax.nn.sigmoid(_dot(xs[5], g1_ref[...])), g2_ref[...]).astype(BF16)
    kk = k * kk_ref[...]
    norm = jnp.sqrt(_seg_sum(kk * kk, bd_ref[...]))
    kk = kk / jnp.maximum(norm, 1e-12)
    _store_pairs(kk_out, kk)
    _store_pairs(b_out, kk * a)
    _store_pairs(k_out, k * (1.0 + (a - 1.0) * ka_ref[...]))


def rwkv_proj(x, shift_s, p):
    first = jnp.concatenate([
        jnp.zeros((RWKV_ROW_TILE, D_MODEL), F32),
        jnp.pad(shift_s[:, None, :], ((0, 0), (0, DEC_SEQ - 1), (0, 0))).reshape(NS, D_MODEL)], axis=0)
    prompt_tiles = NP // RWKV_ROW_TILE
    row = pl.BlockSpec((RWKV_ROW_TILE, D_MODEL), lambda i: (i, 0))
    prev = pl.BlockSpec((SUBLANES, D_MODEL), lambda i: (jnp.maximum(i * (RWKV_ROW_TILE // SUBLANES) - 1, 0), 0))
    first_spec = pl.BlockSpec((RWKV_ROW_TILE, D_MODEL), lambda i: (jnp.maximum(i - prompt_tiles + 1, 0), 0))
    pair = pl.BlockSpec((RWKV_PAIRS, RWKV_ROW_TILE, LANES), lambda i: (0, i, 0))
    full = lambda a: pl.BlockSpec(a.shape, lambda i: (0,) * a.ndim)
    consts = [p["g_mix"], p["mu"], p["wr"], p["wk"], p["wv"], p["w0"], p["w1"], p["w2"], p["a0"], p["a1"],
              p["a2"], p["g1"], p["g2"], p["k_k"], p["k_a"], p["bd"]]
    pair_shape = lambda dt: jax.ShapeDtypeStruct((RWKV_PAIRS, NT, LANES), dt)
    return pl.pallas_call(
        _rwkv_proj_body,
        grid=(NT // RWKV_ROW_TILE,),
        in_specs=[row, prev, first_spec] + [full(c) for c in consts],
        out_specs=[pair] * 6 + [row, pl.BlockSpec((RWKV_ROW_TILE // SUBLANES, D_MODEL), lambda i: (i, 0))],
        out_shape=[pair_shape(BF16), pair_shape(F32)] + [pair_shape(BF16)] * 4
        + [jax.ShapeDtypeStruct((NT, D_MODEL), BF16), jax.ShapeDtypeStruct((NT // SUBLANES, D_MODEL), F32)],
        scratch_shapes=[pltpu.VMEM((RWKV_PAIRS, RWKV_ROW_TILE, LANES), F32)],
        compiler_params=_cparams("parallel"),
        name="rwkv_proj",
    )(x, x, first, *consts)


def _rwkv_chunks(c_len, problems):
    n_grp = RWKV_STACK // c_len
    shift = int(math.log2(c_len))
    assert 2 ** shift == c_len
    row = lax.broadcasted_iota(jnp.int32, (RWKV_STACK, RWKV_STACK), 0)
    col = lax.broadcasted_iota(jnp.int32, (RWKV_STACK, RWKV_STACK), 1)
    same = jnp.right_shift(row, shift) == jnp.right_shift(col, shift)
    strict_t = same & (row < col)
    incl_t = same & (row <= col)
    tri_t = jnp.where(incl_t, 1.0, 0.0).astype(BF16)
    each = lambda f, *cols: [f(*xs) for xs in zip(*cols)]
    r, lw, k, v, kk, b, state, store = zip(*problems)
    lam = each(lambda x: sum(jnp.dot(part, tri_t, preferred_element_type=F32) for part in _split3(x.T)).T, lw)
    lam3 = each(lambda x: x.reshape(n_grp, c_len, LANES), lam)
    lam_end3 = each(lambda x: x[:, c_len - 1:c_len, :], lam3)
    e_end = each(lambda e, x: jnp.exp(e - x).reshape(RWKV_STACK, LANES), lam_end3, lam3)
    e_tot3 = each(jnp.exp, lam_end3)
    e_neg = each(lambda x: jnp.exp(-x), lam)
    at = each(lambda kk_, l, w: -kk_ * jnp.exp(l - w), kk, lam, lw)
    rt = each(lambda r_, l: r_ * jnp.exp(l), r, lam)
    mul = lambda x, y: x * y
    bh, kh, bb, kb = each(mul, b, e_neg), each(mul, k, e_neg), each(mul, b, e_end), each(mul, k, e_end)

    lane = lax.broadcasted_iota(jnp.int32, (RWKV_STACK, LANES), 1)
    halves = (lane < RWKV_HD, lane >= RWKV_HD)
    rhs = each(lambda x, y: jnp.concatenate([x, y], axis=0), bh, kh)
    at_pow, bt_up, rt_up = [], [], []
    for sel in halves:
        lhs = each(lambda x, y: jnp.concatenate([jnp.where(sel, x, 0.0), jnp.where(sel, y, 0.0)], axis=0), at, rt)
        mt = each(_dot_nt, rhs, lhs)
        at_pow.append(each(lambda x: jnp.where(strict_t, x[:RWKV_STACK, :RWKV_STACK], 0.0), mt))
        bt_up.append(each(lambda x: jnp.where(strict_t, x[RWKV_STACK:, :RWKV_STACK], 0.0), mt))
        rt_up.append(each(lambda x: jnp.concatenate([jnp.where(incl_t, x[:RWKV_STACK, RWKV_STACK:], 0.0),
                                                     jnp.where(incl_t, x[RWKV_STACK:, RWKV_STACK:], 0.0)], axis=0), mt))

    rows = lambda t, g: t[g * c_len:(g + 1) * c_len]
    states = each(lambda st: [st(g) for g in range(n_grp)], state)
    p = [each(lambda x, y, st: _dot_nt(jnp.concatenate([rows(x, g), rows(y, g)], axis=0), st[g]), at, rt, states)
         for g in range(n_grp)]
    n_prob = len(problems)
    a_s_t = [jnp.concatenate([p[g][i][:c_len] for g in range(n_grp)], axis=0).T for i in range(n_prob)]
    r_s_t = [jnp.concatenate([p[g][i][c_len:] for g in range(n_grp)], axis=0).T for i in range(n_prob)]
    v_t = each(lambda x: x.T, v)
    half_rows = lambda x, e: x[e * RWKV_HD:(e + 1) * RWKV_HD]
    n_half = len(halves)
    u_t = [each(lambda a_, v_, b_: half_rows(a_, e) + _dot(half_rows(v_, e), b_), a_s_t, v_t, bt_up[e])
           for e in range(n_half)]
    assert shift >= 2
    for it in range(shift - 1):
        u_t = [each(lambda u_, a_: u_ + _dot(u_, a_), u_t[e], at_pow[e]) for e in range(n_half)]
        if it + 2 < shift:
            at_pow = [each(lambda a_: _dot(a_, a_), at_pow[e]) for e in range(n_half)]
    z_t = [each(lambda u_, a_: _dot(u_, a_), u_t[e], at_pow[e]) for e in range(n_half)]
    u_t = [each(lambda u_, z_, a_: u_ + _dot(z_, a_), u_t[e], z_t[e], at_pow[e]) for e in range(n_half)]
    y_t = [each(lambda r_, u_, v_, m_: half_rows(r_, e) + _dot(jnp.concatenate([u_, half_rows(v_, e)], axis=1), m_),
                r_s_t, u_t[e], v_t, rt_up[e]) for e in range(n_half)]
    stack_halves = lambda parts: [jnp.concatenate([parts[e][i] for e in range(n_half)], axis=0).T
                                  for i in range(n_prob)]
    u = stack_halves(u_t)
    y = stack_halves(y_t)

    r128 = lax.broadcasted_iota(jnp.int32, (LANES, LANES), 0)
    c128 = lax.broadcasted_iota(jnp.int32, (LANES, LANES), 1)
    diag = (r128 < RWKV_HD) == (c128 < RWKV_HD)
    for g in range(n_grp):
        upd = each(lambda u_, v_, bb_, kb_: _dot_tn(jnp.concatenate([rows(u_, g), rows(v_, g)], axis=0),
                                                    jnp.concatenate([rows(bb_, g), rows(kb_, g)], axis=0)),
                   u, v, bb, kb)
        for i in range(n_prob):
            store[i](g, states[i][g] * e_tot3[i][g] + jnp.where(diag, upd[i], 0.0))
    return y


RWKV_PROMPT_SEQS = 4
RWKV_STEP_TOKENS = 128


def _rwkv_prompt_body(*refs):
    ins, (y_ref, s_ref, s_scr) = refs[:6 * RWKV_PROMPT_SEQS], refs[6 * RWKV_PROMPT_SEQS:]
    c = pl.program_id(1)

    @pl.when(c == 0)
    def _():
        s_scr[...] = jnp.zeros_like(s_scr)

    def one_chunk(ci, carry):
        rows = pl.ds(pl.multiple_of(ci * RWKV_CHUNK, RWKV_CHUNK), RWKV_CHUNK)
        def problem(seq):
            def store(g, s):
                s_scr[seq, g] = s

            args = [ins[a * RWKV_PROMPT_SEQS + seq][:, rows, :].reshape(RWKV_STACK, LANES).astype(F32)
                    for a in range(6)]
            return (*args, lambda g: s_scr[seq, g], store)

        ys = _rwkv_chunks(RWKV_CHUNK, [problem(seq) for seq in range(RWKV_PROMPT_SEQS)])
        for seq, y in enumerate(ys):
            y_ref[:, seq, rows, :] = y.reshape(RWKV_PAIRS, RWKV_CHUNK, LANES)
        return carry

    lax.fori_loop(0, RWKV_STEP_TOKENS // RWKV_CHUNK, one_chunk, 0)

    @pl.when(c == pl.num_programs(1) - 1)
    def _():
        for seq in range(RWKV_PROMPT_SEQS):
            for p in range(RWKV_PAIRS):
                s_ref[seq, 2 * p] = s_scr[seq, p, :RWKV_HD, :RWKV_HD]
                s_ref[seq, 2 * p + 1] = s_scr[seq, p, RWKV_HD:, RWKV_HD:]


RWKV_SAMPLE_PROBLEMS = 2


def _rwkv_sample_body(r_ref, lw_ref, k_ref, v_ref, kk_ref, b_ref, s0_ref, y_ref, s_ref):
    zero = jnp.zeros((RWKV_HD, RWKV_HD), F32)
    rows_per = RWKV_SAMPLE_SEQS * DEC_SEQ

    def problem(q):
        def state(g):
            p, seq = divmod(g, RWKV_SAMPLE_SEQS)
            seq += q * RWKV_SAMPLE_SEQS
            return jnp.concatenate([jnp.concatenate([s0_ref[seq, 2 * p], zero], axis=1),
                                    jnp.concatenate([zero, s0_ref[seq, 2 * p + 1]], axis=1)], axis=0)

        def store(g, s):
            p, seq = divmod(g, RWKV_SAMPLE_SEQS)
            seq += q * RWKV_SAMPLE_SEQS
            s_ref[seq, 2 * p] = s[:RWKV_HD, :RWKV_HD]
            s_ref[seq, 2 * p + 1] = s[RWKV_HD:, RWKV_HD:]

        args = [ref[:, q * rows_per:(q + 1) * rows_per, :].reshape(RWKV_STACK, LANES).astype(F32)
                for ref in (r_ref, lw_ref, k_ref, v_ref, kk_ref, b_ref)]
        return (*args, state, store)

    ys = _rwkv_chunks(DEC_SEQ, [problem(q) for q in range(RWKV_SAMPLE_PROBLEMS)])
    for q, y in enumerate(ys):
        y_ref[:, q * rows_per:(q + 1) * rows_per, :] = y.reshape(RWKV_PAIRS, rows_per, LANES)


def rwkv_core(r, lw, k, v, kk, b, state_s):
    assert RWKV_PAIRS * RWKV_CHUNK == RWKV_STACK
    nc = SEQ // RWKV_STEP_TOKENS
    st_shape = (RWKV_HEADS, RWKV_HD, RWKV_HD)
    seq_blk = lambda seq: pl.BlockSpec((RWKV_PAIRS, RWKV_STEP_TOKENS, LANES),
                                       lambda bi, c: (0, (bi * RWKV_PROMPT_SEQS + seq) * nc + c, 0))
    inputs = (r, lw, k, v, kk, b)
    y_p, s_p = pl.pallas_call(
        _rwkv_prompt_body,
        grid=(BATCH // RWKV_PROMPT_SEQS, nc),
        in_specs=[seq_blk(seq) for _ in inputs for seq in range(RWKV_PROMPT_SEQS)],
        out_specs=[pl.BlockSpec((RWKV_PAIRS, RWKV_PROMPT_SEQS, RWKV_STEP_TOKENS, LANES), lambda bi, c: (0, bi, c, 0)),
                   pl.BlockSpec((RWKV_PROMPT_SEQS,) + st_shape, lambda bi, c: (bi, 0, 0, 0))],
        out_shape=[jax.ShapeDtypeStruct((RWKV_PAIRS, BATCH, SEQ, LANES), F32),
                   jax.ShapeDtypeStruct((BATCH,) + st_shape, F32)],
        scratch_shapes=[pltpu.VMEM((RWKV_PROMPT_SEQS, RWKV_PAIRS, LANES, LANES), F32)],
        compiler_params=_cparams("parallel", "arbitrary"),
        name="rwkv_prompt",
    )(*[a for a in inputs for _ in range(RWKV_PROMPT_SEQS)])

    step_seqs = RWKV_SAMPLE_PROBLEMS * RWKV_SAMPLE_SEQS
    rows = step_seqs * DEC_SEQ
    row0 = NP // rows
    blk = pl.BlockSpec((RWKV_PAIRS, rows, LANES), lambda i: (0, row0 + i, 0))
    st = pl.BlockSpec((step_seqs,) + st_shape, lambda i: (i, 0, 0, 0))
    y_s, s_s = pl.pallas_call(
        _rwkv_sample_body,
        grid=(DEC_BATCH // step_seqs,),
        in_specs=[blk] * 6 + [st],
        out_specs=[pl.BlockSpec((RWKV_PAIRS, rows, LANES), lambda i: (0, i, 0)), st],
        out_shape=[jax.ShapeDtypeStruct((RWKV_PAIRS, NS, LANES), F32),
                   jax.ShapeDtypeStruct((DEC_BATCH,) + st_shape, F32)],
        compiler_params=_cparams("parallel"),
        name="rwkv_sample",
    )(*inputs, state_s)
    return y_p.reshape(RWKV_PAIRS, NP, LANES), y_s, s_p, s_s


def _rwkv_out_body(x_ref, yp_ref, ys_ref, r_ref, k_ref, v_ref, g_ref, rk_ref, lnw_ref, lnb_ref, bd_ref, wo_ref,
                   o_ref):
    bd = bd_ref[...]
    y = _pick_group(RWKV_ROW_TILE, _load_pairs(yp_ref), _load_pairs(ys_ref))
    yc = y - _seg_sum(y, bd) * (1.0 / RWKV_HD)
    var = _seg_sum(yc * yc, bd) * (1.0 / RWKV_HD)
    yn = yc * lax.rsqrt(var + RWKV_GN_EPS) * lnw_ref[...] + lnb_ref[...]
    rk = _load_pairs(r_ref).astype(F32) * _load_pairs(k_ref).astype(F32)
    bonus = _seg_sum(rk * rk_ref[...], bd) * _load_pairs(v_ref).astype(F32)
    z = (yn + bonus) * g_ref[...].astype(F32)
    o_ref[...] = x_ref[...] + _dot(z, wo_ref[...])


def rwkv_out(x, y_p, y_s, r, k, v, g, p):
    row = pl.BlockSpec((RWKV_ROW_TILE, D_MODEL), lambda i: (i, 0))
    pair_of = lambda f: pl.BlockSpec((RWKV_PAIRS, RWKV_ROW_TILE, LANES), lambda i: (0, f(i), 0))
    full = lambda a: pl.BlockSpec(a.shape, lambda i: (0,) * a.ndim)
    consts = [p["r_k"], p["ln_w"], p["ln_b"], p["bd"], p["w_out"]]
    return pl.pallas_call(
        _rwkv_out_body,
        grid=(NT // RWKV_ROW_TILE,),
        in_specs=[row, *_group_specs(RWKV_ROW_TILE, pair_of)] + [pair_of(lambda i: i)] * 3 + [row]
        + [full(c) for c in consts],
        out_specs=row,
        out_shape=jax.ShapeDtypeStruct((NT, D_MODEL), F32),
        compiler_params=_cparams("parallel"),
        name="rwkv_out",
    )(x, y_p, y_s, r, k, v, g, *consts)


def _pad_cols(w, n):
    return jnp.pad(w, ((0, 0), (0, n - w.shape[1])))


def _pad_rows(w, n):
    return jnp.pad(w, ((0, n - w.shape[0]), (0, 0)))


def kernel(x_prompt, x_sample, state_ret, cache_swa_k, cache_swa_v, state_rwkv_wkv, state_rwkv_shift,
           norm_mix, norm_mlp, norm_final, ret_w_in, ret_w_out,
           swa_w_in, swa_b_in, swa_sinks, swa_w_out,
           rwkv_mu, rwkv_w_rkv, rwkv_w0, rwkv_w1, rwkv_w2, rwkv_a0, rwkv_a1, rwkv_a2,
           rwkv_g1, rwkv_g2, rwkv_k_k, rwkv_k_a, rwkv_r_k, rwkv_ln_w, rwkv_ln_b, rwkv_w_out,
           mlp_w_up, mlp_w_down):
    x = (x_prompt.reshape(NP, D_MODEL), x_sample.reshape(NS, D_MODEL))
    if LAYER_KINDS[0] != 0:
        x = jnp.concatenate(x, axis=0)
    row = lambda t: t.reshape(1, -1)
    ret_p, k_p, k_s, v_p, v_s, wkv_p, wkv_s, sh_p, sh_s = ([] for _ in range(9))
    ret_s = None
    for layer in range(DEPTH):
        kind = LAYER_KINDS[layer]
        j = LAYER_KINDS[:layer].count(kind)
        g_mix = row(norm_mix[layer])
        if kind == 0:
            proj = ret_proj(x, g_mix, ret_w_in[j].astype(BF16))
            o_p, o_s, s_p, ret_s = retention_core(proj, state_ret, j, ret_s)
            x = matmul_residual(x, o_p, o_s, ret_w_out[j].astype(BF16))
            ret_p.append(s_p)
        elif kind == 1:
            proj, k_rot = swa_proj(x, g_mix, swa_w_in[j].astype(BF16), row(swa_b_in[j]))
            o_p, o_s, kc, vc = swa_core(proj, k_rot, swa_sinks[j],
                                        cache_swa_k[j].reshape(DEC_BATCH, WINDOW, SWA_NK),
                                        cache_swa_v[j].reshape(DEC_BATCH, WINDOW, SWA_NK))
            x = matmul_residual(x, o_p, o_s, swa_w_out[j].astype(BF16))
            kv_shape = (-1, WINDOW, SWA_KV_HEADS, SWA_HD)
            k_p.append(k_rot[:NP].reshape(BATCH, SEQ, SWA_NK)[:, -WINDOW:].reshape(kv_shape))
            v_p.append(proj[:NP, SWA_NQ + SWA_NK:].reshape(BATCH, SEQ, SWA_NK)[:, -WINDOW:]
                       .astype(F32).reshape(kv_shape))
            k_s.append(kc.reshape(kv_shape))
            v_s.append(vc.reshape(kv_shape))
        else:
            lora_w = LANES
            lora_g = 2 * LANES
            p = dict(
                mu=rwkv_mu[j], wr=rwkv_w_rkv[j, 0].astype(BF16), wk=rwkv_w_rkv[j, 1].astype(BF16),
                wv=rwkv_w_rkv[j, 2].astype(BF16), w0=row(rwkv_w0[j]),
                w1=_pad_cols(rwkv_w1[j], lora_w).astype(BF16), w2=_pad_rows(rwkv_w2[j], lora_w).astype(BF16),
                a0=row(rwkv_a0[j]),
                a1=_pad_cols(rwkv_a1[j], lora_w).astype(BF16), a2=_pad_rows(rwkv_a2[j], lora_w).astype(BF16),
                g1=_pad_cols(rwkv_g1[j], lora_g).astype(BF16), g2=_pad_rows(rwkv_g2[j], lora_g).astype(BF16),
                k_k=row(rwkv_k_k[j]), k_a=row(rwkv_k_a[j]), r_k=row(rwkv_r_k[j]),
                ln_w=row(rwkv_ln_w[j]), ln_b=row(rwkv_ln_b[j]), bd=_block_diag_ones(),
                w_out=rwkv_w_out[j].astype(BF16), g_mix=g_mix)
            r, lw, k, v, kk, b, g, h_last = rwkv_proj(x, state_rwkv_shift[j], p)
            y_p, y_s, s_p, s_s = rwkv_core(r, lw, k, v, kk, b, state_rwkv_wkv[j])
            x = rwkv_out(x, y_p, y_s, r, k, v, g, p)
            wkv_p.append(s_p)
            wkv_s.append(s_s)
            sh_p.append(h_last[SEQ // SUBLANES - 1:NP // SUBLANES:SEQ // SUBLANES])
            sh_s.append(h_last[NP // SUBLANES:])
        x = mlp(x, row(norm_mlp[layer]), mlp_w_up[layer].astype(BF16), mlp_w_down[layer].astype(BF16),
                final_g=row(norm_final) if layer == DEPTH - 1 else None)
    y_p, y_s = x
    return (y_p.reshape(BATCH, SEQ, D_MODEL), y_s.reshape(DEC_BATCH, DEC_SEQ, D_MODEL),
            jnp.stack(ret_p), ret_s, jnp.stack(k_p), jnp.stack(k_s), jnp.stack(v_p), jnp.stack(v_s),
            jnp.stack(wkv_p), jnp.stack(wkv_s), jnp.stack(sh_p), jnp.stack(sh_s))
```

```python
import functools
import math

import jax
import jax.numpy as jnp
from jax import lax
from jax.experimental import pallas as pl
from jax.experimental.pallas import tpu as pltpu

F32 = jnp.float32
BF16 = jnp.bfloat16

D_MODEL = 1024
BATCH = 8
SEQ = 2048
DEPTH = 4
DEC_BATCH = 128
DEC_SEQ = 8
PAST_LEN = 8192
LAYER_KINDS = tuple(i % 3 for i in range(DEPTH))
NORM_EPS = 1e-6
ROPE_THETA = 10000.0
NEG_INF = -1e30
D_FF = 4 * D_MODEL

RET_HEADS = 4
RET_DK = D_MODEL // RET_HEADS
RET_DV = 2 * D_MODEL // RET_HEADS
RET_CHUNK = 128
RET_PROMPT_CHUNK = 256
RET_NQ = RET_HEADS * RET_DK
RET_NV = RET_HEADS * RET_DV

SWA_HEADS = 16
SWA_KV_HEADS = 4
SWA_GROUP = SWA_HEADS // SWA_KV_HEADS
SWA_HD = D_MODEL // SWA_HEADS
WINDOW = 128
SWA_BLOCK = 128
SWA_NQ = SWA_HEADS * SWA_HD
SWA_NK = SWA_KV_HEADS * SWA_HD

RWKV_HD = 64
RWKV_HEADS = D_MODEL // RWKV_HD
RWKV_GN_EPS = 64e-5
RWKV_CHUNK = 32

NP = BATCH * SEQ
NS = DEC_BATCH * DEC_SEQ
NT = NP + NS

VMEM_LIMIT_V7X = 48 * 1024 * 1024
STREAMED_F32_VMEM_LIMIT_V7X = 56 * 1024 * 1024
LANES = 128
SUBLANES = 8
MXU_DIM_V7X = 256

ROW_TILE = 1024


def _cparams(*sem, vmem=VMEM_LIMIT_V7X):
    return pltpu.CompilerParams(dimension_semantics=sem, vmem_limit_bytes=vmem)


def _dot(a, b):
    return jnp.dot(a.astype(BF16), b.astype(BF16), preferred_element_type=F32)


def _dot_nt(a, b):
    return lax.dot_general(a.astype(BF16), b.astype(BF16), (((1,), (1,)), ((), ())),
                           preferred_element_type=F32)


def _dot_tn(a, b):
    return lax.dot_general(a.astype(BF16), b.astype(BF16), (((0,), (0,)), ((), ())),
                           preferred_element_type=F32)


def _div_pow2(x, n):
    assert n & (n - 1) == 0
    return jnp.right_shift(x, n.bit_length() - 1)


def _mod_pow2(x, n):
    assert n & (n - 1) == 0
    return jnp.bitwise_and(x, n - 1)


def _sigmoid(x):
    return 0.5 * jnp.tanh(0.5 * x) + 0.5


def _rms(x, g):
    ms = jnp.mean(x * x, axis=-1, keepdims=True)
    return x * lax.rsqrt(ms + NORM_EPS) * g


def _split3(x):
    hi = x.astype(BF16)
    r1 = x - hi.astype(F32)
    mid = r1.astype(BF16)
    lo = (r1 - mid.astype(F32)).astype(BF16)
    return hi, mid, lo


def _seg_sum(x, bd):
    xb = x.astype(BF16)
    return jnp.concatenate(
        [jnp.dot(xb[:, j * MXU_DIM_V7X:(j + 1) * MXU_DIM_V7X], bd, preferred_element_type=F32)
         for j in range(D_MODEL // MXU_DIM_V7X)], axis=-1)


def _rope_tables(pos, half):
    inv_freq = jnp.power(ROPE_THETA, -jnp.arange(half, dtype=F32) / half)
    ang = pos.astype(F32)[:, None] * inv_freq[None, :]
    return jnp.cos(ang), jnp.sin(ang)


def _group_specs(tile, block_of):
    n_prompt = NP // tile
    return (block_of(lambda i: jnp.minimum(i, n_prompt - 1)), block_of(lambda i: jnp.maximum(i - n_prompt, 0)))


def _pick_group(tile, p_val, s_val):
    return jnp.where(pl.program_id(0) < NP // tile, p_val, s_val)


def _token_rows(x, tile):
    if isinstance(x, tuple):
        return list(x), list(_group_specs(tile, lambda f: pl.BlockSpec((tile, D_MODEL), lambda *g: (f(g[0]), 0))))
    return [x], [pl.BlockSpec((tile, D_MODEL), lambda *g: (g[0], 0))]


def _read_rows(refs, tile):
    return refs[0][...] if len(refs) == 1 else _pick_group(tile, refs[0][...], refs[1][...])


def _rope_table_inputs(half, widen, tile):
    tp = widen(*_rope_tables(jnp.arange(SEQ, dtype=jnp.int32), half))
    ts = [jnp.tile(t, (DEC_BATCH, 1))
          for t in widen(*_rope_tables(PAST_LEN + jnp.arange(DEC_SEQ, dtype=jnp.int32), half))]
    n_prompt = NP // tile
    per_seq = SEQ // tile
    p_spec = pl.BlockSpec((tile, LANES), lambda *g: (g[0] % per_seq, 0))
    s_spec = pl.BlockSpec((tile, LANES), lambda *g: (jnp.maximum(g[0] - n_prompt, 0), 0))
    return [tp[0], tp[1], ts[0], ts[1]], [p_spec, p_spec, s_spec, s_spec]


RET_PROJ_TILE = RET_NQ + RET_NQ


def _ret_proj_body(n_x, *refs):
    x_refs, (g_ref, w_ref, cosp_ref, sinp_ref, coss_ref, sins_ref, o_ref, xn_ref) = refs[:n_x], refs[n_x:]
    j = pl.program_id(1)

    @pl.when(j == 0)
    def _():
        xn_ref[...] = _rms(_read_rows(x_refs, ROW_TILE), g_ref[...]).astype(BF16)

    acc = jnp.dot(xn_ref[...], w_ref[...], preferred_element_type=F32)

    @pl.when(j == 0)
    def _():
        cos = _pick_group(ROW_TILE, cosp_ref[...], coss_ref[...])
        sin = _pick_group(ROW_TILE, sinp_ref[...], sins_ref[...])
        half = RET_DK // 2
        for blk in range(RET_PROJ_TILE // RET_DK):
            lo, mid, hi = blk * RET_DK, blk * RET_DK + half, (blk + 1) * RET_DK
            x1, x2 = acc[:, lo:mid], acc[:, mid:hi]
            scale = 1.0 if blk < RET_HEADS else RET_DK ** -0.5
            o_ref[:, lo:mid] = ((x1 * cos - x2 * sin) * scale).astype(BF16)
            o_ref[:, mid:hi] = ((x2 * cos + x1 * sin) * scale).astype(BF16)

    @pl.when(j > 0)
    def _():
        o_ref[...] = acc.astype(BF16)


def ret_proj(x, g, w):
    n_out = w.shape[1]
    x_arrays, x_specs = _token_rows(x, ROW_TILE)
    tables, table_specs = _rope_table_inputs(RET_DK // 2, lambda c, s: (c, s), ROW_TILE)
    return pl.pallas_call(
        functools.partial(_ret_proj_body, len(x_arrays)),
        grid=(NT // ROW_TILE, n_out // RET_PROJ_TILE),
        in_specs=x_specs + [
            pl.BlockSpec((1, D_MODEL), lambda i, j: (0, 0)),
            pl.BlockSpec((D_MODEL, RET_PROJ_TILE), lambda i, j: (0, j)),
        ] + table_specs,
        out_specs=pl.BlockSpec((ROW_TILE, RET_PROJ_TILE), lambda i, j: (i, j)),
        out_shape=jax.ShapeDtypeStruct((NT, n_out), BF16),
        scratch_shapes=[pltpu.VMEM((ROW_TILE, D_MODEL), BF16)],
        compiler_params=_cparams("parallel", "arbitrary"),
        name="ret_proj",
    )(*x_arrays, g, w, *tables)


SWA_PROJ_ROW_TILE = 512


def _swa_rope(x, cos, sin):
    n = x.shape[1]
    half = SWA_HD // 2
    lane = lax.broadcasted_iota(jnp.int32, x.shape, 1)
    partner = jnp.where(_mod_pow2(lane, SWA_HD) < half, pltpu.roll(x, n - half, 1), pltpu.roll(x, half, 1))
    reps = n // LANES
    return x * jnp.concatenate([cos] * reps, axis=-1) + partner * jnp.concatenate([sin] * reps, axis=-1)


def _swa_proj_body(x_ref, g_ref, w_ref, b_ref, cosp_ref, sinp_ref, coss_ref, sins_ref, o_ref, kr_ref):
    xn = _rms(x_ref[...], g_ref[...]).astype(BF16)
    acc = jnp.dot(xn, w_ref[...], preferred_element_type=F32) + b_ref[...]
    n_qk = SWA_NQ + SWA_NK
    cos = _pick_group(SWA_PROJ_ROW_TILE, cosp_ref[...], coss_ref[...])
    sin = _pick_group(SWA_PROJ_ROW_TILE, sinp_ref[...], sins_ref[...])
    qk = _swa_rope(acc[:, :n_qk], cos, sin)
    assert math.log2(SWA_HD ** 0.5).is_integer()
    o_ref[:, :SWA_NQ] = (qk[:, :SWA_NQ] * (SWA_HD ** -0.5)).astype(BF16)
    o_ref[:, SWA_NQ:n_qk] = qk[:, SWA_NQ:].astype(BF16)
    o_ref[:, n_qk:] = acc[:, n_qk:].astype(BF16)
    kr_ref[...] = qk[:, SWA_NQ:]


def swa_proj(x, g, w, b):
    n_out = w.shape[1]

    def widen(c, s):
        rep = LANES // SWA_HD
        return jnp.tile(jnp.concatenate([c, c], axis=-1), (1, rep)), jnp.tile(jnp.concatenate([-s, s], axis=-1), (1, rep))

    tile = SWA_PROJ_ROW_TILE
    tables, table_specs = _rope_table_inputs(SWA_HD // 2, widen, tile)
    return pl.pallas_call(
        _swa_proj_body,
        grid=(NT // tile,),
        in_specs=[
            pl.BlockSpec((tile, D_MODEL), lambda i: (i, 0)),
            pl.BlockSpec((1, D_MODEL), lambda i: (0, 0)),
            pl.BlockSpec((D_MODEL, n_out), lambda i: (0, 0)),
            pl.BlockSpec((1, n_out), lambda i: (0, 0)),
        ] + table_specs,
        out_specs=[pl.BlockSpec((tile, n_out), lambda i: (i, 0)), pl.BlockSpec((tile, SWA_NK), lambda i: (i, 0))],
        out_shape=[jax.ShapeDtypeStruct((NT, n_out), BF16), jax.ShapeDtypeStruct((NT, SWA_NK), F32)],
        compiler_params=_cparams("parallel"),
        name="swa_proj",
    )(x, g, w, b, *tables)


def _matmul_residual_body(n_x, *refs):
    x_refs, (ap_ref, as_ref, w_ref, o_ref) = refs[:n_x], refs[n_x:]
    a = _pick_group(ROW_TILE, ap_ref[...], as_ref[...])
    o_ref[...] = _read_rows(x_refs, ROW_TILE) + jnp.dot(a, w_ref[...], preferred_element_type=F32)


def matmul_residual(x, a_p, a_s, w):
    k = a_p.shape[1]
    x_arrays, x_specs = _token_rows(x, ROW_TILE)
    return pl.pallas_call(
        functools.partial(_matmul_residual_body, len(x_arrays)),
        grid=(NT // ROW_TILE,),
        in_specs=x_specs + [
            *_group_specs(ROW_TILE, lambda f: pl.BlockSpec((ROW_TILE, k), lambda i: (f(i), 0))),
            pl.BlockSpec((k, D_MODEL), lambda i: (0, 0)),
        ],
        out_specs=pl.BlockSpec((ROW_TILE, D_MODEL), lambda i: (i, 0)),
        out_shape=jax.ShapeDtypeStruct((NT, D_MODEL), F32),
        compiler_params=_cparams("parallel"),
        name="matmul_residual",
    )(*x_arrays, a_p, a_s, w)


MLP_FF_TILE = 1024


def _mlp_body(has_final_norm, x_ref, g_ref, wu_ref, wd_ref, *rest):
    xn_ref, acc_ref = rest[-2:]
    f = pl.program_id(1)

    @pl.when(f == 0)
    def _():
        xn_ref[...] = _rms(x_ref[...], g_ref[...]).astype(BF16)
        acc_ref[...] = jnp.zeros_like(acc_ref)

    a = jnp.maximum(jnp.dot(xn_ref[...], wu_ref[...].astype(BF16), preferred_element_type=F32), 0.0)
    acc_ref[...] += jnp.dot((a * a).astype(BF16), wd_ref[...].astype(BF16), preferred_element_type=F32)

    @pl.when(f == pl.num_programs(1) - 1)
    def _():
        out = x_ref[...] + acc_ref[...]
        if not has_final_norm:
            o_ref, = rest[:-2]
            o_ref[...] = out
        else:
            gf_ref, yp_ref, ys_ref = rest[:-2]
            y = _rms(out, gf_ref[...])
            is_prompt = pl.program_id(0) < NP // ROW_TILE

            @pl.when(is_prompt)
            def _():
                yp_ref[...] = y

            @pl.when(jnp.logical_not(is_prompt))
            def _():
                ys_ref[...] = y


def mlp(x, g, w_up, w_down, layer, final_g=None):
    row = pl.BlockSpec((ROW_TILE, D_MODEL), lambda i, f: (i, 0))
    vec = pl.BlockSpec((1, D_MODEL), lambda i, f: (0, 0))
    if final_g is None:
        extra_in, extra_specs = [], []
        out_specs = row
        out_shape = jax.ShapeDtypeStruct((NT, D_MODEL), F32)
    else:
        extra_in, extra_specs = [final_g], [vec]
        out_specs = list(_group_specs(ROW_TILE, lambda g_: pl.BlockSpec((ROW_TILE, D_MODEL), lambda i, f: (g_(i), 0))))
        out_shape = [jax.ShapeDtypeStruct((NP, D_MODEL), F32), jax.ShapeDtypeStruct((NS, D_MODEL), F32)]
    return pl.pallas_call(
        functools.partial(_mlp_body, final_g is not None),
        grid=(NT // ROW_TILE, D_FF // MLP_FF_TILE),
        in_specs=[
            row, vec,
            pl.BlockSpec((None, D_MODEL, MLP_FF_TILE), lambda i, f: (layer, 0, f)),
            pl.BlockSpec((None, MLP_FF_TILE, D_MODEL), lambda i, f: (layer, f, 0)),
        ] + extra_specs,
        out_specs=out_specs,
        out_shape=out_shape,
        scratch_shapes=[pltpu.VMEM((ROW_TILE, D_MODEL), BF16), pltpu.VMEM((ROW_TILE, D_MODEL), F32)],
        compiler_params=_cparams("parallel" if final_g is None else "arbitrary", "arbitrary", vmem=STREAMED_F32_VMEM_LIMIT_V7X),
        name="mlp",
    )(x, g, w_up, w_down, *extra_in)


def _ret_tables(c):
    log_gamma = jnp.log1p(-jnp.exp2(-5.0 - jnp.arange(RET_HEADS, dtype=F32)))
    idx = jnp.arange(c, dtype=F32)
    diff = idx[:, None] - idx[None, :]
    inner = jnp.where(diff >= 0, jnp.exp(log_gamma[:, None, None] * jnp.maximum(diff, 0.0)), 0.0)
    q_decay = jnp.exp(log_gamma[:, None] * (idx[None, :] + 1.0))
    k_decay = jnp.exp(log_gamma[:, None] * (c - 1.0 - idx[None, :]))
    chunk_decay = jnp.exp(log_gamma * c)
    rep = lambda t: jnp.broadcast_to(t[:, :, None], (RET_HEADS, c, LANES))
    return inner, rep(q_decay), rep(k_decay), chunk_decay


def _ret_heads(rows, lo, width):
    return jnp.stack([rows[:, lo + h * width:lo + (h + 1) * width] for h in range(RET_HEADS)])


def _ret_chunk(rows_list, s0, inner, qd, kd, cd_ref):
    n_seq = len(rows_list)
    cat = lambda f: jnp.concatenate([f(r) for r in rows_list], axis=0)
    q = cat(lambda r: _ret_heads(r, 0, RET_DK)).astype(BF16)
    k = cat(lambda r: _ret_heads(r, RET_NQ, RET_DK)).astype(F32)
    v = cat(lambda r: _ret_heads(r, 2 * RET_NQ, RET_DV)).astype(BF16)
    g = cat(lambda r: _ret_heads(r, 2 * RET_NQ + RET_NV, RET_DV)).astype(F32)
    per_seq = lambda t: jnp.concatenate([t] * n_seq, axis=0)
    inner, qd, kd = per_seq(inner), per_seq(qd), per_seq(kd)
    cd = per_seq(jnp.stack([jnp.full((1, 1), cd_ref[h], F32) for h in range(RET_HEADS)]))
    qd = jnp.concatenate([qd] * (RET_DV // LANES), axis=-1)
    kd = jnp.concatenate([kd] * (RET_DK // LANES), axis=-1)
    scores = jnp.einsum('bqd,bkd->bqk', q, k.astype(BF16), preferred_element_type=F32) * inner
    o = (jnp.einsum('bqk,bke->bqe', scores.astype(BF16), v, preferred_element_type=F32)
         + jnp.einsum('bqd,bde->bqe', q, s0.astype(BF16), preferred_element_type=F32) * qd)
    s_new = s0 * cd + jnp.einsum('bkd,bke->bde', (k * kd).astype(BF16), v, preferred_element_type=F32)
    o = o * lax.rsqrt(jnp.mean(o * o, axis=-1, keepdims=True) + NORM_EPS)
    return o * (g * _sigmoid(g)), s_new


def _ret_prompt_body(cd_ref, proj_ref, inner_ref, qd_ref, kd_ref, o_ref, s_ref, s_scr):
    c = pl.program_id(1)

    @pl.when(c == 0)
    def _():
        s_scr[...] = jnp.zeros_like(s_scr)

    o, s_new = _ret_chunk([proj_ref[...]], s_scr[...], inner_ref[...], qd_ref[...], kd_ref[...], cd_ref)
    s_scr[...] = s_new
    o_ref[...] = jnp.concatenate([o[h] for h in range(RET_HEADS)], axis=-1).astype(BF16)

    @pl.when(c == pl.num_programs(1) - 1)
    def _():
        s_ref[0] = s_scr[...]


RET_SAMPLE_SEQS = 2


def _ret_sample_body(cd_ref, proj_ref, inner_ref, qd_ref, kd_ref, s0_ref, *rest):
    o_ref, s_ref = rest[-2:]
    rows = proj_ref[...].astype(F32)
    batch = RET_SAMPLE_SEQS * RET_HEADS
    o, s_new = _ret_chunk([rows[seq * DEC_SEQ:(seq + 1) * DEC_SEQ] for seq in range(RET_SAMPLE_SEQS)],
                          s0_ref[0].reshape(batch, RET_DK, RET_DV), inner_ref[...], qd_ref[...], kd_ref[...],
                          cd_ref)
    s_ref[0] = s_new.reshape(RET_SAMPLE_SEQS, RET_HEADS, RET_DK, RET_DV)
    o_ref[...] = jnp.concatenate(
        [jnp.concatenate([o[seq * RET_HEADS + h] for h in range(RET_HEADS)], axis=-1)
         for seq in range(RET_SAMPLE_SEQS)], axis=0).astype(BF16)


def retention_core(proj, state_ret, j, s_stack):
    n_in = proj.shape[1]
    nc = SEQ // RET_PROMPT_CHUNK
    smem = pl.BlockSpec(memory_space=pltpu.SMEM)
    inner, qd, kd, cd = _ret_tables(RET_PROMPT_CHUNK)
    full3 = lambda shape: pl.BlockSpec(shape, lambda b, c: (0, 0, 0))
    o, s_p = pl.pallas_call(
        _ret_prompt_body,
        grid=(BATCH, nc),
        in_specs=[
            smem,
            pl.BlockSpec((RET_PROMPT_CHUNK, n_in), lambda b, c: (b * nc + c, 0)),
            full3(inner.shape), full3(qd.shape), full3(kd.shape),
        ],
        out_specs=[
            pl.BlockSpec((RET_PROMPT_CHUNK, RET_NV), lambda b, c: (b * nc + c, 0)),
            pl.BlockSpec((1, RET_HEADS, RET_DK, RET_DV), lambda b, c: (b, 0, 0, 0)),
        ],
        out_shape=[jax.ShapeDtypeStruct((NP, RET_NV), BF16),
                   jax.ShapeDtypeStruct((BATCH, RET_HEADS, RET_DK, RET_DV), F32)],
        scratch_shapes=[pltpu.VMEM((RET_HEADS, RET_DK, RET_DV), F32)],
        compiler_params=_cparams("parallel", "arbitrary"),
        name="retention_prompt",
    )(cd, proj, inner, qd, kd)

    cs = math.gcd(DEC_SEQ, RET_CHUNK)
    assert cs == DEC_SEQ
    inner, qd, kd, cd = _ret_tables(cs)
    rows = RET_SAMPLE_SEQS * DEC_SEQ
    row0 = NP // rows
    full2 = lambda shape: pl.BlockSpec(shape, lambda b: (0, 0))
    full3 = lambda shape: pl.BlockSpec(shape, lambda b: (0, 0, 0))
    st = pl.BlockSpec((1, RET_SAMPLE_SEQS, RET_HEADS, RET_DK, RET_DV), lambda b: (j, b, 0, 0, 0))
    stacked = [] if s_stack is None else [s_stack]
    o_s, s_s = pl.pallas_call(
        _ret_sample_body,
        grid=(DEC_BATCH // RET_SAMPLE_SEQS,),
        in_specs=[
            smem,
            pl.BlockSpec((rows, n_in), lambda b: (row0 + b, 0)),
            full3(inner.shape), full3(qd.shape), full3(kd.shape),
            st,
        ] + [pl.BlockSpec(memory_space=pl.ANY)] * len(stacked),
        out_specs=[pl.BlockSpec((rows, RET_NV), lambda b: (b, 0)), st],
        out_shape=[jax.ShapeDtypeStruct((NS, RET_NV), BF16),
                   jax.ShapeDtypeStruct(state_ret.shape, F32)],
        input_output_aliases={6: 1} if stacked else {},
        compiler_params=_cparams("parallel"),
        name="retention_sample",
    )(cd, proj, inner, qd, kd, state_ret, *stacked)
    return o, o_s, s_p, s_s


def _sink_attention(q, k, v, valid_t, sink_ref):
    t = q.shape[0]
    head_cols = lambda x, h: x[:, h * SWA_HD:(h + 1) * SWA_HD]
    qs = jnp.stack([jnp.concatenate([head_cols(q, kh * SWA_GROUP + g) for g in range(SWA_GROUP)], axis=0)
                    for kh in range(SWA_KV_HEADS)]).astype(BF16)
    ks = jnp.stack([head_cols(k, kh) for kh in range(SWA_KV_HEADS)]).astype(BF16)
    vs = jnp.stack([head_cols(v, kh) for kh in range(SWA_KV_HEADS)]).astype(BF16)
    sink = jnp.stack([jnp.concatenate([jnp.full((1, t), sink_ref[kh * SWA_GROUP + g], F32)
                                       for g in range(SWA_GROUP)], axis=1) for kh in range(SWA_KV_HEADS)])
    s = jnp.einsum('hkd,hqd->hkq', ks, qs, preferred_element_type=F32)
    s = jnp.where(valid_t[None], s, NEG_INF)
    m = jnp.maximum(jnp.max(s, axis=1, keepdims=True), sink)
    p = jnp.exp(s - m)
    denom = jnp.sum(p, axis=1, keepdims=True) + jnp.exp(sink - m)
    o_t = jnp.einsum('hkd,hkq->hdq', vs, p.astype(BF16), preferred_element_type=F32) / denom
    o = [o_t[kh].T for kh in range(SWA_KV_HEADS)]
    return jnp.concatenate([o[h // SWA_GROUP][(h % SWA_GROUP) * t:(h % SWA_GROUP + 1) * t]
                            for h in range(SWA_HEADS)], axis=-1)


def _swa_prompt_body(sink_ref, q_ref, kc_ref, vc_ref, kp_ref, vp_ref, o_ref):
    blk = pl.program_id(1)
    q = q_ref[...]
    k_all = jnp.concatenate([kp_ref[...], kc_ref[...]], axis=0)
    v_all = jnp.concatenate([vp_ref[...], vc_ref[...]], axis=0)
    shape = (2 * SWA_BLOCK, SWA_GROUP * SWA_BLOCK)
    i = _mod_pow2(lax.broadcasted_iota(jnp.int32, shape, 1), SWA_BLOCK)
    j = lax.broadcasted_iota(jnp.int32, shape, 0)
    rel = i + SWA_BLOCK - j
    valid = (rel >= 0) & (rel <= WINDOW) & (j >= jnp.where(blk > 0, 0, SWA_BLOCK))
    o_ref[...] = _sink_attention(q, k_all, v_all, valid, sink_ref).astype(BF16)


SWA_SAMPLE_SEQS = 8


def _swa_sample_body(sink_ref, q_ref, kn_ref, vn_ref, kc_ref, vc_ref, o_ref, ko_ref, vo_ref):
    rows = SWA_SAMPLE_SEQS * DEC_SEQ
    n_cache = SWA_SAMPLE_SEQS * WINDOW
    q = q_ref[...]
    kn = kn_ref[...]
    vn = vn_ref[...].astype(F32)
    k_all = jnp.concatenate([kc_ref[...].reshape(n_cache, SWA_NK), kn], axis=0)
    v_all = jnp.concatenate([vc_ref[...].reshape(n_cache, SWA_NK), vn], axis=0)
    for b in range(SWA_SAMPLE_SEQS):
        new = slice(b * DEC_SEQ, (b + 1) * DEC_SEQ)
        ko_ref[b, :WINDOW - DEC_SEQ, :] = kc_ref[b, DEC_SEQ:, :]
        ko_ref[b, WINDOW - DEC_SEQ:, :] = kn[new]
        vo_ref[b, :WINDOW - DEC_SEQ, :] = vc_ref[b, DEC_SEQ:, :]
        vo_ref[b, WINDOW - DEC_SEQ:, :] = vn[new]
    shape = (n_cache + rows, SWA_GROUP * rows)
    r = _mod_pow2(lax.broadcasted_iota(jnp.int32, shape, 1), rows)
    c = lax.broadcasted_iota(jnp.int32, shape, 0)
    q_seq, q_t = _div_pow2(r, DEC_SEQ), _mod_pow2(r, DEC_SEQ)
    is_new = c >= n_cache
    k_seq = jnp.where(is_new, _div_pow2(c - n_cache, DEC_SEQ), _div_pow2(c, WINDOW))
    k_slot = jnp.where(is_new, WINDOW + _mod_pow2(c - n_cache, DEC_SEQ), _mod_pow2(c, WINDOW))
    rel = q_t + WINDOW - k_slot
    valid = (q_seq == k_seq) & (rel >= 0) & (rel <= WINDOW)
    o_ref[...] = _sink_attention(q, k_all, v_all, valid, sink_ref).astype(BF16)


def swa_core(proj, k_rot, sinks, cache_k, cache_v):
    nb = SEQ // SWA_BLOCK
    smem = pl.BlockSpec(memory_space=pltpu.SMEM)
    kcol = SWA_NQ // SWA_NK
    cur = lambda b, i: b * nb + i
    prev = lambda b, i: b * nb + jnp.maximum(i - 1, 0)
    o = pl.pallas_call(
        _swa_prompt_body,
        grid=(BATCH, nb),
        in_specs=[
            smem,
            pl.BlockSpec((SWA_BLOCK, SWA_NQ), lambda b, i: (cur(b, i), 0)),
            pl.BlockSpec((SWA_BLOCK, SWA_NK), lambda b, i: (cur(b, i), kcol)),
            pl.BlockSpec((SWA_BLOCK, SWA_NK), lambda b, i: (cur(b, i), kcol + 1)),
            pl.BlockSpec((SWA_BLOCK, SWA_NK), lambda b, i: (prev(b, i), kcol)),
            pl.BlockSpec((SWA_BLOCK, SWA_NK), lambda b, i: (prev(b, i), kcol + 1)),
        ],
        out_specs=pl.BlockSpec((SWA_BLOCK, SWA_NQ), lambda b, i: (cur(b, i), 0)),
        out_shape=jax.ShapeDtypeStruct((NP, SWA_NQ), BF16),
        compiler_params=_cparams("parallel", "arbitrary"),
        name="swa_prompt",
    )(sinks, proj, proj, proj, proj, proj)

    rows = SWA_SAMPLE_SEQS * DEC_SEQ
    row0 = NP // rows
    o_s, k_s, v_s = pl.pallas_call(
        _swa_sample_body,
        grid=(DEC_BATCH // SWA_SAMPLE_SEQS,),
        in_specs=[
            smem,
            pl.BlockSpec((rows, SWA_NQ), lambda g: (row0 + g, 0)),
            pl.BlockSpec((rows, SWA_NK), lambda g: (row0 + g, 0)),
            pl.BlockSpec((rows, SWA_NK), lambda g: (row0 + g, kcol + 1)),
            pl.BlockSpec((SWA_SAMPLE_SEQS, WINDOW, SWA_NK), lambda g: (g, 0, 0)),
            pl.BlockSpec((SWA_SAMPLE_SEQS, WINDOW, SWA_NK), lambda g: (g, 0, 0)),
        ],
        out_specs=[
            pl.BlockSpec((rows, SWA_NQ), lambda g: (g, 0)),
            pl.BlockSpec((SWA_SAMPLE_SEQS, WINDOW, SWA_NK), lambda g: (g, 0, 0)),
            pl.BlockSpec((SWA_SAMPLE_SEQS, WINDOW, SWA_NK), lambda g: (g, 0, 0)),
        ],
        out_shape=[jax.ShapeDtypeStruct((NS, SWA_NQ), BF16),
                   jax.ShapeDtypeStruct((DEC_BATCH, WINDOW, SWA_NK), F32),
                   jax.ShapeDtypeStruct((DEC_BATCH, WINDOW, SWA_NK), F32)],
        compiler_params=_cparams("parallel"),
        name="swa_sample",
    )(sinks, proj, k_rot, proj, cache_k, cache_v)
    return o, o_s, k_s, v_s


RWKV_ROW_TILE = 256
RWKV_PAIRS = RWKV_HEADS // 2
RWKV_STACK = MXU_DIM_V7X
RWKV_SAMPLE_SEQS = RWKV_STACK // (RWKV_PAIRS * DEC_SEQ)


def _block_diag_ones():
    r = jnp.arange(MXU_DIM_V7X) // RWKV_HD
    return (r[:, None] == r[None, :]).astype(BF16)


def _store_pairs(ref, val):
    for p in range(RWKV_PAIRS):
        ref[p] = val[:, p * LANES:(p + 1) * LANES].astype(ref.dtype)


def _load_pairs(ref):
    return jnp.concatenate([ref[p] for p in range(RWKV_PAIRS)], axis=-1)


def _rwkv_proj_body(x_ref, xprev_ref, first_ref, gn_ref, mu_ref, wr_ref, wk_ref, wv_ref, w0_ref, w1_ref, w2_ref,
                    a0_ref, a1_ref, a2_ref, g1_ref, g2_ref, kk_ref, ka_ref, bd_ref,
                    r_out, lw_out, k_out, v_out, kk_out, b_out, g_out, hlast_out, h_scr):
    i = pl.program_id(0)
    h = _rms(x_ref[...], gn_ref[...])
    prev_tile_last = _rms(xprev_ref[...], gn_ref[...])[SUBLANES - 1:SUBLANES, :]
    rowid = lax.broadcasted_iota(jnp.int32, h.shape, 0)
    xp = jnp.where(rowid == 0, prev_tile_last, pltpu.roll(h, 1, 0))
    seq_len_mask = jnp.where(i >= NP // RWKV_ROW_TILE, DEC_SEQ - 1, SEQ - 1)
    is_first = jnp.bitwise_and(i * RWKV_ROW_TILE + rowid, seq_len_mask) == 0
    xp = jnp.where(is_first, first_ref[...], xp)
    _store_pairs(h_scr, h)
    every8th = pl.ds(SUBLANES - 1, RWKV_ROW_TILE // SUBLANES, stride=SUBLANES)
    hlast_out[...] = jnp.concatenate([h_scr[p, every8th, :] for p in range(RWKV_PAIRS)], axis=-1)
    d = xp - h
    xs = [h + d * mu_ref[i:i + 1, :] for i in range(6)]
    _store_pairs(r_out, _dot(xs[0], wr_ref[...]))
    k = _dot(xs[1], wk_ref[...])
    _store_pairs(v_out, _dot(xs[2], wv_ref[...]))
    z = w0_ref[...] + _dot(jnp.tanh(_dot(xs[3], w1_ref[...])), w2_ref[...])
    _store_pairs(lw_out, -math.exp(-0.5) * _sigmoid(z))
    a = _sigmoid(a0_ref[...] + _dot(_dot(xs[4], a1_ref[...]), a2_ref[...]))
    g_out[...] = _dot(_sigmoid(_dot(xs[5], g1_ref[...])), g2_ref[...]).astype(BF16)
    kk = k * kk_ref[...]
    kk = kk * jnp.minimum(lax.rsqrt(_seg_sum(kk * kk, bd_ref[...])), 1e12)
    _store_pairs(kk_out, kk)
    _store_pairs(b_out, kk * a)
    _store_pairs(k_out, k * (1.0 + (a - 1.0) * ka_ref[...]))


def rwkv_proj(x, shift_s, p):
    first = jnp.concatenate([
        jnp.zeros((RWKV_ROW_TILE, D_MODEL), F32),
        jnp.pad(shift_s[:, None, :], ((0, 0), (0, DEC_SEQ - 1), (0, 0))).reshape(NS, D_MODEL)], axis=0)
    prompt_tiles = NP // RWKV_ROW_TILE
    row = pl.BlockSpec((RWKV_ROW_TILE, D_MODEL), lambda i: (i, 0))
    prev = pl.BlockSpec((SUBLANES, D_MODEL), lambda i: (jnp.maximum(i * (RWKV_ROW_TILE // SUBLANES) - 1, 0), 0))
    first_spec = pl.BlockSpec((RWKV_ROW_TILE, D_MODEL), lambda i: (jnp.maximum(i - prompt_tiles + 1, 0), 0))
    pair = pl.BlockSpec((RWKV_PAIRS, RWKV_ROW_TILE, LANES), lambda i: (0, i, 0))
    full = lambda a: pl.BlockSpec(a.shape, lambda i: (0,) * a.ndim)
    consts = [p["g_mix"], p["mu"], p["wr"], p["wk"], p["wv"], p["w0"], p["w1"], p["w2"], p["a0"], p["a1"],
              p["a2"], p["g1"], p["g2"], p["k_k"], p["k_a"], p["bd"]]
    pair_shape = lambda dt: jax.ShapeDtypeStruct((RWKV_PAIRS, NT, LANES), dt)
    return pl.pallas_call(
        _rwkv_proj_body,
        grid=(NT // RWKV_ROW_TILE,),
        in_specs=[row, prev, first_spec] + [full(c) for c in consts],
        out_specs=[pair] * 6 + [row, pl.BlockSpec((RWKV_ROW_TILE // SUBLANES, D_MODEL), lambda i: (i, 0))],
        out_shape=[pair_shape(BF16), pair_shape(F32)] + [pair_shape(BF16)] * 4
        + [jax.ShapeDtypeStruct((NT, D_MODEL), BF16), jax.ShapeDtypeStruct((NT // SUBLANES, D_MODEL), F32)],
        scratch_shapes=[pltpu.VMEM((RWKV_PAIRS, RWKV_ROW_TILE, LANES), F32)],
        compiler_params=_cparams("parallel"),
        name="rwkv_proj",
    )(x, x, first, *consts)


def _rwkv_chunks(c_len, problems):
    n_grp = RWKV_STACK // c_len
    shift = int(math.log2(c_len))
    assert 2 ** shift == c_len
    row = lax.broadcasted_iota(jnp.int32, (RWKV_STACK, RWKV_STACK), 0)
    col = lax.broadcasted_iota(jnp.int32, (RWKV_STACK, RWKV_STACK), 1)
    same = jnp.right_shift(row, shift) == jnp.right_shift(col, shift)
    strict_t = same & (row < col)
    incl_t = same & (row <= col)
    tri_t = jnp.where(incl_t, 1.0, 0.0).astype(BF16)
    each = lambda f, *cols: [f(*xs) for xs in zip(*cols)]
    r, lw, k, v, kk, b, state, store = zip(*problems)
    lam = each(lambda x: sum(jnp.dot(part, tri_t, preferred_element_type=F32) for part in _split3(x.T)).T, lw)
    lam3 = each(lambda x: x.reshape(n_grp, c_len, LANES), lam)
    lam_end3 = each(lambda x: x[:, c_len - 1:c_len, :], lam3)
    e_end = each(lambda e, x: jnp.exp(e - x).reshape(RWKV_STACK, LANES), lam_end3, lam3)
    e_tot3 = each(jnp.exp, lam_end3)
    e_neg = each(lambda x: jnp.exp(-x), lam)
    at = each(lambda kk_, l, w: -kk_ * jnp.exp(l - w), kk, lam, lw)
    rt = each(lambda r_, l: r_ * jnp.exp(l), r, lam)
    mul = lambda x, y: x * y
    bh, kh, bb, kb = each(mul, b, e_neg), each(mul, k, e_neg), each(mul, b, e_end), each(mul, k, e_end)

    lane = lax.broadcasted_iota(jnp.int32, (RWKV_STACK, LANES), 1)
    halves = (lane < RWKV_HD, lane >= RWKV_HD)
    rhs = each(lambda x, y: jnp.concatenate([x, y], axis=0), bh, kh)
    at_pow, bt_up, rt_up = [], [], []
    for sel in halves:
        lhs = each(lambda x, y: jnp.concatenate([jnp.where(sel, x, 0.0), jnp.where(sel, y, 0.0)], axis=0), at, rt)
        mt = each(_dot_nt, rhs, lhs)
        at_pow.append(each(lambda x: jnp.where(strict_t, x[:RWKV_STACK, :RWKV_STACK], 0.0), mt))
        bt_up.append(each(lambda x: jnp.where(strict_t, x[RWKV_STACK:, :RWKV_STACK], 0.0), mt))
        rt_up.append(each(lambda x: jnp.concatenate([jnp.where(incl_t, x[:RWKV_STACK, RWKV_STACK:], 0.0),
                                                     jnp.where(incl_t, x[RWKV_STACK:, RWKV_STACK:], 0.0)], axis=0), mt))

    rows = lambda t, g: t[g * c_len:(g + 1) * c_len]
    states = each(lambda st: [st(g) for g in range(n_grp)], state)
    p = [each(lambda x, y, st: _dot_nt(jnp.concatenate([rows(x, g), rows(y, g)], axis=0), st[g]), at, rt, states)
         for g in range(n_grp)]
    n_prob = len(problems)
    a_s_t = [jnp.concatenate([p[g][i][:c_len] for g in range(n_grp)], axis=0).T for i in range(n_prob)]
    r_s_t = [jnp.concatenate([p[g][i][c_len:] for g in range(n_grp)], axis=0).T for i in range(n_prob)]
    v_t = each(lambda x: x.T, v)
    half_rows = lambda x, e: x[e * RWKV_HD:(e + 1) * RWKV_HD]
    n_half = len(halves)
    u_t = [each(lambda a_, v_, b_: half_rows(a_, e) + _dot(half_rows(v_, e), b_), a_s_t, v_t, bt_up[e])
           for e in range(n_half)]
    assert shift >= 2
    for it in range(shift - 1):
        u_t = [each(lambda u_, a_: u_ + _dot(u_, a_), u_t[e], at_pow[e]) for e in range(n_half)]
        if it + 2 < shift:
            at_pow = [each(lambda a_: _dot(a_, a_), at_pow[e]) for e in range(n_half)]
    z_t = [each(lambda u_, a_: _dot(u_, a_), u_t[e], at_pow[e]) for e in range(n_half)]
    u_t = [each(lambda u_, z_, a_: u_ + _dot(z_, a_), u_t[e], z_t[e], at_pow[e]) for e in range(n_half)]
    y_t = [each(lambda r_, u_, v_, m_: half_rows(r_, e) + _dot(jnp.concatenate([u_, half_rows(v_, e)], axis=1), m_),
                r_s_t, u_t[e], v_t, rt_up[e]) for e in range(n_half)]
    stack_halves = lambda parts: [jnp.concatenate([parts[e][i] for e in range(n_half)], axis=0).T
                                  for i in range(n_prob)]
    u = stack_halves(u_t)
    y = stack_halves(y_t)

    r128 = lax.broadcasted_iota(jnp.int32, (LANES, LANES), 0)
    c128 = lax.broadcasted_iota(jnp.int32, (LANES, LANES), 1)
    diag = (r128 < RWKV_HD) == (c128 < RWKV_HD)
    for g in range(n_grp):
        upd = each(lambda u_, v_, bb_, kb_: _dot_tn(jnp.concatenate([rows(u_, g), rows(v_, g)], axis=0),
                                                    jnp.concatenate([rows(bb_, g), rows(kb_, g)], axis=0)),
                   u, v, bb, kb)
        for i in range(n_prob):
            store[i](g, states[i][g] * e_tot3[i][g] + jnp.where(diag, upd[i], 0.0))
    return y


RWKV_PROMPT_SEQS = 4
RWKV_STEP_TOKENS = 128


def _rwkv_prompt_body(*refs):
    ins, (y_ref, s_ref, s_scr) = refs[:6 * RWKV_PROMPT_SEQS], refs[6 * RWKV_PROMPT_SEQS:]
    c = pl.program_id(1)

    @pl.when(c == 0)
    def _():
        s_scr[...] = jnp.zeros_like(s_scr)

    def one_chunk(ci, carry):
        rows = pl.ds(pl.multiple_of(ci * RWKV_CHUNK, RWKV_CHUNK), RWKV_CHUNK)
        def problem(seq):
            def store(g, s):
                s_scr[seq, g] = s

            args = [ins[a * RWKV_PROMPT_SEQS + seq][:, rows, :].reshape(RWKV_STACK, LANES).astype(F32)
                    for a in range(6)]
            return (*args, lambda g: s_scr[seq, g], store)

        ys = _rwkv_chunks(RWKV_CHUNK, [problem(seq) for seq in range(RWKV_PROMPT_SEQS)])
        for seq, y in enumerate(ys):
            y_ref[:, seq, rows, :] = y.reshape(RWKV_PAIRS, RWKV_CHUNK, LANES)
        return carry

    lax.fori_loop(0, RWKV_STEP_TOKENS // RWKV_CHUNK, one_chunk, 0)

    @pl.when(c == pl.num_programs(1) - 1)
    def _():
        for seq in range(RWKV_PROMPT_SEQS):
            for p in range(RWKV_PAIRS):
                s_ref[seq, 2 * p] = s_scr[seq, p, :RWKV_HD, :RWKV_HD]
                s_ref[seq, 2 * p + 1] = s_scr[seq, p, RWKV_HD:, RWKV_HD:]


RWKV_SAMPLE_PROBLEMS = 2


def _rwkv_sample_body(r_ref, lw_ref, k_ref, v_ref, kk_ref, b_ref, s0_ref, y_ref, s_ref):
    zero = jnp.zeros((RWKV_HD, RWKV_HD), F32)
    rows_per = RWKV_SAMPLE_SEQS * DEC_SEQ

    def problem(q):
        def state(g):
            p, seq = divmod(g, RWKV_SAMPLE_SEQS)
            seq += q * RWKV_SAMPLE_SEQS
            return jnp.concatenate([jnp.concatenate([s0_ref[seq, 2 * p], zero], axis=1),
                                    jnp.concatenate([zero, s0_ref[seq, 2 * p + 1]], axis=1)], axis=0)

        def store(g, s):
            p, seq = divmod(g, RWKV_SAMPLE_SEQS)
            seq += q * RWKV_SAMPLE_SEQS
            s_ref[seq, 2 * p] = s[:RWKV_HD, :RWKV_HD]
            s_ref[seq, 2 * p + 1] = s[RWKV_HD:, RWKV_HD:]

        args = [ref[:, q * rows_per:(q + 1) * rows_per, :].reshape(RWKV_STACK, LANES).astype(F32)
                for ref in (r_ref, lw_ref, k_ref, v_ref, kk_ref, b_ref)]
        return (*args, state, store)

    ys = _rwkv_chunks(DEC_SEQ, [problem(q) for q in range(RWKV_SAMPLE_PROBLEMS)])
    for q, y in enumerate(ys):
        y_ref[:, q * rows_per:(q + 1) * rows_per, :] = y.reshape(RWKV_PAIRS, rows_per, LANES)


def rwkv_core(r, lw, k, v, kk, b, state_s):
    assert RWKV_PAIRS * RWKV_CHUNK == RWKV_STACK
    nc = SEQ // RWKV_STEP_TOKENS
    st_shape = (RWKV_HEADS, RWKV_HD, RWKV_HD)
    seq_blk = lambda seq: pl.BlockSpec((RWKV_PAIRS, RWKV_STEP_TOKENS, LANES),
                                       lambda bi, c: (0, (bi * RWKV_PROMPT_SEQS + seq) * nc + c, 0))
    inputs = (r, lw, k, v, kk, b)
    y_p, s_p = pl.pallas_call(
        _rwkv_prompt_body,
        grid=(BATCH // RWKV_PROMPT_SEQS, nc),
        in_specs=[seq_blk(seq) for _ in inputs for seq in range(RWKV_PROMPT_SEQS)],
        out_specs=[pl.BlockSpec((RWKV_PAIRS, RWKV_PROMPT_SEQS, RWKV_STEP_TOKENS, LANES), lambda bi, c: (0, bi, c, 0)),
                   pl.BlockSpec((RWKV_PROMPT_SEQS,) + st_shape, lambda bi, c: (bi, 0, 0, 0))],
        out_shape=[jax.ShapeDtypeStruct((RWKV_PAIRS, BATCH, SEQ, LANES), F32),
                   jax.ShapeDtypeStruct((BATCH,) + st_shape, F32)],
        scratch_shapes=[pltpu.VMEM((RWKV_PROMPT_SEQS, RWKV_PAIRS, LANES, LANES), F32)],
        compiler_params=_cparams("parallel", "arbitrary"),
        name="rwkv_prompt",
    )(*[a for a in inputs for _ in range(RWKV_PROMPT_SEQS)])

    step_seqs = RWKV_SAMPLE_PROBLEMS * RWKV_SAMPLE_SEQS
    rows = step_seqs * DEC_SEQ
    row0 = NP // rows
    blk = pl.BlockSpec((RWKV_PAIRS, rows, LANES), lambda i: (0, row0 + i, 0))
    st = pl.BlockSpec((step_seqs,) + st_shape, lambda i: (i, 0, 0, 0))
    y_s, s_s = pl.pallas_call(
        _rwkv_sample_body,
        grid=(DEC_BATCH // step_seqs,),
        in_specs=[blk] * 6 + [st],
        out_specs=[pl.BlockSpec((RWKV_PAIRS, rows, LANES), lambda i: (0, i, 0)), st],
        out_shape=[jax.ShapeDtypeStruct((RWKV_PAIRS, NS, LANES), F32),
                   jax.ShapeDtypeStruct((DEC_BATCH,) + st_shape, F32)],
        compiler_params=_cparams("parallel"),
        name="rwkv_sample",
    )(*inputs, state_s)
    return y_p.reshape(RWKV_PAIRS, NP, LANES), y_s, s_p, s_s


def _rwkv_out_body(x_ref, yp_ref, ys_ref, r_ref, k_ref, v_ref, g_ref, rk_ref, lnw_ref, lnb_ref, bd_ref, wo_ref,
                   o_ref):
    bd = bd_ref[...]
    y = _pick_group(RWKV_ROW_TILE, _load_pairs(yp_ref), _load_pairs(ys_ref))
    yc = y - _seg_sum(y, bd) * (1.0 / RWKV_HD)
    var = _seg_sum(yc * yc, bd) * (1.0 / RWKV_HD)
    yn = yc * lax.rsqrt(var + RWKV_GN_EPS) * lnw_ref[...] + lnb_ref[...]
    rk = _load_pairs(r_ref).astype(F32) * _load_pairs(k_ref).astype(F32)
    bonus = _seg_sum(rk * rk_ref[...], bd) * _load_pairs(v_ref).astype(F32)
    z = (yn + bonus) * g_ref[...].astype(F32)
    o_ref[...] = x_ref[...] + _dot(z, wo_ref[...])


def rwkv_out(x, y_p, y_s, r, k, v, g, p):
    row = pl.BlockSpec((RWKV_ROW_TILE, D_MODEL), lambda i: (i, 0))
    pair_of = lambda f: pl.BlockSpec((RWKV_PAIRS, RWKV_ROW_TILE, LANES), lambda i: (0, f(i), 0))
    full = lambda a: pl.BlockSpec(a.shape, lambda i: (0,) * a.ndim)
    consts = [p["r_k"], p["ln_w"], p["ln_b"], p["bd"], p["w_out"]]
    return pl.pallas_call(
        _rwkv_out_body,
        grid=(NT // RWKV_ROW_TILE,),
        in_specs=[row, *_group_specs(RWKV_ROW_TILE, pair_of)] + [pair_of(lambda i: i)] * 3 + [row]
        + [full(c) for c in consts],
        out_specs=row,
        out_shape=jax.ShapeDtypeStruct((NT, D_MODEL), F32),
        compiler_params=_cparams("parallel"),
        name="rwkv_out",
    )(x, y_p, y_s, r, k, v, g, *consts)


def _pad_cols(w, n):
    return jnp.pad(w, ((0, 0), (0, n - w.shape[1])))


def _pad_rows(w, n):
    return jnp.pad(w, ((0, n - w.shape[0]), (0, 0)))


def kernel(x_prompt, x_sample, state_ret, cache_swa_k, cache_swa_v, state_rwkv_wkv, state_rwkv_shift,
           norm_mix, norm_mlp, norm_final, ret_w_in, ret_w_out,
           swa_w_in, swa_b_in, swa_sinks, swa_w_out,
           rwkv_mu, rwkv_w_rkv, rwkv_w0, rwkv_w1, rwkv_w2, rwkv_a0, rwkv_a1, rwkv_a2,
           rwkv_g1, rwkv_g2, rwkv_k_k, rwkv_k_a, rwkv_r_k, rwkv_ln_w, rwkv_ln_b, rwkv_w_out,
           mlp_w_up, mlp_w_down):
    x = (x_prompt.reshape(NP, D_MODEL), x_sample.reshape(NS, D_MODEL))
    if LAYER_KINDS[0] != 0:
        x = jnp.concatenate(x, axis=0)
    row = lambda t: t.reshape(1, -1)
    ret_p, k_p, k_s, v_p, v_s, wkv_p, wkv_s, sh_p, sh_s = ([] for _ in range(9))
    ret_s = None
    for layer in range(DEPTH):
        kind = LAYER_KINDS[layer]
        j = LAYER_KINDS[:layer].count(kind)
        g_mix = row(norm_mix[layer])
        if kind == 0:
            proj = ret_proj(x, g_mix, ret_w_in[j].astype(BF16))
            o_p, o_s, s_p, ret_s = retention_core(proj, state_ret, j, ret_s)
            x = matmul_residual(x, o_p, o_s, ret_w_out[j].astype(BF16))
            ret_p.append(s_p)
        elif kind == 1:
            proj, k_rot = swa_proj(x, g_mix, swa_w_in[j].astype(BF16), row(swa_b_in[j]))
            o_p, o_s, kc, vc = swa_core(proj, k_rot, swa_sinks[j],
                                        cache_swa_k[j].reshape(DEC_BATCH, WINDOW, SWA_NK),
                                        cache_swa_v[j].reshape(DEC_BATCH, WINDOW, SWA_NK))
            x = matmul_residual(x, o_p, o_s, swa_w_out[j].astype(BF16))
            kv_shape = (-1, WINDOW, SWA_KV_HEADS, SWA_HD)
            k_p.append(k_rot[:NP].reshape(BATCH, SEQ, SWA_NK)[:, -WINDOW:].reshape(kv_shape))
            v_p.append(proj[:NP, SWA_NQ + SWA_NK:].reshape(BATCH, SEQ, SWA_NK)[:, -WINDOW:]
                       .astype(F32).reshape(kv_shape))
            k_s.append(kc.reshape(kv_shape))
            v_s.append(vc.reshape(kv_shape))
        else:
            lora_w = LANES
            lora_g = 2 * LANES
            p = dict(
                mu=rwkv_mu[j], wr=rwkv_w_rkv[j, 0].astype(BF16), wk=rwkv_w_rkv[j, 1].astype(BF16),
                wv=rwkv_w_rkv[j, 2].astype(BF16), w0=row(rwkv_w0[j]),
                w1=_pad_cols(rwkv_w1[j], lora_w).astype(BF16), w2=_pad_rows(rwkv_w2[j], lora_w).astype(BF16),
                a0=row(rwkv_a0[j]),
                a1=_pad_cols(rwkv_a1[j], lora_w).astype(BF16), a2=_pad_rows(rwkv_a2[j], lora_w).astype(BF16),
                g1=_pad_cols(rwkv_g1[j], lora_g).astype(BF16), g2=_pad_rows(rwkv_g2[j], lora_g).astype(BF16),
                k_k=row(rwkv_k_k[j]), k_a=row(rwkv_k_a[j]), r_k=row(rwkv_r_k[j]),
                ln_w=row(rwkv_ln_w[j]), ln_b=row(rwkv_ln_b[j]), bd=_block_diag_ones(),
                w_out=rwkv_w_out[j].astype(BF16), g_mix=g_mix)
            r, lw, k, v, kk, b, g, h_last = rwkv_proj(x, state_rwkv_shift[j], p)
            y_p, y_s, s_p, s_s = rwkv_core(r, lw, k, v, kk, b, state_rwkv_wkv[j])
            x = rwkv_out(x, y_p, y_s, r, k, v, g, p)
            wkv_p.append(s_p)
            wkv_s.append(s_s)
            sh_p.append(h_last[SEQ // SUBLANES - 1:NP // SUBLANES:SEQ // SUBLANES])
            sh_s.append(h_last[NP // SUBLANES:])
        x = mlp(x, row(norm_mlp[layer]), mlp_w_up, mlp_w_down, layer,
                final_g=row(norm_final) if layer == DEPTH - 1 else None)
    y_p, y_s = x
    return (y_p.reshape(BATCH, SEQ, D_MODEL), y_s.reshape(DEC_BATCH, DEC_SEQ, D_MODEL),
            jnp.stack(ret_p), ret_s, jnp.stack(k_p), jnp.stack(k_s), jnp.stack(v_p), jnp.stack(v_s),
            jnp.stack(wkv_p), jnp.stack(wkv_s), jnp.stack(sh_p), jnp.stack(sh_s))
```

```python
import functools
import math

import jax
import jax.numpy as jnp
from jax import lax
from jax.experimental import pallas as pl
from jax.experimental.pallas import tpu as pltpu

F32 = jnp.float32
BF16 = jnp.bfloat16

D_MODEL = 1024
BATCH = 8
SEQ = 2048
DEPTH = 4
DEC_BATCH = 128
DEC_SEQ = 8
PAST_LEN = 8192
LAYER_KINDS = tuple(i % 3 for i in range(DEPTH))
NORM_EPS = 1e-6
ROPE_THETA = 10000.0
NEG_INF = -1e30
D_FF = 4 * D_MODEL

RET_HEADS = 4
RET_DK = D_MODEL // RET_HEADS
RET_DV = 2 * D_MODEL // RET_HEADS
RET_CHUNK = 128
RET_PROMPT_CHUNK = 256
RET_NQ = RET_HEADS * RET_DK
RET_NV = RET_HEADS * RET_DV

SWA_HEADS = 16
SWA_KV_HEADS = 4
SWA_GROUP = SWA_HEADS // SWA_KV_HEADS
SWA_HD = D_MODEL // SWA_HEADS
WINDOW = 128
SWA_BLOCK = 128
SWA_NQ = SWA_HEADS * SWA_HD
SWA_NK = SWA_KV_HEADS * SWA_HD

RWKV_HD = 64
RWKV_HEADS = D_MODEL // RWKV_HD
RWKV_GN_EPS = 64e-5
RWKV_CHUNK = 32

NP = BATCH * SEQ
NS = DEC_BATCH * DEC_SEQ
NT = NP + NS

VMEM_LIMIT_V7X = 48 * 1024 * 1024
STREAMED_F32_VMEM_LIMIT_V7X = 56 * 1024 * 1024
LANES = 128
SUBLANES = 8
MXU_DIM_V7X = 256

ROW_TILE = 1024


def _cparams(*sem, vmem=VMEM_LIMIT_V7X):
    return pltpu.CompilerParams(dimension_semantics=sem, vmem_limit_bytes=vmem)


def _dot(a, b):
    return jnp.dot(a.astype(BF16), b.astype(BF16), preferred_element_type=F32)


def _dot_nt(a, b):
    return lax.dot_general(a.astype(BF16), b.astype(BF16), (((1,), (1,)), ((), ())),
                           preferred_element_type=F32)


def _dot_tn(a, b):
    return lax.dot_general(a.astype(BF16), b.astype(BF16), (((0,), (0,)), ((), ())),
                           preferred_element_type=F32)


def _div_pow2(x, n):
    assert n & (n - 1) == 0
    return jnp.right_shift(x, n.bit_length() - 1)


def _mod_pow2(x, n):
    assert n & (n - 1) == 0
    return jnp.bitwise_and(x, n - 1)


def _sigmoid(x):
    return 0.5 * jnp.tanh(0.5 * x) + 0.5


def _rms(x, g):
    ms = jnp.mean(x * x, axis=-1, keepdims=True)
    return x * lax.rsqrt(ms + NORM_EPS) * g


def _split3(x):
    hi = x.astype(BF16)
    r1 = x - hi.astype(F32)
    mid = r1.astype(BF16)
    lo = (r1 - mid.astype(F32)).astype(BF16)
    return hi, mid, lo


def _seg_sum(x, bd):
    xb = x.astype(BF16)
    return jnp.concatenate(
        [jnp.dot(xb[:, j * MXU_DIM_V7X:(j + 1) * MXU_DIM_V7X], bd, preferred_element_type=F32)
         for j in range(D_MODEL // MXU_DIM_V7X)], axis=-1)


def _rope_tables(pos, half):
    inv_freq = jnp.power(ROPE_THETA, -jnp.arange(half, dtype=F32) / half)
    ang = pos.astype(F32)[:, None] * inv_freq[None, :]
    return jnp.cos(ang), jnp.sin(ang)


def _group_specs(tile, block_of):
    n_prompt = NP // tile
    return (block_of(lambda i: jnp.minimum(i, n_prompt - 1)), block_of(lambda i: jnp.maximum(i - n_prompt, 0)))


def _pick_group(tile, p_val, s_val):
    return jnp.where(pl.program_id(0) < NP // tile, p_val, s_val)


def _token_rows(x, tile):
    if isinstance(x, tuple):
        return list(x), list(_group_specs(tile, lambda f: pl.BlockSpec((tile, D_MODEL), lambda *g: (f(g[0]), 0))))
    return [x], [pl.BlockSpec((tile, D_MODEL), lambda *g: (g[0], 0))]


def _read_rows(refs, tile):
    return refs[0][...] if len(refs) == 1 else _pick_group(tile, refs[0][...], refs[1][...])


def _rope_table_inputs(half, widen, tile):
    tp = widen(*_rope_tables(jnp.arange(SEQ, dtype=jnp.int32), half))
    ts = [jnp.tile(t, (DEC_BATCH, 1))
          for t in widen(*_rope_tables(PAST_LEN + jnp.arange(DEC_SEQ, dtype=jnp.int32), half))]
    n_prompt = NP // tile
    per_seq = SEQ // tile
    p_spec = pl.BlockSpec((tile, LANES), lambda *g: (g[0] % per_seq, 0))
    s_spec = pl.BlockSpec((tile, LANES), lambda *g: (jnp.maximum(g[0] - n_prompt, 0), 0))
    return [tp[0], tp[1], ts[0], ts[1]], [p_spec, p_spec, s_spec, s_spec]


RET_PROJ_TILE = RET_NQ + RET_NQ


def _ret_proj_body(n_x, *refs):
    x_refs, (g_ref, w_ref, cosp_ref, sinp_ref, coss_ref, sins_ref, o_ref, xn_ref) = refs[:n_x], refs[n_x:]
    j = pl.program_id(1)

    @pl.when(j == 0)
    def _():
        xn_ref[...] = _rms(_read_rows(x_refs, ROW_TILE), g_ref[...]).astype(BF16)

    acc = jnp.dot(xn_ref[...], w_ref[...], preferred_element_type=F32)

    @pl.when(j == 0)
    def _():
        cos = _pick_group(ROW_TILE, cosp_ref[...], coss_ref[...])
        sin = _pick_group(ROW_TILE, sinp_ref[...], sins_ref[...])
        half = RET_DK // 2
        for blk in range(RET_PROJ_TILE // RET_DK):
            lo, mid, hi = blk * RET_DK, blk * RET_DK + half, (blk + 1) * RET_DK
            x1, x2 = acc[:, lo:mid], acc[:, mid:hi]
            scale = 1.0 if blk < RET_HEADS else RET_DK ** -0.5
            o_ref[:, lo:mid] = ((x1 * cos - x2 * sin) * scale).astype(BF16)
            o_ref[:, mid:hi] = ((x2 * cos + x1 * sin) * scale).astype(BF16)

    @pl.when(j > 0)
    def _():
        o_ref[...] = acc.astype(BF16)


def ret_proj(x, g, w):
    n_out = w.shape[1]
    x_arrays, x_specs = _token_rows(x, ROW_TILE)
    tables, table_specs = _rope_table_inputs(RET_DK // 2, lambda c, s: (c, s), ROW_TILE)
    return pl.pallas_call(
        functools.partial(_ret_proj_body, len(x_arrays)),
        grid=(NT // ROW_TILE, n_out // RET_PROJ_TILE),
        in_specs=x_specs + [
            pl.BlockSpec((1, D_MODEL), lambda i, j: (0, 0)),
            pl.BlockSpec((D_MODEL, RET_PROJ_TILE), lambda i, j: (0, j)),
        ] + table_specs,
        out_specs=pl.BlockSpec((ROW_TILE, RET_PROJ_TILE), lambda i, j: (i, j)),
        out_shape=jax.ShapeDtypeStruct((NT, n_out), BF16),
        scratch_shapes=[pltpu.VMEM((ROW_TILE, D_MODEL), BF16)],
        compiler_params=_cparams("parallel", "arbitrary"),
        name="ret_proj",
    )(*x_arrays, g, w, *tables)


SWA_PROJ_ROW_TILE = 512


def _swa_rope(x, cos, sin):
    n = x.shape[1]
    half = SWA_HD // 2
    lane = lax.broadcasted_iota(jnp.int32, x.shape, 1)
    partner = jnp.where(_mod_pow2(lane, SWA_HD) < half, pltpu.roll(x, n - half, 1), pltpu.roll(x, half, 1))
    reps = n // LANES
    return x * jnp.concatenate([cos] * reps, axis=-1) + partner * jnp.concatenate([sin] * reps, axis=-1)


def _swa_proj_body(x_ref, g_ref, w_ref, b_ref, cosp_ref, sinp_ref, coss_ref, sins_ref, o_ref, kr_ref):
    xn = _rms(x_ref[...], g_ref[...]).astype(BF16)
    acc = jnp.dot(xn, w_ref[...], preferred_element_type=F32) + b_ref[...]
    n_qk = SWA_NQ + SWA_NK
    cos = _pick_group(SWA_PROJ_ROW_TILE, cosp_ref[...], coss_ref[...])
    sin = _pick_group(SWA_PROJ_ROW_TILE, sinp_ref[...], sins_ref[...])
    qk = _swa_rope(acc[:, :n_qk], cos, sin)
    assert math.log2(SWA_HD ** 0.5).is_integer()
    o_ref[:, :SWA_NQ] = (qk[:, :SWA_NQ] * (SWA_HD ** -0.5)).astype(BF16)
    o_ref[:, SWA_NQ:n_qk] = qk[:, SWA_NQ:].astype(BF16)
    o_ref[:, n_qk:] = acc[:, n_qk:].astype(BF16)
    kr_ref[...] = qk[:, SWA_NQ:]


def swa_proj(x, g, w, b):
    n_out = w.shape[1]

    def widen(c, s):
        rep = LANES // SWA_HD
        return jnp.tile(jnp.concatenate([c, c], axis=-1), (1, rep)), jnp.tile(jnp.concatenate([-s, s], axis=-1), (1, rep))

    tile = SWA_PROJ_ROW_TILE
    tables, table_specs = _rope_table_inputs(SWA_HD // 2, widen, tile)
    return pl.pallas_call(
        _swa_proj_body,
        grid=(NT // tile,),
        in_specs=[
            pl.BlockSpec((tile, D_MODEL), lambda i: (i, 0)),
            pl.BlockSpec((1, D_MODEL), lambda i: (0, 0)),
            pl.BlockSpec((D_MODEL, n_out), lambda i: (0, 0)),
            pl.BlockSpec((1, n_out), lambda i: (0, 0)),
        ] + table_specs,
        out_specs=[pl.BlockSpec((tile, n_out), lambda i: (i, 0)), pl.BlockSpec((tile, SWA_NK), lambda i: (i, 0))],
        out_shape=[jax.ShapeDtypeStruct((NT, n_out), BF16), jax.ShapeDtypeStruct((NT, SWA_NK), F32)],
        compiler_params=_cparams("parallel"),
        name="swa_proj",
    )(x, g, w, b, *tables)


def _matmul_residual_body(n_x, *refs):
    x_refs, (ap_ref, as_ref, w_ref, o_ref) = refs[:n_x], refs[n_x:]
    a = _pick_group(ROW_TILE, ap_ref[...], as_ref[...])
    o_ref[...] = _read_rows(x_refs, ROW_TILE) + jnp.dot(a, w_ref[...], preferred_element_type=F32)


def matmul_residual(x, a_p, a_s, w):
    k = a_p.shape[1]
    x_arrays, x_specs = _token_rows(x, ROW_TILE)
    return pl.pallas_call(
        functools.partial(_matmul_residual_body, len(x_arrays)),
        grid=(NT // ROW_TILE,),
        in_specs=x_specs + [
            *_group_specs(ROW_TILE, lambda f: pl.BlockSpec((ROW_TILE, k), lambda i: (f(i), 0))),
            pl.BlockSpec((k, D_MODEL), lambda i: (0, 0)),
        ],
        out_specs=pl.BlockSpec((ROW_TILE, D_MODEL), lambda i: (i, 0)),
        out_shape=jax.ShapeDtypeStruct((NT, D_MODEL), F32),
        compiler_params=_cparams("parallel"),
        name="matmul_residual",
    )(*x_arrays, a_p, a_s, w)


MLP_FF_TILE = 1024


def _mlp_body(has_final_norm, x_ref, g_ref, wu_ref, wd_ref, *rest):
    xn_ref, acc_ref = rest[-2:]
    f = pl.program_id(1)

    @pl.when(f == 0)
    def _():
        xn_ref[...] = _rms(x_ref[...], g_ref[...]).astype(BF16)
        acc_ref[...] = jnp.zeros_like(acc_ref)

    a = jnp.maximum(jnp.dot(xn_ref[...], wu_ref[...].astype(BF16), preferred_element_type=F32), 0.0)
    acc_ref[...] += jnp.dot((a * a).astype(BF16), wd_ref[...].astype(BF16), preferred_element_type=F32)

    @pl.when(f == pl.num_programs(1) - 1)
    def _():
        out = x_ref[...] + acc_ref[...]
        if not has_final_norm:
            o_ref, = rest[:-2]
            o_ref[...] = out
        else:
            gf_ref, yp_ref, ys_ref = rest[:-2]
            y = _rms(out, gf_ref[...])
            is_prompt = pl.program_id(0) < NP // ROW_TILE

            @pl.when(is_prompt)
            def _():
                yp_ref[...] = y

            @pl.when(jnp.logical_not(is_prompt))
            def _():
                ys_ref[...] = y


def mlp(x, g, w_up, w_down, layer, final_g=None):
    row = pl.BlockSpec((ROW_TILE, D_MODEL), lambda i, f: (i, 0))
    vec = pl.BlockSpec((1, D_MODEL), lambda i, f: (0, 0))
    if final_g is None:
        extra_in, extra_specs = [], []
        out_specs = row
        out_shape = jax.ShapeDtypeStruct((NT, D_MODEL), F32)
    else:
        extra_in, extra_specs = [final_g], [vec]
        out_specs = list(_group_specs(ROW_TILE, lambda g_: pl.BlockSpec((ROW_TILE, D_MODEL), lambda i, f: (g_(i), 0))))
        out_shape = [jax.ShapeDtypeStruct((NP, D_MODEL), F32), jax.ShapeDtypeStruct((NS, D_MODEL), F32)]
    return pl.pallas_call(
        functools.partial(_mlp_body, final_g is not None),
        grid=(NT // ROW_TILE, D_FF // MLP_FF_TILE),
        in_specs=[
            row, vec,
            pl.BlockSpec((None, D_MODEL, MLP_FF_TILE), lambda i, f: (layer, 0, f)),
            pl.BlockSpec((None, MLP_FF_TILE, D_MODEL), lambda i, f: (layer, f, 0)),
        ] + extra_specs,
        out_specs=out_specs,
        out_shape=out_shape,
        scratch_shapes=[pltpu.VMEM((ROW_TILE, D_MODEL), BF16), pltpu.VMEM((ROW_TILE, D_MODEL), F32)],
        compiler_params=_cparams("parallel" if final_g is None else "arbitrary", "arbitrary", vmem=STREAMED_F32_VMEM_LIMIT_V7X),
        name="mlp",
    )(x, g, w_up, w_down, *extra_in)


def _ret_tables(c):
    log_gamma = jnp.log1p(-jnp.exp2(-5.0 - jnp.arange(RET_HEADS, dtype=F32)))
    idx = jnp.arange(c, dtype=F32)
    diff = idx[:, None] - idx[None, :]
    inner = jnp.where(diff >= 0, jnp.exp(log_gamma[:, None, None] * jnp.maximum(diff, 0.0)), 0.0)
    q_decay = jnp.exp(log_gamma[:, None] * (idx[None, :] + 1.0))
    k_decay = jnp.exp(log_gamma[:, None] * (c - 1.0 - idx[None, :]))
    chunk_decay = jnp.exp(log_gamma * c)
    rep = lambda t: jnp.broadcast_to(t[:, :, None], (RET_HEADS, c, LANES))
    return inner, rep(q_decay), rep(k_decay), chunk_decay


def _ret_heads(rows, lo, width):
    return jnp.stack([rows[:, lo + h * width:lo + (h + 1) * width] for h in range(RET_HEADS)])


def _ret_chunk(rows_list, s0, inner, qd, kd, cd_ref):
    n_seq = len(rows_list)
    cat = lambda f: jnp.concatenate([f(r) for r in rows_list], axis=0)
    q = cat(lambda r: _ret_heads(r, 0, RET_DK)).astype(BF16)
    k = cat(lambda r: _ret_heads(r, RET_NQ, RET_DK)).astype(F32)
    v = cat(lambda r: _ret_heads(r, 2 * RET_NQ, RET_DV)).astype(BF16)
    g = cat(lambda r: _ret_heads(r, 2 * RET_NQ + RET_NV, RET_DV)).astype(F32)
    per_seq = lambda t: jnp.concatenate([t] * n_seq, axis=0)
    inner, qd, kd = per_seq(inner), per_seq(qd), per_seq(kd)
    cd = per_seq(jnp.stack([jnp.full((1, 1), cd_ref[h], F32) for h in range(RET_HEADS)]))
    qd = jnp.concatenate([qd] * (RET_DV // LANES), axis=-1)
    kd = jnp.concatenate([kd] * (RET_DK // LANES), axis=-1)
    scores = jnp.einsum('bqd,bkd->bqk', q, k.astype(BF16), preferred_element_type=F32) * inner
    o = (jnp.einsum('bqk,bke->bqe', scores.astype(BF16), v, preferred_element_type=F32)
         + jnp.einsum('bqd,bde->bqe', q, s0.astype(BF16), preferred_element_type=F32) * qd)
    s_new = s0 * cd + jnp.einsum('bkd,bke->bde', (k * kd).astype(BF16), v, preferred_element_type=F32)
    o = o * lax.rsqrt(jnp.mean(o * o, axis=-1, keepdims=True) + NORM_EPS)
    return o * (g * _sigmoid(g)), s_new


RET_SAMPLE_SEQS = 2


def _retention_body(cdp_ref, cds_ref, projp_ref, innerp_ref, qdp_ref, kdp_ref,
                    projs_ref, inners_ref, qds_ref, kds_ref, s0_ref, *rest):
    op_ref, sp_ref, os_ref, ss_ref, s_scr = rest[-5:]
    c = pl.program_id(0) % (SEQ // RET_PROMPT_CHUNK)

    @pl.when(c == 0)
    def _():
        s_scr[...] = jnp.zeros_like(s_scr)

    o, s_new = _ret_chunk([projp_ref[...]], s_scr[...], innerp_ref[...], qdp_ref[...], kdp_ref[...], cdp_ref)
    s_scr[...] = s_new
    op_ref[...] = jnp.concatenate([o[h] for h in range(RET_HEADS)], axis=-1).astype(BF16)

    @pl.when(c == SEQ // RET_PROMPT_CHUNK - 1)
    def _():
        sp_ref[0] = s_scr[...]

    rows = projs_ref[...].astype(F32)
    batch = RET_SAMPLE_SEQS * RET_HEADS
    o, s_new = _ret_chunk([rows[seq * DEC_SEQ:(seq + 1) * DEC_SEQ] for seq in range(RET_SAMPLE_SEQS)],
                          s0_ref[0].reshape(batch, RET_DK, RET_DV), inners_ref[...], qds_ref[...], kds_ref[...],
                          cds_ref)
    ss_ref[0] = s_new.reshape(RET_SAMPLE_SEQS, RET_HEADS, RET_DK, RET_DV)
    os_ref[...] = jnp.concatenate(
        [jnp.concatenate([o[seq * RET_HEADS + h] for h in range(RET_HEADS)], axis=-1)
         for seq in range(RET_SAMPLE_SEQS)], axis=0).astype(BF16)


def retention_core(proj, state_ret, j, s_stack):
    n_in = proj.shape[1]
    nc = SEQ // RET_PROMPT_CHUNK
    n_steps = BATCH * nc
    assert n_steps == DEC_BATCH // RET_SAMPLE_SEQS
    assert math.gcd(DEC_SEQ, RET_CHUNK) == DEC_SEQ
    smem = pl.BlockSpec(memory_space=pltpu.SMEM)
    full3 = lambda t: pl.BlockSpec(t.shape, lambda i: (0, 0, 0))
    inner_p, qd_p, kd_p, cd_p = _ret_tables(RET_PROMPT_CHUNK)
    inner_s, qd_s, kd_s, cd_s = _ret_tables(DEC_SEQ)
    rows = RET_SAMPLE_SEQS * DEC_SEQ
    row0 = NP // rows
    st = pl.BlockSpec((1, RET_SAMPLE_SEQS, RET_HEADS, RET_DK, RET_DV), lambda i: (j, i, 0, 0, 0))
    stacked = [] if s_stack is None else [s_stack]
    o_p, s_p, o_s, s_s = pl.pallas_call(
        _retention_body,
        grid=(n_steps,),
        in_specs=[
            smem, smem,
            pl.BlockSpec((RET_PROMPT_CHUNK, n_in), lambda i: (i, 0)),
            full3(inner_p), full3(qd_p), full3(kd_p),
            pl.BlockSpec((rows, n_in), lambda i: (row0 + i, 0)),
            full3(inner_s), full3(qd_s), full3(kd_s),
            st,
        ] + [pl.BlockSpec(memory_space=pl.ANY)] * len(stacked),
        out_specs=[
            pl.BlockSpec((RET_PROMPT_CHUNK, RET_NV), lambda i: (i, 0)),
            pl.BlockSpec((1, RET_HEADS, RET_DK, RET_DV), lambda i: (i // nc, 0, 0, 0)),
            pl.BlockSpec((rows, RET_NV), lambda i: (i, 0)),
            st,
        ],
        out_shape=[jax.ShapeDtypeStruct((NP, RET_NV), BF16),
                   jax.ShapeDtypeStruct((BATCH, RET_HEADS, RET_DK, RET_DV), F32),
                   jax.ShapeDtypeStruct((NS, RET_NV), BF16),
                   jax.ShapeDtypeStruct(state_ret.shape, F32)],
        input_output_aliases={11: 3} if stacked else {},
        scratch_shapes=[pltpu.VMEM((RET_HEADS, RET_DK, RET_DV), F32)],
        compiler_params=_cparams("arbitrary"),
        name="retention",
    )(cd_p, cd_s, proj, inner_p, qd_p, kd_p, proj, inner_s, qd_s, kd_s, state_ret, *stacked)
    return o_p, o_s, s_p, s_s


def _sink_attention(q, k, v, valid_t, sink_ref):
    t = q.shape[0]
    head_cols = lambda x, h: x[:, h * SWA_HD:(h + 1) * SWA_HD]
    qs = jnp.stack([jnp.concatenate([head_cols(q, kh * SWA_GROUP + g) for g in range(SWA_GROUP)], axis=0)
                    for kh in range(SWA_KV_HEADS)]).astype(BF16)
    ks = jnp.stack([head_cols(k, kh) for kh in range(SWA_KV_HEADS)]).astype(BF16)
    vs = jnp.stack([head_cols(v, kh) for kh in range(SWA_KV_HEADS)]).astype(BF16)
    sink = jnp.stack([jnp.concatenate([jnp.full((1, t), sink_ref[kh * SWA_GROUP + g], F32)
                                       for g in range(SWA_GROUP)], axis=1) for kh in range(SWA_KV_HEADS)])
    s = jnp.einsum('hkd,hqd->hkq', ks, qs, preferred_element_type=F32)
    s = jnp.where(valid_t[None], s, NEG_INF)
    m = jnp.maximum(jnp.max(s, axis=1, keepdims=True), sink)
    p = jnp.exp(s - m)
    denom = jnp.sum(p, axis=1, keepdims=True) + jnp.exp(sink - m)
    o_t = jnp.einsum('hkd,hkq->hdq', vs, p.astype(BF16), preferred_element_type=F32) / denom
    o = [o_t[kh].T for kh in range(SWA_KV_HEADS)]
    return jnp.concatenate([o[h // SWA_GROUP][(h % SWA_GROUP) * t:(h % SWA_GROUP + 1) * t]
                            for h in range(SWA_HEADS)], axis=-1)


def _swa_prompt_body(sink_ref, q_ref, kc_ref, vc_ref, kp_ref, vp_ref, o_ref):
    blk = pl.program_id(1)
    q = q_ref[...]
    k_all = jnp.concatenate([kp_ref[...], kc_ref[...]], axis=0)
    v_all = jnp.concatenate([vp_ref[...], vc_ref[...]], axis=0)
    shape = (2 * SWA_BLOCK, SWA_GROUP * SWA_BLOCK)
    i = _mod_pow2(lax.broadcasted_iota(jnp.int32, shape, 1), SWA_BLOCK)
    j = lax.broadcasted_iota(jnp.int32, shape, 0)
    rel = i + SWA_BLOCK - j
    valid = (rel >= 0) & (rel <= WINDOW) & (j >= jnp.where(blk > 0, 0, SWA_BLOCK))
    o_ref[...] = _sink_attention(q, k_all, v_all, valid, sink_ref).astype(BF16)


SWA_SAMPLE_SEQS = 8


def _swa_sample_body(sink_ref, q_ref, kn_ref, vn_ref, kc_ref, vc_ref, o_ref, ko_ref, vo_ref):
    rows = SWA_SAMPLE_SEQS * DEC_SEQ
    n_cache = SWA_SAMPLE_SEQS * WINDOW
    q = q_ref[...]
    kn = kn_ref[...]
    vn = vn_ref[...].astype(F32)
    k_all = jnp.concatenate([kc_ref[...].reshape(n_cache, SWA_NK), kn], axis=0)
    v_all = jnp.concatenate([vc_ref[...].reshape(n_cache, SWA_NK), vn], axis=0)
    for b in range(SWA_SAMPLE_SEQS):
        new = slice(b * DEC_SEQ, (b + 1) * DEC_SEQ)
        ko_ref[b, :WINDOW - DEC_SEQ, :] = kc_ref[b, DEC_SEQ:, :]
        ko_ref[b, WINDOW - DEC_SEQ:, :] = kn[new]
        vo_ref[b, :WINDOW - DEC_SEQ, :] = vc_ref[b, DEC_SEQ:, :]
        vo_ref[b, WINDOW - DEC_SEQ:, :] = vn[new]
    shape = (n_cache + rows, SWA_GROUP * rows)
    r = _mod_pow2(lax.broadcasted_iota(jnp.int32, shape, 1), rows)
    c = lax.broadcasted_iota(jnp.int32, shape, 0)
    q_seq, q_t = _div_pow2(r, DEC_SEQ), _mod_pow2(r, DEC_SEQ)
    is_new = c >= n_cache
    k_seq = jnp.where(is_new, _div_pow2(c - n_cache, DEC_SEQ), _div_pow2(c, WINDOW))
    k_slot = jnp.where(is_new, WINDOW + _mod_pow2(c - n_cache, DEC_SEQ), _mod_pow2(c, WINDOW))
    rel = q_t + WINDOW - k_slot
    valid = (q_seq == k_seq) & (rel >= 0) & (rel <= WINDOW)
    o_ref[...] = _sink_attention(q, k_all, v_all, valid, sink_ref).astype(BF16)


def swa_core(proj, k_rot, sinks, cache_k, cache_v):
    nb = SEQ // SWA_BLOCK
    smem = pl.BlockSpec(memory_space=pltpu.SMEM)
    kcol = SWA_NQ // SWA_NK
    cur = lambda b, i: b * nb + i
    prev = lambda b, i: b * nb + jnp.maximum(i - 1, 0)
    o = pl.pallas_call(
        _swa_prompt_body,
        grid=(BATCH, nb),
        in_specs=[
            smem,
            pl.BlockSpec((SWA_BLOCK, SWA_NQ), lambda b, i: (cur(b, i), 0)),
            pl.BlockSpec((SWA_BLOCK, SWA_NK), lambda b, i: (cur(b, i), kcol)),
            pl.BlockSpec((SWA_BLOCK, SWA_NK), lambda b, i: (cur(b, i), kcol + 1)),
            pl.BlockSpec((SWA_BLOCK, SWA_NK), lambda b, i: (prev(b, i), kcol)),
            pl.BlockSpec((SWA_BLOCK, SWA_NK), lambda b, i: (prev(b, i), kcol + 1)),
        ],
        out_specs=pl.BlockSpec((SWA_BLOCK, SWA_NQ), lambda b, i: (cur(b, i), 0)),
        out_shape=jax.ShapeDtypeStruct((NP, SWA_NQ), BF16),
        compiler_params=_cparams("parallel", "arbitrary"),
        name="swa_prompt",
    )(sinks, proj, proj, proj, proj, proj)

    rows = SWA_SAMPLE_SEQS * DEC_SEQ
    row0 = NP // rows
    o_s, k_s, v_s = pl.pallas_call(
        _swa_sample_body,
        grid=(DEC_BATCH // SWA_SAMPLE_SEQS,),
        in_specs=[
            smem,
            pl.BlockSpec((rows, SWA_NQ), lambda g: (row0 + g, 0)),
            pl.BlockSpec((rows, SWA_NK), lambda g: (row0 + g, 0)),
            pl.BlockSpec((rows, SWA_NK), lambda g: (row0 + g, kcol + 1)),
            pl.BlockSpec((SWA_SAMPLE_SEQS, WINDOW, SWA_NK), lambda g: (g, 0, 0)),
            pl.BlockSpec((SWA_SAMPLE_SEQS, WINDOW, SWA_NK), lambda g: (g, 0, 0)),
        ],
        out_specs=[
            pl.BlockSpec((rows, SWA_NQ), lambda g: (g, 0)),
            pl.BlockSpec((SWA_SAMPLE_SEQS, WINDOW, SWA_NK), lambda g: (g, 0, 0)),
            pl.BlockSpec((SWA_SAMPLE_SEQS, WINDOW, SWA_NK), lambda g: (g, 0, 0)),
        ],
        out_shape=[jax.ShapeDtypeStruct((NS, SWA_NQ), BF16),
                   jax.ShapeDtypeStruct((DEC_BATCH, WINDOW, SWA_NK), F32),
                   jax.ShapeDtypeStruct((DEC_BATCH, WINDOW, SWA_NK), F32)],
        compiler_params=_cparams("parallel"),
        name="swa_sample",
    )(sinks, proj, k_rot, proj, cache_k, cache_v)
    return o, o_s, k_s, v_s


RWKV_ROW_TILE = 256
RWKV_PAIRS = RWKV_HEADS // 2
RWKV_STACK = MXU_DIM_V7X
RWKV_SAMPLE_SEQS = RWKV_STACK // (RWKV_PAIRS * DEC_SEQ)


def _block_diag_ones():
    r = jnp.arange(MXU_DIM_V7X) // RWKV_HD
    return (r[:, None] == r[None, :]).astype(BF16)


def _store_pairs(ref, val):
    for p in range(RWKV_PAIRS):
        ref[p] = val[:, p * LANES:(p + 1) * LANES].astype(ref.dtype)


def _load_pairs(ref):
    return jnp.concatenate([ref[p] for p in range(RWKV_PAIRS)], axis=-1)


def _rwkv_proj_body(x_ref, xprev_ref, first_ref, gn_ref, mu_ref, wr_ref, wk_ref, wv_ref, w0_ref, w1_ref, w2_ref,
                    a0_ref, a1_ref, a2_ref, g1_ref, g2_ref, kk_ref, ka_ref, bd_ref,
                    r_out, lw_out, k_out, v_out, kk_out, b_out, g_out, hlast_out, h_scr):
    i = pl.program_id(0)
    h = _rms(x_ref[...], gn_ref[...])
    prev_tile_last = _rms(xprev_ref[...], gn_ref[...])[SUBLANES - 1:SUBLANES, :]
    rowid = lax.broadcasted_iota(jnp.int32, h.shape, 0)
    xp = jnp.where(rowid == 0, prev_tile_last, pltpu.roll(h, 1, 0))
    seq_len_mask = jnp.where(i >= NP // RWKV_ROW_TILE, DEC_SEQ - 1, SEQ - 1)
    is_first = jnp.bitwise_and(i * RWKV_ROW_TILE + rowid, seq_len_mask) == 0
    xp = jnp.where(is_first, first_ref[...], xp)
    _store_pairs(h_scr, h)
    every8th = pl.ds(SUBLANES - 1, RWKV_ROW_TILE // SUBLANES, stride=SUBLANES)
    hlast_out[...] = jnp.concatenate([h_scr[p, every8th, :] for p in range(RWKV_PAIRS)], axis=-1)
    d = xp - h
    xs = [h + d * mu_ref[i:i + 1, :] for i in range(6)]
    _store_pairs(r_out, _dot(xs[0], wr_ref[...]))
    k = _dot(xs[1], wk_ref[...])
    _store_pairs(v_out, _dot(xs[2], wv_ref[...]))
    z = w0_ref[...] + _dot(jnp.tanh(_dot(xs[3], w1_ref[...])), w2_ref[...])
    _store_pairs(lw_out, -math.exp(-0.5) * _sigmoid(z))
    a = _sigmoid(a0_ref[...] + _dot(_dot(xs[4], a1_ref[...]), a2_ref[...]))
    g_out[...] = _dot(_sigmoid(_dot(xs[5], g1_ref[...])), g2_ref[...]).astype(BF16)
    kk = k * kk_ref[...]
    kk = kk * jnp.minimum(lax.rsqrt(_seg_sum(kk * kk, bd_ref[...])), 1e12)
    _store_pairs(kk_out, kk)
    _store_pairs(b_out, kk * a)
    _store_pairs(k_out, k * (1.0 + (a - 1.0) * ka_ref[...]))


def rwkv_proj(x, shift_s, p):
    first = jnp.concatenate([
        jnp.zeros((RWKV_ROW_TILE, D_MODEL), F32),
        jnp.pad(shift_s[:, None, :], ((0, 0), (0, DEC_SEQ - 1), (0, 0))).reshape(NS, D_MODEL)], axis=0)
    prompt_tiles = NP // RWKV_ROW_TILE
    row = pl.BlockSpec((RWKV_ROW_TILE, D_MODEL), lambda i: (i, 0))
    prev = pl.BlockSpec((SUBLANES, D_MODEL), lambda i: (jnp.maximum(i * (RWKV_ROW_TILE // SUBLANES) - 1, 0), 0))
    first_spec = pl.BlockSpec((RWKV_ROW_TILE, D_MODEL), lambda i: (jnp.maximum(i - prompt_tiles + 1, 0), 0))
    pair = pl.BlockSpec((RWKV_PAIRS, RWKV_ROW_TILE, LANES), lambda i: (0, i, 0))
    full = lambda a: pl.BlockSpec(a.shape, lambda i: (0,) * a.ndim)
    consts = [p["g_mix"], p["mu"], p["wr"], p["wk"], p["wv"], p["w0"], p["w1"], p["w2"], p["a0"], p["a1"],
              p["a2"], p["g1"], p["g2"], p["k_k"], p["k_a"], p["bd"]]
    pair_shape = lambda dt: jax.ShapeDtypeStruct((RWKV_PAIRS, NT, LANES), dt)
    return pl.pallas_call(
        _rwkv_proj_body,
        grid=(NT // RWKV_ROW_TILE,),
        in_specs=[row, prev, first_spec] + [full(c) for c in consts],
        out_specs=[pair] * 6 + [row, pl.BlockSpec((RWKV_ROW_TILE // SUBLANES, D_MODEL), lambda i: (i, 0))],
        out_shape=[pair_shape(BF16), pair_shape(F32)] + [pair_shape(BF16)] * 4
        + [jax.ShapeDtypeStruct((NT, D_MODEL), BF16), jax.ShapeDtypeStruct((NT // SUBLANES, D_MODEL), F32)],
        scratch_shapes=[pltpu.VMEM((RWKV_PAIRS, RWKV_ROW_TILE, LANES), F32)],
        compiler_params=_cparams("parallel"),
        name="rwkv_proj",
    )(x, x, first, *consts)


def _rwkv_chunks(c_len, problems):
    n_grp = RWKV_STACK // c_len
    shift = int(math.log2(c_len))
    assert 2 ** shift == c_len
    row = lax.broadcasted_iota(jnp.int32, (RWKV_STACK, RWKV_STACK), 0)
    col = lax.broadcasted_iota(jnp.int32, (RWKV_STACK, RWKV_STACK), 1)
    same = jnp.right_shift(row, shift) == jnp.right_shift(col, shift)
    strict_t = same & (row < col)
    incl_t = same & (row <= col)
    tri_t = jnp.where(incl_t, 1.0, 0.0).astype(BF16)
    each = lambda f, *cols: [f(*xs) for xs in zip(*cols)]
    r, lw, k, v, kk, b, state, store = zip(*problems)
    lam = each(lambda x: sum(jnp.dot(part, tri_t, preferred_element_type=F32) for part in _split3(x.T)).T, lw)
    lam3 = each(lambda x: x.reshape(n_grp, c_len, LANES), lam)
    lam_end3 = each(lambda x: x[:, c_len - 1:c_len, :], lam3)
    e_end = each(lambda e, x: jnp.exp(e - x).reshape(RWKV_STACK, LANES), lam_end3, lam3)
    e_tot3 = each(jnp.exp, lam_end3)
    e_neg = each(lambda x: jnp.exp(-x), lam)
    at = each(lambda kk_, l, w: -kk_ * jnp.exp(l - w), kk, lam, lw)
    rt = each(lambda r_, l: r_ * jnp.exp(l), r, lam)
    mul = lambda x, y: x * y
    bh, kh, bb, kb = each(mul, b, e_neg), each(mul, k, e_neg), each(mul, b, e_end), each(mul, k, e_end)

    lane = lax.broadcasted_iota(jnp.int32, (RWKV_STACK, LANES), 1)
    halves = (lane < RWKV_HD, lane >= RWKV_HD)
    rhs = each(lambda x, y: jnp.concatenate([x, y], axis=0), bh, kh)
    at_pow, bt_up, rt_up = [], [], []
    for sel in halves:
        lhs = each(lambda x, y: jnp.concatenate([jnp.where(sel, x, 0.0), jnp.where(sel, y, 0.0)], axis=0), at, rt)
        mt = each(_dot_nt, rhs, lhs)
        at_pow.append(each(lambda x: jnp.where(strict_t, x[:RWKV_STACK, :RWKV_STACK], 0.0), mt))
        bt_up.append(each(lambda x: jnp.where(strict_t, x[RWKV_STACK:, :RWKV_STACK], 0.0), mt))
        rt_up.append(each(lambda x: jnp.concatenate([jnp.where(incl_t, x[:RWKV_STACK, RWKV_STACK:], 0.0),
                                                     jnp.where(incl_t, x[RWKV_STACK:, RWKV_STACK:], 0.0)], axis=0), mt))

    rows = lambda t, g: t[g * c_len:(g + 1) * c_len]
    states = each(lambda st: [st(g) for g in range(n_grp)], state)
    p = [each(lambda x, y, st: _dot_nt(jnp.concatenate([rows(x, g), rows(y, g)], axis=0), st[g]), at, rt, states)
         for g in range(n_grp)]
    n_prob = len(problems)
    a_s_t = [jnp.concatenate([p[g][i][:c_len] for g in range(n_grp)], axis=0).T for i in range(n_prob)]
    r_s_t = [jnp.concatenate([p[g][i][c_len:] for g in range(n_grp)], axis=0).T for i in range(n_prob)]
    v_t = each(lambda x: x.T, v)
    half_rows = lambda x, e: x[e * RWKV_HD:(e + 1) * RWKV_HD]
    n_half = len(halves)
    u_t = [each(lambda a_, v_, b_: half_rows(a_, e) + _dot(half_rows(v_, e), b_), a_s_t, v_t, bt_up[e])
           for e in range(n_half)]
    assert shift >= 2
    for it in range(shift - 1):
        u_t = [each(lambda u_, a_: u_ + _dot(u_, a_), u_t[e], at_pow[e]) for e in range(n_half)]
        if it + 2 < shift:
            at_pow = [each(lambda a_: _dot(a_, a_), at_pow[e]) for e in range(n_half)]
    z_t = [each(lambda u_, a_: _dot(u_, a_), u_t[e], at_pow[e]) for e in range(n_half)]
    u_t = [each(lambda u_, z_, a_: u_ + _dot(z_, a_), u_t[e], z_t[e], at_pow[e]) for e in range(n_half)]
    y_t = [each(lambda r_, u_, v_, m_: half_rows(r_, e) + _dot(jnp.concatenate([u_, half_rows(v_, e)], axis=1), m_),
                r_s_t, u_t[e], v_t, rt_up[e]) for e in range(n_half)]
    stack_halves = lambda parts: [jnp.concatenate([parts[e][i] for e in range(n_half)], axis=0).T
                                  for i in range(n_prob)]
    u = stack_halves(u_t)
    y = stack_halves(y_t)

    r128 = lax.broadcasted_iota(jnp.int32, (LANES, LANES), 0)
    c128 = lax.broadcasted_iota(jnp.int32, (LANES, LANES), 1)
    diag = (r128 < RWKV_HD) == (c128 < RWKV_HD)
    for g in range(n_grp):
        upd = each(lambda u_, v_, bb_, kb_: _dot_tn(jnp.concatenate([rows(u_, g), rows(v_, g)], axis=0),
                                                    jnp.concatenate([rows(bb_, g), rows(kb_, g)], axis=0)),
                   u, v, bb, kb)
        for i in range(n_prob):
            store[i](g, states[i][g] * e_tot3[i][g] + jnp.where(diag, upd[i], 0.0))
    return y


RWKV_PROMPT_SEQS = 4
RWKV_STEP_TOKENS = 128


def _rwkv_prompt_body(*refs):
    ins, (y_ref, s_ref, s_scr) = refs[:6 * RWKV_PROMPT_SEQS], refs[6 * RWKV_PROMPT_SEQS:]
    c = pl.program_id(1)

    @pl.when(c == 0)
    def _():
        s_scr[...] = jnp.zeros_like(s_scr)

    def one_chunk(ci, carry):
        rows = pl.ds(pl.multiple_of(ci * RWKV_CHUNK, RWKV_CHUNK), RWKV_CHUNK)
        def problem(seq):
            def store(g, s):
                s_scr[seq, g] = s

            args = [ins[a * RWKV_PROMPT_SEQS + seq][:, rows, :].reshape(RWKV_STACK, LANES).astype(F32)
                    for a in range(6)]
            return (*args, lambda g: s_scr[seq, g], store)

        ys = _rwkv_chunks(RWKV_CHUNK, [problem(seq) for seq in range(RWKV_PROMPT_SEQS)])
        for seq, y in enumerate(ys):
            y_ref[:, seq, rows, :] = y.reshape(RWKV_PAIRS, RWKV_CHUNK, LANES)
        return carry

    lax.fori_loop(0, RWKV_STEP_TOKENS // RWKV_CHUNK, one_chunk, 0)

    @pl.when(c == pl.num_programs(1) - 1)
    def _():
        for seq in range(RWKV_PROMPT_SEQS):
            for p in range(RWKV_PAIRS):
                s_ref[seq, 2 * p] = s_scr[seq, p, :RWKV_HD, :RWKV_HD]
                s_ref[seq, 2 * p + 1] = s_scr[seq, p, RWKV_HD:, RWKV_HD:]


RWKV_SAMPLE_PROBLEMS = 2


def _rwkv_sample_body(r_ref, lw_ref, k_ref, v_ref, kk_ref, b_ref, s0_ref, y_ref, s_ref):
    zero = jnp.zeros((RWKV_HD, RWKV_HD), F32)
    rows_per = RWKV_SAMPLE_SEQS * DEC_SEQ

    def problem(q):
        def state(g):
            p, seq = divmod(g, RWKV_SAMPLE_SEQS)
            seq += q * RWKV_SAMPLE_SEQS
            return jnp.concatenate([jnp.concatenate([s0_ref[seq, 2 * p], zero], axis=1),
                                    jnp.concatenate([zero, s0_ref[seq, 2 * p + 1]], axis=1)], axis=0)

        def store(g, s):
            p, seq = divmod(g, RWKV_SAMPLE_SEQS)
            seq += q * RWKV_SAMPLE_SEQS
            s_ref[seq, 2 * p] = s[:RWKV_HD, :RWKV_HD]
            s_ref[seq, 2 * p + 1] = s[RWKV_HD:, RWKV_HD:]

        args = [ref[:, q * rows_per:(q + 1) * rows_per, :].reshape(RWKV_STACK, LANES).astype(F32)
                for ref in (r_ref, lw_ref, k_ref, v_ref, kk_ref, b_ref)]
        return (*args, state, store)

    ys = _rwkv_chunks(DEC_SEQ, [problem(q) for q in range(RWKV_SAMPLE_PROBLEMS)])
    for q, y in enumerate(ys):
        y_ref[:, q * rows_per:(q + 1) * rows_per, :] = y.reshape(RWKV_PAIRS, rows_per, LANES)


def rwkv_core(r, lw, k, v, kk, b, state_s):
    assert RWKV_PAIRS * RWKV_CHUNK == RWKV_STACK
    nc = SEQ // RWKV_STEP_TOKENS
    st_shape = (RWKV_HEADS, RWKV_HD, RWKV_HD)
    seq_blk = lambda seq: pl.BlockSpec((RWKV_PAIRS, RWKV_STEP_TOKENS, LANES),
                                       lambda bi, c: (0, (bi * RWKV_PROMPT_SEQS + seq) * nc + c, 0))
    inputs = (r, lw, k, v, kk, b)
    y_p, s_p = pl.pallas_call(
        _rwkv_prompt_body,
        grid=(BATCH // RWKV_PROMPT_SEQS, nc),
        in_specs=[seq_blk(seq) for _ in inputs for seq in range(RWKV_PROMPT_SEQS)],
        out_specs=[pl.BlockSpec((RWKV_PAIRS, RWKV_PROMPT_SEQS, RWKV_STEP_TOKENS, LANES), lambda bi, c: (0, bi, c, 0)),
                   pl.BlockSpec((RWKV_PROMPT_SEQS,) + st_shape, lambda bi, c: (bi, 0, 0, 0))],
        out_shape=[jax.ShapeDtypeStruct((RWKV_PAIRS, BATCH, SEQ, LANES), F32),
                   jax.ShapeDtypeStruct((BATCH,) + st_shape, F32)],
        scratch_shapes=[pltpu.VMEM((RWKV_PROMPT_SEQS, RWKV_PAIRS, LANES, LANES), F32)],
        compiler_params=_cparams("parallel", "arbitrary"),
        name="rwkv_prompt",
    )(*[a for a in inputs for _ in range(RWKV_PROMPT_SEQS)])

    step_seqs = RWKV_SAMPLE_PROBLEMS * RWKV_SAMPLE_SEQS
    rows = step_seqs * DEC_SEQ
    row0 = NP // rows
    blk = pl.BlockSpec((RWKV_PAIRS, rows, LANES), lambda i: (0, row0 + i, 0))
    st = pl.BlockSpec((step_seqs,) + st_shape, lambda i: (i, 0, 0, 0))
    y_s, s_s = pl.pallas_call(
        _rwkv_sample_body,
        grid=(DEC_BATCH // step_seqs,),
        in_specs=[blk] * 6 + [st],
        out_specs=[pl.BlockSpec((RWKV_PAIRS, rows, LANES), lambda i: (0, i, 0)), st],
        out_shape=[jax.ShapeDtypeStruct((RWKV_PAIRS, NS, LANES), F32),
                   jax.ShapeDtypeStruct((DEC_BATCH,) + st_shape, F32)],
        compiler_params=_cparams("parallel"),
        name="rwkv_sample",
    )(*inputs, state_s)
    return y_p.reshape(RWKV_PAIRS, NP, LANES), y_s, s_p, s_s


def _rwkv_out_body(x_ref, yp_ref, ys_ref, r_ref, k_ref, v_ref, g_ref, rk_ref, lnw_ref, lnb_ref, bd_ref, wo_ref,
                   o_ref):
    bd = bd_ref[...]
    y = _pick_group(RWKV_ROW_TILE, _load_pairs(yp_ref), _load_pairs(ys_ref))
    yc = y - _seg_sum(y, bd) * (1.0 / RWKV_HD)
    var = _seg_sum(yc * yc, bd) * (1.0 / RWKV_HD)
    yn = yc * lax.rsqrt(var + RWKV_GN_EPS) * lnw_ref[...] + lnb_ref[...]
    rk = _load_pairs(r_ref).astype(F32) * _load_pairs(k_ref).astype(F32)
    bonus = _seg_sum(rk * rk_ref[...], bd) * _load_pairs(v_ref).astype(F32)
    z = (yn + bonus) * g_ref[...].astype(F32)
    o_ref[...] = x_ref[...] + _dot(z, wo_ref[...])


def rwkv_out(x, y_p, y_s, r, k, v, g, p):
    row = pl.BlockSpec((RWKV_ROW_TILE, D_MODEL), lambda i: (i, 0))
    pair_of = lambda f: pl.BlockSpec((RWKV_PAIRS, RWKV_ROW_TILE, LANES), lambda i: (0, f(i), 0))
    full = lambda a: pl.BlockSpec(a.shape, lambda i: (0,) * a.ndim)
    consts = [p["r_k"], p["ln_w"], p["ln_b"], p["bd"], p["w_out"]]
    return pl.pallas_call(
        _rwkv_out_body,
        grid=(NT // RWKV_ROW_TILE,),
        in_specs=[row, *_group_specs(RWKV_ROW_TILE, pair_of)] + [pair_of(lambda i: i)] * 3 + [row]
        + [full(c) for c in consts],
        out_specs=row,
        out_shape=jax.ShapeDtypeStruct((NT, D_MODEL), F32),
        compiler_params=_cparams("parallel"),
        name="rwkv_out",
    )(x, y_p, y_s, r, k, v, g, *consts)


def _pad_cols(w, n):
    return jnp.pad(w, ((0, 0), (0, n - w.shape[1])))


def _pad_rows(w, n):
    return jnp.pad(w, ((0, n - w.shape[0]), (0, 0)))


def kernel(x_prompt, x_sample, state_ret, cache_swa_k, cache_swa_v, state_rwkv_wkv, state_rwkv_shift,
           norm_mix, norm_mlp, norm_final, ret_w_in, ret_w_out,
           swa_w_in, swa_b_in, swa_sinks, swa_w_out,
           rwkv_mu, rwkv_w_rkv, rwkv_w0, rwkv_w1, rwkv_w2, rwkv_a0, rwkv_a1, rwkv_a2,
           rwkv_g1, rwkv_g2, rwkv_k_k, rwkv_k_a, rwkv_r_k, rwkv_ln_w, rwkv_ln_b, rwkv_w_out,
           mlp_w_up, mlp_w_down):
    x = (x_prompt.reshape(NP, D_MODEL), x_sample.reshape(NS, D_MODEL))
    if LAYER_KINDS[0] != 0:
        x = jnp.concatenate(x, axis=0)
    row = lambda t: t.reshape(1, -1)
    ret_p, k_p, k_s, v_p, v_s, wkv_p, wkv_s, sh_p, sh_s = ([] for _ in range(9))
    ret_s = None
    for layer in range(DEPTH):
        kind = LAYER_KINDS[layer]
        j = LAYER_KINDS[:layer].count(kind)
        g_mix = row(norm_mix[layer])
        if kind == 0:
            proj = ret_proj(x, g_mix, ret_w_in[j].astype(BF16))
            o_p, o_s, s_p, ret_s = retention_core(proj, state_ret, j, ret_s)
            x = matmul_residual(x, o_p, o_s, ret_w_out[j].astype(BF16))
            ret_p.append(s_p)
        elif kind == 1:
            proj, k_rot = swa_proj(x, g_mix, swa_w_in[j].astype(BF16), row(swa_b_in[j]))
            o_p, o_s, kc, vc = swa_core(proj, k_rot, swa_sinks[j],
                                        cache_swa_k[j].reshape(DEC_BATCH, WINDOW, SWA_NK),
                                        cache_swa_v[j].reshape(DEC_BATCH, WINDOW, SWA_NK))
            x = matmul_residual(x, o_p, o_s, swa_w_out[j].astype(BF16))
            kv_shape = (-1, WINDOW, SWA_KV_HEADS, SWA_HD)
            last = slice(SEQ // WINDOW - 1, NP // WINDOW, SEQ // WINDOW)
            k_p.append(k_rot.reshape(NT // WINDOW, WINDOW, SWA_NK)[last].reshape(kv_shape))
            v_p.append(proj.reshape(NT // WINDOW, WINDOW, -1)[last, :, SWA_NQ + SWA_NK:]
                       .astype(F32).reshape(kv_shape))
            k_s.append(kc.reshape(kv_shape))
            v_s.append(vc.reshape(kv_shape))
        else:
            lora_w = LANES
            lora_g = 2 * LANES
            p = dict(
                mu=rwkv_mu[j], wr=rwkv_w_rkv[j, 0].astype(BF16), wk=rwkv_w_rkv[j, 1].astype(BF16),
                wv=rwkv_w_rkv[j, 2].astype(BF16), w0=row(rwkv_w0[j]),
                w1=_pad_cols(rwkv_w1[j], lora_w).astype(BF16), w2=_pad_rows(rwkv_w2[j], lora_w).astype(BF16),
                a0=row(rwkv_a0[j]),
                a1=_pad_cols(rwkv_a1[j], lora_w).astype(BF16), a2=_pad_rows(rwkv_a2[j], lora_w).astype(BF16),
                g1=_pad_cols(rwkv_g1[j], lora_g).astype(BF16), g2=_pad_rows(rwkv_g2[j], lora_g).astype(BF16),
                k_k=row(rwkv_k_k[j]), k_a=row(rwkv_k_a[j]), r_k=row(rwkv_r_k[j]),
                ln_w=row(rwkv_ln_w[j]), ln_b=row(rwkv_ln_b[j]), bd=_block_diag_ones(),
                w_out=rwkv_w_out[j].astype(BF16), g_mix=g_mix)
            r, lw, k, v, kk, b, g, h_last = rwkv_proj(x, state_rwkv_shift[j], p)
            y_p, y_s, s_p, s_s = rwkv_core(r, lw, k, v, kk, b, state_rwkv_wkv[j])
            x = rwkv_out(x, y_p, y_s, r, k, v, g, p)
            wkv_p.append(s_p)
            wkv_s.append(s_s)
            sh_p.append(h_last[SEQ // SUBLANES - 1:NP // SUBLANES:SEQ // SUBLANES])
            sh_s.append(h_last[NP // SUBLANES:])
        x = mlp(x, row(norm_mlp[layer]), mlp_w_up, mlp_w_down, layer,
                final_g=row(norm_final) if layer == DEPTH - 1 else None)
    y_p, y_s = x
    return (y_p.reshape(BATCH, SEQ, D_MODEL), y_s.reshape(DEC_BATCH, DEC_SEQ, D_MODEL),
            jnp.stack(ret_p), ret_s, jnp.stack(k_p), jnp.stack(k_s), jnp.stack(v_p), jnp.stack(v_s),
            jnp.stack(wkv_p), jnp.stack(wkv_s), jnp.stack(sh_p), jnp.stack(sh_s))
```

```python
import functools
import math

import jax
import jax.numpy as jnp
from jax import lax
from jax.experimental import pallas as pl
from jax.experimental.pallas import tpu as pltpu

F32 = jnp.float32
BF16 = jnp.bfloat16

D_MODEL = 1024
BATCH = 8
SEQ = 2048
DEPTH = 4
DEC_BATCH = 128
DEC_SEQ = 8
PAST_LEN = 8192
LAYER_KINDS = tuple(i % 3 for i in range(DEPTH))
NORM_EPS = 1e-6
ROPE_THETA = 10000.0
NEG_INF = -1e30
D_FF = 4 * D_MODEL

RET_HEADS = 4
RET_DK = D_MODEL // RET_HEADS
RET_DV = 2 * D_MODEL // RET_HEADS
RET_CHUNK = 128
RET_PROMPT_CHUNK = 256
RET_NQ = RET_HEADS * RET_DK
RET_NV = RET_HEADS * RET_DV

SWA_HEADS = 16
SWA_KV_HEADS = 4
SWA_GROUP = SWA_HEADS // SWA_KV_HEADS
SWA_HD = D_MODEL // SWA_HEADS
WINDOW = 128
SWA_BLOCK = 128
SWA_NQ = SWA_HEADS * SWA_HD
SWA_NK = SWA_KV_HEADS * SWA_HD

RWKV_HD = 64
RWKV_HEADS = D_MODEL // RWKV_HD
RWKV_GN_EPS = 64e-5
RWKV_CHUNK = 32

NP = BATCH * SEQ
NS = DEC_BATCH * DEC_SEQ
NT = NP + NS

VMEM_LIMIT_V7X = 48 * 1024 * 1024
STREAMED_F32_VMEM_LIMIT_V7X = 56 * 1024 * 1024
LANES = 128
SUBLANES = 8
MXU_DIM_V7X = 256

ROW_TILE = 1024


def _cparams(*sem, vmem=VMEM_LIMIT_V7X):
    return pltpu.CompilerParams(dimension_semantics=sem, vmem_limit_bytes=vmem)


def _dot(a, b):
    return jnp.dot(a.astype(BF16), b.astype(BF16), preferred_element_type=F32)


def _dot_nt(a, b):
    return lax.dot_general(a.astype(BF16), b.astype(BF16), (((1,), (1,)), ((), ())),
                           preferred_element_type=F32)


def _dot_tn(a, b):
    return lax.dot_general(a.astype(BF16), b.astype(BF16), (((0,), (0,)), ((), ())),
                           preferred_element_type=F32)


def _div_pow2(x, n):
    assert n & (n - 1) == 0
    return jnp.right_shift(x, n.bit_length() - 1)


def _mod_pow2(x, n):
    assert n & (n - 1) == 0
    return jnp.bitwise_and(x, n - 1)


def _sigmoid(x):
    return 0.5 * jnp.tanh(0.5 * x) + 0.5


def _rms(x, g):
    ms = jnp.mean(x * x, axis=-1, keepdims=True)
    return x * lax.rsqrt(ms + NORM_EPS) * g


def _split3(x):
    hi = x.astype(BF16)
    r1 = x - hi.astype(F32)
    mid = r1.astype(BF16)
    lo = (r1 - mid.astype(F32)).astype(BF16)
    return hi, mid, lo


def _seg_sum(x, bd):
    xb = x.astype(BF16)
    return jnp.concatenate(
        [jnp.dot(xb[:, j * MXU_DIM_V7X:(j + 1) * MXU_DIM_V7X], bd, preferred_element_type=F32)
         for j in range(D_MODEL // MXU_DIM_V7X)], axis=-1)


def _rope_tables(pos, half):
    inv_freq = jnp.power(ROPE_THETA, -jnp.arange(half, dtype=F32) / half)
    ang = pos.astype(F32)[:, None] * inv_freq[None, :]
    return jnp.cos(ang), jnp.sin(ang)


def _group_specs(tile, block_of):
    n_prompt = NP // tile
    return (block_of(lambda i: jnp.minimum(i, n_prompt - 1)), block_of(lambda i: jnp.maximum(i - n_prompt, 0)))


def _pick_group(tile, p_val, s_val):
    return jnp.where(pl.program_id(0) < NP // tile, p_val, s_val)


def _token_rows(x, tile):
    if isinstance(x, tuple):
        return list(x), list(_group_specs(tile, lambda f: pl.BlockSpec((tile, D_MODEL), lambda *g: (f(g[0]), 0))))
    return [x], [pl.BlockSpec((tile, D_MODEL), lambda *g: (g[0], 0))]


def _read_rows(refs, tile):
    return refs[0][...] if len(refs) == 1 else _pick_group(tile, refs[0][...], refs[1][...])


def _rope_table_inputs(half, widen, tile):
    tp = widen(*_rope_tables(jnp.arange(SEQ, dtype=jnp.int32), half))
    ts = [jnp.tile(t, (DEC_BATCH, 1))
          for t in widen(*_rope_tables(PAST_LEN + jnp.arange(DEC_SEQ, dtype=jnp.int32), half))]
    n_prompt = NP // tile
    per_seq = SEQ // tile
    p_spec = pl.BlockSpec((tile, LANES), lambda *g: (g[0] % per_seq, 0))
    s_spec = pl.BlockSpec((tile, LANES), lambda *g: (jnp.maximum(g[0] - n_prompt, 0), 0))
    return [tp[0], tp[1], ts[0], ts[1]], [p_spec, p_spec, s_spec, s_spec]


RET_PROJ_TILE = RET_NQ + RET_NQ


def _ret_proj_body(n_x, *refs):
    x_refs, (g_ref, w_ref, cosp_ref, sinp_ref, coss_ref, sins_ref, o_ref, xn_ref) = refs[:n_x], refs[n_x:]
    j = pl.program_id(1)

    @pl.when(j == 0)
    def _():
        xn_ref[...] = _rms(_read_rows(x_refs, ROW_TILE), g_ref[...]).astype(BF16)

    acc = jnp.dot(xn_ref[...], w_ref[...], preferred_element_type=F32)
    o_ref[...] = acc.astype(BF16)

    @pl.when(j == 0)
    def _():
        cos = _pick_group(ROW_TILE, cosp_ref[...], coss_ref[...])
        sin = _pick_group(ROW_TILE, sinp_ref[...], sins_ref[...])
        half = RET_DK // 2
        for blk in range(RET_PROJ_TILE // RET_DK):
            lo, mid, hi = blk * RET_DK, blk * RET_DK + half, (blk + 1) * RET_DK
            x1, x2 = acc[:, lo:mid], acc[:, mid:hi]
            scale = 1.0 if blk < RET_HEADS else RET_DK ** -0.5
            o_ref[:, lo:mid] = ((x1 * cos - x2 * sin) * scale).astype(BF16)
            o_ref[:, mid:hi] = ((x2 * cos + x1 * sin) * scale).astype(BF16)


def ret_proj(x, g, w):
    n_out = w.shape[1]
    x_arrays, x_specs = _token_rows(x, ROW_TILE)
    tables, table_specs = _rope_table_inputs(RET_DK // 2, lambda c, s: (c, s), ROW_TILE)
    return pl.pallas_call(
        functools.partial(_ret_proj_body, len(x_arrays)),
        grid=(NT // ROW_TILE, n_out // RET_PROJ_TILE),
        in_specs=x_specs + [
            pl.BlockSpec((1, D_MODEL), lambda i, j: (0, 0)),
            pl.BlockSpec((D_MODEL, RET_PROJ_TILE), lambda i, j: (0, j)),
        ] + table_specs,
        out_specs=pl.BlockSpec((ROW_TILE, RET_PROJ_TILE), lambda i, j: (i, j)),
        out_shape=jax.ShapeDtypeStruct((NT, n_out), BF16),
        scratch_shapes=[pltpu.VMEM((ROW_TILE, D_MODEL), BF16)],
        compiler_params=_cparams("parallel", "arbitrary"),
        name="ret_proj",
    )(*x_arrays, g, w, *tables)


SWA_PROJ_ROW_TILE = 512


def _swa_rope(x, cos, sin):
    n = x.shape[1]
    half = SWA_HD // 2
    lane = lax.broadcasted_iota(jnp.int32, x.shape, 1)
    partner = jnp.where(_mod_pow2(lane, SWA_HD) < half, pltpu.roll(x, n - half, 1), pltpu.roll(x, half, 1))
    reps = n // LANES
    return x * jnp.concatenate([cos] * reps, axis=-1) + partner * jnp.concatenate([sin] * reps, axis=-1)


def _swa_proj_body(x_ref, g_ref, w_ref, b_ref, cosp_ref, sinp_ref, coss_ref, sins_ref, o_ref, kr_ref):
    xn = _rms(x_ref[...], g_ref[...]).astype(BF16)
    acc = jnp.dot(xn, w_ref[...], preferred_element_type=F32) + b_ref[...]
    n_qk = SWA_NQ + SWA_NK
    cos = _pick_group(SWA_PROJ_ROW_TILE, cosp_ref[...], coss_ref[...])
    sin = _pick_group(SWA_PROJ_ROW_TILE, sinp_ref[...], sins_ref[...])
    qk = _swa_rope(acc[:, :n_qk], cos, sin)
    assert math.log2(SWA_HD ** 0.5).is_integer()
    o_ref[:, :SWA_NQ] = (qk[:, :SWA_NQ] * (SWA_HD ** -0.5)).astype(BF16)
    o_ref[:, SWA_NQ:n_qk] = qk[:, SWA_NQ:].astype(BF16)
    o_ref[:, n_qk:] = acc[:, n_qk:].astype(BF16)
    kr_ref[...] = qk[:, SWA_NQ:]


def swa_proj(x, g, w, b):
    n_out = w.shape[1]

    def widen(c, s):
        rep = LANES // SWA_HD
        return jnp.tile(jnp.concatenate([c, c], axis=-1), (1, rep)), jnp.tile(jnp.concatenate([-s, s], axis=-1), (1, rep))

    tile = SWA_PROJ_ROW_TILE
    tables, table_specs = _rope_table_inputs(SWA_HD // 2, widen, tile)
    return pl.pallas_call(
        _swa_proj_body,
        grid=(NT // tile,),
        in_specs=[
            pl.BlockSpec((tile, D_MODEL), lambda i: (i, 0)),
            pl.BlockSpec((1, D_MODEL), lambda i: (0, 0)),
            pl.BlockSpec((D_MODEL, n_out), lambda i: (0, 0)),
            pl.BlockSpec((1, n_out), lambda i: (0, 0)),
        ] + table_specs,
        out_specs=[pl.BlockSpec((tile, n_out), lambda i: (i, 0)), pl.BlockSpec((tile, SWA_NK), lambda i: (i, 0))],
        out_shape=[jax.ShapeDtypeStruct((NT, n_out), BF16), jax.ShapeDtypeStruct((NT, SWA_NK), F32)],
        compiler_params=_cparams("parallel"),
        name="swa_proj",
    )(x, g, w, b, *tables)


def _matmul_residual_body(n_x, *refs):
    x_refs, (ap_ref, as_ref, w_ref, o_ref) = refs[:n_x], refs[n_x:]
    a = _pick_group(ROW_TILE, ap_ref[...], as_ref[...])
    o_ref[...] = _read_rows(x_refs, ROW_TILE) + jnp.dot(a, w_ref[...], preferred_element_type=F32)


def matmul_residual(x, a_p, a_s, w):
    k = a_p.shape[1]
    x_arrays, x_specs = _token_rows(x, ROW_TILE)
    return pl.pallas_call(
        functools.partial(_matmul_residual_body, len(x_arrays)),
        grid=(NT // ROW_TILE,),
        in_specs=x_specs + [
            *_group_specs(ROW_TILE, lambda f: pl.BlockSpec((ROW_TILE, k), lambda i: (f(i), 0))),
            pl.BlockSpec((k, D_MODEL), lambda i: (0, 0)),
        ],
        out_specs=pl.BlockSpec((ROW_TILE, D_MODEL), lambda i: (i, 0)),
        out_shape=jax.ShapeDtypeStruct((NT, D_MODEL), F32),
        compiler_params=_cparams("parallel"),
        name="matmul_residual",
    )(*x_arrays, a_p, a_s, w)


MLP_FF_TILE = 1024


def _mlp_body(has_final_norm, x_ref, g_ref, wu_ref, wd_ref, *rest):
    rest = list(rest)
    gf_ref = rest.pop(0) if has_final_norm else None
    outs, (xn_ref, acc_ref) = rest[:-2], rest[-2:]
    f = pl.program_id(1)

    @pl.when(f == 0)
    def _():
        xn_ref[...] = _rms(x_ref[...], g_ref[...]).astype(BF16)
        acc_ref[...] = jnp.zeros_like(acc_ref)

    a = jnp.maximum(jnp.dot(xn_ref[...], wu_ref[...].astype(BF16), preferred_element_type=F32), 0.0)
    acc_ref[...] += jnp.dot((a * a).astype(BF16), wd_ref[...].astype(BF16), preferred_element_type=F32)

    @pl.when(f == pl.num_programs(1) - 1)
    def _():
        out = x_ref[...] + acc_ref[...]
        if not has_final_norm:
            outs[0][...] = out
        else:
            yp_ref, ys_ref = outs
            y = _rms(out, gf_ref[...])
            is_prompt = pl.program_id(0) < NP // ROW_TILE

            @pl.when(is_prompt)
            def _():
                yp_ref[...] = y

            @pl.when(jnp.logical_not(is_prompt))
            def _():
                ys_ref[...] = y


def mlp(x, g, w_up, w_down, layer, final_g=None):
    row = pl.BlockSpec((ROW_TILE, D_MODEL), lambda i, f: (i, 0))
    vec = pl.BlockSpec((1, D_MODEL), lambda i, f: (0, 0))
    if final_g is None:
        extra_in, extra_specs = [], []
        out_specs = row
        out_shape = jax.ShapeDtypeStruct((NT, D_MODEL), F32)
    else:
        extra_in, extra_specs = [final_g], [vec]
        out_specs = list(_group_specs(ROW_TILE, lambda g_: pl.BlockSpec((ROW_TILE, D_MODEL), lambda i, f: (g_(i), 0))))
        out_shape = [jax.ShapeDtypeStruct((NP, D_MODEL), F32), jax.ShapeDtypeStruct((NS, D_MODEL), F32)]
    return pl.pallas_call(
        functools.partial(_mlp_body, final_g is not None),
        grid=(NT // ROW_TILE, D_FF // MLP_FF_TILE),
        in_specs=[
            row, vec,
            pl.BlockSpec((None, D_MODEL, MLP_FF_TILE), lambda i, f: (layer, 0, f)),
            pl.BlockSpec((None, MLP_FF_TILE, D_MODEL), lambda i, f: (layer, f, 0)),
        ] + extra_specs,
        out_specs=out_specs,
        out_shape=out_shape,
        scratch_shapes=[pltpu.VMEM((ROW_TILE, D_MODEL), BF16), pltpu.VMEM((ROW_TILE, D_MODEL), F32)],
        compiler_params=_cparams("parallel" if final_g is None else "arbitrary", "arbitrary", vmem=STREAMED_F32_VMEM_LIMIT_V7X),
        name="mlp",
    )(x, g, w_up, w_down, *extra_in)


def _ret_tables(c):
    log_gamma = jnp.log1p(-jnp.exp2(-5.0 - jnp.arange(RET_HEADS, dtype=F32)))
    idx = jnp.arange(c, dtype=F32)
    diff = idx[:, None] - idx[None, :]
    inner = jnp.where(diff >= 0, jnp.exp(log_gamma[:, None, None] * jnp.maximum(diff, 0.0)), 0.0)
    q_decay = jnp.exp(log_gamma[:, None] * (idx[None, :] + 1.0))
    k_decay = jnp.exp(log_gamma[:, None] * (c - 1.0 - idx[None, :]))
    chunk_decay = jnp.exp(log_gamma * c)
    rep = lambda t: jnp.broadcast_to(t[:, :, None], (RET_HEADS, c, LANES))
    return inner, rep(q_decay), rep(k_decay), chunk_decay


def _ret_heads(rows, lo, width):
    return jnp.stack([rows[:, lo + h * width:lo + (h + 1) * width] for h in range(RET_HEADS)])


def _ret_chunk(rows_list, s0, inner, qd, kd, cd_ref):
    n_seq = len(rows_list)
    cat = lambda f: jnp.concatenate([f(r) for r in rows_list], axis=0)
    q = cat(lambda r: _ret_heads(r, 0, RET_DK)).astype(BF16)
    k = cat(lambda r: _ret_heads(r, RET_NQ, RET_DK)).astype(F32)
    v = cat(lambda r: _ret_heads(r, 2 * RET_NQ, RET_DV)).astype(BF16)
    g = cat(lambda r: _ret_heads(r, 2 * RET_NQ + RET_NV, RET_DV)).astype(F32)
    per_seq = lambda t: jnp.concatenate([t] * n_seq, axis=0)
    inner, qd, kd = per_seq(inner), per_seq(qd), per_seq(kd)
    cd = per_seq(jnp.stack([jnp.full((1, 1), cd_ref[h], F32) for h in range(RET_HEADS)]))
    qd = jnp.concatenate([qd] * (RET_DV // LANES), axis=-1)
    kd = jnp.concatenate([kd] * (RET_DK // LANES), axis=-1)
    scores = jnp.einsum('bqd,bkd->bqk', q, k.astype(BF16), preferred_element_type=F32) * inner
    o = (jnp.einsum('bqk,bke->bqe', scores.astype(BF16), v, preferred_element_type=F32)
         + jnp.einsum('bqd,bde->bqe', q, s0.astype(BF16), preferred_element_type=F32) * qd)
    s_new = s0 * cd + jnp.einsum('bkd,bke->bde', (k * kd).astype(BF16), v, preferred_element_type=F32)
    o = o * lax.rsqrt(jnp.mean(o * o, axis=-1, keepdims=True) + NORM_EPS)
    return o * (g * _sigmoid(g)), s_new


RET_SAMPLE_SEQS = 2


def _retention_body(cdp_ref, cds_ref, projp_ref, innerp_ref, qdp_ref, kdp_ref,
                    projs_ref, inners_ref, qds_ref, kds_ref, s0_ref, *rest):
    op_ref, sp_ref, os_ref, ss_ref, s_scr = rest[-5:]
    c = pl.program_id(0) % (SEQ // RET_PROMPT_CHUNK)

    @pl.when(c == 0)
    def _():
        s_scr[...] = jnp.zeros_like(s_scr)

    o, s_new = _ret_chunk([projp_ref[...]], s_scr[...], innerp_ref[...], qdp_ref[...], kdp_ref[...], cdp_ref)
    s_scr[...] = s_new
    op_ref[...] = jnp.concatenate([o[h] for h in range(RET_HEADS)], axis=-1).astype(BF16)

    @pl.when(c == SEQ // RET_PROMPT_CHUNK - 1)
    def _():
        sp_ref[0, 0] = s_scr[...]

    rows = projs_ref[...].astype(F32)
    batch = RET_SAMPLE_SEQS * RET_HEADS
    o, s_new = _ret_chunk([rows[seq * DEC_SEQ:(seq + 1) * DEC_SEQ] for seq in range(RET_SAMPLE_SEQS)],
                          s0_ref[0].reshape(batch, RET_DK, RET_DV), inners_ref[...], qds_ref[...], kds_ref[...],
                          cds_ref)
    ss_ref[0] = s_new.reshape(RET_SAMPLE_SEQS, RET_HEADS, RET_DK, RET_DV)
    os_ref[...] = jnp.concatenate(
        [jnp.concatenate([o[seq * RET_HEADS + h] for h in range(RET_HEADS)], axis=-1)
         for seq in range(RET_SAMPLE_SEQS)], axis=0).astype(BF16)


def retention_core(proj, state_ret, j, stacks):
    n_in = proj.shape[1]
    nc = SEQ // RET_PROMPT_CHUNK
    n_steps = BATCH * nc
    assert n_steps == DEC_BATCH // RET_SAMPLE_SEQS
    assert math.gcd(DEC_SEQ, RET_CHUNK) == DEC_SEQ
    smem = pl.BlockSpec(memory_space=pltpu.SMEM)
    full3 = lambda t: pl.BlockSpec(t.shape, lambda i: (0, 0, 0))
    inner_p, qd_p, kd_p, cd_p = _ret_tables(RET_PROMPT_CHUNK)
    inner_s, qd_s, kd_s, cd_s = _ret_tables(DEC_SEQ)
    rows = RET_SAMPLE_SEQS * DEC_SEQ
    row0 = NP // rows
    st = pl.BlockSpec((1, RET_SAMPLE_SEQS, RET_HEADS, RET_DK, RET_DV), lambda i: (j, i, 0, 0, 0))
    stacked = [] if stacks is None else list(stacks)
    o_p, s_p, o_s, s_s = pl.pallas_call(
        _retention_body,
        grid=(n_steps,),
        in_specs=[
            smem, smem,
            pl.BlockSpec((RET_PROMPT_CHUNK, n_in), lambda i: (i, 0)),
            full3(inner_p), full3(qd_p), full3(kd_p),
            pl.BlockSpec((rows, n_in), lambda i: (row0 + i, 0)),
            full3(inner_s), full3(qd_s), full3(kd_s),
            st,
        ] + [pl.BlockSpec(memory_space=pl.ANY)] * len(stacked),
        out_specs=[
            pl.BlockSpec((RET_PROMPT_CHUNK, RET_NV), lambda i: (i, 0)),
            pl.BlockSpec((1, 1, RET_HEADS, RET_DK, RET_DV), lambda i: (j, i // nc, 0, 0, 0)),
            pl.BlockSpec((rows, RET_NV), lambda i: (i, 0)),
            st,
        ],
        out_shape=[jax.ShapeDtypeStruct((NP, RET_NV), BF16),
                   jax.ShapeDtypeStruct((state_ret.shape[0], BATCH, RET_HEADS, RET_DK, RET_DV), F32),
                   jax.ShapeDtypeStruct((NS, RET_NV), BF16),
                   jax.ShapeDtypeStruct(state_ret.shape, F32)],
        input_output_aliases={11: 1, 12: 3} if stacked else {},
        scratch_shapes=[pltpu.VMEM((RET_HEADS, RET_DK, RET_DV), F32)],
        compiler_params=_cparams("arbitrary"),
        name="retention",
    )(cd_p, cd_s, proj, inner_p, qd_p, kd_p, proj, inner_s, qd_s, kd_s, state_ret, *stacked)
    return o_p, o_s, s_p, s_s


def _sink_attention(q, k, v, valid_t, sink_ref):
    t = q.shape[0]
    head_cols = lambda x, h: x[:, h * SWA_HD:(h + 1) * SWA_HD]
    qs = jnp.stack([jnp.concatenate([head_cols(q, kh * SWA_GROUP + g) for g in range(SWA_GROUP)], axis=0)
                    for kh in range(SWA_KV_HEADS)]).astype(BF16)
    ks = jnp.stack([head_cols(k, kh) for kh in range(SWA_KV_HEADS)]).astype(BF16)
    vs = jnp.stack([head_cols(v, kh) for kh in range(SWA_KV_HEADS)]).astype(BF16)
    sink = jnp.stack([jnp.concatenate([jnp.full((1, t), sink_ref[kh * SWA_GROUP + g], F32)
                                       for g in range(SWA_GROUP)], axis=1) for kh in range(SWA_KV_HEADS)])
    s = jnp.einsum('hkd,hqd->hkq', ks, qs, preferred_element_type=F32)
    s = jnp.where(valid_t[None], s, NEG_INF)
    m = jnp.maximum(jnp.max(s, axis=1, keepdims=True), sink)
    p = jnp.exp(s - m)
    denom = jnp.sum(p, axis=1, keepdims=True) + jnp.exp(sink - m)
    o_t = jnp.einsum('hkd,hkq->hdq', vs, p.astype(BF16), preferred_element_type=F32) / denom
    o = [o_t[kh].T for kh in range(SWA_KV_HEADS)]
    return jnp.concatenate([o[h // SWA_GROUP][(h % SWA_GROUP) * t:(h % SWA_GROUP + 1) * t]
                            for h in range(SWA_HEADS)], axis=-1)


def _swa_prompt_body(sink_ref, q_ref, kc_ref, vc_ref, kp_ref, vp_ref, o_ref):
    blk = pl.program_id(1)
    q = q_ref[...]
    k_all = jnp.concatenate([kp_ref[...], kc_ref[...]], axis=0)
    v_all = jnp.concatenate([vp_ref[...], vc_ref[...]], axis=0)
    shape = (2 * SWA_BLOCK, SWA_GROUP * SWA_BLOCK)
    i = _mod_pow2(lax.broadcasted_iota(jnp.int32, shape, 1), SWA_BLOCK)
    j = lax.broadcasted_iota(jnp.int32, shape, 0)
    rel = i + SWA_BLOCK - j
    valid = (rel >= 0) & (rel <= WINDOW) & (j >= jnp.where(blk > 0, 0, SWA_BLOCK))
    o_ref[...] = _sink_attention(q, k_all, v_all, valid, sink_ref).astype(BF16)


SWA_SAMPLE_SEQS = 8


def _swa_sample_body(sink_ref, q_ref, kn_ref, vn_ref, kc_ref, vc_ref, o_ref, ko_ref, vo_ref):
    rows = SWA_SAMPLE_SEQS * DEC_SEQ
    n_cache = SWA_SAMPLE_SEQS * WINDOW
    q = q_ref[...]
    kn = kn_ref[...]
    vn = vn_ref[...].astype(F32)
    k_all = jnp.concatenate([kc_ref[...].reshape(n_cache, SWA_NK), kn], axis=0)
    v_all = jnp.concatenate([vc_ref[...].reshape(n_cache, SWA_NK), vn], axis=0)
    for b in range(SWA_SAMPLE_SEQS):
        new = slice(b * DEC_SEQ, (b + 1) * DEC_SEQ)
        ko_ref[b, :WINDOW - DEC_SEQ, :] = kc_ref[b, DEC_SEQ:, :]
        ko_ref[b, WINDOW - DEC_SEQ:, :] = kn[new]
        vo_ref[b, :WINDOW - DEC_SEQ, :] = vc_ref[b, DEC_SEQ:, :]
        vo_ref[b, WINDOW - DEC_SEQ:, :] = vn[new]
    shape = (n_cache + rows, SWA_GROUP * rows)
    r = _mod_pow2(lax.broadcasted_iota(jnp.int32, shape, 1), rows)
    c = lax.broadcasted_iota(jnp.int32, shape, 0)
    q_seq, q_t = _div_pow2(r, DEC_SEQ), _mod_pow2(r, DEC_SEQ)
    is_new = c >= n_cache
    k_seq = jnp.where(is_new, _div_pow2(c - n_cache, DEC_SEQ), _div_pow2(c, WINDOW))
    k_slot = jnp.where(is_new, WINDOW + _mod_pow2(c - n_cache, DEC_SEQ), _mod_pow2(c, WINDOW))
    rel = q_t + WINDOW - k_slot
    valid = (q_seq == k_seq) & (rel >= 0) & (rel <= WINDOW)
    o_ref[...] = _sink_attention(q, k_all, v_all, valid, sink_ref).astype(BF16)


def swa_core(proj, k_rot, sinks, cache_k, cache_v):
    nb = SEQ // SWA_BLOCK
    smem = pl.BlockSpec(memory_space=pltpu.SMEM)
    kcol = SWA_NQ // SWA_NK
    cur = lambda b, i: b * nb + i
    prev = lambda b, i: b * nb + jnp.maximum(i - 1, 0)
    o = pl.pallas_call(
        _swa_prompt_body,
        grid=(BATCH, nb),
        in_specs=[
            smem,
            pl.BlockSpec((SWA_BLOCK, SWA_NQ), lambda b, i: (cur(b, i), 0)),
            pl.BlockSpec((SWA_BLOCK, SWA_NK), lambda b, i: (cur(b, i), kcol)),
            pl.BlockSpec((SWA_BLOCK, SWA_NK), lambda b, i: (cur(b, i), kcol + 1)),
            pl.BlockSpec((SWA_BLOCK, SWA_NK), lambda b, i: (prev(b, i), kcol)),
            pl.BlockSpec((SWA_BLOCK, SWA_NK), lambda b, i: (prev(b, i), kcol + 1)),
        ],
        out_specs=pl.BlockSpec((SWA_BLOCK, SWA_NQ), lambda b, i: (cur(b, i), 0)),
        out_shape=jax.ShapeDtypeStruct((NP, SWA_NQ), BF16),
        compiler_params=_cparams("parallel", "arbitrary"),
        name="swa_prompt",
    )(sinks, proj, proj, proj, proj, proj)

    rows = SWA_SAMPLE_SEQS * DEC_SEQ
    row0 = NP // rows
    o_s, k_s, v_s = pl.pallas_call(
        _swa_sample_body,
        grid=(DEC_BATCH // SWA_SAMPLE_SEQS,),
        in_specs=[
            smem,
            pl.BlockSpec((rows, SWA_NQ), lambda g: (row0 + g, 0)),
            pl.BlockSpec((rows, SWA_NK), lambda g: (row0 + g, 0)),
            pl.BlockSpec((rows, SWA_NK), lambda g: (row0 + g, kcol + 1)),
            pl.BlockSpec((SWA_SAMPLE_SEQS, WINDOW, SWA_NK), lambda g: (g, 0, 0)),
            pl.BlockSpec((SWA_SAMPLE_SEQS, WINDOW, SWA_NK), lambda g: (g, 0, 0)),
        ],
        out_specs=[
            pl.BlockSpec((rows, SWA_NQ), lambda g: (g, 0)),
            pl.BlockSpec((SWA_SAMPLE_SEQS, WINDOW, SWA_NK), lambda g: (g, 0, 0)),
            pl.BlockSpec((SWA_SAMPLE_SEQS, WINDOW, SWA_NK), lambda g: (g, 0, 0)),
        ],
        out_shape=[jax.ShapeDtypeStruct((NS, SWA_NQ), BF16),
                   jax.ShapeDtypeStruct((DEC_BATCH, WINDOW, SWA_NK), F32),
                   jax.ShapeDtypeStruct((DEC_BATCH, WINDOW, SWA_NK), F32)],
        compiler_params=_cparams("parallel"),
        name="swa_sample",
    )(sinks, proj, k_rot, proj, cache_k, cache_v)
    return o, o_s, k_s, v_s


RWKV_ROW_TILE = 256
RWKV_PAIRS = RWKV_HEADS // 2
RWKV_STACK = MXU_DIM_V7X
RWKV_SAMPLE_SEQS = RWKV_STACK // (RWKV_PAIRS * DEC_SEQ)


def _block_diag_ones():
    r = jnp.arange(MXU_DIM_V7X) // RWKV_HD
    return (r[:, None] == r[None, :]).astype(BF16)


def _store_pairs(ref, val):
    for p in range(RWKV_PAIRS):
        ref[p] = val[:, p * LANES:(p + 1) * LANES].astype(ref.dtype)


def _load_pairs(ref):
    return jnp.concatenate([ref[p] for p in range(RWKV_PAIRS)], axis=-1)


def _rwkv_proj_body(x_ref, xprev_ref, first_ref, gn_ref, mu_ref, wr_ref, wk_ref, wv_ref, w0_ref, w1_ref, w2_ref,
                    a0_ref, a1_ref, a2_ref, g1_ref, g2_ref, kk_ref, ka_ref, bd_ref,
                    r_out, lw_out, k_out, v_out, kk_out, b_out, g_out, hlast_out, h_scr):
    i = pl.program_id(0)
    h = _rms(x_ref[...], gn_ref[...])
    prev_tile_last = _rms(xprev_ref[...], gn_ref[...])[SUBLANES - 1:SUBLANES, :]
    rowid = lax.broadcasted_iota(jnp.int32, h.shape, 0)
    xp = jnp.where(rowid == 0, prev_tile_last, pltpu.roll(h, 1, 0))
    seq_len_mask = jnp.where(i >= NP // RWKV_ROW_TILE, DEC_SEQ - 1, SEQ - 1)
    is_first = jnp.bitwise_and(i * RWKV_ROW_TILE + rowid, seq_len_mask) == 0
    xp = jnp.where(is_first, first_ref[...], xp)
    _store_pairs(h_scr, h)
    every8th = pl.ds(SUBLANES - 1, RWKV_ROW_TILE // SUBLANES, stride=SUBLANES)
    hlast_out[...] = jnp.concatenate([h_scr[p, every8th, :] for p in range(RWKV_PAIRS)], axis=-1)
    d = xp - h
    xs = [h + d * mu_ref[i:i + 1, :] for i in range(6)]
    _store_pairs(r_out, _dot(xs[0], wr_ref[...]))
    k = _dot(xs[1], wk_ref[...])
    _store_pairs(v_out, _dot(xs[2], wv_ref[...]))
    z = w0_ref[...] + _dot(jnp.tanh(_dot(xs[3], w1_ref[...])), w2_ref[...])
    _store_pairs(lw_out, -math.exp(-0.5) * _sigmoid(z))
    a = _sigmoid(a0_ref[...] + _dot(_dot(xs[4], a1_ref[...]), a2_ref[...]))
    g_out[...] = _dot(_sigmoid(_dot(xs[5], g1_ref[...])), g2_ref[...]).astype(BF16)
    kk = k * kk_ref[...]
    kk = kk * jnp.minimum(lax.rsqrt(_seg_sum(kk * kk, bd_ref[...])), 1e12)
    _store_pairs(kk_out, kk)
    _store_pairs(b_out, kk * a)
    _store_pairs(k_out, k * (1.0 + (a - 1.0) * ka_ref[...]))


def rwkv_proj(x, shift_s, p):
    first = jnp.concatenate([
        jnp.zeros((RWKV_ROW_TILE, D_MODEL), F32),
        jnp.pad(shift_s[:, None, :], ((0, 0), (0, DEC_SEQ - 1), (0, 0))).reshape(NS, D_MODEL)], axis=0)
    prompt_tiles = NP // RWKV_ROW_TILE
    row = pl.BlockSpec((RWKV_ROW_TILE, D_MODEL), lambda i: (i, 0))
    prev = pl.BlockSpec((SUBLANES, D_MODEL), lambda i: (jnp.maximum(i * (RWKV_ROW_TILE // SUBLANES) - 1, 0), 0))
    first_spec = pl.BlockSpec((RWKV_ROW_TILE, D_MODEL), lambda i: (jnp.maximum(i - prompt_tiles + 1, 0), 0))
    pair = pl.BlockSpec((RWKV_PAIRS, RWKV_ROW_TILE, LANES), lambda i: (0, i, 0))
    full = lambda a: pl.BlockSpec(a.shape, lambda i: (0,) * a.ndim)
    consts = [p["g_mix"], p["mu"], p["wr"], p["wk"], p["wv"], p["w0"], p["w1"], p["w2"], p["a0"], p["a1"],
              p["a2"], p["g1"], p["g2"], p["k_k"], p["k_a"], p["bd"]]
    pair_shape = lambda dt: jax.ShapeDtypeStruct((RWKV_PAIRS, NT, LANES), dt)
    return pl.pallas_call(
        _rwkv_proj_body,
        grid=(NT // RWKV_ROW_TILE,),
        in_specs=[row, prev, first_spec] + [full(c) for c in consts],
        out_specs=[pair] * 6 + [row, pl.BlockSpec((RWKV_ROW_TILE // SUBLANES, D_MODEL), lambda i: (i, 0))],
        out_shape=[pair_shape(BF16), pair_shape(F32)] + [pair_shape(BF16)] * 4
        + [jax.ShapeDtypeStruct((NT, D_MODEL), BF16), jax.ShapeDtypeStruct((NT // SUBLANES, D_MODEL), F32)],
        scratch_shapes=[pltpu.VMEM((RWKV_PAIRS, RWKV_ROW_TILE, LANES), F32)],
        compiler_params=_cparams("parallel"),
        name="rwkv_proj",
    )(x, x, first, *consts)


def _rwkv_chunks(c_len, problems):
    n_grp = RWKV_STACK // c_len
    shift = int(math.log2(c_len))
    assert 2 ** shift == c_len
    row = lax.broadcasted_iota(jnp.int32, (RWKV_STACK, RWKV_STACK), 0)
    col = lax.broadcasted_iota(jnp.int32, (RWKV_STACK, RWKV_STACK), 1)
    same = jnp.right_shift(row, shift) == jnp.right_shift(col, shift)
    strict_t = same & (row < col)
    incl_t = same & (row <= col)
    tri_t = jnp.where(incl_t, 1.0, 0.0).astype(BF16)
    each = lambda f, *cols: [f(*xs) for xs in zip(*cols)]
    r, lw, k, v, kk, b, state, store = zip(*problems)
    lam = each(lambda x: sum(jnp.dot(part, tri_t, preferred_element_type=F32) for part in _split3(x.T)).T, lw)
    lam3 = each(lambda x: x.reshape(n_grp, c_len, LANES), lam)
    lam_end3 = each(lambda x: x[:, c_len - 1:c_len, :], lam3)
    e_end = each(lambda e, x: jnp.exp(e - x).reshape(RWKV_STACK, LANES), lam_end3, lam3)
    e_tot3 = each(jnp.exp, lam_end3)
    e_neg = each(lambda x: jnp.exp(-x), lam)
    at = each(lambda kk_, l, w: -kk_ * jnp.exp(l - w), kk, lam, lw)
    rt = each(lambda r_, l: r_ * jnp.exp(l), r, lam)
    mul = lambda x, y: x * y
    bh, kh, bb, kb = each(mul, b, e_neg), each(mul, k, e_neg), each(mul, b, e_end), each(mul, k, e_end)

    lane = lax.broadcasted_iota(jnp.int32, (RWKV_STACK, LANES), 1)
    halves = (lane < RWKV_HD, lane >= RWKV_HD)
    rhs = each(lambda x, y: jnp.concatenate([x, y], axis=0), bh, kh)
    at_pow, bt_up, rt_up = [], [], []
    for sel in halves:
        lhs = each(lambda x, y: jnp.concatenate([jnp.where(sel, x, 0.0), jnp.where(sel, y, 0.0)], axis=0), at, rt)
        mt = each(_dot_nt, rhs, lhs)
        at_pow.append(each(lambda x: jnp.where(strict_t, x[:RWKV_STACK, :RWKV_STACK], 0.0), mt))
        bt_up.append(each(lambda x: jnp.where(strict_t, x[RWKV_STACK:, :RWKV_STACK], 0.0), mt))
        rt_up.append(each(lambda x: jnp.concatenate([jnp.where(incl_t, x[:RWKV_STACK, RWKV_STACK:], 0.0),
                                                     jnp.where(incl_t, x[RWKV_STACK:, RWKV_STACK:], 0.0)], axis=0), mt))

    rows = lambda t, g: t[g * c_len:(g + 1) * c_len]
    states = each(lambda st: [st(g) for g in range(n_grp)], state)
    p = [each(lambda x, y, st: _dot_nt(jnp.concatenate([rows(x, g), rows(y, g)], axis=0), st[g]), at, rt, states)
         for g in range(n_grp)]
    n_prob = len(problems)
    a_s_t = [jnp.concatenate([p[g][i][:c_len] for g in range(n_grp)], axis=0).T for i in range(n_prob)]
    r_s_t = [jnp.concatenate([p[g][i][c_len:] for g in range(n_grp)], axis=0).T for i in range(n_prob)]
    v_t = each(lambda x: x.T, v)
    half_rows = lambda x, e: x[e * RWKV_HD:(e + 1) * RWKV_HD]
    n_half = len(halves)
    u_t = [each(lambda a_, v_, b_: half_rows(a_, e) + _dot(half_rows(v_, e), b_), a_s_t, v_t, bt_up[e])
           for e in range(n_half)]
    assert shift >= 2
    for it in range(shift - 1):
        u_t = [each(lambda u_, a_: u_ + _dot(u_, a_), u_t[e], at_pow[e]) for e in range(n_half)]
        if it + 2 < shift:
            at_pow = [each(lambda a_: _dot(a_, a_), at_pow[e]) for e in range(n_half)]
    z_t = [each(lambda u_, a_: _dot(u_, a_), u_t[e], at_pow[e]) for e in range(n_half)]
    u_t = [each(lambda u_, z_, a_: u_ + _dot(z_, a_), u_t[e], z_t[e], at_pow[e]) for e in range(n_half)]
    y_t = [each(lambda r_, u_, v_, m_: half_rows(r_, e) + _dot(jnp.concatenate([u_, half_rows(v_, e)], axis=1), m_),
                r_s_t, u_t[e], v_t, rt_up[e]) for e in range(n_half)]
    stack_halves = lambda parts: [jnp.concatenate([parts[e][i] for e in range(n_half)], axis=0).T
                                  for i in range(n_prob)]
    u = stack_halves(u_t)
    y = stack_halves(y_t)

    r128 = lax.broadcasted_iota(jnp.int32, (LANES, LANES), 0)
    c128 = lax.broadcasted_iota(jnp.int32, (LANES, LANES), 1)
    diag = (r128 < RWKV_HD) == (c128 < RWKV_HD)
    for g in range(n_grp):
        upd = each(lambda u_, v_, bb_, kb_: _dot_tn(jnp.concatenate([rows(u_, g), rows(v_, g)], axis=0),
                                                    jnp.concatenate([rows(bb_, g), rows(kb_, g)], axis=0)),
                   u, v, bb, kb)
        for i in range(n_prob):
            store[i](g, states[i][g] * e_tot3[i][g] + jnp.where(diag, upd[i], 0.0))
    return y


RWKV_PROMPT_SEQS = 4
RWKV_STEP_TOKENS = 128


def _rwkv_prompt_body(*refs):
    ins, (y_ref, s_ref, s_scr) = refs[:6 * RWKV_PROMPT_SEQS], refs[6 * RWKV_PROMPT_SEQS:]
    c = pl.program_id(1)

    @pl.when(c == 0)
    def _():
        s_scr[...] = jnp.zeros_like(s_scr)

    def one_chunk(ci, carry):
        rows = pl.ds(pl.multiple_of(ci * RWKV_CHUNK, RWKV_CHUNK), RWKV_CHUNK)
        def problem(seq):
            def store(g, s):
                s_scr[seq, g] = s

            args = [ins[a * RWKV_PROMPT_SEQS + seq][:, rows, :].reshape(RWKV_STACK, LANES).astype(F32)
                    for a in range(6)]
            return (*args, lambda g: s_scr[seq, g], store)

        ys = _rwkv_chunks(RWKV_CHUNK, [problem(seq) for seq in range(RWKV_PROMPT_SEQS)])
        for seq, y in enumerate(ys):
            y_ref[:, seq, rows, :] = y.reshape(RWKV_PAIRS, RWKV_CHUNK, LANES)
        return carry

    lax.fori_loop(0, RWKV_STEP_TOKENS // RWKV_CHUNK, one_chunk, 0)

    @pl.when(c == pl.num_programs(1) - 1)
    def _():
        for seq in range(RWKV_PROMPT_SEQS):
            for p in range(RWKV_PAIRS):
                s_ref[seq, 2 * p] = s_scr[seq, p, :RWKV_HD, :RWKV_HD]
                s_ref[seq, 2 * p + 1] = s_scr[seq, p, RWKV_HD:, RWKV_HD:]


RWKV_SAMPLE_PROBLEMS = 2


def _rwkv_sample_body(r_ref, lw_ref, k_ref, v_ref, kk_ref, b_ref, s0_ref, y_ref, s_ref):
    zero = jnp.zeros((RWKV_HD, RWKV_HD), F32)
    rows_per = RWKV_SAMPLE_SEQS * DEC_SEQ

    def problem(q):
        def state(g):
            p, seq = divmod(g, RWKV_SAMPLE_SEQS)
            seq += q * RWKV_SAMPLE_SEQS
            return jnp.concatenate([jnp.concatenate([s0_ref[seq, 2 * p], zero], axis=1),
                                    jnp.concatenate([zero, s0_ref[seq, 2 * p + 1]], axis=1)], axis=0)

        def store(g, s):
            p, seq = divmod(g, RWKV_SAMPLE_SEQS)
            seq += q * RWKV_SAMPLE_SEQS
            s_ref[seq, 2 * p] = s[:RWKV_HD, :RWKV_HD]
            s_ref[seq, 2 * p + 1] = s[RWKV_HD:, RWKV_HD:]

        args = [ref[:, q * rows_per:(q + 1) * rows_per, :].reshape(RWKV_STACK, LANES).astype(F32)
                for ref in (r_ref, lw_ref, k_ref, v_ref, kk_ref, b_ref)]
        return (*args, state, store)

    ys = _rwkv_chunks(DEC_SEQ, [problem(q) for q in range(RWKV_SAMPLE_PROBLEMS)])
    for q, y in enumerate(ys):
        y_ref[:, q * rows_per:(q + 1) * rows_per, :] = y.reshape(RWKV_PAIRS, rows_per, LANES)


def rwkv_core(r, lw, k, v, kk, b, state_s, j):
    assert RWKV_PAIRS * RWKV_CHUNK == RWKV_STACK
    nc = SEQ // RWKV_STEP_TOKENS
    st_shape = (RWKV_HEADS, RWKV_HD, RWKV_HD)
    seq_blk = lambda seq: pl.BlockSpec((RWKV_PAIRS, RWKV_STEP_TOKENS, LANES),
                                       lambda bi, c: (0, (bi * RWKV_PROMPT_SEQS + seq) * nc + c, 0))
    inputs = (r, lw, k, v, kk, b)
    y_p, s_p = pl.pallas_call(
        _rwkv_prompt_body,
        grid=(BATCH // RWKV_PROMPT_SEQS, nc),
        in_specs=[seq_blk(seq) for _ in inputs for seq in range(RWKV_PROMPT_SEQS)],
        out_specs=[pl.BlockSpec((RWKV_PAIRS, RWKV_PROMPT_SEQS, RWKV_STEP_TOKENS, LANES), lambda bi, c: (0, bi, c, 0)),
                   pl.BlockSpec((RWKV_PROMPT_SEQS,) + st_shape, lambda bi, c: (bi, 0, 0, 0))],
        out_shape=[jax.ShapeDtypeStruct((RWKV_PAIRS, BATCH, SEQ, LANES), F32),
                   jax.ShapeDtypeStruct((BATCH,) + st_shape, F32)],
        scratch_shapes=[pltpu.VMEM((RWKV_PROMPT_SEQS, RWKV_PAIRS, LANES, LANES), F32)],
        compiler_params=_cparams("parallel", "arbitrary"),
        name="rwkv_prompt",
    )(*[a for a in inputs for _ in range(RWKV_PROMPT_SEQS)])

    step_seqs = RWKV_SAMPLE_PROBLEMS * RWKV_SAMPLE_SEQS
    rows = step_seqs * DEC_SEQ
    row0 = NP // rows
    blk = pl.BlockSpec((RWKV_PAIRS, rows, LANES), lambda i: (0, row0 + i, 0))
    st = pl.BlockSpec((step_seqs,) + st_shape, lambda i: (i, 0, 0, 0))
    st_in = pl.BlockSpec((None, step_seqs) + st_shape, lambda i: (j, i, 0, 0, 0))
    y_s, s_s = pl.pallas_call(
        _rwkv_sample_body,
        grid=(DEC_BATCH // step_seqs,),
        in_specs=[blk] * 6 + [st_in],
        out_specs=[pl.BlockSpec((RWKV_PAIRS, rows, LANES), lambda i: (0, i, 0)), st],
        out_shape=[jax.ShapeDtypeStruct((RWKV_PAIRS, NS, LANES), F32),
                   jax.ShapeDtypeStruct((DEC_BATCH,) + st_shape, F32)],
        compiler_params=_cparams("parallel"),
        name="rwkv_sample",
    )(*inputs, state_s)
    return y_p.reshape(RWKV_PAIRS, NP, LANES), y_s, s_p, s_s


def _rwkv_out_body(x_ref, yp_ref, ys_ref, r_ref, k_ref, v_ref, g_ref, rk_ref, lnw_ref, lnb_ref, bd_ref, wo_ref,
                   o_ref):
    bd = bd_ref[...]
    y = _pick_group(RWKV_ROW_TILE, _load_pairs(yp_ref), _load_pairs(ys_ref))
    yc = y - _seg_sum(y, bd) * (1.0 / RWKV_HD)
    var = _seg_sum(yc * yc, bd) * (1.0 / RWKV_HD)
    yn = yc * lax.rsqrt(var + RWKV_GN_EPS) * lnw_ref[...] + lnb_ref[...]
    rk = _load_pairs(r_ref).astype(F32) * _load_pairs(k_ref).astype(F32)
    bonus = _seg_sum(rk * rk_ref[...], bd) * _load_pairs(v_ref).astype(F32)
    z = (yn + bonus) * g_ref[...].astype(F32)
    o_ref[...] = x_ref[...] + _dot(z, wo_ref[...])


def rwkv_out(x, y_p, y_s, r, k, v, g, p):
    row = pl.BlockSpec((RWKV_ROW_TILE, D_MODEL), lambda i: (i, 0))
    pair_of = lambda f: pl.BlockSpec((RWKV_PAIRS, RWKV_ROW_TILE, LANES), lambda i: (0, f(i), 0))
    full = lambda a: pl.BlockSpec(a.shape, lambda i: (0,) * a.ndim)
    consts = [p["r_k"], p["ln_w"], p["ln_b"], p["bd"], p["w_out"]]
    return pl.pallas_call(
        _rwkv_out_body,
        grid=(NT // RWKV_ROW_TILE,),
        in_specs=[row, *_group_specs(RWKV_ROW_TILE, pair_of)] + [pair_of(lambda i: i)] * 3 + [row]
        + [full(c) for c in consts],
        out_specs=row,
        out_shape=jax.ShapeDtypeStruct((NT, D_MODEL), F32),
        compiler_params=_cparams("parallel"),
        name="rwkv_out",
    )(x, y_p, y_s, r, k, v, g, *consts)


def _pad_cols(w, n):
    return jnp.pad(w, ((0, 0), (0, n - w.shape[1])))


def _pad_rows(w, n):
    return jnp.pad(w, ((0, n - w.shape[0]), (0, 0)))


def kernel(x_prompt, x_sample, state_ret, cache_swa_k, cache_swa_v, state_rwkv_wkv, state_rwkv_shift,
           norm_mix, norm_mlp, norm_final, ret_w_in, ret_w_out,
           swa_w_in, swa_b_in, swa_sinks, swa_w_out,
           rwkv_mu, rwkv_w_rkv, rwkv_w0, rwkv_w1, rwkv_w2, rwkv_a0, rwkv_a1, rwkv_a2,
           rwkv_g1, rwkv_g2, rwkv_k_k, rwkv_k_a, rwkv_r_k, rwkv_ln_w, rwkv_ln_b, rwkv_w_out,
           mlp_w_up, mlp_w_down):
    x = (x_prompt.reshape(NP, D_MODEL), x_sample.reshape(NS, D_MODEL))
    if LAYER_KINDS[0] != 0:
        x = jnp.concatenate(x, axis=0)
    row = lambda t: t.reshape(1, -1)
    k_p, k_s, v_p, v_s, wkv_p, wkv_s, sh_p, sh_s = ([] for _ in range(8))
    ret_states = None
    for layer in range(DEPTH):
        kind = LAYER_KINDS[layer]
        j = LAYER_KINDS[:layer].count(kind)
        g_mix = row(norm_mix[layer])
        if kind == 0:
            proj = ret_proj(x, g_mix, ret_w_in[j].astype(BF16))
            o_p, o_s, *ret_states = retention_core(proj, state_ret, j, ret_states)
            x = matmul_residual(x, o_p, o_s, ret_w_out[j].astype(BF16))
        elif kind == 1:
            proj, k_rot = swa_proj(x, g_mix, swa_w_in[j].astype(BF16), row(swa_b_in[j]))
            o_p, o_s, kc, vc = swa_core(proj, k_rot, swa_sinks[j],
                                        cache_swa_k[j].reshape(DEC_BATCH, WINDOW, SWA_NK),
                                        cache_swa_v[j].reshape(DEC_BATCH, WINDOW, SWA_NK))
            x = matmul_residual(x, o_p, o_s, swa_w_out[j].astype(BF16))
            kv_shape = (-1, WINDOW, SWA_KV_HEADS, SWA_HD)
            last = slice(SEQ // WINDOW - 1, NP // WINDOW, SEQ // WINDOW)
            k_p.append(k_rot.reshape(NT // WINDOW, WINDOW, SWA_NK)[last].reshape(kv_shape))
            v_p.append(proj.reshape(NT // WINDOW, WINDOW, -1)[last, :, SWA_NQ + SWA_NK:]
                       .astype(F32).reshape(kv_shape))
            k_s.append(kc.reshape(kv_shape))
            v_s.append(vc.reshape(kv_shape))
        else:
            lora_w = LANES
            lora_g = 2 * LANES
            p = dict(
                mu=rwkv_mu[j], wr=rwkv_w_rkv[j, 0].astype(BF16), wk=rwkv_w_rkv[j, 1].astype(BF16),
                wv=rwkv_w_rkv[j, 2].astype(BF16), w0=row(rwkv_w0[j]),
                w1=_pad_cols(rwkv_w1[j], lora_w).astype(BF16), w2=_pad_rows(rwkv_w2[j], lora_w).astype(BF16),
                a0=row(rwkv_a0[j]),
                a1=_pad_cols(rwkv_a1[j], lora_w).astype(BF16), a2=_pad_rows(rwkv_a2[j], lora_w).astype(BF16),
                g1=_pad_cols(rwkv_g1[j], lora_g).astype(BF16), g2=_pad_rows(rwkv_g2[j], lora_g).astype(BF16),
                k_k=row(rwkv_k_k[j]), k_a=row(rwkv_k_a[j]), r_k=row(rwkv_r_k[j]),
                ln_w=row(rwkv_ln_w[j]), ln_b=row(rwkv_ln_b[j]), bd=_block_diag_ones(),
                w_out=rwkv_w_out[j].astype(BF16), g_mix=g_mix)
            r, lw, k, v, kk, b, g, h_last = rwkv_proj(x, state_rwkv_shift[j], p)
            y_p, y_s, s_p, s_s = rwkv_core(r, lw, k, v, kk, b, state_rwkv_wkv, j)
            x = rwkv_out(x, y_p, y_s, r, k, v, g, p)
            wkv_p.append(s_p)
            wkv_s.append(s_s)
            sh_p.append(h_last[SEQ // SUBLANES - 1:NP // SUBLANES:SEQ // SUBLANES])
            sh_s.append(h_last[NP // SUBLANES:])
        x = mlp(x, row(norm_mlp[layer]), mlp_w_up, mlp_w_down, layer,
                final_g=row(norm_final) if layer == DEPTH - 1 else None)
    y_p, y_s = x
    return (y_p.reshape(BATCH, SEQ, D_MODEL), y_s.reshape(DEC_BATCH, DEC_SEQ, D_MODEL),
            ret_states[0], ret_states[1], jnp.stack(k_p), jnp.stack(k_s), jnp.stack(v_p), jnp.stack(v_s),
            jnp.stack(wkv_p), jnp.stack(wkv_s), jnp.stack(sh_p), jnp.stack(sh_s))
```

```python
import functools
import math

import jax
import jax.numpy as jnp
from jax import lax
from jax.experimental import pallas as pl
from jax.experimental.pallas import tpu as pltpu

F32 = jnp.float32
BF16 = jnp.bfloat16

D_MODEL = 1024
BATCH = 8
SEQ = 2048
DEPTH = 4
DEC_BATCH = 128
DEC_SEQ = 8
PAST_LEN = 8192
LAYER_KINDS = tuple(i % 3 for i in range(DEPTH))
NORM_EPS = 1e-6
ROPE_THETA = 10000.0
NEG_INF = -1e30
D_FF = 4 * D_MODEL

RET_HEADS = 4
RET_DK = D_MODEL // RET_HEADS
RET_DV = 2 * D_MODEL // RET_HEADS
RET_CHUNK = 128
RET_PROMPT_CHUNK = 256
RET_NQ = RET_HEADS * RET_DK
RET_NV = RET_HEADS * RET_DV

SWA_HEADS = 16
SWA_KV_HEADS = 4
SWA_GROUP = SWA_HEADS // SWA_KV_HEADS
SWA_HD = D_MODEL // SWA_HEADS
WINDOW = 128
SWA_BLOCK = 128
SWA_NQ = SWA_HEADS * SWA_HD
SWA_NK = SWA_KV_HEADS * SWA_HD

RWKV_HD = 64
RWKV_HEADS = D_MODEL // RWKV_HD
RWKV_GN_EPS = 64e-5
RWKV_CHUNK = 32

NP = BATCH * SEQ
NS = DEC_BATCH * DEC_SEQ
NT = NP + NS

VMEM_LIMIT_V7X = 48 * 1024 * 1024
STREAMED_F32_VMEM_LIMIT_V7X = 56 * 1024 * 1024
LANES = 128
SUBLANES = 8
MXU_DIM_V7X = 256

ROW_TILE = 1024


def _cparams(*sem, vmem=VMEM_LIMIT_V7X):
    return pltpu.CompilerParams(dimension_semantics=sem, vmem_limit_bytes=vmem)


def _dot(a, b):
    return jnp.dot(a.astype(BF16), b.astype(BF16), preferred_element_type=F32)


def _dot_nt(a, b):
    return lax.dot_general(a.astype(BF16), b.astype(BF16), (((1,), (1,)), ((), ())),
                           preferred_element_type=F32)


def _dot_tn(a, b):
    return lax.dot_general(a.astype(BF16), b.astype(BF16), (((0,), (0,)), ((), ())),
                           preferred_element_type=F32)


def _div_pow2(x, n):
    assert n & (n - 1) == 0
    return jnp.right_shift(x, n.bit_length() - 1)


def _mod_pow2(x, n):
    assert n & (n - 1) == 0
    return jnp.bitwise_and(x, n - 1)


def _sigmoid(x):
    return 0.5 * jnp.tanh(0.5 * x) + 0.5


def _rms(x, g):
    ms = jnp.mean(x * x, axis=-1, keepdims=True)
    return x * lax.rsqrt(ms + NORM_EPS) * g


def _split3(x):
    hi = x.astype(BF16)
    r1 = x - hi.astype(F32)
    mid = r1.astype(BF16)
    lo = (r1 - mid.astype(F32)).astype(BF16)
    return hi, mid, lo


def _seg_sum(x, bd):
    xb = x.astype(BF16)
    return jnp.concatenate(
        [jnp.dot(xb[:, j * MXU_DIM_V7X:(j + 1) * MXU_DIM_V7X], bd, preferred_element_type=F32)
         for j in range(D_MODEL // MXU_DIM_V7X)], axis=-1)


def _rope_tables(pos, half):
    inv_freq = jnp.power(ROPE_THETA, -jnp.arange(half, dtype=F32) / half)
    ang = pos.astype(F32)[:, None] * inv_freq[None, :]
    return jnp.cos(ang), jnp.sin(ang)


def _group_specs(tile, block_of):
    n_prompt = NP // tile
    return (block_of(lambda i: jnp.minimum(i, n_prompt - 1)), block_of(lambda i: jnp.maximum(i - n_prompt, 0)))


def _pick_group(tile, p_val, s_val):
    return jnp.where(pl.program_id(0) < NP // tile, p_val, s_val)


def _token_rows(x, tile):
    if isinstance(x, tuple):
        per_seq = SEQ // tile
        f_p, f_s = _group_specs(tile, lambda f: f)
        return list(x), [pl.BlockSpec((None, tile, D_MODEL), lambda *g: (f_p(g[0]) // per_seq, f_p(g[0]) % per_seq, 0)),
                         pl.BlockSpec((tile, D_MODEL), lambda *g: (f_s(g[0]), 0))]
    return [x], [pl.BlockSpec((tile, D_MODEL), lambda *g: (g[0], 0))]


def _read_rows(refs, tile):
    return refs[0][...] if len(refs) == 1 else _pick_group(tile, refs[0][...], refs[1][...])


def _rope_table_inputs(half, widen, tile):
    tp = widen(*_rope_tables(jnp.arange(SEQ, dtype=jnp.int32), half))
    ts = [jnp.tile(t, (DEC_BATCH, 1))
          for t in widen(*_rope_tables(PAST_LEN + jnp.arange(DEC_SEQ, dtype=jnp.int32), half))]
    n_prompt = NP // tile
    per_seq = SEQ // tile
    p_spec = pl.BlockSpec((tile, LANES), lambda *g: (g[0] % per_seq, 0))
    s_spec = pl.BlockSpec((tile, LANES), lambda *g: (jnp.maximum(g[0] - n_prompt, 0), 0))
    return [tp[0], tp[1], ts[0], ts[1]], [p_spec, p_spec, s_spec, s_spec]


RET_PROJ_TILE = RET_NQ + RET_NQ


def _ret_proj_body(n_x, *refs):
    x_refs, (g_ref, w_ref, cosp_ref, sinp_ref, coss_ref, sins_ref, o_ref, xn_ref) = refs[:n_x], refs[n_x:]
    j = pl.program_id(1)

    @pl.when(j == 0)
    def _():
        xn_ref[...] = _rms(_read_rows(x_refs, ROW_TILE), g_ref[...]).astype(BF16)

    acc = jnp.dot(xn_ref[...], w_ref[...], preferred_element_type=F32)
    o_ref[...] = acc.astype(BF16)

    @pl.when(j == 0)
    def _():
        cos = _pick_group(ROW_TILE, cosp_ref[...], coss_ref[...])
        sin = _pick_group(ROW_TILE, sinp_ref[...], sins_ref[...])
        half = RET_DK // 2
        for blk in range(RET_PROJ_TILE // RET_DK):
            lo, mid, hi = blk * RET_DK, blk * RET_DK + half, (blk + 1) * RET_DK
            x1, x2 = acc[:, lo:mid], acc[:, mid:hi]
            scale = 1.0 if blk < RET_HEADS else RET_DK ** -0.5
            o_ref[:, lo:mid] = ((x1 * cos - x2 * sin) * scale).astype(BF16)
            o_ref[:, mid:hi] = ((x2 * cos + x1 * sin) * scale).astype(BF16)


def ret_proj(x, g, w):
    n_out = w.shape[1]
    x_arrays, x_specs = _token_rows(x, ROW_TILE)
    tables, table_specs = _rope_table_inputs(RET_DK // 2, lambda c, s: (c, s), ROW_TILE)
    return pl.pallas_call(
        functools.partial(_ret_proj_body, len(x_arrays)),
        grid=(NT // ROW_TILE, n_out // RET_PROJ_TILE),
        in_specs=x_specs + [
            pl.BlockSpec((1, D_MODEL), lambda i, j: (0, 0)),
            pl.BlockSpec((D_MODEL, RET_PROJ_TILE), lambda i, j: (0, j)),
        ] + table_specs,
        out_specs=pl.BlockSpec((ROW_TILE, RET_PROJ_TILE), lambda i, j: (i, j)),
        out_shape=jax.ShapeDtypeStruct((NT, n_out), BF16),
        scratch_shapes=[pltpu.VMEM((ROW_TILE, D_MODEL), BF16)],
        compiler_params=_cparams("parallel", "arbitrary"),
        name="ret_proj",
    )(*x_arrays, g, w, *tables)


SWA_PROJ_ROW_TILE = 512


def _swa_rope(x, cos, sin):
    n = x.shape[1]
    half = SWA_HD // 2
    lane = lax.broadcasted_iota(jnp.int32, x.shape, 1)
    partner = jnp.where(_mod_pow2(lane, SWA_HD) < half, pltpu.roll(x, n - half, 1), pltpu.roll(x, half, 1))
    reps = n // LANES
    return x * jnp.concatenate([cos] * reps, axis=-1) + partner * jnp.concatenate([sin] * reps, axis=-1)


def _swa_proj_body(x_ref, g_ref, w_ref, b_ref, cosp_ref, sinp_ref, coss_ref, sins_ref, o_ref, kr_ref):
    xn = _rms(x_ref[...], g_ref[...]).astype(BF16)
    acc = jnp.dot(xn, w_ref[...], preferred_element_type=F32) + b_ref[...]
    n_qk = SWA_NQ + SWA_NK
    cos = _pick_group(SWA_PROJ_ROW_TILE, cosp_ref[...], coss_ref[...])
    sin = _pick_group(SWA_PROJ_ROW_TILE, sinp_ref[...], sins_ref[...])
    qk = _swa_rope(acc[:, :n_qk], cos, sin)
    assert math.log2(SWA_HD ** 0.5).is_integer()
    o_ref[:, :SWA_NQ] = (qk[:, :SWA_NQ] * (SWA_HD ** -0.5)).astype(BF16)
    o_ref[:, SWA_NQ:n_qk] = qk[:, SWA_NQ:].astype(BF16)
    o_ref[:, n_qk:] = acc[:, n_qk:].astype(BF16)
    kr_ref[...] = qk[:, SWA_NQ:]


def swa_proj(x, g, w, b):
    n_out = w.shape[1]

    def widen(c, s):
        rep = LANES // SWA_HD
        return jnp.tile(jnp.concatenate([c, c], axis=-1), (1, rep)), jnp.tile(jnp.concatenate([-s, s], axis=-1), (1, rep))

    tile = SWA_PROJ_ROW_TILE
    tables, table_specs = _rope_table_inputs(SWA_HD // 2, widen, tile)
    return pl.pallas_call(
        _swa_proj_body,
        grid=(NT // tile,),
        in_specs=[
            pl.BlockSpec((tile, D_MODEL), lambda i: (i, 0)),
            pl.BlockSpec((1, D_MODEL), lambda i: (0, 0)),
            pl.BlockSpec((D_MODEL, n_out), lambda i: (0, 0)),
            pl.BlockSpec((1, n_out), lambda i: (0, 0)),
        ] + table_specs,
        out_specs=[pl.BlockSpec((tile, n_out), lambda i: (i, 0)), pl.BlockSpec((tile, SWA_NK), lambda i: (i, 0))],
        out_shape=[jax.ShapeDtypeStruct((NT, n_out), BF16), jax.ShapeDtypeStruct((NT, SWA_NK), F32)],
        compiler_params=_cparams("parallel"),
        name="swa_proj",
    )(x, g, w, b, *tables)


def _matmul_residual_body(n_x, *refs):
    x_refs, (ap_ref, as_ref, w_ref, o_ref) = refs[:n_x], refs[n_x:]
    a = _pick_group(ROW_TILE, ap_ref[...], as_ref[...])
    o_ref[...] = _read_rows(x_refs, ROW_TILE) + jnp.dot(a, w_ref[...], preferred_element_type=F32)


def matmul_residual(x, a_p, a_s, w):
    k = a_p.shape[1]
    x_arrays, x_specs = _token_rows(x, ROW_TILE)
    return pl.pallas_call(
        functools.partial(_matmul_residual_body, len(x_arrays)),
        grid=(NT // ROW_TILE,),
        in_specs=x_specs + [
            *_group_specs(ROW_TILE, lambda f: pl.BlockSpec((ROW_TILE, k), lambda i: (f(i), 0))),
            pl.BlockSpec((k, D_MODEL), lambda i: (0, 0)),
        ],
        out_specs=pl.BlockSpec((ROW_TILE, D_MODEL), lambda i: (i, 0)),
        out_shape=jax.ShapeDtypeStruct((NT, D_MODEL), F32),
        compiler_params=_cparams("parallel"),
        name="matmul_residual",
    )(*x_arrays, a_p, a_s, w)


MLP_FF_TILE = 1024


def _mlp_body(has_final_norm, x_ref, g_ref, wu_ref, wd_ref, *rest):
    rest = list(rest)
    gf_ref = rest.pop(0) if has_final_norm else None
    outs, (xn_ref, acc_ref) = rest[:-2], rest[-2:]
    f = pl.program_id(1)

    @pl.when(f == 0)
    def _():
        xn_ref[...] = _rms(x_ref[...], g_ref[...]).astype(BF16)
        acc_ref[...] = jnp.zeros_like(acc_ref)

    a = jnp.maximum(jnp.dot(xn_ref[...], wu_ref[...].astype(BF16), preferred_element_type=F32), 0.0)
    acc_ref[...] += jnp.dot((a * a).astype(BF16), wd_ref[...].astype(BF16), preferred_element_type=F32)

    @pl.when(f == pl.num_programs(1) - 1)
    def _():
        out = x_ref[...] + acc_ref[...]
        if not has_final_norm:
            outs[0][...] = out
        else:
            yp_ref, ys_ref = outs
            y = _rms(out, gf_ref[...])
            is_prompt = pl.program_id(0) < NP // ROW_TILE

            @pl.when(is_prompt)
            def _():
                yp_ref[...] = y

            @pl.when(jnp.logical_not(is_prompt))
            def _():
                ys_ref[...] = y


def mlp(x, g, w_up, w_down, layer, final_g=None):
    row = pl.BlockSpec((ROW_TILE, D_MODEL), lambda i, f: (i, 0))
    vec = pl.BlockSpec((1, D_MODEL), lambda i, f: (0, 0))
    if final_g is None:
        extra_in, extra_specs = [], []
        out_specs = row
        out_shape = jax.ShapeDtypeStruct((NT, D_MODEL), F32)
    else:
        extra_in, extra_specs = [final_g], [vec]
        out_specs = list(_group_specs(ROW_TILE, lambda g_: pl.BlockSpec((ROW_TILE, D_MODEL), lambda i, f: (g_(i), 0))))
        out_shape = [jax.ShapeDtypeStruct((NP, D_MODEL), F32), jax.ShapeDtypeStruct((NS, D_MODEL), F32)]
    return pl.pallas_call(
        functools.partial(_mlp_body, final_g is not None),
        grid=(NT // ROW_TILE, D_FF // MLP_FF_TILE),
        in_specs=[
            row, vec,
            pl.BlockSpec((None, D_MODEL, MLP_FF_TILE), lambda i, f: (layer, 0, f)),
            pl.BlockSpec((None, MLP_FF_TILE, D_MODEL), lambda i, f: (layer, f, 0)),
        ] + extra_specs,
        out_specs=out_specs,
        out_shape=out_shape,
        scratch_shapes=[pltpu.VMEM((ROW_TILE, D_MODEL), BF16), pltpu.VMEM((ROW_TILE, D_MODEL), F32)],
        compiler_params=_cparams("parallel" if final_g is None else "arbitrary", "arbitrary", vmem=STREAMED_F32_VMEM_LIMIT_V7X),
        name="mlp",
    )(x, g, w_up, w_down, *extra_in)


def _ret_tables(c):
    log_gamma = jnp.log1p(-jnp.exp2(-5.0 - jnp.arange(RET_HEADS, dtype=F32)))
    idx = jnp.arange(c, dtype=F32)
    diff = idx[:, None] - idx[None, :]
    inner = jnp.where(diff >= 0, jnp.exp(log_gamma[:, None, None] * jnp.maximum(diff, 0.0)), 0.0)
    q_decay = jnp.exp(log_gamma[:, None] * (idx[None, :] + 1.0))
    k_decay = jnp.exp(log_gamma[:, None] * (c - 1.0 - idx[None, :]))
    chunk_decay = jnp.exp(log_gamma * c)
    rep = lambda t: jnp.broadcast_to(t[:, :, None], (RET_HEADS, c, LANES))
    return inner, rep(q_decay), rep(k_decay), chunk_decay


def _ret_heads(rows, lo, width):
    return jnp.stack([rows[:, lo + h * width:lo + (h + 1) * width] for h in range(RET_HEADS)])


def _ret_chunk(rows_list, s0, inner, qd, kd, cd_ref):
    n_seq = len(rows_list)
    cat = lambda f: jnp.concatenate([f(r) for r in rows_list], axis=0)
    q = cat(lambda r: _ret_heads(r, 0, RET_DK)).astype(BF16)
    k = cat(lambda r: _ret_heads(r, RET_NQ, RET_DK)).astype(F32)
    v = cat(lambda r: _ret_heads(r, 2 * RET_NQ, RET_DV)).astype(BF16)
    g = cat(lambda r: _ret_heads(r, 2 * RET_NQ + RET_NV, RET_DV)).astype(F32)
    per_seq = lambda t: jnp.concatenate([t] * n_seq, axis=0)
    inner, qd, kd = per_seq(inner), per_seq(qd), per_seq(kd)
    cd = per_seq(jnp.stack([jnp.full((1, 1), cd_ref[h], F32) for h in range(RET_HEADS)]))
    qd = jnp.concatenate([qd] * (RET_DV // LANES), axis=-1)
    kd = jnp.concatenate([kd] * (RET_DK // LANES), axis=-1)
    scores = jnp.einsum('bqd,bkd->bqk', q, k.astype(BF16), preferred_element_type=F32) * inner
    o = (jnp.einsum('bqk,bke->bqe', scores.astype(BF16), v, preferred_element_type=F32)
         + jnp.einsum('bqd,bde->bqe', q, s0.astype(BF16), preferred_element_type=F32) * qd)
    s_new = s0 * cd + jnp.einsum('bkd,bke->bde', (k * kd).astype(BF16), v, preferred_element_type=F32)
    o = o * lax.rsqrt(jnp.mean(o * o, axis=-1, keepdims=True) + NORM_EPS)
    return o * (g * _sigmoid(g)), s_new


RET_SAMPLE_SEQS = 2


def _retention_body(cdp_ref, cds_ref, projp_ref, innerp_ref, qdp_ref, kdp_ref,
                    projs_ref, inners_ref, qds_ref, kds_ref, s0_ref, *rest):
    op_ref, sp_ref, os_ref, ss_ref, s_scr = rest[-5:]
    c = pl.program_id(0) % (SEQ // RET_PROMPT_CHUNK)

    @pl.when(c == 0)
    def _():
        s_scr[...] = jnp.zeros_like(s_scr)

    o, s_new = _ret_chunk([projp_ref[...]], s_scr[...], innerp_ref[...], qdp_ref[...], kdp_ref[...], cdp_ref)
    s_scr[...] = s_new
    op_ref[...] = jnp.concatenate([o[h] for h in range(RET_HEADS)], axis=-1).astype(BF16)

    @pl.when(c == SEQ // RET_PROMPT_CHUNK - 1)
    def _():
        sp_ref[0, 0] = s_scr[...]

    rows = projs_ref[...].astype(F32)
    batch = RET_SAMPLE_SEQS * RET_HEADS
    o, s_new = _ret_chunk([rows[seq * DEC_SEQ:(seq + 1) * DEC_SEQ] for seq in range(RET_SAMPLE_SEQS)],
                          s0_ref[0].reshape(batch, RET_DK, RET_DV), inners_ref[...], qds_ref[...], kds_ref[...],
                          cds_ref)
    ss_ref[0] = s_new.reshape(RET_SAMPLE_SEQS, RET_HEADS, RET_DK, RET_DV)
    os_ref[...] = jnp.concatenate(
        [jnp.concatenate([o[seq * RET_HEADS + h] for h in range(RET_HEADS)], axis=-1)
         for seq in range(RET_SAMPLE_SEQS)], axis=0).astype(BF16)


def retention_core(proj, state_ret, j, stacks):
    n_in = proj.shape[1]
    nc = SEQ // RET_PROMPT_CHUNK
    n_steps = BATCH * nc
    assert n_steps == DEC_BATCH // RET_SAMPLE_SEQS
    assert math.gcd(DEC_SEQ, RET_CHUNK) == DEC_SEQ
    smem = pl.BlockSpec(memory_space=pltpu.SMEM)
    full3 = lambda t: pl.BlockSpec(t.shape, lambda i: (0, 0, 0))
    inner_p, qd_p, kd_p, cd_p = _ret_tables(RET_PROMPT_CHUNK)
    inner_s, qd_s, kd_s, cd_s = _ret_tables(DEC_SEQ)
    rows = RET_SAMPLE_SEQS * DEC_SEQ
    row0 = NP // rows
    st = pl.BlockSpec((1, RET_SAMPLE_SEQS, RET_HEADS, RET_DK, RET_DV), lambda i: (j, i, 0, 0, 0))
    stacked = [] if stacks is None else list(stacks)
    o_p, s_p, o_s, s_s = pl.pallas_call(
        _retention_body,
        grid=(n_steps,),
        in_specs=[
            smem, smem,
            pl.BlockSpec((RET_PROMPT_CHUNK, n_in), lambda i: (i, 0)),
            full3(inner_p), full3(qd_p), full3(kd_p),
            pl.BlockSpec((rows, n_in), lambda i: (row0 + i, 0)),
            full3(inner_s), full3(qd_s), full3(kd_s),
            st,
        ] + [pl.BlockSpec(memory_space=pl.ANY)] * len(stacked),
        out_specs=[
            pl.BlockSpec((RET_PROMPT_CHUNK, RET_NV), lambda i: (i, 0)),
            pl.BlockSpec((1, 1, RET_HEADS, RET_DK, RET_DV), lambda i: (j, i // nc, 0, 0, 0)),
            pl.BlockSpec((rows, RET_NV), lambda i: (i, 0)),
            st,
        ],
        out_shape=[jax.ShapeDtypeStruct((NP, RET_NV), BF16),
                   jax.ShapeDtypeStruct((state_ret.shape[0], BATCH, RET_HEADS, RET_DK, RET_DV), F32),
                   jax.ShapeDtypeStruct((NS, RET_NV), BF16),
                   jax.ShapeDtypeStruct(state_ret.shape, F32)],
        input_output_aliases={11: 1, 12: 3} if stacked else {},
        scratch_shapes=[pltpu.VMEM((RET_HEADS, RET_DK, RET_DV), F32)],
        compiler_params=_cparams("arbitrary"),
        name="retention",
    )(cd_p, cd_s, proj, inner_p, qd_p, kd_p, proj, inner_s, qd_s, kd_s, state_ret, *stacked)
    return o_p, o_s, s_p, s_s


def _sink_attention(q, k, v, valid_t, sink_ref):
    t = q.shape[0]
    head_cols = lambda x, h: x[:, h * SWA_HD:(h + 1) * SWA_HD]
    qs = jnp.stack([jnp.concatenate([head_cols(q, kh * SWA_GROUP + g) for g in range(SWA_GROUP)], axis=0)
                    for kh in range(SWA_KV_HEADS)]).astype(BF16)
    ks = jnp.stack([head_cols(k, kh) for kh in range(SWA_KV_HEADS)]).astype(BF16)
    vs = jnp.stack([head_cols(v, kh) for kh in range(SWA_KV_HEADS)]).astype(BF16)
    sink = jnp.stack([jnp.concatenate([jnp.full((1, t), sink_ref[kh * SWA_GROUP + g], F32)
                                       for g in range(SWA_GROUP)], axis=1) for kh in range(SWA_KV_HEADS)])
    s = jnp.einsum('hkd,hqd->hkq', ks, qs, preferred_element_type=F32)
    s = jnp.where(valid_t[None], s, NEG_INF)
    m = jnp.maximum(jnp.max(s, axis=1, keepdims=True), sink)
    p = jnp.exp(s - m)
    denom = jnp.sum(p, axis=1, keepdims=True) + jnp.exp(sink - m)
    o_t = jnp.einsum('hkd,hkq->hdq', vs, p.astype(BF16), preferred_element_type=F32) / denom
    o = [o_t[kh].T for kh in range(SWA_KV_HEADS)]
    return jnp.concatenate([o[h // SWA_GROUP][(h % SWA_GROUP) * t:(h % SWA_GROUP + 1) * t]
                            for h in range(SWA_HEADS)], axis=-1)


def _swa_prompt_body(sink_ref, q_ref, kc_ref, vc_ref, kp_ref, vp_ref, o_ref):
    blk = pl.program_id(1)
    q = q_ref[...]
    k_all = jnp.concatenate([kp_ref[...], kc_ref[...]], axis=0)
    v_all = jnp.concatenate([vp_ref[...], vc_ref[...]], axis=0)
    shape = (2 * SWA_BLOCK, SWA_GROUP * SWA_BLOCK)
    i = _mod_pow2(lax.broadcasted_iota(jnp.int32, shape, 1), SWA_BLOCK)
    j = lax.broadcasted_iota(jnp.int32, shape, 0)
    rel = i + SWA_BLOCK - j
    valid = (rel >= 0) & (rel <= WINDOW) & (j >= jnp.where(blk > 0, 0, SWA_BLOCK))
    o_ref[...] = _sink_attention(q, k_all, v_all, valid, sink_ref).astype(BF16)


SWA_SAMPLE_SEQS = 8


def _swa_sample_body(sink_ref, q_ref, kn_ref, vn_ref, kc_ref, vc_ref, o_ref, ko_ref, vo_ref):
    rows = SWA_SAMPLE_SEQS * DEC_SEQ
    n_cache = SWA_SAMPLE_SEQS * WINDOW
    q = q_ref[...]
    kn = kn_ref[...]
    vn = vn_ref[...].astype(F32)
    k_all = jnp.concatenate([kc_ref[...].reshape(n_cache, SWA_NK), kn], axis=0)
    v_all = jnp.concatenate([vc_ref[...].reshape(n_cache, SWA_NK), vn], axis=0)
    for b in range(SWA_SAMPLE_SEQS):
        new = slice(b * DEC_SEQ, (b + 1) * DEC_SEQ)
        ko_ref[b, :WINDOW - DEC_SEQ, :] = kc_ref[b, DEC_SEQ:, :]
        ko_ref[b, WINDOW - DEC_SEQ:, :] = kn[new]
        vo_ref[b, :WINDOW - DEC_SEQ, :] = vc_ref[b, DEC_SEQ:, :]
        vo_ref[b, WINDOW - DEC_SEQ:, :] = vn[new]
    shape = (n_cache + rows, SWA_GROUP * rows)
    r = _mod_pow2(lax.broadcasted_iota(jnp.int32, shape, 1), rows)
    c = lax.broadcasted_iota(jnp.int32, shape, 0)
    q_seq, q_t = _div_pow2(r, DEC_SEQ), _mod_pow2(r, DEC_SEQ)
    is_new = c >= n_cache
    k_seq = jnp.where(is_new, _div_pow2(c - n_cache, DEC_SEQ), _div_pow2(c, WINDOW))
    k_slot = jnp.where(is_new, WINDOW + _mod_pow2(c - n_cache, DEC_SEQ), _mod_pow2(c, WINDOW))
    rel = q_t + WINDOW - k_slot
    valid = (q_seq == k_seq) & (rel >= 0) & (rel <= WINDOW)
    o_ref[...] = _sink_attention(q, k_all, v_all, valid, sink_ref).astype(BF16)


def swa_core(proj, k_rot, sinks, cache_k, cache_v):
    nb = SEQ // SWA_BLOCK
    smem = pl.BlockSpec(memory_space=pltpu.SMEM)
    kcol = SWA_NQ // SWA_NK
    cur = lambda b, i: b * nb + i
    prev = lambda b, i: b * nb + jnp.maximum(i - 1, 0)
    o = pl.pallas_call(
        _swa_prompt_body,
        grid=(BATCH, nb),
        in_specs=[
            smem,
            pl.BlockSpec((SWA_BLOCK, SWA_NQ), lambda b, i: (cur(b, i), 0)),
            pl.BlockSpec((SWA_BLOCK, SWA_NK), lambda b, i: (cur(b, i), kcol)),
            pl.BlockSpec((SWA_BLOCK, SWA_NK), lambda b, i: (cur(b, i), kcol + 1)),
            pl.BlockSpec((SWA_BLOCK, SWA_NK), lambda b, i: (prev(b, i), kcol)),
            pl.BlockSpec((SWA_BLOCK, SWA_NK), lambda b, i: (prev(b, i), kcol + 1)),
        ],
        out_specs=pl.BlockSpec((SWA_BLOCK, SWA_NQ), lambda b, i: (cur(b, i), 0)),
        out_shape=jax.ShapeDtypeStruct((NP, SWA_NQ), BF16),
        compiler_params=_cparams("parallel", "arbitrary"),
        name="swa_prompt",
    )(sinks, proj, proj, proj, proj, proj)

    rows = SWA_SAMPLE_SEQS * DEC_SEQ
    row0 = NP // rows
    o_s, k_s, v_s = pl.pallas_call(
        _swa_sample_body,
        grid=(DEC_BATCH // SWA_SAMPLE_SEQS,),
        in_specs=[
            smem,
            pl.BlockSpec((rows, SWA_NQ), lambda g: (row0 + g, 0)),
            pl.BlockSpec((rows, SWA_NK), lambda g: (row0 + g, 0)),
            pl.BlockSpec((rows, SWA_NK), lambda g: (row0 + g, kcol + 1)),
            pl.BlockSpec((SWA_SAMPLE_SEQS, WINDOW, SWA_NK), lambda g: (g, 0, 0)),
            pl.BlockSpec((SWA_SAMPLE_SEQS, WINDOW, SWA_NK), lambda g: (g, 0, 0)),
        ],
        out_specs=[
            pl.BlockSpec((rows, SWA_NQ), lambda g: (g, 0)),
            pl.BlockSpec((SWA_SAMPLE_SEQS, WINDOW, SWA_NK), lambda g: (g, 0, 0)),
            pl.BlockSpec((SWA_SAMPLE_SEQS, WINDOW, SWA_NK), lambda g: (g, 0, 0)),
        ],
        out_shape=[jax.ShapeDtypeStruct((NS, SWA_NQ), BF16),
                   jax.ShapeDtypeStruct((DEC_BATCH, WINDOW, SWA_NK), F32),
                   jax.ShapeDtypeStruct((DEC_BATCH, WINDOW, SWA_NK), F32)],
        compiler_params=_cparams("parallel"),
        name="swa_sample",
    )(sinks, proj, k_rot, proj, cache_k, cache_v)
    return o, o_s, k_s, v_s


RWKV_ROW_TILE = 256
RWKV_PAIRS = RWKV_HEADS // 2
RWKV_STACK = MXU_DIM_V7X
RWKV_SAMPLE_SEQS = RWKV_STACK // (RWKV_PAIRS * DEC_SEQ)


def _block_diag_ones():
    r = jnp.arange(MXU_DIM_V7X) // RWKV_HD
    return (r[:, None] == r[None, :]).astype(BF16)


def _store_pairs(ref, val):
    for p in range(RWKV_PAIRS):
        ref[p] = val[:, p * LANES:(p + 1) * LANES].astype(ref.dtype)


def _load_pairs(ref):
    return jnp.concatenate([ref[p] for p in range(RWKV_PAIRS)], axis=-1)


def _rwkv_proj_body(x_ref, xprev_ref, first_ref, gn_ref, mu_ref, wr_ref, wk_ref, wv_ref, w0_ref, w1_ref, w2_ref,
                    a0_ref, a1_ref, a2_ref, g1_ref, g2_ref, kk_ref, ka_ref, bd_ref,
                    r_out, lw_out, k_out, v_out, kk_out, b_out, g_out, hlast_out, h_scr):
    i = pl.program_id(0)
    h = _rms(x_ref[...], gn_ref[...])
    prev_tile_last = _rms(xprev_ref[...], gn_ref[...])[SUBLANES - 1:SUBLANES, :]
    rowid = lax.broadcasted_iota(jnp.int32, h.shape, 0)
    xp = jnp.where(rowid == 0, prev_tile_last, pltpu.roll(h, 1, 0))
    seq_len_mask = jnp.where(i >= NP // RWKV_ROW_TILE, DEC_SEQ - 1, SEQ - 1)
    is_first = jnp.bitwise_and(i * RWKV_ROW_TILE + rowid, seq_len_mask) == 0
    xp = jnp.where(is_first, first_ref[...], xp)
    _store_pairs(h_scr, h)
    every8th = pl.ds(SUBLANES - 1, RWKV_ROW_TILE // SUBLANES, stride=SUBLANES)
    hlast_out[...] = jnp.concatenate([h_scr[p, every8th, :] for p in range(RWKV_PAIRS)], axis=-1)
    d = xp - h
    xs = [h + d * mu_ref[i:i + 1, :] for i in range(6)]
    _store_pairs(r_out, _dot(xs[0], wr_ref[...]))
    k = _dot(xs[1], wk_ref[...])
    _store_pairs(v_out, _dot(xs[2], wv_ref[...]))
    z = w0_ref[...] + _dot(jnp.tanh(_dot(xs[3], w1_ref[...])), w2_ref[...])
    _store_pairs(lw_out, -math.exp(-0.5) * _sigmoid(z))
    a = _sigmoid(a0_ref[...] + _dot(_dot(xs[4], a1_ref[...]), a2_ref[...]))
    g_out[...] = _dot(_sigmoid(_dot(xs[5], g1_ref[...])), g2_ref[...]).astype(BF16)
    kk = k * kk_ref[...]
    kk = kk * jnp.minimum(lax.rsqrt(_seg_sum(kk * kk, bd_ref[...])), 1e12)
    _store_pairs(kk_out, kk)
    _store_pairs(b_out, kk * a)
    _store_pairs(k_out, k * (1.0 + (a - 1.0) * ka_ref[...]))


def rwkv_proj(x, shift_s, p):
    first = jnp.concatenate([
        jnp.zeros((RWKV_ROW_TILE, D_MODEL), F32),
        jnp.pad(shift_s[:, None, :], ((0, 0), (0, DEC_SEQ - 1), (0, 0))).reshape(NS, D_MODEL)], axis=0)
    prompt_tiles = NP // RWKV_ROW_TILE
    row = pl.BlockSpec((RWKV_ROW_TILE, D_MODEL), lambda i: (i, 0))
    prev = pl.BlockSpec((SUBLANES, D_MODEL), lambda i: (jnp.maximum(i * (RWKV_ROW_TILE // SUBLANES) - 1, 0), 0))
    first_spec = pl.BlockSpec((RWKV_ROW_TILE, D_MODEL), lambda i: (jnp.maximum(i - prompt_tiles + 1, 0), 0))
    pair = pl.BlockSpec((RWKV_PAIRS, RWKV_ROW_TILE, LANES), lambda i: (0, i, 0))
    full = lambda a: pl.BlockSpec(a.shape, lambda i: (0,) * a.ndim)
    consts = [p["g_mix"], p["mu"], p["wr"], p["wk"], p["wv"], p["w0"], p["w1"], p["w2"], p["a0"], p["a1"],
              p["a2"], p["g1"], p["g2"], p["k_k"], p["k_a"], p["bd"]]
    pair_shape = lambda dt: jax.ShapeDtypeStruct((RWKV_PAIRS, NT, LANES), dt)
    return pl.pallas_call(
        _rwkv_proj_body,
        grid=(NT // RWKV_ROW_TILE,),
        in_specs=[row, prev, first_spec] + [full(c) for c in consts],
        out_specs=[pair] * 6 + [row, pl.BlockSpec((RWKV_ROW_TILE // SUBLANES, D_MODEL), lambda i: (i, 0))],
        out_shape=[pair_shape(BF16), pair_shape(F32)] + [pair_shape(BF16)] * 4
        + [jax.ShapeDtypeStruct((NT, D_MODEL), BF16), jax.ShapeDtypeStruct((NT // SUBLANES, D_MODEL), F32)],
        scratch_shapes=[pltpu.VMEM((RWKV_PAIRS, RWKV_ROW_TILE, LANES), F32)],
        compiler_params=_cparams("parallel"),
        name="rwkv_proj",
    )(x, x, first, *consts)


def _rwkv_chunks(c_len, problems):
    n_grp = RWKV_STACK // c_len
    shift = int(math.log2(c_len))
    assert 2 ** shift == c_len
    row = lax.broadcasted_iota(jnp.int32, (RWKV_STACK, RWKV_STACK), 0)
    col = lax.broadcasted_iota(jnp.int32, (RWKV_STACK, RWKV_STACK), 1)
    same = jnp.right_shift(row, shift) == jnp.right_shift(col, shift)
    strict_t = same & (row < col)
    incl_t = same & (row <= col)
    tri_t = jnp.where(incl_t, 1.0, 0.0).astype(BF16)
    each = lambda f, *cols: [f(*xs) for xs in zip(*cols)]
    r, lw, k, v, kk, b, state, store = zip(*problems)
    lam = each(lambda x: sum(jnp.dot(part, tri_t, preferred_element_type=F32) for part in _split3(x.T)).T, lw)
    lam3 = each(lambda x: x.reshape(n_grp, c_len, LANES), lam)
    lam_end3 = each(lambda x: x[:, c_len - 1:c_len, :], lam3)
    e_end = each(lambda e, x: jnp.exp(e - x).reshape(RWKV_STACK, LANES), lam_end3, lam3)
    e_tot3 = each(jnp.exp, lam_end3)
    e_neg = each(lambda x: jnp.exp(-x), lam)
    at = each(lambda kk_, l, w: -kk_ * jnp.exp(l - w), kk, lam, lw)
    rt = each(lambda r_, l: r_ * jnp.exp(l), r, lam)
    mul = lambda x, y: x * y
    bh, kh, bb, kb = each(mul, b, e_neg), each(mul, k, e_neg), each(mul, b, e_end), each(mul, k, e_end)

    lane = lax.broadcasted_iota(jnp.int32, (RWKV_STACK, LANES), 1)
    halves = (lane < RWKV_HD, lane >= RWKV_HD)
    rhs = each(lambda x, y: jnp.concatenate([x, y], axis=0), bh, kh)
    at_pow, bt_up, rt_up = [], [], []
    for sel in halves:
        lhs = each(lambda x, y: jnp.concatenate([jnp.where(sel, x, 0.0), jnp.where(sel, y, 0.0)], axis=0), at, rt)
        mt = each(_dot_nt, rhs, lhs)
        at_pow.append(each(lambda x: jnp.where(strict_t, x[:RWKV_STACK, :RWKV_STACK], 0.0), mt))
        bt_up.append(each(lambda x: jnp.where(strict_t, x[RWKV_STACK:, :RWKV_STACK], 0.0), mt))
        rt_up.append(each(lambda x: jnp.concatenate([jnp.where(incl_t, x[:RWKV_STACK, RWKV_STACK:], 0.0),
                                                     jnp.where(incl_t, x[RWKV_STACK:, RWKV_STACK:], 0.0)], axis=0), mt))

    rows = lambda t, g: t[g * c_len:(g + 1) * c_len]
    states = each(lambda st: [st(g) for g in range(n_grp)], state)
    p = [each(lambda x, y, st: _dot_nt(jnp.concatenate([rows(x, g), rows(y, g)], axis=0), st[g]), at, rt, states)
         for g in range(n_grp)]
    n_prob = len(problems)
    a_s_t = [jnp.concatenate([p[g][i][:c_len] for g in range(n_grp)], axis=0).T for i in range(n_prob)]
    r_s_t = [jnp.concatenate([p[g][i][c_len:] for g in range(n_grp)], axis=0).T for i in range(n_prob)]
    v_t = each(lambda x: x.T, v)
    half_rows = lambda x, e: x[e * RWKV_HD:(e + 1) * RWKV_HD]
    n_half = len(halves)
    u_t = [each(lambda a_, v_, b_: half_rows(a_, e) + _dot(half_rows(v_, e), b_), a_s_t, v_t, bt_up[e])
           for e in range(n_half)]
    assert shift >= 2
    for it in range(shift - 1):
        u_t = [each(lambda u_, a_: u_ + _dot(u_, a_), u_t[e], at_pow[e]) for e in range(n_half)]
        if it + 2 < shift:
            at_pow = [each(lambda a_: _dot(a_, a_), at_pow[e]) for e in range(n_half)]
    z_t = [each(lambda u_, a_: _dot(u_, a_), u_t[e], at_pow[e]) for e in range(n_half)]
    u_t = [each(lambda u_, z_, a_: u_ + _dot(z_, a_), u_t[e], z_t[e], at_pow[e]) for e in range(n_half)]
    y_t = [each(lambda r_, u_, v_, m_: half_rows(r_, e) + _dot(jnp.concatenate([u_, half_rows(v_, e)], axis=1), m_),
                r_s_t, u_t[e], v_t, rt_up[e]) for e in range(n_half)]
    stack_halves = lambda parts: [jnp.concatenate([parts[e][i] for e in range(n_half)], axis=0).T
                                  for i in range(n_prob)]
    u = stack_halves(u_t)
    y = stack_halves(y_t)

    r128 = lax.broadcasted_iota(jnp.int32, (LANES, LANES), 0)
    c128 = lax.broadcasted_iota(jnp.int32, (LANES, LANES), 1)
    diag = (r128 < RWKV_HD) == (c128 < RWKV_HD)
    for g in range(n_grp):
        upd = each(lambda u_, v_, bb_, kb_: _dot_tn(jnp.concatenate([rows(u_, g), rows(v_, g)], axis=0),
                                                    jnp.concatenate([rows(bb_, g), rows(kb_, g)], axis=0)),
                   u, v, bb, kb)
        for i in range(n_prob):
            store[i](g, states[i][g] * e_tot3[i][g] + jnp.where(diag, upd[i], 0.0))
    return y


RWKV_PROMPT_SEQS = 4
RWKV_STEP_TOKENS = 128


def _rwkv_prompt_body(*refs):
    ins, (y_ref, s_ref, s_scr) = refs[:6 * RWKV_PROMPT_SEQS], refs[6 * RWKV_PROMPT_SEQS:]
    c = pl.program_id(1)

    @pl.when(c == 0)
    def _():
        s_scr[...] = jnp.zeros_like(s_scr)

    def one_chunk(ci, carry):
        rows = pl.ds(pl.multiple_of(ci * RWKV_CHUNK, RWKV_CHUNK), RWKV_CHUNK)
        def problem(seq):
            def store(g, s):
                s_scr[seq, g] = s

            args = [ins[a * RWKV_PROMPT_SEQS + seq][:, rows, :].reshape(RWKV_STACK, LANES).astype(F32)
                    for a in range(6)]
            return (*args, lambda g: s_scr[seq, g], store)

        ys = _rwkv_chunks(RWKV_CHUNK, [problem(seq) for seq in range(RWKV_PROMPT_SEQS)])
        for seq, y in enumerate(ys):
            y_ref[:, seq, rows, :] = y.reshape(RWKV_PAIRS, RWKV_CHUNK, LANES)
        return carry

    lax.fori_loop(0, RWKV_STEP_TOKENS // RWKV_CHUNK, one_chunk, 0)

    @pl.when(c == pl.num_programs(1) - 1)
    def _():
        for seq in range(RWKV_PROMPT_SEQS):
            for p in range(RWKV_PAIRS):
                s_ref[seq, 2 * p] = s_scr[seq, p, :RWKV_HD, :RWKV_HD]
                s_ref[seq, 2 * p + 1] = s_scr[seq, p, RWKV_HD:, RWKV_HD:]


RWKV_SAMPLE_PROBLEMS = 2


def _rwkv_sample_body(r_ref, lw_ref, k_ref, v_ref, kk_ref, b_ref, s0_ref, y_ref, s_ref):
    zero = jnp.zeros((RWKV_HD, RWKV_HD), F32)
    rows_per = RWKV_SAMPLE_SEQS * DEC_SEQ

    def problem(q):
        def state(g):
            p, seq = divmod(g, RWKV_SAMPLE_SEQS)
            seq += q * RWKV_SAMPLE_SEQS
            return jnp.concatenate([jnp.concatenate([s0_ref[seq, 2 * p], zero], axis=1),
                                    jnp.concatenate([zero, s0_ref[seq, 2 * p + 1]], axis=1)], axis=0)

        def store(g, s):
            p, seq = divmod(g, RWKV_SAMPLE_SEQS)
            seq += q * RWKV_SAMPLE_SEQS
            s_ref[seq, 2 * p] = s[:RWKV_HD, :RWKV_HD]
            s_ref[seq, 2 * p + 1] = s[RWKV_HD:, RWKV_HD:]

        args = [ref[:, q * rows_per:(q + 1) * rows_per, :].reshape(RWKV_STACK, LANES).astype(F32)
                for ref in (r_ref, lw_ref, k_ref, v_ref, kk_ref, b_ref)]
        return (*args, state, store)

    ys = _rwkv_chunks(DEC_SEQ, [problem(q) for q in range(RWKV_SAMPLE_PROBLEMS)])
    for q, y in enumerate(ys):
        y_ref[:, q * rows_per:(q + 1) * rows_per, :] = y.reshape(RWKV_PAIRS, rows_per, LANES)


def rwkv_core(r, lw, k, v, kk, b, state_s, j):
    assert RWKV_PAIRS * RWKV_CHUNK == RWKV_STACK
    nc = SEQ // RWKV_STEP_TOKENS
    st_shape = (RWKV_HEADS, RWKV_HD, RWKV_HD)
    seq_blk = lambda seq: pl.BlockSpec((RWKV_PAIRS, RWKV_STEP_TOKENS, LANES),
                                       lambda bi, c: (0, (bi * RWKV_PROMPT_SEQS + seq) * nc + c, 0))
    inputs = (r, lw, k, v, kk, b)
    y_p, s_p = pl.pallas_call(
        _rwkv_prompt_body,
        grid=(BATCH // RWKV_PROMPT_SEQS, nc),
        in_specs=[seq_blk(seq) for _ in inputs for seq in range(RWKV_PROMPT_SEQS)],
        out_specs=[pl.BlockSpec((RWKV_PAIRS, RWKV_PROMPT_SEQS, RWKV_STEP_TOKENS, LANES), lambda bi, c: (0, bi, c, 0)),
                   pl.BlockSpec((RWKV_PROMPT_SEQS,) + st_shape, lambda bi, c: (bi, 0, 0, 0))],
        out_shape=[jax.ShapeDtypeStruct((RWKV_PAIRS, BATCH, SEQ, LANES), F32),
                   jax.ShapeDtypeStruct((BATCH,) + st_shape, F32)],
        scratch_shapes=[pltpu.VMEM((RWKV_PROMPT_SEQS, RWKV_PAIRS, LANES, LANES), F32)],
        compiler_params=_cparams("parallel", "arbitrary"),
        name="rwkv_prompt",
    )(*[a for a in inputs for _ in range(RWKV_PROMPT_SEQS)])

    step_seqs = RWKV_SAMPLE_PROBLEMS * RWKV_SAMPLE_SEQS
    rows = step_seqs * DEC_SEQ
    row0 = NP // rows
    blk = pl.BlockSpec((RWKV_PAIRS, rows, LANES), lambda i: (0, row0 + i, 0))
    st = pl.BlockSpec((step_seqs,) + st_shape, lambda i: (i, 0, 0, 0))
    st_in = pl.BlockSpec((None, step_seqs) + st_shape, lambda i: (j, i, 0, 0, 0))
    y_s, s_s = pl.pallas_call(
        _rwkv_sample_body,
        grid=(DEC_BATCH // step_seqs,),
        in_specs=[blk] * 6 + [st_in],
        out_specs=[pl.BlockSpec((RWKV_PAIRS, rows, LANES), lambda i: (0, i, 0)), st],
        out_shape=[jax.ShapeDtypeStruct((RWKV_PAIRS, NS, LANES), F32),
                   jax.ShapeDtypeStruct((DEC_BATCH,) + st_shape, F32)],
        compiler_params=_cparams("parallel"),
        name="rwkv_sample",
    )(*inputs, state_s)
    return y_p.reshape(RWKV_PAIRS, NP, LANES), y_s, s_p, s_s


def _rwkv_out_body(x_ref, yp_ref, ys_ref, r_ref, k_ref, v_ref, g_ref, rk_ref, lnw_ref, lnb_ref, bd_ref, wo_ref,
                   o_ref):
    bd = bd_ref[...]
    y = _pick_group(RWKV_ROW_TILE, _load_pairs(yp_ref), _load_pairs(ys_ref))
    yc = y - _seg_sum(y, bd) * (1.0 / RWKV_HD)
    var = _seg_sum(yc * yc, bd) * (1.0 / RWKV_HD)
    yn = yc * lax.rsqrt(var + RWKV_GN_EPS) * lnw_ref[...] + lnb_ref[...]
    rk = _load_pairs(r_ref).astype(F32) * _load_pairs(k_ref).astype(F32)
    bonus = _seg_sum(rk * rk_ref[...], bd) * _load_pairs(v_ref).astype(F32)
    z = (yn + bonus) * g_ref[...].astype(F32)
    o_ref[...] = x_ref[...] + _dot(z, wo_ref[...])


def rwkv_out(x, y_p, y_s, r, k, v, g, p):
    row = pl.BlockSpec((RWKV_ROW_TILE, D_MODEL), lambda i: (i, 0))
    pair_of = lambda f: pl.BlockSpec((RWKV_PAIRS, RWKV_ROW_TILE, LANES), lambda i: (0, f(i), 0))
    full = lambda a: pl.BlockSpec(a.shape, lambda i: (0,) * a.ndim)
    consts = [p["r_k"], p["ln_w"], p["ln_b"], p["bd"], p["w_out"]]
    return pl.pallas_call(
        _rwkv_out_body,
        grid=(NT // RWKV_ROW_TILE,),
        in_specs=[row, *_group_specs(RWKV_ROW_TILE, pair_of)] + [pair_of(lambda i: i)] * 3 + [row]
        + [full(c) for c in consts],
        out_specs=row,
        out_shape=jax.ShapeDtypeStruct((NT, D_MODEL), F32),
        compiler_params=_cparams("parallel"),
        name="rwkv_out",
    )(x, y_p, y_s, r, k, v, g, *consts)


def _pad_cols(w, n):
    return jnp.pad(w, ((0, 0), (0, n - w.shape[1])))


def _pad_rows(w, n):
    return jnp.pad(w, ((0, n - w.shape[0]), (0, 0)))


def kernel(x_prompt, x_sample, state_ret, cache_swa_k, cache_swa_v, state_rwkv_wkv, state_rwkv_shift,
           norm_mix, norm_mlp, norm_final, ret_w_in, ret_w_out,
           swa_w_in, swa_b_in, swa_sinks, swa_w_out,
           rwkv_mu, rwkv_w_rkv, rwkv_w0, rwkv_w1, rwkv_w2, rwkv_a0, rwkv_a1, rwkv_a2,
           rwkv_g1, rwkv_g2, rwkv_k_k, rwkv_k_a, rwkv_r_k, rwkv_ln_w, rwkv_ln_b, rwkv_w_out,
           mlp_w_up, mlp_w_down):
    x = (x_prompt, x_sample.reshape(NS, D_MODEL))
    if LAYER_KINDS[0] != 0:
        x = jnp.concatenate([x_prompt.reshape(NP, D_MODEL), x[1]], axis=0)
    row = lambda t: t.reshape(1, -1)
    k_p, k_s, v_p, v_s, wkv_p, wkv_s, sh_p, sh_s = ([] for _ in range(8))
    ret_states = None
    for layer in range(DEPTH):
        kind = LAYER_KINDS[layer]
        j = LAYER_KINDS[:layer].count(kind)
        g_mix = row(norm_mix[layer])
        if kind == 0:
            proj = ret_proj(x, g_mix, ret_w_in[j].astype(BF16))
            o_p, o_s, *ret_states = retention_core(proj, state_ret, j, ret_states)
            x = matmul_residual(x, o_p, o_s, ret_w_out[j].astype(BF16))
        elif kind == 1:
            proj, k_rot = swa_proj(x, g_mix, swa_w_in[j].astype(BF16), row(swa_b_in[j]))
            o_p, o_s, kc, vc = swa_core(proj, k_rot, swa_sinks[j],
                                        cache_swa_k[j].reshape(DEC_BATCH, WINDOW, SWA_NK),
                                        cache_swa_v[j].reshape(DEC_BATCH, WINDOW, SWA_NK))
            x = matmul_residual(x, o_p, o_s, swa_w_out[j].astype(BF16))
            kv_shape = (-1, WINDOW, SWA_KV_HEADS, SWA_HD)
            last = slice(SEQ // WINDOW - 1, NP // WINDOW, SEQ // WINDOW)
            k_p.append(k_rot.reshape(NT // WINDOW, WINDOW, SWA_NK)[last].reshape(kv_shape))
            v_p.append(proj.reshape(NT // WINDOW, WINDOW, -1)[last, :, SWA_NQ + SWA_NK:]
                       .astype(F32).reshape(kv_shape))
            k_s.append(kc.reshape(kv_shape))
            v_s.append(vc.reshape(kv_shape))
        else:
            lora_w = LANES
            lora_g = 2 * LANES
            p = dict(
                mu=rwkv_mu[j], wr=rwkv_w_rkv[j, 0].astype(BF16), wk=rwkv_w_rkv[j, 1].astype(BF16),
                wv=rwkv_w_rkv[j, 2].astype(BF16), w0=row(rwkv_w0[j]),
                w1=_pad_cols(rwkv_w1[j], lora_w).astype(BF16), w2=_pad_rows(rwkv_w2[j], lora_w).astype(BF16),
                a0=row(rwkv_a0[j]),
                a1=_pad_cols(rwkv_a1[j], lora_w).astype(BF16), a2=_pad_rows(rwkv_a2[j], lora_w).astype(BF16),
                g1=_pad_cols(rwkv_g1[j], lora_g).astype(BF16), g2=_pad_rows(rwkv_g2[j], lora_g).astype(BF16),
                k_k=row(rwkv_k_k[j]), k_a=row(rwkv_k_a[j]), r_k=row(rwkv_r_k[j]),
                ln_w=row(rwkv_ln_w[j]), ln_b=row(rwkv_ln_b[j]), bd=_block_diag_ones(),
                w_out=rwkv_w_out[j].astype(BF16), g_mix=g_mix)
            r, lw, k, v, kk, b, g, h_last = rwkv_proj(x, state_rwkv_shift[j], p)
            y_p, y_s, s_p, s_s = rwkv_core(r, lw, k, v, kk, b, state_rwkv_wkv, j)
            x = rwkv_out(x, y_p, y_s, r, k, v, g, p)
            wkv_p.append(s_p)
            wkv_s.append(s_s)
            sh_p.append(h_last[SEQ // SUBLANES - 1:NP // SUBLANES:SEQ // SUBLANES])
            sh_s.append(h_last[NP // SUBLANES:])
        x = mlp(x, row(norm_mlp[layer]), mlp_w_up, mlp_w_down, layer,
                final_g=row(norm_final) if layer == DEPTH - 1 else None)
    y_p, y_s = x
    return (y_p.reshape(BATCH, SEQ, D_MODEL), y_s.reshape(DEC_BATCH, DEC_SEQ, D_MODEL),
            ret_states[0], ret_states[1], jnp.stack(k_p), jnp.stack(k_s), jnp.stack(v_p), jnp.stack(v_s),
            jnp.stack(wkv_p), jnp.stack(wkv_s), jnp.stack(sh_p), jnp.stack(sh_s))
```

```python
import functools
import math

import jax
import jax.numpy as jnp
from jax import lax
from jax.experimental import pallas as pl
from jax.experimental.pallas import tpu as pltpu

F32 = jnp.float32
BF16 = jnp.bfloat16

D_MODEL = 1024
BATCH = 8
SEQ = 2048
DEPTH = 4
DEC_BATCH = 128
DEC_SEQ = 8
PAST_LEN = 8192
LAYER_KINDS = tuple(i % 3 for i in range(DEPTH))
NORM_EPS = 1e-6
ROPE_THETA = 10000.0
NEG_INF = -1e30
D_FF = 4 * D_MODEL

RET_HEADS = 4
RET_DK = D_MODEL // RET_HEADS
RET_DV = 2 * D_MODEL // RET_HEADS
RET_CHUNK = 128
RET_PROMPT_CHUNK = 256
RET_NQ = RET_HEADS * RET_DK
RET_NV = RET_HEADS * RET_DV

SWA_HEADS = 16
SWA_KV_HEADS = 4
SWA_GROUP = SWA_HEADS // SWA_KV_HEADS
SWA_HD = D_MODEL // SWA_HEADS
WINDOW = 128
SWA_BLOCK = 128
SWA_NQ = SWA_HEADS * SWA_HD
SWA_NK = SWA_KV_HEADS * SWA_HD

RWKV_HD = 64
RWKV_HEADS = D_MODEL // RWKV_HD
RWKV_GN_EPS = 64e-5
RWKV_CHUNK = 32

NP = BATCH * SEQ
NS = DEC_BATCH * DEC_SEQ
NT = NP + NS

VMEM_LIMIT_V7X = 48 * 1024 * 1024
STREAMED_F32_VMEM_LIMIT_V7X = 56 * 1024 * 1024
LANES = 128
SUBLANES = 8
MXU_DIM_V7X = 256

ROW_TILE = 1024


def _cparams(*sem, vmem=VMEM_LIMIT_V7X):
    return pltpu.CompilerParams(dimension_semantics=sem, vmem_limit_bytes=vmem)


def _dot(a, b):
    return jnp.dot(a.astype(BF16), b.astype(BF16), preferred_element_type=F32)


def _dot_nt(a, b):
    return lax.dot_general(a.astype(BF16), b.astype(BF16), (((1,), (1,)), ((), ())),
                           preferred_element_type=F32)


def _dot_tn(a, b):
    return lax.dot_general(a.astype(BF16), b.astype(BF16), (((0,), (0,)), ((), ())),
                           preferred_element_type=F32)


def _div_pow2(x, n):
    assert n & (n - 1) == 0
    return jnp.right_shift(x, n.bit_length() - 1)


def _mod_pow2(x, n):
    assert n & (n - 1) == 0
    return jnp.bitwise_and(x, n - 1)


def _sigmoid(x):
    return 0.5 * jnp.tanh(0.5 * x) + 0.5


def _rms(x, g):
    ms = jnp.mean(x * x, axis=-1, keepdims=True)
    return x * lax.rsqrt(ms + NORM_EPS) * g


def _split3(x):
    hi = x.astype(BF16)
    r1 = x - hi.astype(F32)
    mid = r1.astype(BF16)
    lo = (r1 - mid.astype(F32)).astype(BF16)
    return hi, mid, lo


def _seg_sum(x, bd):
    xb = x.astype(BF16)
    return jnp.concatenate(
        [jnp.dot(xb[:, j * MXU_DIM_V7X:(j + 1) * MXU_DIM_V7X], bd, preferred_element_type=F32)
         for j in range(D_MODEL // MXU_DIM_V7X)], axis=-1)


def _rope_tables(pos, half):
    inv_freq = jnp.power(ROPE_THETA, -jnp.arange(half, dtype=F32) / half)
    ang = pos.astype(F32)[:, None] * inv_freq[None, :]
    return jnp.cos(ang), jnp.sin(ang)


def _group_specs(tile, block_of):
    n_prompt = NP // tile
    return (block_of(lambda i: jnp.minimum(i, n_prompt - 1)), block_of(lambda i: jnp.maximum(i - n_prompt, 0)))


def _pick_group(tile, p_val, s_val):
    return jnp.where(pl.program_id(0) < NP // tile, p_val, s_val)


def _token_rows(x, tile):
    if isinstance(x, tuple):
        per_seq = SEQ // tile
        f_p, f_s = _group_specs(tile, lambda f: f)
        return list(x), [pl.BlockSpec((None, tile, D_MODEL), lambda *g: (f_p(g[0]) // per_seq, f_p(g[0]) % per_seq, 0)),
                         pl.BlockSpec((tile, D_MODEL), lambda *g: (f_s(g[0]), 0))]
    return [x], [pl.BlockSpec((tile, D_MODEL), lambda *g: (g[0], 0))]


def _read_rows(refs, tile):
    return refs[0][...] if len(refs) == 1 else _pick_group(tile, refs[0][...], refs[1][...])


def _rope_table_inputs(half, widen, tile):
    tp = widen(*_rope_tables(jnp.arange(SEQ, dtype=jnp.int32), half))
    ts = [jnp.tile(t, (DEC_BATCH, 1))
          for t in widen(*_rope_tables(PAST_LEN + jnp.arange(DEC_SEQ, dtype=jnp.int32), half))]
    n_prompt = NP // tile
    per_seq = SEQ // tile
    p_spec = pl.BlockSpec((tile, LANES), lambda *g: (g[0] % per_seq, 0))
    s_spec = pl.BlockSpec((tile, LANES), lambda *g: (jnp.maximum(g[0] - n_prompt, 0), 0))
    return [tp[0], tp[1], ts[0], ts[1]], [p_spec, p_spec, s_spec, s_spec]


RET_PROJ_TILE = RET_NQ + RET_NQ


def _ret_proj_body(n_x, *refs):
    x_refs, (g_ref, w_ref, o_ref, xn_ref) = refs[:n_x], refs[n_x:]

    @pl.when(pl.program_id(1) == 0)
    def _():
        xn_ref[...] = _rms(_read_rows(x_refs, ROW_TILE), g_ref[...]).astype(BF16)

    o_ref[...] = jnp.dot(xn_ref[...], w_ref[...], preferred_element_type=F32).astype(BF16)


def ret_proj(x, g, w):
    n_out = w.shape[1]
    x_arrays, x_specs = _token_rows(x, ROW_TILE)
    return pl.pallas_call(
        functools.partial(_ret_proj_body, len(x_arrays)),
        grid=(NT // ROW_TILE, n_out // RET_PROJ_TILE),
        in_specs=x_specs + [
            pl.BlockSpec((1, D_MODEL), lambda i, j: (0, 0)),
            pl.BlockSpec((D_MODEL, RET_PROJ_TILE), lambda i, j: (0, j)),
        ],
        out_specs=pl.BlockSpec((ROW_TILE, RET_PROJ_TILE), lambda i, j: (i, j)),
        out_shape=jax.ShapeDtypeStruct((NT, n_out), BF16),
        scratch_shapes=[pltpu.VMEM((ROW_TILE, D_MODEL), BF16)],
        compiler_params=_cparams("parallel", "arbitrary"),
        name="ret_proj",
    )(*x_arrays, g, w)


SWA_PROJ_ROW_TILE = 512


def _swa_rope(x, cos, sin):
    n = x.shape[1]
    half = SWA_HD // 2
    lane = lax.broadcasted_iota(jnp.int32, x.shape, 1)
    partner = jnp.where(_mod_pow2(lane, SWA_HD) < half, pltpu.roll(x, n - half, 1), pltpu.roll(x, half, 1))
    reps = n // LANES
    return x * jnp.concatenate([cos] * reps, axis=-1) + partner * jnp.concatenate([sin] * reps, axis=-1)


def _swa_proj_body(x_ref, g_ref, w_ref, b_ref, cosp_ref, sinp_ref, coss_ref, sins_ref, o_ref, kr_ref):
    xn = _rms(x_ref[...], g_ref[...]).astype(BF16)
    acc = jnp.dot(xn, w_ref[...], preferred_element_type=F32) + b_ref[...]
    n_qk = SWA_NQ + SWA_NK
    cos = _pick_group(SWA_PROJ_ROW_TILE, cosp_ref[...], coss_ref[...])
    sin = _pick_group(SWA_PROJ_ROW_TILE, sinp_ref[...], sins_ref[...])
    qk = _swa_rope(acc[:, :n_qk], cos, sin)
    assert math.log2(SWA_HD ** 0.5).is_integer()
    o_ref[:, :SWA_NQ] = (qk[:, :SWA_NQ] * (SWA_HD ** -0.5)).astype(BF16)
    o_ref[:, SWA_NQ:n_qk] = qk[:, SWA_NQ:].astype(BF16)
    o_ref[:, n_qk:] = acc[:, n_qk:].astype(BF16)
    kr_ref[...] = qk[:, SWA_NQ:]


def swa_proj(x, g, w, b):
    n_out = w.shape[1]

    def widen(c, s):
        rep = LANES // SWA_HD
        return jnp.tile(jnp.concatenate([c, c], axis=-1), (1, rep)), jnp.tile(jnp.concatenate([-s, s], axis=-1), (1, rep))

    tile = SWA_PROJ_ROW_TILE
    tables, table_specs = _rope_table_inputs(SWA_HD // 2, widen, tile)
    return pl.pallas_call(
        _swa_proj_body,
        grid=(NT // tile,),
        in_specs=[
            pl.BlockSpec((tile, D_MODEL), lambda i: (i, 0)),
            pl.BlockSpec((1, D_MODEL), lambda i: (0, 0)),
            pl.BlockSpec((D_MODEL, n_out), lambda i: (0, 0)),
            pl.BlockSpec((1, n_out), lambda i: (0, 0)),
        ] + table_specs,
        out_specs=[pl.BlockSpec((tile, n_out), lambda i: (i, 0)), pl.BlockSpec((tile, SWA_NK), lambda i: (i, 0))],
        out_shape=[jax.ShapeDtypeStruct((NT, n_out), BF16), jax.ShapeDtypeStruct((NT, SWA_NK), F32)],
        compiler_params=_cparams("parallel"),
        name="swa_proj",
    )(x, g, w, b, *tables)


def _matmul_residual_body(n_x, *refs):
    x_refs, (ap_ref, as_ref, w_ref, o_ref) = refs[:n_x], refs[n_x:]
    a = _pick_group(ROW_TILE, ap_ref[...], as_ref[...])
    o_ref[...] = _read_rows(x_refs, ROW_TILE) + jnp.dot(a, w_ref[...], preferred_element_type=F32)


def matmul_residual(x, a_p, a_s, w):
    k = a_p.shape[1]
    x_arrays, x_specs = _token_rows(x, ROW_TILE)
    return pl.pallas_call(
        functools.partial(_matmul_residual_body, len(x_arrays)),
        grid=(NT // ROW_TILE,),
        in_specs=x_specs + [
            *_group_specs(ROW_TILE, lambda f: pl.BlockSpec((ROW_TILE, k), lambda i: (f(i), 0))),
            pl.BlockSpec((k, D_MODEL), lambda i: (0, 0)),
        ],
        out_specs=pl.BlockSpec((ROW_TILE, D_MODEL), lambda i: (i, 0)),
        out_shape=jax.ShapeDtypeStruct((NT, D_MODEL), F32),
        compiler_params=_cparams("parallel"),
        name="matmul_residual",
    )(*x_arrays, a_p, a_s, w)


MLP_FF_TILE = 1024


def _mlp_body(has_final_norm, x_ref, g_ref, wu_ref, wd_ref, *rest):
    if has_final_norm:
        gf_ref, yp_ref, ys_ref, xn_ref, acc_ref = rest
    else:
        acc_ref, xn_ref = rest
    f = pl.program_id(1)

    @pl.when(f == 0)
    def _():
        x = x_ref[...]
        xn_ref[...] = _rms(x, g_ref[...]).astype(BF16)
        acc_ref[...] = x

    a = jnp.maximum(jnp.dot(xn_ref[...], wu_ref[...].astype(BF16), preferred_element_type=F32), 0.0)
    acc_ref[...] += jnp.dot((a * a).astype(BF16), wd_ref[...].astype(BF16), preferred_element_type=F32)

    if has_final_norm:
        @pl.when(f == pl.num_programs(1) - 1)
        def _():
            y = _rms(acc_ref[...], gf_ref[...])
            is_prompt = pl.program_id(0) < NP // ROW_TILE

            @pl.when(is_prompt)
            def _():
                yp_ref[...] = y

            @pl.when(jnp.logical_not(is_prompt))
            def _():
                ys_ref[...] = y


def mlp(x, g, w_up, w_down, layer, final_g=None):
    row = pl.BlockSpec((ROW_TILE, D_MODEL), lambda i, f: (i, 0))
    vec = pl.BlockSpec((1, D_MODEL), lambda i, f: (0, 0))
    if final_g is None:
        extra_in, extra_specs = [], []
        out_specs = row
        out_shape = jax.ShapeDtypeStruct((NT, D_MODEL), F32)
    else:
        extra_in, extra_specs = [final_g], [vec]
        out_specs = list(_group_specs(ROW_TILE, lambda g_: pl.BlockSpec((ROW_TILE, D_MODEL), lambda i, f: (g_(i), 0))))
        out_shape = [jax.ShapeDtypeStruct((NP, D_MODEL), F32), jax.ShapeDtypeStruct((NS, D_MODEL), F32)]
    return pl.pallas_call(
        functools.partial(_mlp_body, final_g is not None),
        grid=(NT // ROW_TILE, D_FF // MLP_FF_TILE),
        in_specs=[
            row, vec,
            pl.BlockSpec((None, D_MODEL, MLP_FF_TILE), lambda i, f: (layer, 0, f)),
            pl.BlockSpec((None, MLP_FF_TILE, D_MODEL), lambda i, f: (layer, f, 0)),
        ] + extra_specs,
        out_specs=out_specs,
        out_shape=out_shape,
        scratch_shapes=[pltpu.VMEM((ROW_TILE, D_MODEL), BF16)]
        + ([] if final_g is None else [pltpu.VMEM((ROW_TILE, D_MODEL), F32)]),
        compiler_params=_cparams("parallel" if final_g is None else "arbitrary", "arbitrary", vmem=STREAMED_F32_VMEM_LIMIT_V7X),
        name="mlp",
    )(x, g, w_up, w_down, *extra_in)


def _ret_tables(c):
    log_gamma = jnp.log1p(-jnp.exp2(-5.0 - jnp.arange(RET_HEADS, dtype=F32)))
    idx = jnp.arange(c, dtype=F32)
    diff = idx[:, None] - idx[None, :]
    inner = jnp.where(diff >= 0, jnp.exp(log_gamma[:, None, None] * jnp.maximum(diff, 0.0)), 0.0)
    q_decay = jnp.exp(log_gamma[:, None] * (idx[None, :] + 1.0))
    k_decay = jnp.exp(log_gamma[:, None] * (c - 1.0 - idx[None, :]))
    chunk_decay = jnp.exp(log_gamma * c)
    rep = lambda t: jnp.broadcast_to(t[:, :, None], (RET_HEADS, c, LANES))
    return inner, rep(q_decay), rep(k_decay), chunk_decay


def _ret_heads(rows, lo, width):
    return jnp.stack([rows[:, lo + h * width:lo + (h + 1) * width] for h in range(RET_HEADS)])


def _ret_chunk(rows_list, s0, inner, qd, kd, cd_ref, cos, sin):
    n_seq = len(rows_list)
    cat = lambda f: jnp.concatenate([f(r) for r in rows_list], axis=0)

    def rope(x):
        x1, x2 = x[..., :RET_DK // 2], x[..., RET_DK // 2:]
        return jnp.concatenate([x1 * cos - x2 * sin, x2 * cos + x1 * sin], axis=-1)

    q = rope(cat(lambda r: _ret_heads(r, 0, RET_DK)).astype(F32)).astype(BF16)
    k = rope(cat(lambda r: _ret_heads(r, RET_NQ, RET_DK)).astype(F32)) * (RET_DK ** -0.5)
    v = cat(lambda r: _ret_heads(r, 2 * RET_NQ, RET_DV)).astype(BF16)
    g = cat(lambda r: _ret_heads(r, 2 * RET_NQ + RET_NV, RET_DV)).astype(F32)
    per_seq = lambda t: jnp.concatenate([t] * n_seq, axis=0)
    inner, qd, kd = per_seq(inner), per_seq(qd), per_seq(kd)
    cd = per_seq(jnp.stack([jnp.full((1, 1), cd_ref[h], F32) for h in range(RET_HEADS)]))
    qd = jnp.concatenate([qd] * (RET_DV // LANES), axis=-1)
    kd = jnp.concatenate([kd] * (RET_DK // LANES), axis=-1)
    scores = jnp.einsum('bqd,bkd->bqk', q, k.astype(BF16), preferred_element_type=F32) * inner
    o = (jnp.einsum('bqk,bke->bqe', scores.astype(BF16), v, preferred_element_type=F32)
         + jnp.einsum('bqd,bde->bqe', q, s0.astype(BF16), preferred_element_type=F32) * qd)
    s_new = s0 * cd + jnp.einsum('bkd,bke->bde', (k * kd).astype(BF16), v, preferred_element_type=F32)
    o = o * lax.rsqrt(jnp.mean(o * o, axis=-1, keepdims=True) + NORM_EPS)
    return o * (g * _sigmoid(g)), s_new


RET_SAMPLE_SEQS = 2


def _retention_body(cdp_ref, cds_ref, projp_ref, innerp_ref, qdp_ref, kdp_ref, cosp_ref, sinp_ref,
                    projs_ref, inners_ref, qds_ref, kds_ref, coss_ref, sins_ref, s0_ref, *rest):
    op_ref, sp_ref, os_ref, ss_ref, s_scr = rest[-5:]
    c = pl.program_id(0) % (SEQ // RET_PROMPT_CHUNK)

    @pl.when(c == 0)
    def _():
        s_scr[...] = jnp.zeros_like(s_scr)

    o, s_new = _ret_chunk([projp_ref[...]], s_scr[...], innerp_ref[...], qdp_ref[...], kdp_ref[...], cdp_ref,
                          cosp_ref[...], sinp_ref[...])
    s_scr[...] = s_new
    op_ref[...] = jnp.concatenate([o[h] for h in range(RET_HEADS)], axis=-1).astype(BF16)

    @pl.when(c == SEQ // RET_PROMPT_CHUNK - 1)
    def _():
        sp_ref[0, 0] = s_scr[...]

    rows = projs_ref[...].astype(F32)
    batch = RET_SAMPLE_SEQS * RET_HEADS
    o, s_new = _ret_chunk([rows[seq * DEC_SEQ:(seq + 1) * DEC_SEQ] for seq in range(RET_SAMPLE_SEQS)],
                          s0_ref[0].reshape(batch, RET_DK, RET_DV), inners_ref[...], qds_ref[...], kds_ref[...],
                          cds_ref, coss_ref[...], sins_ref[...])
    ss_ref[0] = s_new.reshape(RET_SAMPLE_SEQS, RET_HEADS, RET_DK, RET_DV)
    os_ref[...] = jnp.concatenate(
        [jnp.concatenate([o[seq * RET_HEADS + h] for h in range(RET_HEADS)], axis=-1)
         for seq in range(RET_SAMPLE_SEQS)], axis=0).astype(BF16)


def retention_core(proj, state_ret, j, stacks):
    n_in = proj.shape[1]
    nc = SEQ // RET_PROMPT_CHUNK
    n_steps = BATCH * nc
    assert n_steps == DEC_BATCH // RET_SAMPLE_SEQS
    assert math.gcd(DEC_SEQ, RET_CHUNK) == DEC_SEQ
    smem = pl.BlockSpec(memory_space=pltpu.SMEM)
    full3 = lambda t: pl.BlockSpec(t.shape, lambda i: (0, 0, 0))
    inner_p, qd_p, kd_p, cd_p = _ret_tables(RET_PROMPT_CHUNK)
    inner_s, qd_s, kd_s, cd_s = _ret_tables(DEC_SEQ)
    cos_p, sin_p = _rope_tables(jnp.arange(SEQ, dtype=jnp.int32), RET_DK // 2)
    cos_s, sin_s = _rope_tables(PAST_LEN + jnp.arange(DEC_SEQ, dtype=jnp.int32), RET_DK // 2)
    rows = RET_SAMPLE_SEQS * DEC_SEQ
    row0 = NP // rows
    st = pl.BlockSpec((1, RET_SAMPLE_SEQS, RET_HEADS, RET_DK, RET_DV), lambda i: (j, i, 0, 0, 0))
    stacked = [] if stacks is None else list(stacks)
    o_p, s_p, o_s, s_s = pl.pallas_call(
        _retention_body,
        grid=(n_steps,),
        in_specs=[
            smem, smem,
            pl.BlockSpec((RET_PROMPT_CHUNK, n_in), lambda i: (i, 0)),
            full3(inner_p), full3(qd_p), full3(kd_p),
            pl.BlockSpec((RET_PROMPT_CHUNK, LANES), lambda i: (i % nc, 0)),
            pl.BlockSpec((RET_PROMPT_CHUNK, LANES), lambda i: (i % nc, 0)),
            pl.BlockSpec((rows, n_in), lambda i: (row0 + i, 0)),
            full3(inner_s), full3(qd_s), full3(kd_s),
            pl.BlockSpec((DEC_SEQ, LANES), lambda i: (0, 0)),
            pl.BlockSpec((DEC_SEQ, LANES), lambda i: (0, 0)),
            st,
        ] + [pl.BlockSpec(memory_space=pl.ANY)] * len(stacked),
        out_specs=[
            pl.BlockSpec((RET_PROMPT_CHUNK, RET_NV), lambda i: (i, 0)),
            pl.BlockSpec((1, 1, RET_HEADS, RET_DK, RET_DV), lambda i: (j, i // nc, 0, 0, 0)),
            pl.BlockSpec((rows, RET_NV), lambda i: (i, 0)),
            st,
        ],
        out_shape=[jax.ShapeDtypeStruct((NP, RET_NV), BF16),
                   jax.ShapeDtypeStruct((state_ret.shape[0], BATCH, RET_HEADS, RET_DK, RET_DV), F32),
                   jax.ShapeDtypeStruct((NS, RET_NV), BF16),
                   jax.ShapeDtypeStruct(state_ret.shape, F32)],
        input_output_aliases={15: 1, 16: 3} if stacked else {},
        scratch_shapes=[pltpu.VMEM((RET_HEADS, RET_DK, RET_DV), F32)],
        compiler_params=_cparams("arbitrary"),
        name="retention",
    )(cd_p, cd_s, proj, inner_p, qd_p, kd_p, cos_p, sin_p, proj, inner_s, qd_s, kd_s, cos_s, sin_s, state_ret,
      *stacked)
    return o_p, o_s, s_p, s_s


def _sink_attention(q, k, v, valid_t, sink_ref):
    t = q.shape[0]
    head_cols = lambda x, h: x[:, h * SWA_HD:(h + 1) * SWA_HD]
    qs = jnp.stack([jnp.concatenate([head_cols(q, kh * SWA_GROUP + g) for g in range(SWA_GROUP)], axis=0)
                    for kh in range(SWA_KV_HEADS)]).astype(BF16)
    ks = jnp.stack([head_cols(k, kh) for kh in range(SWA_KV_HEADS)]).astype(BF16)
    vs = jnp.stack([head_cols(v, kh) for kh in range(SWA_KV_HEADS)]).astype(BF16)
    sink = jnp.stack([jnp.concatenate([jnp.full((1, t), sink_ref[kh * SWA_GROUP + g], F32)
                                       for g in range(SWA_GROUP)], axis=1) for kh in range(SWA_KV_HEADS)])
    s = jnp.einsum('hkd,hqd->hkq', ks, qs, preferred_element_type=F32)
    s = jnp.where(valid_t[None], s, NEG_INF)
    m = jnp.maximum(jnp.max(s, axis=1, keepdims=True), sink)
    p = jnp.exp(s - m)
    denom = jnp.sum(p, axis=1, keepdims=True) + jnp.exp(sink - m)
    o_t = jnp.einsum('hkd,hkq->hdq', vs, p.astype(BF16), preferred_element_type=F32) / denom
    o = [o_t[kh].T for kh in range(SWA_KV_HEADS)]
    return jnp.concatenate([o[h // SWA_GROUP][(h % SWA_GROUP) * t:(h % SWA_GROUP + 1) * t]
                            for h in range(SWA_HEADS)], axis=-1)


def _swa_prompt_body(sink_ref, q_ref, kc_ref, vc_ref, kp_ref, vp_ref, o_ref):
    blk = pl.program_id(1)
    q = q_ref[...]
    k_all = jnp.concatenate([kp_ref[...], kc_ref[...]], axis=0)
    v_all = jnp.concatenate([vp_ref[...], vc_ref[...]], axis=0)
    shape = (2 * SWA_BLOCK, SWA_GROUP * SWA_BLOCK)
    i = _mod_pow2(lax.broadcasted_iota(jnp.int32, shape, 1), SWA_BLOCK)
    j = lax.broadcasted_iota(jnp.int32, shape, 0)
    rel = i + SWA_BLOCK - j
    valid = (rel >= 0) & (rel <= WINDOW) & (j >= jnp.where(blk > 0, 0, SWA_BLOCK))
    o_ref[...] = _sink_attention(q, k_all, v_all, valid, sink_ref).astype(BF16)


SWA_SAMPLE_SEQS = 8


def _swa_sample_body(sink_ref, q_ref, kn_ref, vn_ref, kc_ref, vc_ref, o_ref, ko_ref, vo_ref):
    rows = SWA_SAMPLE_SEQS * DEC_SEQ
    n_cache = SWA_SAMPLE_SEQS * WINDOW
    q = q_ref[...]
    kn = kn_ref[...]
    vn = vn_ref[...].astype(F32)
    k_all = jnp.concatenate([kc_ref[...].reshape(n_cache, SWA_NK), kn], axis=0)
    v_all = jnp.concatenate([vc_ref[...].reshape(n_cache, SWA_NK), vn], axis=0)
    for b in range(SWA_SAMPLE_SEQS):
        new = slice(b * DEC_SEQ, (b + 1) * DEC_SEQ)
        ko_ref[b, :WINDOW - DEC_SEQ, :] = kc_ref[b, DEC_SEQ:, :]
        ko_ref[b, WINDOW - DEC_SEQ:, :] = kn[new]
        vo_ref[b, :WINDOW - DEC_SEQ, :] = vc_ref[b, DEC_SEQ:, :]
        vo_ref[b, WINDOW - DEC_SEQ:, :] = vn[new]
    shape = (n_cache + rows, SWA_GROUP * rows)
    r = _mod_pow2(lax.broadcasted_iota(jnp.int32, shape, 1), rows)
    c = lax.broadcasted_iota(jnp.int32, shape, 0)
    q_seq, q_t = _div_pow2(r, DEC_SEQ), _mod_pow2(r, DEC_SEQ)
    is_new = c >= n_cache
    k_seq = jnp.where(is_new, _div_pow2(c - n_cache, DEC_SEQ), _div_pow2(c, WINDOW))
    k_slot = jnp.where(is_new, WINDOW + _mod_pow2(c - n_cache, DEC_SEQ), _mod_pow2(c, WINDOW))
    rel = q_t + WINDOW - k_slot
    valid = (q_seq == k_seq) & (rel >= 0) & (rel <= WINDOW)
    o_ref[...] = _sink_attention(q, k_all, v_all, valid, sink_ref).astype(BF16)


def swa_core(proj, k_rot, sinks, cache_k, cache_v):
    nb = SEQ // SWA_BLOCK
    smem = pl.BlockSpec(memory_space=pltpu.SMEM)
    kcol = SWA_NQ // SWA_NK
    cur = lambda b, i: b * nb + i
    prev = lambda b, i: b * nb + jnp.maximum(i - 1, 0)
    o = pl.pallas_call(
        _swa_prompt_body,
        grid=(BATCH, nb),
        in_specs=[
            smem,
            pl.BlockSpec((SWA_BLOCK, SWA_NQ), lambda b, i: (cur(b, i), 0)),
            pl.BlockSpec((SWA_BLOCK, SWA_NK), lambda b, i: (cur(b, i), kcol)),
            pl.BlockSpec((SWA_BLOCK, SWA_NK), lambda b, i: (cur(b, i), kcol + 1)),
            pl.BlockSpec((SWA_BLOCK, SWA_NK), lambda b, i: (prev(b, i), kcol)),
            pl.BlockSpec((SWA_BLOCK, SWA_NK), lambda b, i: (prev(b, i), kcol + 1)),
        ],
        out_specs=pl.BlockSpec((SWA_BLOCK, SWA_NQ), lambda b, i: (cur(b, i), 0)),
        out_shape=jax.ShapeDtypeStruct((NP, SWA_NQ), BF16),
        compiler_params=_cparams("parallel", "arbitrary"),
        name="swa_prompt",
    )(sinks, proj, proj, proj, proj, proj)

    rows = SWA_SAMPLE_SEQS * DEC_SEQ
    row0 = NP // rows
    o_s, k_s, v_s = pl.pallas_call(
        _swa_sample_body,
        grid=(DEC_BATCH // SWA_SAMPLE_SEQS,),
        in_specs=[
            smem,
            pl.BlockSpec((rows, SWA_NQ), lambda g: (row0 + g, 0)),
            pl.BlockSpec((rows, SWA_NK), lambda g: (row0 + g, 0)),
            pl.BlockSpec((rows, SWA_NK), lambda g: (row0 + g, kcol + 1)),
            pl.BlockSpec((SWA_SAMPLE_SEQS, WINDOW, SWA_NK), lambda g: (g, 0, 0)),
            pl.BlockSpec((SWA_SAMPLE_SEQS, WINDOW, SWA_NK), lambda g: (g, 0, 0)),
        ],
        out_specs=[
            pl.BlockSpec((rows, SWA_NQ), lambda g: (g, 0)),
            pl.BlockSpec((SWA_SAMPLE_SEQS, WINDOW, SWA_NK), lambda g: (g, 0, 0)),
            pl.BlockSpec((SWA_SAMPLE_SEQS, WINDOW, SWA_NK), lambda g: (g, 0, 0)),
        ],
        out_shape=[jax.ShapeDtypeStruct((NS, SWA_NQ), BF16),
                   jax.ShapeDtypeStruct((DEC_BATCH, WINDOW, SWA_NK), F32),
                   jax.ShapeDtypeStruct((DEC_BATCH, WINDOW, SWA_NK), F32)],
        compiler_params=_cparams("parallel"),
        name="swa_sample",
    )(sinks, proj, k_rot, proj, cache_k, cache_v)
    return o, o_s, k_s, v_s


RWKV_ROW_TILE = 256
RWKV_PAIRS = RWKV_HEADS // 2
RWKV_STACK = MXU_DIM_V7X
RWKV_SAMPLE_SEQS = RWKV_STACK // (RWKV_PAIRS * DEC_SEQ)


def _block_diag_ones():
    r = jnp.arange(MXU_DIM_V7X) // RWKV_HD
    return (r[:, None] == r[None, :]).astype(BF16)


def _store_pairs(ref, val):
    for p in range(RWKV_PAIRS):
        ref[p] = val[:, p * LANES:(p + 1) * LANES].astype(ref.dtype)


def _load_pairs(ref):
    return jnp.concatenate([ref[p] for p in range(RWKV_PAIRS)], axis=-1)


def _rwkv_proj_body(x_ref, xprev_ref, first_ref, gn_ref, mu_ref, wr_ref, wk_ref, wv_ref, w0_ref, w1_ref, w2_ref,
                    a0_ref, a1_ref, a2_ref, g1_ref, g2_ref, kk_ref, ka_ref, bd_ref,
                    r_out, lw_out, k_out, v_out, kk_out, b_out, g_out, hlast_out, h_scr):
    i = pl.program_id(0)
    h = _rms(x_ref[...], gn_ref[...])
    prev_tile_last = _rms(xprev_ref[...], gn_ref[...])[SUBLANES - 1:SUBLANES, :]
    rowid = lax.broadcasted_iota(jnp.int32, h.shape, 0)
    xp = jnp.where(rowid == 0, prev_tile_last, pltpu.roll(h, 1, 0))
    seq_len_mask = jnp.where(i >= NP // RWKV_ROW_TILE, DEC_SEQ - 1, SEQ - 1)
    is_first = jnp.bitwise_and(i * RWKV_ROW_TILE + rowid, seq_len_mask) == 0
    xp = jnp.where(is_first, first_ref[...], xp)
    _store_pairs(h_scr, h)
    every8th = pl.ds(SUBLANES - 1, RWKV_ROW_TILE // SUBLANES, stride=SUBLANES)
    hlast_out[...] = jnp.concatenate([h_scr[p, every8th, :] for p in range(RWKV_PAIRS)], axis=-1)
    d = xp - h
    xs = [h + d * mu_ref[i:i + 1, :] for i in range(6)]
    _store_pairs(r_out, _dot(xs[0], wr_ref[...]))
    k = _dot(xs[1], wk_ref[...])
    _store_pairs(v_out, _dot(xs[2], wv_ref[...]))
    z = w0_ref[...] + _dot(jnp.tanh(_dot(xs[3], w1_ref[...])), w2_ref[...])
    _store_pairs(lw_out, -math.exp(-0.5) * _sigmoid(z))
    a = _sigmoid(a0_ref[...] + _dot(_dot(xs[4], a1_ref[...]), a2_ref[...]))
    g_out[...] = _dot(_sigmoid(_dot(xs[5], g1_ref[...])), g2_ref[...]).astype(BF16)
    kk = k * kk_ref[...]
    kk = kk * jnp.minimum(lax.rsqrt(_seg_sum(kk * kk, bd_ref[...])), 1e12)
    _store_pairs(kk_out, kk)
    _store_pairs(b_out, kk * a)
    _store_pairs(k_out, k * (1.0 + (a - 1.0) * ka_ref[...]))


def rwkv_proj(x, shift_s, p):
    first = jnp.concatenate([
        jnp.zeros((RWKV_ROW_TILE, D_MODEL), F32),
        jnp.pad(shift_s[:, None, :], ((0, 0), (0, DEC_SEQ - 1), (0, 0))).reshape(NS, D_MODEL)], axis=0)
    prompt_tiles = NP // RWKV_ROW_TILE
    row = pl.BlockSpec((RWKV_ROW_TILE, D_MODEL), lambda i: (i, 0))
    prev = pl.BlockSpec((SUBLANES, D_MODEL), lambda i: (jnp.maximum(i * (RWKV_ROW_TILE // SUBLANES) - 1, 0), 0))
    first_spec = pl.BlockSpec((RWKV_ROW_TILE, D_MODEL), lambda i: (jnp.maximum(i - prompt_tiles + 1, 0), 0))
    pair = pl.BlockSpec((RWKV_PAIRS, RWKV_ROW_TILE, LANES), lambda i: (0, i, 0))
    full = lambda a: pl.BlockSpec(a.shape, lambda i: (0,) * a.ndim)
    consts = [p["g_mix"], p["mu"], p["wr"], p["wk"], p["wv"], p["w0"], p["w1"], p["w2"], p["a0"], p["a1"],
              p["a2"], p["g1"], p["g2"], p["k_k"], p["k_a"], p["bd"]]
    pair_shape = lambda dt: jax.ShapeDtypeStruct((RWKV_PAIRS, NT, LANES), dt)
    return pl.pallas_call(
        _rwkv_proj_body,
        grid=(NT // RWKV_ROW_TILE,),
        in_specs=[row, prev, first_spec] + [full(c) for c in consts],
        out_specs=[pair] * 6 + [row, pl.BlockSpec((RWKV_ROW_TILE // SUBLANES, D_MODEL), lambda i: (i, 0))],
        out_shape=[pair_shape(BF16), pair_shape(F32)] + [pair_shape(BF16)] * 4
        + [jax.ShapeDtypeStruct((NT, D_MODEL), BF16), jax.ShapeDtypeStruct((NT // SUBLANES, D_MODEL), F32)],
        scratch_shapes=[pltpu.VMEM((RWKV_PAIRS, RWKV_ROW_TILE, LANES), F32)],
        compiler_params=_cparams("parallel"),
        name="rwkv_proj",
    )(x, x, first, *consts)


def _rwkv_chunks(c_len, problems):
    n_grp = RWKV_STACK // c_len
    shift = int(math.log2(c_len))
    assert 2 ** shift == c_len
    row = lax.broadcasted_iota(jnp.int32, (RWKV_STACK, RWKV_STACK), 0)
    col = lax.broadcasted_iota(jnp.int32, (RWKV_STACK, RWKV_STACK), 1)
    same = jnp.right_shift(row, shift) == jnp.right_shift(col, shift)
    strict_t = same & (row < col)
    incl_t = same & (row <= col)
    tri_t = jnp.where(incl_t, 1.0, 0.0).astype(BF16)
    each = lambda f, *cols: [f(*xs) for xs in zip(*cols)]
    r, lw, k, v, kk, b, state, store = zip(*problems)
    lam = each(lambda x: sum(jnp.dot(part, tri_t, preferred_element_type=F32) for part in _split3(x.T)).T, lw)
    lam3 = each(lambda x: x.reshape(n_grp, c_len, LANES), lam)
    lam_end3 = each(lambda x: x[:, c_len - 1:c_len, :], lam3)
    e_end = each(lambda e, x: jnp.exp(e - x).reshape(RWKV_STACK, LANES), lam_end3, lam3)
    e_tot3 = each(jnp.exp, lam_end3)
    e_neg = each(lambda x: jnp.exp(-x), lam)
    at = each(lambda kk_, l, w: -kk_ * jnp.exp(l - w), kk, lam, lw)
    rt = each(lambda r_, l: r_ * jnp.exp(l), r, lam)
    mul = lambda x, y: x * y
    bh, kh, bb, kb = each(mul, b, e_neg), each(mul, k, e_neg), each(mul, b, e_end), each(mul, k, e_end)

    lane = lax.broadcasted_iota(jnp.int32, (RWKV_STACK, LANES), 1)
    halves = (lane < RWKV_HD, lane >= RWKV_HD)
    rhs = each(lambda x, y: jnp.concatenate([x, y], axis=0), bh, kh)
    at_pow, bt_up, rt_up = [], [], []
    for sel in halves:
        lhs = each(lambda x, y: jnp.concatenate([jnp.where(sel, x, 0.0), jnp.where(sel, y, 0.0)], axis=0), at, rt)
        mt = each(_dot_nt, rhs, lhs)
        at_pow.append(each(lambda x: jnp.where(strict_t, x[:RWKV_STACK, :RWKV_STACK], 0.0), mt))
        bt_up.append(each(lambda x: jnp.where(strict_t, x[RWKV_STACK:, :RWKV_STACK], 0.0), mt))
        rt_up.append(each(lambda x: jnp.concatenate([jnp.where(incl_t, x[:RWKV_STACK, RWKV_STACK:], 0.0),
                                                     jnp.where(incl_t, x[RWKV_STACK:, RWKV_STACK:], 0.0)], axis=0), mt))

    rows = lambda t, g: t[g * c_len:(g + 1) * c_len]
    states = each(lambda st: [st(g) for g in range(n_grp)], state)
    p = [each(lambda x, y, st: _dot_nt(jnp.concatenate([rows(x, g), rows(y, g)], axis=0), st[g]), at, rt, states)
         for g in range(n_grp)]
    n_prob = len(problems)
    a_s_t = [jnp.concatenate([p[g][i][:c_len] for g in range(n_grp)], axis=0).T for i in range(n_prob)]
    r_s_t = [jnp.concatenate([p[g][i][c_len:] for g in range(n_grp)], axis=0).T for i in range(n_prob)]
    v_t = each(lambda x: x.T, v)
    half_rows = lambda x, e: x[e * RWKV_HD:(e + 1) * RWKV_HD]
    n_half = len(halves)
    u_t = [each(lambda a_, v_, b_: half_rows(a_, e) + _dot(half_rows(v_, e), b_), a_s_t, v_t, bt_up[e])
           for e in range(n_half)]
    assert shift >= 2
    for it in range(shift - 1):
        u_t = [each(lambda u_, a_: u_ + _dot(u_, a_), u_t[e], at_pow[e]) for e in range(n_half)]
        if it + 2 < shift:
            at_pow = [each(lambda a_: _dot(a_, a_), at_pow[e]) for e in range(n_half)]
    z_t = [each(lambda u_, a_: _dot(u_, a_), u_t[e], at_pow[e]) for e in range(n_half)]
    u_t = [each(lambda u_, z_, a_: u_ + _dot(z_, a_), u_t[e], z_t[e], at_pow[e]) for e in range(n_half)]
    y_t = [each(lambda r_, u_, v_, m_: half_rows(r_, e) + _dot(jnp.concatenate([u_, half_rows(v_, e)], axis=1), m_),
                r_s_t, u_t[e], v_t, rt_up[e]) for e in range(n_half)]
    stack_halves = lambda parts: [jnp.concatenate([parts[e][i] for e in range(n_half)], axis=0).T
                                  for i in range(n_prob)]
    u = stack_halves(u_t)
    y = stack_halves(y_t)

    r128 = lax.broadcasted_iota(jnp.int32, (LANES, LANES), 0)
    c128 = lax.broadcasted_iota(jnp.int32, (LANES, LANES), 1)
    diag = (r128 < RWKV_HD) == (c128 < RWKV_HD)
    for g in range(n_grp):
        upd = each(lambda u_, v_, bb_, kb_: _dot_tn(jnp.concatenate([rows(u_, g), rows(v_, g)], axis=0),
                                                    jnp.concatenate([rows(bb_, g), rows(kb_, g)], axis=0)),
                   u, v, bb, kb)
        for i in range(n_prob):
            store[i](g, states[i][g] * e_tot3[i][g] + jnp.where(diag, upd[i], 0.0))
    return y


RWKV_PROMPT_SEQS = 4
RWKV_STEP_TOKENS = 128


def _rwkv_prompt_body(*refs):
    ins, (y_ref, s_ref, s_scr) = refs[:6 * RWKV_PROMPT_SEQS], refs[6 * RWKV_PROMPT_SEQS:]
    c = pl.program_id(1)

    @pl.when(c == 0)
    def _():
        s_scr[...] = jnp.zeros_like(s_scr)

    def one_chunk(ci, carry):
        rows = pl.ds(pl.multiple_of(ci * RWKV_CHUNK, RWKV_CHUNK), RWKV_CHUNK)
        def problem(seq):
            def store(g, s):
                s_scr[seq, g] = s

            args = [ins[a * RWKV_PROMPT_SEQS + seq][:, rows, :].reshape(RWKV_STACK, LANES).astype(F32)
                    for a in range(6)]
            return (*args, lambda g: s_scr[seq, g], store)

        ys = _rwkv_chunks(RWKV_CHUNK, [problem(seq) for seq in range(RWKV_PROMPT_SEQS)])
        for seq, y in enumerate(ys):
            y_ref[:, seq, rows, :] = y.reshape(RWKV_PAIRS, RWKV_CHUNK, LANES)
        return carry

    lax.fori_loop(0, RWKV_STEP_TOKENS // RWKV_CHUNK, one_chunk, 0)

    @pl.when(c == pl.num_programs(1) - 1)
    def _():
        for seq in range(RWKV_PROMPT_SEQS):
            for p in range(RWKV_PAIRS):
                s_ref[seq, 2 * p] = s_scr[seq, p, :RWKV_HD, :RWKV_HD]
                s_ref[seq, 2 * p + 1] = s_scr[seq, p, RWKV_HD:, RWKV_HD:]


RWKV_SAMPLE_PROBLEMS = 2


def _rwkv_sample_body(r_ref, lw_ref, k_ref, v_ref, kk_ref, b_ref, s0_ref, y_ref, s_ref):
    zero = jnp.zeros((RWKV_HD, RWKV_HD), F32)
    rows_per = RWKV_SAMPLE_SEQS * DEC_SEQ

    def problem(q):
        def state(g):
            p, seq = divmod(g, RWKV_SAMPLE_SEQS)
            seq += q * RWKV_SAMPLE_SEQS
            return jnp.concatenate([jnp.concatenate([s0_ref[seq, 2 * p], zero], axis=1),
                                    jnp.concatenate([zero, s0_ref[seq, 2 * p + 1]], axis=1)], axis=0)

        def store(g, s):
            p, seq = divmod(g, RWKV_SAMPLE_SEQS)
            seq += q * RWKV_SAMPLE_SEQS
            s_ref[seq, 2 * p] = s[:RWKV_HD, :RWKV_HD]
            s_ref[seq, 2 * p + 1] = s[RWKV_HD:, RWKV_HD:]

        args = [ref[:, q * rows_per:(q + 1) * rows_per, :].reshape(RWKV_STACK, LANES).astype(F32)
                for ref in (r_ref, lw_ref, k_ref, v_ref, kk_ref, b_ref)]
        return (*args, state, store)

    ys = _rwkv_chunks(DEC_SEQ, [problem(q) for q in range(RWKV_SAMPLE_PROBLEMS)])
    for q, y in enumerate(ys):
        y_ref[:, q * rows_per:(q + 1) * rows_per, :] = y.reshape(RWKV_PAIRS, rows_per, LANES)


def rwkv_core(r, lw, k, v, kk, b, state_s, j):
    assert RWKV_PAIRS * RWKV_CHUNK == RWKV_STACK
    nc = SEQ // RWKV_STEP_TOKENS
    st_shape = (RWKV_HEADS, RWKV_HD, RWKV_HD)
    seq_blk = lambda seq: pl.BlockSpec((RWKV_PAIRS, RWKV_STEP_TOKENS, LANES),
                                       lambda bi, c: (0, (bi * RWKV_PROMPT_SEQS + seq) * nc + c, 0))
    inputs = (r, lw, k, v, kk, b)
    y_p, s_p = pl.pallas_call(
        _rwkv_prompt_body,
        grid=(BATCH // RWKV_PROMPT_SEQS, nc),
        in_specs=[seq_blk(seq) for _ in inputs for seq in range(RWKV_PROMPT_SEQS)],
        out_specs=[pl.BlockSpec((RWKV_PAIRS, RWKV_PROMPT_SEQS, RWKV_STEP_TOKENS, LANES), lambda bi, c: (0, bi, c, 0)),
                   pl.BlockSpec((RWKV_PROMPT_SEQS,) + st_shape, lambda bi, c: (bi, 0, 0, 0))],
        out_shape=[jax.ShapeDtypeStruct((RWKV_PAIRS, BATCH, SEQ, LANES), F32),
                   jax.ShapeDtypeStruct((BATCH,) + st_shape, F32)],
        scratch_shapes=[pltpu.VMEM((RWKV_PROMPT_SEQS, RWKV_PAIRS, LANES, LANES), F32)],
        compiler_params=_cparams("parallel", "arbitrary"),
        name="rwkv_prompt",
    )(*[a for a in inputs for _ in range(RWKV_PROMPT_SEQS)])

    step_seqs = RWKV_SAMPLE_PROBLEMS * RWKV_SAMPLE_SEQS
    rows = step_seqs * DEC_SEQ
    row0 = NP // rows
    blk = pl.BlockSpec((RWKV_PAIRS, rows, LANES), lambda i: (0, row0 + i, 0))
    st = pl.BlockSpec((step_seqs,) + st_shape, lambda i: (i, 0, 0, 0))
    st_in = pl.BlockSpec((None, step_seqs) + st_shape, lambda i: (j, i, 0, 0, 0))
    y_s, s_s = pl.pallas_call(
        _rwkv_sample_body,
        grid=(DEC_BATCH // step_seqs,),
        in_specs=[blk] * 6 + [st_in],
        out_specs=[pl.BlockSpec((RWKV_PAIRS, rows, LANES), lambda i: (0, i, 0)), st],
        out_shape=[jax.ShapeDtypeStruct((RWKV_PAIRS, NS, LANES), F32),
                   jax.ShapeDtypeStruct((DEC_BATCH,) + st_shape, F32)],
        compiler_params=_cparams("parallel"),
        name="rwkv_sample",
    )(*inputs, state_s)
    return y_p.reshape(RWKV_PAIRS, NP, LANES), y_s, s_p, s_s


def _rwkv_out_body(x_ref, yp_ref, ys_ref, r_ref, k_ref, v_ref, g_ref, rk_ref, lnw_ref, lnb_ref, bd_ref, wo_ref,
                   o_ref):
    bd = bd_ref[...]
    y = _pick_group(RWKV_ROW_TILE, _load_pairs(yp_ref), _load_pairs(ys_ref))
    yc = y - _seg_sum(y, bd) * (1.0 / RWKV_HD)
    var = _seg_sum(yc * yc, bd) * (1.0 / RWKV_HD)
    yn = yc * lax.rsqrt(var + RWKV_GN_EPS) * lnw_ref[...] + lnb_ref[...]
    rk = _load_pairs(r_ref).astype(F32) * _load_pairs(k_ref).astype(F32)
    bonus = _seg_sum(rk * rk_ref[...], bd) * _load_pairs(v_ref).astype(F32)
    z = (yn + bonus) * g_ref[...].astype(F32)
    o_ref[...] = x_ref[...] + _dot(z, wo_ref[...])


def rwkv_out(x, y_p, y_s, r, k, v, g, p):
    row = pl.BlockSpec((RWKV_ROW_TILE, D_MODEL), lambda i: (i, 0))
    pair_of = lambda f: pl.BlockSpec((RWKV_PAIRS, RWKV_ROW_TILE, LANES), lambda i: (0, f(i), 0))
    full = lambda a: pl.BlockSpec(a.shape, lambda i: (0,) * a.ndim)
    consts = [p["r_k"], p["ln_w"], p["ln_b"], p["bd"], p["w_out"]]
    return pl.pallas_call(
        _rwkv_out_body,
        grid=(NT // RWKV_ROW_TILE,),
        in_specs=[row, *_group_specs(RWKV_ROW_TILE, pair_of)] + [pair_of(lambda i: i)] * 3 + [row]
        + [full(c) for c in consts],
        out_specs=row,
        out_shape=jax.ShapeDtypeStruct((NT, D_MODEL), F32),
        compiler_params=_cparams("parallel"),
        name="rwkv_out",
    )(x, y_p, y_s, r, k, v, g, *consts)


def _pad_cols(w, n):
    return jnp.pad(w, ((0, 0), (0, n - w.shape[1])))


def _pad_rows(w, n):
    return jnp.pad(w, ((0, n - w.shape[0]), (0, 0)))


def kernel(x_prompt, x_sample, state_ret, cache_swa_k, cache_swa_v, state_rwkv_wkv, state_rwkv_shift,
           norm_mix, norm_mlp, norm_final, ret_w_in, ret_w_out,
           swa_w_in, swa_b_in, swa_sinks, swa_w_out,
           rwkv_mu, rwkv_w_rkv, rwkv_w0, rwkv_w1, rwkv_w2, rwkv_a0, rwkv_a1, rwkv_a2,
           rwkv_g1, rwkv_g2, rwkv_k_k, rwkv_k_a, rwkv_r_k, rwkv_ln_w, rwkv_ln_b, rwkv_w_out,
           mlp_w_up, mlp_w_down):
    x = (x_prompt, x_sample.reshape(NS, D_MODEL))
    if LAYER_KINDS[0] != 0:
        x = jnp.concatenate([x_prompt.reshape(NP, D_MODEL), x[1]], axis=0)
    row = lambda t: t.reshape(1, -1)
    k_p, k_s, v_p, v_s, wkv_p, wkv_s, sh_p, sh_s = ([] for _ in range(8))
    ret_states = None
    for layer in range(DEPTH):
        kind = LAYER_KINDS[layer]
        j = LAYER_KINDS[:layer].count(kind)
        g_mix = row(norm_mix[layer])
        if kind == 0:
            proj = ret_proj(x, g_mix, ret_w_in[j].astype(BF16))
            o_p, o_s, *ret_states = retention_core(proj, state_ret, j, ret_states)
            x = matmul_residual(x, o_p, o_s, ret_w_out[j].astype(BF16))
        elif kind == 1:
            proj, k_rot = swa_proj(x, g_mix, swa_w_in[j].astype(BF16), row(swa_b_in[j]))
            o_p, o_s, kc, vc = swa_core(proj, k_rot, swa_sinks[j],
                                        cache_swa_k[j].reshape(DEC_BATCH, WINDOW, SWA_NK),
                                        cache_swa_v[j].reshape(DEC_BATCH, WINDOW, SWA_NK))
            x = matmul_residual(x, o_p, o_s, swa_w_out[j].astype(BF16))
            kv_shape = (-1, WINDOW, SWA_KV_HEADS, SWA_HD)
            last = slice(SEQ // WINDOW - 1, NP // WINDOW, SEQ // WINDOW)
            k_p.append(k_rot.reshape(NT // WINDOW, WINDOW, SWA_NK)[last].reshape(kv_shape))
            v_p.append(proj.reshape(NT // WINDOW, WINDOW, -1)[last, :, SWA_NQ + SWA_NK:]
                       .astype(F32).reshape(kv_shape))
            k_s.append(kc.reshape(kv_shape))
            v_s.append(vc.reshape(kv_shape))
        else:
            lora_w = LANES
            lora_g = 2 * LANES
            p = dict(
                mu=rwkv_mu[j], wr=rwkv_w_rkv[j, 0].astype(BF16), wk=rwkv_w_rkv[j, 1].astype(BF16),
                wv=rwkv_w_rkv[j, 2].astype(BF16), w0=row(rwkv_w0[j]),
                w1=_pad_cols(rwkv_w1[j], lora_w).astype(BF16), w2=_pad_rows(rwkv_w2[j], lora_w).astype(BF16),
                a0=row(rwkv_a0[j]),
                a1=_pad_cols(rwkv_a1[j], lora_w).astype(BF16), a2=_pad_rows(rwkv_a2[j], lora_w).astype(BF16),
                g1=_pad_cols(rwkv_g1[j], lora_g).astype(BF16), g2=_pad_rows(rwkv_g2[j], lora_g).astype(BF16),
                k_k=row(rwkv_k_k[j]), k_a=row(rwkv_k_a[j]), r_k=row(rwkv_r_k[j]),
                ln_w=row(rwkv_ln_w[j]), ln_b=row(rwkv_ln_b[j]), bd=_block_diag_ones(),
                w_out=rwkv_w_out[j].astype(BF16), g_mix=g_mix)
            r, lw, k, v, kk, b, g, h_last = rwkv_proj(x, state_rwkv_shift[j], p)
            y_p, y_s, s_p, s_s = rwkv_core(r, lw, k, v, kk, b, state_rwkv_wkv, j)
            x = rwkv_out(x, y_p, y_s, r, k, v, g, p)
            wkv_p.append(s_p)
            wkv_s.append(s_s)
            sh_p.append(h_last[SEQ // SUBLANES - 1:NP // SUBLANES:SEQ // SUBLANES])
            sh_s.append(h_last[NP // SUBLANES:])
        x = mlp(x, row(norm_mlp[layer]), mlp_w_up, mlp_w_down, layer,
                final_g=row(norm_final) if layer == DEPTH - 1 else None)
    y_p, y_s = x
    return (y_p.reshape(BATCH, SEQ, D_MODEL), y_s.reshape(DEC_BATCH, DEC_SEQ, D_MODEL),
            ret_states[0], ret_states[1], jnp.stack(k_p), jnp.stack(k_s), jnp.stack(v_p), jnp.stack(v_s),
            jnp.stack(wkv_p), jnp.stack(wkv_s), jnp.stack(sh_p), jnp.stack(sh_s))
```

```python
import functools
import math

import jax
import jax.numpy as jnp
from jax import lax
from jax.experimental import pallas as pl
from jax.experimental.pallas import tpu as pltpu

F32 = jnp.float32
BF16 = jnp.bfloat16

D_MODEL = 1024
BATCH = 8
SEQ = 2048
DEPTH = 4
DEC_BATCH = 128
DEC_SEQ = 8
PAST_LEN = 8192
LAYER_KINDS = tuple(i % 3 for i in range(DEPTH))
NORM_EPS = 1e-6
ROPE_THETA = 10000.0
NEG_INF = -1e30
D_FF = 4 * D_MODEL

RET_HEADS = 4
RET_DK = D_MODEL // RET_HEADS
RET_DV = 2 * D_MODEL // RET_HEADS
RET_CHUNK = 128
RET_PROMPT_CHUNK = 256
RET_NQ = RET_HEADS * RET_DK
RET_NV = RET_HEADS * RET_DV

SWA_HEADS = 16
SWA_KV_HEADS = 4
SWA_GROUP = SWA_HEADS // SWA_KV_HEADS
SWA_HD = D_MODEL // SWA_HEADS
WINDOW = 128
SWA_BLOCK = 128
SWA_NQ = SWA_HEADS * SWA_HD
SWA_NK = SWA_KV_HEADS * SWA_HD

RWKV_HD = 64
RWKV_HEADS = D_MODEL // RWKV_HD
RWKV_GN_EPS = 64e-5
RWKV_CHUNK = 32

NP = BATCH * SEQ
NS = DEC_BATCH * DEC_SEQ
NT = NP + NS

VMEM_LIMIT_V7X = 48 * 1024 * 1024
STREAMED_F32_VMEM_LIMIT_V7X = 56 * 1024 * 1024
LANES = 128
SUBLANES = 8
MXU_DIM_V7X = 256

ROW_TILE = 1024


def _cparams(*sem, vmem=VMEM_LIMIT_V7X):
    return pltpu.CompilerParams(dimension_semantics=sem, vmem_limit_bytes=vmem)


def _dot(a, b):
    return jnp.dot(a.astype(BF16), b.astype(BF16), preferred_element_type=F32)


def _dot_nt(a, b):
    return lax.dot_general(a.astype(BF16), b.astype(BF16), (((1,), (1,)), ((), ())),
                           preferred_element_type=F32)


def _dot_tn(a, b):
    return lax.dot_general(a.astype(BF16), b.astype(BF16), (((0,), (0,)), ((), ())),
                           preferred_element_type=F32)


def _div_pow2(x, n):
    assert n & (n - 1) == 0
    return jnp.right_shift(x, n.bit_length() - 1)


def _mod_pow2(x, n):
    assert n & (n - 1) == 0
    return jnp.bitwise_and(x, n - 1)


def _sigmoid(x):
    return 0.5 * jnp.tanh(0.5 * x) + 0.5


def _rms(x, g):
    ms = jnp.mean(x * x, axis=-1, keepdims=True)
    return x * lax.rsqrt(ms + NORM_EPS) * g


def _split3(x):
    hi = x.astype(BF16)
    r1 = x - hi.astype(F32)
    mid = r1.astype(BF16)
    lo = (r1 - mid.astype(F32)).astype(BF16)
    return hi, mid, lo


def _seg_sum(x, bd):
    xb = x.astype(BF16)
    return jnp.concatenate(
        [jnp.dot(xb[:, j * MXU_DIM_V7X:(j + 1) * MXU_DIM_V7X], bd, preferred_element_type=F32)
         for j in range(D_MODEL // MXU_DIM_V7X)], axis=-1)


def _rope_tables(pos, half):
    inv_freq = jnp.power(ROPE_THETA, -jnp.arange(half, dtype=F32) / half)
    ang = pos.astype(F32)[:, None] * inv_freq[None, :]
    return jnp.cos(ang), jnp.sin(ang)


def _group_specs(tile, block_of):
    n_prompt = NP // tile
    return (block_of(lambda i: jnp.minimum(i, n_prompt - 1)), block_of(lambda i: jnp.maximum(i - n_prompt, 0)))


def _pick_group(tile, p_val, s_val):
    return jnp.where(pl.program_id(0) < NP // tile, p_val, s_val)


def _token_rows(x, tile):
    if isinstance(x, tuple):
        per_seq = SEQ // tile
        f_p, f_s = _group_specs(tile, lambda f: f)
        return list(x), [pl.BlockSpec((None, tile, D_MODEL), lambda *g: (f_p(g[0]) // per_seq, f_p(g[0]) % per_seq, 0)),
                         pl.BlockSpec((tile, D_MODEL), lambda *g: (f_s(g[0]), 0))]
    return [x], [pl.BlockSpec((tile, D_MODEL), lambda *g: (g[0], 0))]


def _read_rows(refs, tile):
    return refs[0][...] if len(refs) == 1 else _pick_group(tile, refs[0][...], refs[1][...])


def _rope_table_inputs(half, widen, tile):
    tp = widen(*_rope_tables(jnp.arange(SEQ, dtype=jnp.int32), half))
    ts = [jnp.tile(t, (DEC_BATCH, 1))
          for t in widen(*_rope_tables(PAST_LEN + jnp.arange(DEC_SEQ, dtype=jnp.int32), half))]
    n_prompt = NP // tile
    per_seq = SEQ // tile
    p_spec = pl.BlockSpec((tile, LANES), lambda *g: (g[0] % per_seq, 0))
    s_spec = pl.BlockSpec((tile, LANES), lambda *g: (jnp.maximum(g[0] - n_prompt, 0), 0))
    return [tp[0], tp[1], ts[0], ts[1]], [p_spec, p_spec, s_spec, s_spec]


RET_PROJ_TILE = 2048


def _ret_proj_body(n_x, *refs):
    x_refs, (g_ref, w_ref, o_ref, xn_ref) = refs[:n_x], refs[n_x:]

    @pl.when(pl.program_id(1) == 0)
    def _():
        xn_ref[...] = _rms(_read_rows(x_refs, ROW_TILE), g_ref[...]).astype(BF16)

    o_ref[...] = jnp.dot(xn_ref[...], w_ref[...], preferred_element_type=F32).astype(BF16)


def ret_proj(x, g, w):
    n_out = w.shape[1]
    x_arrays, x_specs = _token_rows(x, ROW_TILE)
    return pl.pallas_call(
        functools.partial(_ret_proj_body, len(x_arrays)),
        grid=(NT // ROW_TILE, n_out // RET_PROJ_TILE),
        in_specs=x_specs + [
            pl.BlockSpec((1, D_MODEL), lambda i, j: (0, 0)),
            pl.BlockSpec((D_MODEL, RET_PROJ_TILE), lambda i, j: (0, j)),
        ],
        out_specs=pl.BlockSpec((ROW_TILE, RET_PROJ_TILE), lambda i, j: (i, j)),
        out_shape=jax.ShapeDtypeStruct((NT, n_out), BF16),
        scratch_shapes=[pltpu.VMEM((ROW_TILE, D_MODEL), BF16)],
        compiler_params=_cparams("parallel", "arbitrary"),
        name="ret_proj",
    )(*x_arrays, g, w)


SWA_PROJ_ROW_TILE = 512


def _swa_rope(x, cos, sin):
    n = x.shape[1]
    half = SWA_HD // 2
    lane = lax.broadcasted_iota(jnp.int32, x.shape, 1)
    partner = jnp.where(_mod_pow2(lane, SWA_HD) < half, pltpu.roll(x, n - half, 1), pltpu.roll(x, half, 1))
    reps = n // LANES
    return x * jnp.concatenate([cos] * reps, axis=-1) + partner * jnp.concatenate([sin] * reps, axis=-1)


def _swa_proj_body(x_ref, g_ref, w_ref, b_ref, cosp_ref, sinp_ref, coss_ref, sins_ref, o_ref, kr_ref):
    xn = _rms(x_ref[...], g_ref[...]).astype(BF16)
    acc = jnp.dot(xn, w_ref[...], preferred_element_type=F32) + b_ref[...]
    n_qk = SWA_NQ + SWA_NK
    cos = _pick_group(SWA_PROJ_ROW_TILE, cosp_ref[...], coss_ref[...])
    sin = _pick_group(SWA_PROJ_ROW_TILE, sinp_ref[...], sins_ref[...])
    qk = _swa_rope(acc[:, :n_qk], cos, sin)
    assert math.log2(SWA_HD ** 0.5).is_integer()
    o_ref[:, :SWA_NQ] = (qk[:, :SWA_NQ] * (SWA_HD ** -0.5)).astype(BF16)
    o_ref[:, SWA_NQ:n_qk] = qk[:, SWA_NQ:].astype(BF16)
    o_ref[:, n_qk:] = acc[:, n_qk:].astype(BF16)
    kr_ref[...] = qk[:, SWA_NQ:]


def swa_proj(x, g, w, b):
    n_out = w.shape[1]

    def widen(c, s):
        rep = LANES // SWA_HD
        return jnp.tile(jnp.concatenate([c, c], axis=-1), (1, rep)), jnp.tile(jnp.concatenate([-s, s], axis=-1), (1, rep))

    tile = SWA_PROJ_ROW_TILE
    tables, table_specs = _rope_table_inputs(SWA_HD // 2, widen, tile)
    return pl.pallas_call(
        _swa_proj_body,
        grid=(NT // tile,),
        in_specs=[
            pl.BlockSpec((tile, D_MODEL), lambda i: (i, 0)),
            pl.BlockSpec((1, D_MODEL), lambda i: (0, 0)),
            pl.BlockSpec((D_MODEL, n_out), lambda i: (0, 0)),
            pl.BlockSpec((1, n_out), lambda i: (0, 0)),
        ] + table_specs,
        out_specs=[pl.BlockSpec((tile, n_out), lambda i: (i, 0)), pl.BlockSpec((tile, SWA_NK), lambda i: (i, 0))],
        out_shape=[jax.ShapeDtypeStruct((NT, n_out), BF16), jax.ShapeDtypeStruct((NT, SWA_NK), F32)],
        compiler_params=_cparams("parallel"),
        name="swa_proj",
    )(x, g, w, b, *tables)


def _matmul_residual_body(n_x, *refs):
    x_refs, (ap_ref, as_ref, w_ref, o_ref) = refs[:n_x], refs[n_x:]
    a = _pick_group(ROW_TILE, ap_ref[...], as_ref[...])
    o_ref[...] = _read_rows(x_refs, ROW_TILE) + jnp.dot(a, w_ref[...], preferred_element_type=F32)


def matmul_residual(x, a_p, a_s, w):
    k = a_p.shape[1]
    x_arrays, x_specs = _token_rows(x, ROW_TILE)
    return pl.pallas_call(
        functools.partial(_matmul_residual_body, len(x_arrays)),
        grid=(NT // ROW_TILE,),
        in_specs=x_specs + [
            *_group_specs(ROW_TILE, lambda f: pl.BlockSpec((ROW_TILE, k), lambda i: (f(i), 0))),
            pl.BlockSpec((k, D_MODEL), lambda i: (0, 0)),
        ],
        out_specs=pl.BlockSpec((ROW_TILE, D_MODEL), lambda i: (i, 0)),
        out_shape=jax.ShapeDtypeStruct((NT, D_MODEL), F32),
        compiler_params=_cparams("parallel"),
        name="matmul_residual",
    )(*x_arrays, a_p, a_s, w)


MLP_FF_TILE = 1024


def _mlp_body(has_final_norm, x_ref, g_ref, wu_ref, wd_ref, *rest):
    if has_final_norm:
        gf_ref, yp_ref, ys_ref, xn_ref, acc_ref = rest
    else:
        acc_ref, xn_ref = rest
    f = pl.program_id(1)

    @pl.when(f == 0)
    def _():
        x = x_ref[...]
        xn_ref[...] = _rms(x, g_ref[...]).astype(BF16)
        acc_ref[...] = x

    a = jnp.maximum(jnp.dot(xn_ref[...], wu_ref[...].astype(BF16), preferred_element_type=F32), 0.0)
    acc_ref[...] += jnp.dot((a * a).astype(BF16), wd_ref[...].astype(BF16), preferred_element_type=F32)

    if has_final_norm:
        @pl.when(f == pl.num_programs(1) - 1)
        def _():
            y = _rms(acc_ref[...], gf_ref[...])
            is_prompt = pl.program_id(0) < NP // ROW_TILE

            @pl.when(is_prompt)
            def _():
                yp_ref[...] = y

            @pl.when(jnp.logical_not(is_prompt))
            def _():
                ys_ref[...] = y


def mlp(x, g, w_up, w_down, layer, final_g=None):
    row = pl.BlockSpec((ROW_TILE, D_MODEL), lambda i, f: (i, 0))
    vec = pl.BlockSpec((1, D_MODEL), lambda i, f: (0, 0))
    if final_g is None:
        extra_in, extra_specs = [], []
        out_specs = row
        out_shape = jax.ShapeDtypeStruct((NT, D_MODEL), F32)
    else:
        extra_in, extra_specs = [final_g], [vec]
        out_specs = list(_group_specs(ROW_TILE, lambda g_: pl.BlockSpec((ROW_TILE, D_MODEL), lambda i, f: (g_(i), 0))))
        out_shape = [jax.ShapeDtypeStruct((NP, D_MODEL), F32), jax.ShapeDtypeStruct((NS, D_MODEL), F32)]
    return pl.pallas_call(
        functools.partial(_mlp_body, final_g is not None),
        grid=(NT // ROW_TILE, D_FF // MLP_FF_TILE),
        in_specs=[
            row, vec,
            pl.BlockSpec((None, D_MODEL, MLP_FF_TILE), lambda i, f: (layer, 0, f)),
            pl.BlockSpec((None, MLP_FF_TILE, D_MODEL), lambda i, f: (layer, f, 0)),
        ] + extra_specs,
        out_specs=out_specs,
        out_shape=out_shape,
        scratch_shapes=[pltpu.VMEM((ROW_TILE, D_MODEL), BF16)]
        + ([] if final_g is None else [pltpu.VMEM((ROW_TILE, D_MODEL), F32)]),
        compiler_params=_cparams("parallel" if final_g is None else "arbitrary", "arbitrary", vmem=STREAMED_F32_VMEM_LIMIT_V7X),
        name="mlp",
    )(x, g, w_up, w_down, *extra_in)


def _ret_tables(c):
    log_gamma = jnp.log1p(-jnp.exp2(-5.0 - jnp.arange(RET_HEADS, dtype=F32)))
    idx = jnp.arange(c, dtype=F32)
    diff = idx[:, None] - idx[None, :]
    inner = jnp.where(diff >= 0, jnp.exp(log_gamma[:, None, None] * jnp.maximum(diff, 0.0)), 0.0)
    q_decay = jnp.exp(log_gamma[:, None] * (idx[None, :] + 1.0))
    k_decay = jnp.exp(log_gamma[:, None] * (c - 1.0 - idx[None, :]))
    chunk_decay = jnp.exp(log_gamma * c)
    rep = lambda t: jnp.broadcast_to(t[:, :, None], (RET_HEADS, c, LANES))
    return inner, rep(q_decay), rep(k_decay), chunk_decay


def _ret_heads(rows, lo, width):
    return jnp.stack([rows[:, lo + h * width:lo + (h + 1) * width] for h in range(RET_HEADS)])


def _ret_chunk(rows_list, s0, inner, qd, kd, cd_ref, cos, sin):
    n_seq = len(rows_list)
    cat = lambda f: jnp.concatenate([f(r) for r in rows_list], axis=0)

    def rope(x):
        x1, x2 = x[..., :RET_DK // 2], x[..., RET_DK // 2:]
        return jnp.concatenate([x1 * cos - x2 * sin, x2 * cos + x1 * sin], axis=-1)

    q = rope(cat(lambda r: _ret_heads(r, 0, RET_DK)).astype(F32)).astype(BF16)
    k = rope(cat(lambda r: _ret_heads(r, RET_NQ, RET_DK)).astype(F32)) * (RET_DK ** -0.5)
    v = cat(lambda r: _ret_heads(r, 2 * RET_NQ, RET_DV)).astype(BF16)
    g = cat(lambda r: _ret_heads(r, 2 * RET_NQ + RET_NV, RET_DV)).astype(F32)
    per_seq = lambda t: jnp.concatenate([t] * n_seq, axis=0)
    inner, qd, kd = per_seq(inner), per_seq(qd), per_seq(kd)
    cd = per_seq(jnp.stack([jnp.full((1, 1), cd_ref[h], F32) for h in range(RET_HEADS)]))
    qd = jnp.concatenate([qd] * (RET_DV // LANES), axis=-1)
    kd = jnp.concatenate([kd] * (RET_DK // LANES), axis=-1)
    scores = jnp.einsum('bqd,bkd->bqk', q, k.astype(BF16), preferred_element_type=F32) * inner
    o = (jnp.einsum('bqk,bke->bqe', scores.astype(BF16), v, preferred_element_type=F32)
         + jnp.einsum('bqd,bde->bqe', q, s0.astype(BF16), preferred_element_type=F32) * qd)
    s_new = s0 * cd + jnp.einsum('bkd,bke->bde', (k * kd).astype(BF16), v, preferred_element_type=F32)
    o = o * lax.rsqrt(jnp.mean(o * o, axis=-1, keepdims=True) + NORM_EPS)
    return o * (g * _sigmoid(g)), s_new


RET_SAMPLE_SEQS = 2


def _retention_body(cdp_ref, cds_ref, projp_ref, innerp_ref, qdp_ref, kdp_ref, cosp_ref, sinp_ref,
                    projs_ref, inners_ref, qds_ref, kds_ref, coss_ref, sins_ref, s0_ref, *rest):
    op_ref, sp_ref, os_ref, ss_ref, s_scr = rest[-5:]
    c = pl.program_id(0) % (SEQ // RET_PROMPT_CHUNK)

    @pl.when(c == 0)
    def _():
        s_scr[...] = jnp.zeros_like(s_scr)

    o, s_new = _ret_chunk([projp_ref[...]], s_scr[...], innerp_ref[...], qdp_ref[...], kdp_ref[...], cdp_ref,
                          cosp_ref[...], sinp_ref[...])
    s_scr[...] = s_new
    op_ref[...] = jnp.concatenate([o[h] for h in range(RET_HEADS)], axis=-1).astype(BF16)

    @pl.when(c == SEQ // RET_PROMPT_CHUNK - 1)
    def _():
        sp_ref[0, 0] = s_scr[...]

    rows = projs_ref[...].astype(F32)
    batch = RET_SAMPLE_SEQS * RET_HEADS
    o, s_new = _ret_chunk([rows[seq * DEC_SEQ:(seq + 1) * DEC_SEQ] for seq in range(RET_SAMPLE_SEQS)],
                          s0_ref[0].reshape(batch, RET_DK, RET_DV), inners_ref[...], qds_ref[...], kds_ref[...],
                          cds_ref, coss_ref[...], sins_ref[...])
    ss_ref[0] = s_new.reshape(RET_SAMPLE_SEQS, RET_HEADS, RET_DK, RET_DV)
    os_ref[...] = jnp.concatenate(
        [jnp.concatenate([o[seq * RET_HEADS + h] for h in range(RET_HEADS)], axis=-1)
         for seq in range(RET_SAMPLE_SEQS)], axis=0).astype(BF16)


def retention_core(proj, state_ret, j, stacks):
    n_in = proj.shape[1]
    nc = SEQ // RET_PROMPT_CHUNK
    n_steps = BATCH * nc
    assert n_steps == DEC_BATCH // RET_SAMPLE_SEQS
    assert math.gcd(DEC_SEQ, RET_CHUNK) == DEC_SEQ
    smem = pl.BlockSpec(memory_space=pltpu.SMEM)
    full3 = lambda t: pl.BlockSpec(t.shape, lambda i: (0, 0, 0))
    inner_p, qd_p, kd_p, cd_p = _ret_tables(RET_PROMPT_CHUNK)
    inner_s, qd_s, kd_s, cd_s = _ret_tables(DEC_SEQ)
    cos_p, sin_p = _rope_tables(jnp.arange(SEQ, dtype=jnp.int32), RET_DK // 2)
    cos_s, sin_s = _rope_tables(PAST_LEN + jnp.arange(DEC_SEQ, dtype=jnp.int32), RET_DK // 2)
    rows = RET_SAMPLE_SEQS * DEC_SEQ
    row0 = NP // rows
    st = pl.BlockSpec((1, RET_SAMPLE_SEQS, RET_HEADS, RET_DK, RET_DV), lambda i: (j, i, 0, 0, 0))
    stacked = [] if stacks is None else list(stacks)
    o_p, s_p, o_s, s_s = pl.pallas_call(
        _retention_body,
        grid=(n_steps,),
        in_specs=[
            smem, smem,
            pl.BlockSpec((RET_PROMPT_CHUNK, n_in), lambda i: (i, 0)),
            full3(inner_p), full3(qd_p), full3(kd_p),
            pl.BlockSpec((RET_PROMPT_CHUNK, LANES), lambda i: (i % nc, 0)),
            pl.BlockSpec((RET_PROMPT_CHUNK, LANES), lambda i: (i % nc, 0)),
            pl.BlockSpec((rows, n_in), lambda i: (row0 + i, 0)),
            full3(inner_s), full3(qd_s), full3(kd_s),
            pl.BlockSpec((DEC_SEQ, LANES), lambda i: (0, 0)),
            pl.BlockSpec((DEC_SEQ, LANES), lambda i: (0, 0)),
            st,
        ] + [pl.BlockSpec(memory_space=pl.ANY)] * len(stacked),
        out_specs=[
            pl.BlockSpec((RET_PROMPT_CHUNK, RET_NV), lambda i: (i, 0)),
            pl.BlockSpec((1, 1, RET_HEADS, RET_DK, RET_DV), lambda i: (j, i // nc, 0, 0, 0)),
            pl.BlockSpec((rows, RET_NV), lambda i: (i, 0)),
            st,
        ],
        out_shape=[jax.ShapeDtypeStruct((NP, RET_NV), BF16),
                   jax.ShapeDtypeStruct((state_ret.shape[0], BATCH, RET_HEADS, RET_DK, RET_DV), F32),
                   jax.ShapeDtypeStruct((NS, RET_NV), BF16),
                   jax.ShapeDtypeStruct(state_ret.shape, F32)],
        input_output_aliases={15: 1, 16: 3} if stacked else {},
        scratch_shapes=[pltpu.VMEM((RET_HEADS, RET_DK, RET_DV), F32)],
        compiler_params=_cparams("arbitrary"),
        name="retention",
    )(cd_p, cd_s, proj, inner_p, qd_p, kd_p, cos_p, sin_p, proj, inner_s, qd_s, kd_s, cos_s, sin_s, state_ret,
      *stacked)
    return o_p, o_s, s_p, s_s


def _sink_attention(q, k, v, valid_t, sink_ref):
    t = q.shape[0]
    head_cols = lambda x, h: x[:, h * SWA_HD:(h + 1) * SWA_HD]
    qs = jnp.stack([jnp.concatenate([head_cols(q, kh * SWA_GROUP + g) for g in range(SWA_GROUP)], axis=0)
                    for kh in range(SWA_KV_HEADS)]).astype(BF16)
    ks = jnp.stack([head_cols(k, kh) for kh in range(SWA_KV_HEADS)]).astype(BF16)
    vs = jnp.stack([head_cols(v, kh) for kh in range(SWA_KV_HEADS)]).astype(BF16)
    sink = jnp.stack([jnp.concatenate([jnp.full((1, t), sink_ref[kh * SWA_GROUP + g], F32)
                                       for g in range(SWA_GROUP)], axis=1) for kh in range(SWA_KV_HEADS)])
    s = jnp.einsum('hkd,hqd->hkq', ks, qs, preferred_element_type=F32)
    s = jnp.where(valid_t[None], s, NEG_INF)
    m = jnp.maximum(jnp.max(s, axis=1, keepdims=True), sink)
    p = jnp.exp(s - m)
    denom = jnp.sum(p, axis=1, keepdims=True) + jnp.exp(sink - m)
    o_t = jnp.einsum('hkd,hkq->hdq', vs, p.astype(BF16), preferred_element_type=F32) / denom
    o = [o_t[kh].T for kh in range(SWA_KV_HEADS)]
    return jnp.concatenate([o[h // SWA_GROUP][(h % SWA_GROUP) * t:(h % SWA_GROUP + 1) * t]
                            for h in range(SWA_HEADS)], axis=-1)


def _swa_prompt_body(sink_ref, q_ref, kc_ref, vc_ref, kp_ref, vp_ref, o_ref):
    blk = pl.program_id(1)
    q = q_ref[...]
    k_all = jnp.concatenate([kp_ref[...], kc_ref[...]], axis=0)
    v_all = jnp.concatenate([vp_ref[...], vc_ref[...]], axis=0)
    shape = (2 * SWA_BLOCK, SWA_GROUP * SWA_BLOCK)
    i = _mod_pow2(lax.broadcasted_iota(jnp.int32, shape, 1), SWA_BLOCK)
    j = lax.broadcasted_iota(jnp.int32, shape, 0)
    rel = i + SWA_BLOCK - j
    valid = (rel >= 0) & (rel <= WINDOW) & (j >= jnp.where(blk > 0, 0, SWA_BLOCK))
    o_ref[...] = _sink_attention(q, k_all, v_all, valid, sink_ref).astype(BF16)


SWA_SAMPLE_SEQS = 8


def _swa_sample_body(sink_ref, q_ref, kn_ref, vn_ref, kc_ref, vc_ref, o_ref, ko_ref, vo_ref):
    rows = SWA_SAMPLE_SEQS * DEC_SEQ
    n_cache = SWA_SAMPLE_SEQS * WINDOW
    q = q_ref[...]
    kn = kn_ref[...]
    vn = vn_ref[...].astype(F32)
    k_all = jnp.concatenate([kc_ref[...].reshape(n_cache, SWA_NK), kn], axis=0)
    v_all = jnp.concatenate([vc_ref[...].reshape(n_cache, SWA_NK), vn], axis=0)
    for b in range(SWA_SAMPLE_SEQS):
        new = slice(b * DEC_SEQ, (b + 1) * DEC_SEQ)
        ko_ref[b, :WINDOW - DEC_SEQ, :] = kc_ref[b, DEC_SEQ:, :]
        ko_ref[b, WINDOW - DEC_SEQ:, :] = kn[new]
        vo_ref[b, :WINDOW - DEC_SEQ, :] = vc_ref[b, DEC_SEQ:, :]
        vo_ref[b, WINDOW - DEC_SEQ:, :] = vn[new]
    shape = (n_cache + rows, SWA_GROUP * rows)
    r = _mod_pow2(lax.broadcasted_iota(jnp.int32, shape, 1), rows)
    c = lax.broadcasted_iota(jnp.int32, shape, 0)
    q_seq, q_t = _div_pow2(r, DEC_SEQ), _mod_pow2(r, DEC_SEQ)
    is_new = c >= n_cache
    k_seq = jnp.where(is_new, _div_pow2(c - n_cache, DEC_SEQ), _div_pow2(c, WINDOW))
    k_slot = jnp.where(is_new, WINDOW + _mod_pow2(c - n_cache, DEC_SEQ), _mod_pow2(c, WINDOW))
    rel = q_t + WINDOW - k_slot
    valid = (q_seq == k_seq) & (rel >= 0) & (rel <= WINDOW)
    o_ref[...] = _sink_attention(q, k_all, v_all, valid, sink_ref).astype(BF16)


def swa_core(proj, k_rot, sinks, cache_k, cache_v):
    nb = SEQ // SWA_BLOCK
    smem = pl.BlockSpec(memory_space=pltpu.SMEM)
    kcol = SWA_NQ // SWA_NK
    cur = lambda b, i: b * nb + i
    prev = lambda b, i: b * nb + jnp.maximum(i - 1, 0)
    o = pl.pallas_call(
        _swa_prompt_body,
        grid=(BATCH, nb),
        in_specs=[
            smem,
            pl.BlockSpec((SWA_BLOCK, SWA_NQ), lambda b, i: (cur(b, i), 0)),
            pl.BlockSpec((SWA_BLOCK, SWA_NK), lambda b, i: (cur(b, i), kcol)),
            pl.BlockSpec((SWA_BLOCK, SWA_NK), lambda b, i: (cur(b, i), kcol + 1)),
            pl.BlockSpec((SWA_BLOCK, SWA_NK), lambda b, i: (prev(b, i), kcol)),
            pl.BlockSpec((SWA_BLOCK, SWA_NK), lambda b, i: (prev(b, i), kcol + 1)),
        ],
        out_specs=pl.BlockSpec((SWA_BLOCK, SWA_NQ), lambda b, i: (cur(b, i), 0)),
        out_shape=jax.ShapeDtypeStruct((NP, SWA_NQ), BF16),
        compiler_params=_cparams("parallel", "arbitrary"),
        name="swa_prompt",
    )(sinks, proj, proj, proj, proj, proj)

    rows = SWA_SAMPLE_SEQS * DEC_SEQ
    row0 = NP // rows
    o_s, k_s, v_s = pl.pallas_call(
        _swa_sample_body,
        grid=(DEC_BATCH // SWA_SAMPLE_SEQS,),
        in_specs=[
            smem,
            pl.BlockSpec((rows, SWA_NQ), lambda g: (row0 + g, 0)),
            pl.BlockSpec((rows, SWA_NK), lambda g: (row0 + g, 0)),
            pl.BlockSpec((rows, SWA_NK), lambda g: (row0 + g, kcol + 1)),
            pl.BlockSpec((SWA_SAMPLE_SEQS, WINDOW, SWA_NK), lambda g: (g, 0, 0)),
            pl.BlockSpec((SWA_SAMPLE_SEQS, WINDOW, SWA_NK), lambda g: (g, 0, 0)),
        ],
        out_specs=[
            pl.BlockSpec((rows, SWA_NQ), lambda g: (g, 0)),
            pl.BlockSpec((SWA_SAMPLE_SEQS, WINDOW, SWA_NK), lambda g: (g, 0, 0)),
            pl.BlockSpec((SWA_SAMPLE_SEQS, WINDOW, SWA_NK), lambda g: (g, 0, 0)),
        ],
        out_shape=[jax.ShapeDtypeStruct((NS, SWA_NQ), BF16),
                   jax.ShapeDtypeStruct((DEC_BATCH, WINDOW, SWA_NK), F32),
                   jax.ShapeDtypeStruct((DEC_BATCH, WINDOW, SWA_NK), F32)],
        compiler_params=_cparams("parallel"),
        name="swa_sample",
    )(sinks, proj, k_rot, proj, cache_k, cache_v)
    return o, o_s, k_s, v_s


RWKV_ROW_TILE = 256
RWKV_PAIRS = RWKV_HEADS // 2
RWKV_STACK = MXU_DIM_V7X
RWKV_SAMPLE_SEQS = RWKV_STACK // (RWKV_PAIRS * DEC_SEQ)


def _block_diag_ones():
    r = jnp.arange(MXU_DIM_V7X) // RWKV_HD
    return (r[:, None] == r[None, :]).astype(BF16)


def _store_pairs(ref, val):
    for p in range(RWKV_PAIRS):
        ref[p] = val[:, p * LANES:(p + 1) * LANES].astype(ref.dtype)


def _load_pairs(ref):
    return jnp.concatenate([ref[p] for p in range(RWKV_PAIRS)], axis=-1)


def _rwkv_proj_body(x_ref, xprev_ref, first_ref, gn_ref, mu_ref, wr_ref, wk_ref, wv_ref, w0_ref, w1_ref, w2_ref,
                    a0_ref, a1_ref, a2_ref, g1_ref, g2_ref, kk_ref, ka_ref, bd_ref,
                    r_out, lw_out, k_out, v_out, kk_out, b_out, g_out, hlast_out, h_scr):
    i = pl.program_id(0)
    h = _rms(x_ref[...], gn_ref[...])
    prev_tile_last = _rms(xprev_ref[...], gn_ref[...])[SUBLANES - 1:SUBLANES, :]
    rowid = lax.broadcasted_iota(jnp.int32, h.shape, 0)
    xp = jnp.where(rowid == 0, prev_tile_last, pltpu.roll(h, 1, 0))
    seq_len_mask = jnp.where(i >= NP // RWKV_ROW_TILE, DEC_SEQ - 1, SEQ - 1)
    is_first = jnp.bitwise_and(i * RWKV_ROW_TILE + rowid, seq_len_mask) == 0
    xp = jnp.where(is_first, first_ref[...], xp)
    _store_pairs(h_scr, h)
    every8th = pl.ds(SUBLANES - 1, RWKV_ROW_TILE // SUBLANES, stride=SUBLANES)
    hlast_out[...] = jnp.concatenate([h_scr[p, every8th, :] for p in range(RWKV_PAIRS)], axis=-1)
    d = xp - h
    xs = [h + d * mu_ref[i:i + 1, :] for i in range(6)]
    _store_pairs(r_out, _dot(xs[0], wr_ref[...]))
    k = _dot(xs[1], wk_ref[...])
    _store_pairs(v_out, _dot(xs[2], wv_ref[...]))
    z = w0_ref[...] + _dot(jnp.tanh(_dot(xs[3], w1_ref[...])), w2_ref[...])
    _store_pairs(lw_out, -math.exp(-0.5) * _sigmoid(z))
    a = _sigmoid(a0_ref[...] + _dot(_dot(xs[4], a1_ref[...]), a2_ref[...]))
    g_out[...] = _dot(_sigmoid(_dot(xs[5], g1_ref[...])), g2_ref[...]).astype(BF16)
    kk = k * kk_ref[...]
    kk = kk * jnp.minimum(lax.rsqrt(_seg_sum(kk * kk, bd_ref[...])), 1e12)
    _store_pairs(kk_out, kk)
    _store_pairs(b_out, kk * a)
    _store_pairs(k_out, k * (1.0 + (a - 1.0) * ka_ref[...]))


def rwkv_proj(x, shift_s, p):
    first = jnp.concatenate([
        jnp.zeros((RWKV_ROW_TILE, D_MODEL), F32),
        jnp.pad(shift_s[:, None, :], ((0, 0), (0, DEC_SEQ - 1), (0, 0))).reshape(NS, D_MODEL)], axis=0)
    prompt_tiles = NP // RWKV_ROW_TILE
    row = pl.BlockSpec((RWKV_ROW_TILE, D_MODEL), lambda i: (i, 0))
    prev = pl.BlockSpec((SUBLANES, D_MODEL), lambda i: (jnp.maximum(i * (RWKV_ROW_TILE // SUBLANES) - 1, 0), 0))
    first_spec = pl.BlockSpec((RWKV_ROW_TILE, D_MODEL), lambda i: (jnp.maximum(i - prompt_tiles + 1, 0), 0))
    pair = pl.BlockSpec((RWKV_PAIRS, RWKV_ROW_TILE, LANES), lambda i: (0, i, 0))
    full = lambda a: pl.BlockSpec(a.shape, lambda i: (0,) * a.ndim)
    consts = [p["g_mix"], p["mu"], p["wr"], p["wk"], p["wv"], p["w0"], p["w1"], p["w2"], p["a0"], p["a1"],
              p["a2"], p["g1"], p["g2"], p["k_k"], p["k_a"], p["bd"]]
    pair_shape = lambda dt: jax.ShapeDtypeStruct((RWKV_PAIRS, NT, LANES), dt)
    return pl.pallas_call(
        _rwkv_proj_body,
        grid=(NT // RWKV_ROW_TILE,),
        in_specs=[row, prev, first_spec] + [full(c) for c in consts],
        out_specs=[pair] * 6 + [row, pl.BlockSpec((RWKV_ROW_TILE // SUBLANES, D_MODEL), lambda i: (i, 0))],
        out_shape=[pair_shape(BF16), pair_shape(F32)] + [pair_shape(BF16)] * 4
        + [jax.ShapeDtypeStruct((NT, D_MODEL), BF16), jax.ShapeDtypeStruct((NT // SUBLANES, D_MODEL), F32)],
        scratch_shapes=[pltpu.VMEM((RWKV_PAIRS, RWKV_ROW_TILE, LANES), F32)],
        compiler_params=_cparams("parallel"),
        name="rwkv_proj",
    )(x, x, first, *consts)


def _rwkv_chunks(c_len, problems):
    n_grp = RWKV_STACK // c_len
    shift = int(math.log2(c_len))
    assert 2 ** shift == c_len
    row = lax.broadcasted_iota(jnp.int32, (RWKV_STACK, RWKV_STACK), 0)
    col = lax.broadcasted_iota(jnp.int32, (RWKV_STACK, RWKV_STACK), 1)
    same = jnp.right_shift(row, shift) == jnp.right_shift(col, shift)
    strict_t = same & (row < col)
    incl_t = same & (row <= col)
    tri_t = jnp.where(incl_t, 1.0, 0.0).astype(BF16)
    each = lambda f, *cols: [f(*xs) for xs in zip(*cols)]
    r, lw, k, v, kk, b, state, store = zip(*problems)
    lam = each(lambda x: sum(jnp.dot(part, tri_t, preferred_element_type=F32) for part in _split3(x.T)).T, lw)
    lam3 = each(lambda x: x.reshape(n_grp, c_len, LANES), lam)
    lam_end3 = each(lambda x: x[:, c_len - 1:c_len, :], lam3)
    e_end = each(lambda e, x: jnp.exp(e - x).reshape(RWKV_STACK, LANES), lam_end3, lam3)
    e_tot3 = each(jnp.exp, lam_end3)
    e_neg = each(lambda x: jnp.exp(-x), lam)
    at = each(lambda kk_, l, w: -kk_ * jnp.exp(l - w), kk, lam, lw)
    rt = each(lambda r_, l: r_ * jnp.exp(l), r, lam)
    mul = lambda x, y: x * y
    bh, kh, bb, kb = each(mul, b, e_neg), each(mul, k, e_neg), each(mul, b, e_end), each(mul, k, e_end)

    lane = lax.broadcasted_iota(jnp.int32, (RWKV_STACK, LANES), 1)
    halves = (lane < RWKV_HD, lane >= RWKV_HD)
    rhs = each(lambda x, y: jnp.concatenate([x, y], axis=0), bh, kh)
    at_pow, bt_up, rt_up = [], [], []
    for sel in halves:
        lhs = each(lambda x, y: jnp.concatenate([jnp.where(sel, x, 0.0), jnp.where(sel, y, 0.0)], axis=0), at, rt)
        mt = each(_dot_nt, rhs, lhs)
        at_pow.append(each(lambda x: jnp.where(strict_t, x[:RWKV_STACK, :RWKV_STACK], 0.0), mt))
        bt_up.append(each(lambda x: jnp.where(strict_t, x[RWKV_STACK:, :RWKV_STACK], 0.0), mt))
        rt_up.append(each(lambda x: jnp.concatenate([jnp.where(incl_t, x[:RWKV_STACK, RWKV_STACK:], 0.0),
                                                     jnp.where(incl_t, x[RWKV_STACK:, RWKV_STACK:], 0.0)], axis=0), mt))

    rows = lambda t, g: t[g * c_len:(g + 1) * c_len]
    states = each(lambda st: [st(g) for g in range(n_grp)], state)
    p = [each(lambda x, y, st: _dot_nt(jnp.concatenate([rows(x, g), rows(y, g)], axis=0), st[g]), at, rt, states)
         for g in range(n_grp)]
    n_prob = len(problems)
    a_s_t = [jnp.concatenate([p[g][i][:c_len] for g in range(n_grp)], axis=0).T for i in range(n_prob)]
    r_s_t = [jnp.concatenate([p[g][i][c_len:] for g in range(n_grp)], axis=0).T for i in range(n_prob)]
    v_t = each(lambda x: x.T, v)
    half_rows = lambda x, e: x[e * RWKV_HD:(e + 1) * RWKV_HD]
    n_half = len(halves)
    u_t = [each(lambda a_, v_, b_: half_rows(a_, e) + _dot(half_rows(v_, e), b_), a_s_t, v_t, bt_up[e])
           for e in range(n_half)]
    assert shift >= 2
    for it in range(shift - 1):
        u_t = [each(lambda u_, a_: u_ + _dot(u_, a_), u_t[e], at_pow[e]) for e in range(n_half)]
        if it + 2 < shift:
            at_pow = [each(lambda a_: _dot(a_, a_), at_pow[e]) for e in range(n_half)]
    z_t = [each(lambda u_, a_: _dot(u_, a_), u_t[e], at_pow[e]) for e in range(n_half)]
    u_t = [each(lambda u_, z_, a_: u_ + _dot(z_, a_), u_t[e], z_t[e], at_pow[e]) for e in range(n_half)]
    y_t = [each(lambda r_, u_, v_, m_: half_rows(r_, e) + _dot(jnp.concatenate([u_, half_rows(v_, e)], axis=1), m_),
                r_s_t, u_t[e], v_t, rt_up[e]) for e in range(n_half)]
    stack_halves = lambda parts: [jnp.concatenate([parts[e][i] for e in range(n_half)], axis=0).T
                                  for i in range(n_prob)]
    u = stack_halves(u_t)
    y = stack_halves(y_t)

    r128 = lax.broadcasted_iota(jnp.int32, (LANES, LANES), 0)
    c128 = lax.broadcasted_iota(jnp.int32, (LANES, LANES), 1)
    diag = (r128 < RWKV_HD) == (c128 < RWKV_HD)
    for g in range(n_grp):
        upd = each(lambda u_, v_, bb_, kb_: _dot_tn(jnp.concatenate([rows(u_, g), rows(v_, g)], axis=0),
                                                    jnp.concatenate([rows(bb_, g), rows(kb_, g)], axis=0)),
                   u, v, bb, kb)
        for i in range(n_prob):
            store[i](g, states[i][g] * e_tot3[i][g] + jnp.where(diag, upd[i], 0.0))
    return y


RWKV_PROMPT_SEQS = 4
RWKV_STEP_TOKENS = 128


def _rwkv_prompt_body(*refs):
    ins, (y_ref, s_ref, s_scr) = refs[:6 * RWKV_PROMPT_SEQS], refs[6 * RWKV_PROMPT_SEQS:]
    c = pl.program_id(1)

    @pl.when(c == 0)
    def _():
        s_scr[...] = jnp.zeros_like(s_scr)

    def one_chunk(ci, carry):
        rows = pl.ds(pl.multiple_of(ci * RWKV_CHUNK, RWKV_CHUNK), RWKV_CHUNK)
        def problem(seq):
            def store(g, s):
                s_scr[seq, g] = s

            args = [ins[a * RWKV_PROMPT_SEQS + seq][:, rows, :].reshape(RWKV_STACK, LANES).astype(F32)
                    for a in range(6)]
            return (*args, lambda g: s_scr[seq, g], store)

        ys = _rwkv_chunks(RWKV_CHUNK, [problem(seq) for seq in range(RWKV_PROMPT_SEQS)])
        for seq, y in enumerate(ys):
            y_ref[:, seq, rows, :] = y.reshape(RWKV_PAIRS, RWKV_CHUNK, LANES)
        return carry

    lax.fori_loop(0, RWKV_STEP_TOKENS // RWKV_CHUNK, one_chunk, 0)

    @pl.when(c == pl.num_programs(1) - 1)
    def _():
        for seq in range(RWKV_PROMPT_SEQS):
            for p in range(RWKV_PAIRS):
                s_ref[seq, 2 * p] = s_scr[seq, p, :RWKV_HD, :RWKV_HD]
                s_ref[seq, 2 * p + 1] = s_scr[seq, p, RWKV_HD:, RWKV_HD:]


RWKV_SAMPLE_PROBLEMS = 2


def _rwkv_sample_body(r_ref, lw_ref, k_ref, v_ref, kk_ref, b_ref, s0_ref, y_ref, s_ref):
    zero = jnp.zeros((RWKV_HD, RWKV_HD), F32)
    rows_per = RWKV_SAMPLE_SEQS * DEC_SEQ

    def problem(q):
        def state(g):
            p, seq = divmod(g, RWKV_SAMPLE_SEQS)
            seq += q * RWKV_SAMPLE_SEQS
            return jnp.concatenate([jnp.concatenate([s0_ref[seq, 2 * p], zero], axis=1),
                                    jnp.concatenate([zero, s0_ref[seq, 2 * p + 1]], axis=1)], axis=0)

        def store(g, s):
            p, seq = divmod(g, RWKV_SAMPLE_SEQS)
            seq += q * RWKV_SAMPLE_SEQS
            s_ref[seq, 2 * p] = s[:RWKV_HD, :RWKV_HD]
            s_ref[seq, 2 * p + 1] = s[RWKV_HD:, RWKV_HD:]

        args = [ref[:, q * rows_per:(q + 1) * rows_per, :].reshape(RWKV_STACK, LANES).astype(F32)
                for ref in (r_ref, lw_ref, k_ref, v_ref, kk_ref, b_ref)]
        return (*args, state, store)

    ys = _rwkv_chunks(DEC_SEQ, [problem(q) for q in range(RWKV_SAMPLE_PROBLEMS)])
    for q, y in enumerate(ys):
        y_ref[:, q * rows_per:(q + 1) * rows_per, :] = y.reshape(RWKV_PAIRS, rows_per, LANES)


def rwkv_core(r, lw, k, v, kk, b, state_s, j):
    assert RWKV_PAIRS * RWKV_CHUNK == RWKV_STACK
    nc = SEQ // RWKV_STEP_TOKENS
    st_shape = (RWKV_HEADS, RWKV_HD, RWKV_HD)
    seq_blk = lambda seq: pl.BlockSpec((RWKV_PAIRS, RWKV_STEP_TOKENS, LANES),
                                       lambda bi, c: (0, (bi * RWKV_PROMPT_SEQS + seq) * nc + c, 0))
    inputs = (r, lw, k, v, kk, b)
    y_p, s_p = pl.pallas_call(
        _rwkv_prompt_body,
        grid=(BATCH // RWKV_PROMPT_SEQS, nc),
        in_specs=[seq_blk(seq) for _ in inputs for seq in range(RWKV_PROMPT_SEQS)],
        out_specs=[pl.BlockSpec((RWKV_PAIRS, RWKV_PROMPT_SEQS, RWKV_STEP_TOKENS, LANES), lambda bi, c: (0, bi, c, 0)),
                   pl.BlockSpec((RWKV_PROMPT_SEQS,) + st_shape, lambda bi, c: (bi, 0, 0, 0))],
        out_shape=[jax.ShapeDtypeStruct((RWKV_PAIRS, BATCH, SEQ, LANES), F32),
                   jax.ShapeDtypeStruct((BATCH,) + st_shape, F32)],
        scratch_shapes=[pltpu.VMEM((RWKV_PROMPT_SEQS, RWKV_PAIRS, LANES, LANES), F32)],
        compiler_params=_cparams("parallel", "arbitrary"),
        name="rwkv_prompt",
    )(*[a for a in inputs for _ in range(RWKV_PROMPT_SEQS)])

    step_seqs = RWKV_SAMPLE_PROBLEMS * RWKV_SAMPLE_SEQS
    rows = step_seqs * DEC_SEQ
    row0 = NP // rows
    blk = pl.BlockSpec((RWKV_PAIRS, rows, LANES), lambda i: (0, row0 + i, 0))
    st = pl.BlockSpec((step_seqs,) + st_shape, lambda i: (i, 0, 0, 0))
    st_in = pl.BlockSpec((None, step_seqs) + st_shape, lambda i: (j, i, 0, 0, 0))
    y_s, s_s = pl.pallas_call(
        _rwkv_sample_body,
        grid=(DEC_BATCH // step_seqs,),
        in_specs=[blk] * 6 + [st_in],
        out_specs=[pl.BlockSpec((RWKV_PAIRS, rows, LANES), lambda i: (0, i, 0)), st],
        out_shape=[jax.ShapeDtypeStruct((RWKV_PAIRS, NS, LANES), F32),
                   jax.ShapeDtypeStruct((DEC_BATCH,) + st_shape, F32)],
        compiler_params=_cparams("parallel"),
        name="rwkv_sample",
    )(*inputs, state_s)
    return y_p.reshape(RWKV_PAIRS, NP, LANES), y_s, s_p, s_s


def _rwkv_out_body(x_ref, yp_ref, ys_ref, r_ref, k_ref, v_ref, g_ref, rk_ref, lnw_ref, lnb_ref, bd_ref, wo_ref,
                   o_ref):
    bd = bd_ref[...]
    y = _pick_group(RWKV_ROW_TILE, _load_pairs(yp_ref), _load_pairs(ys_ref))
    yc = y - _seg_sum(y, bd) * (1.0 / RWKV_HD)
    var = _seg_sum(yc * yc, bd) * (1.0 / RWKV_HD)
    yn = yc * lax.rsqrt(var + RWKV_GN_EPS) * lnw_ref[...] + lnb_ref[...]
    rk = _load_pairs(r_ref).astype(F32) * _load_pairs(k_ref).astype(F32)
    bonus = _seg_sum(rk * rk_ref[...], bd) * _load_pairs(v_ref).astype(F32)
    z = (yn + bonus) * g_ref[...].astype(F32)
    o_ref[...] = x_ref[...] + _dot(z, wo_ref[...])


def rwkv_out(x, y_p, y_s, r, k, v, g, p):
    row = pl.BlockSpec((RWKV_ROW_TILE, D_MODEL), lambda i: (i, 0))
    pair_of = lambda f: pl.BlockSpec((RWKV_PAIRS, RWKV_ROW_TILE, LANES), lambda i: (0, f(i), 0))
    full = lambda a: pl.BlockSpec(a.shape, lambda i: (0,) * a.ndim)
    consts = [p["r_k"], p["ln_w"], p["ln_b"], p["bd"], p["w_out"]]
    return pl.pallas_call(
        _rwkv_out_body,
        grid=(NT // RWKV_ROW_TILE,),
        in_specs=[row, *_group_specs(RWKV_ROW_TILE, pair_of)] + [pair_of(lambda i: i)] * 3 + [row]
        + [full(c) for c in consts],
        out_specs=row,
        out_shape=jax.ShapeDtypeStruct((NT, D_MODEL), F32),
        compiler_params=_cparams("parallel"),
        name="rwkv_out",
    )(x, y_p, y_s, r, k, v, g, *consts)


def _pad_cols(w, n):
    return jnp.pad(w, ((0, 0), (0, n - w.shape[1])))


def _pad_rows(w, n):
    return jnp.pad(w, ((0, n - w.shape[0]), (0, 0)))


def kernel(x_prompt, x_sample, state_ret, cache_swa_k, cache_swa_v, state_rwkv_wkv, state_rwkv_shift,
           norm_mix, norm_mlp, norm_final, ret_w_in, ret_w_out,
           swa_w_in, swa_b_in, swa_sinks, swa_w_out,
           rwkv_mu, rwkv_w_rkv, rwkv_w0, rwkv_w1, rwkv_w2, rwkv_a0, rwkv_a1, rwkv_a2,
           rwkv_g1, rwkv_g2, rwkv_k_k, rwkv_k_a, rwkv_r_k, rwkv_ln_w, rwkv_ln_b, rwkv_w_out,
           mlp_w_up, mlp_w_down):
    x = (x_prompt, x_sample.reshape(NS, D_MODEL))
    if LAYER_KINDS[0] != 0:
        x = jnp.concatenate([x_prompt.reshape(NP, D_MODEL), x[1]], axis=0)
    row = lambda t: t.reshape(1, -1)
    k_p, k_s, v_p, v_s, wkv_p, wkv_s, sh_p, sh_s = ([] for _ in range(8))
    ret_states = None
    for layer in range(DEPTH):
        kind = LAYER_KINDS[layer]
        j = LAYER_KINDS[:layer].count(kind)
        g_mix = row(norm_mix[layer])
        if kind == 0:
            proj = ret_proj(x, g_mix, ret_w_in[j].astype(BF16))
            o_p, o_s, *ret_states = retention_core(proj, state_ret, j, ret_states)
            x = matmul_residual(x, o_p, o_s, ret_w_out[j].astype(BF16))
        elif kind == 1:
            proj, k_rot = swa_proj(x, g_mix, swa_w_in[j].astype(BF16), row(swa_b_in[j]))
            o_p, o_s, kc, vc = swa_core(proj, k_rot, swa_sinks[j],
                                        cache_swa_k[j].reshape(DEC_BATCH, WINDOW, SWA_NK),
                                        cache_swa_v[j].reshape(DEC_BATCH, WINDOW, SWA_NK))
            x = matmul_residual(x, o_p, o_s, swa_w_out[j].astype(BF16))
            kv_shape = (-1, WINDOW, SWA_KV_HEADS, SWA_HD)
            last = slice(SEQ // WINDOW - 1, NP // WINDOW, SEQ // WINDOW)
            k_p.append(k_rot.reshape(NT // WINDOW, WINDOW, SWA_NK)[last].reshape(kv_shape))
            v_p.append(proj.reshape(NT // WINDOW, WINDOW, -1)[last, :, SWA_NQ + SWA_NK:]
                       .astype(F32).reshape(kv_shape))
            k_s.append(kc.reshape(kv_shape))
            v_s.append(vc.reshape(kv_shape))
        else:
            lora_w = LANES
            lora_g = 2 * LANES
            p = dict(
                mu=rwkv_mu[j], wr=rwkv_w_rkv[j, 0].astype(BF16), wk=rwkv_w_rkv[j, 1].astype(BF16),
                wv=rwkv_w_rkv[j, 2].astype(BF16), w0=row(rwkv_w0[j]),
                w1=_pad_cols(rwkv_w1[j], lora_w).astype(BF16), w2=_pad_rows(rwkv_w2[j], lora_w).astype(BF16),
                a0=row(rwkv_a0[j]),
                a1=_pad_cols(rwkv_a1[j], lora_w).astype(BF16), a2=_pad_rows(rwkv_a2[j], lora_w).astype(BF16),
                g1=_pad_cols(rwkv_g1[j], lora_g).astype(BF16), g2=_pad_rows(rwkv_g2[j], lora_g).astype(BF16),
                k_k=row(rwkv_k_k[j]), k_a=row(rwkv_k_a[j]), r_k=row(rwkv_r_k[j]),
                ln_w=row(rwkv_ln_w[j]), ln_b=row(rwkv_ln_b[j]), bd=_block_diag_ones(),
                w_out=rwkv_w_out[j].astype(BF16), g_mix=g_mix)
            r, lw, k, v, kk, b, g, h_last = rwkv_proj(x, state_rwkv_shift[j], p)
            y_p, y_s, s_p, s_s = rwkv_core(r, lw, k, v, kk, b, state_rwkv_wkv, j)
            x = rwkv_out(x, y_p, y_s, r, k, v, g, p)
            wkv_p.append(s_p)
            wkv_s.append(s_s)
            sh_p.append(h_last[SEQ // SUBLANES - 1:NP // SUBLANES:SEQ // SUBLANES])
            sh_s.append(h_last[NP // SUBLANES:])
        x = mlp(x, row(norm_mlp[layer]), mlp_w_up, mlp_w_down, layer,
                final_g=row(norm_final) if layer == DEPTH - 1 else None)
    y_p, y_s = x
    return (y_p.reshape(BATCH, SEQ, D_MODEL), y_s.reshape(DEC_BATCH, DEC_SEQ, D_MODEL),
            ret_states[0], ret_states[1], jnp.stack(k_p), jnp.stack(k_s), jnp.stack(v_p), jnp.stack(v_s),
            jnp.stack(wkv_p), jnp.stack(wkv_s), jnp.stack(sh_p), jnp.stack(sh_s))
```

```python
import functools
import math

import jax
import jax.numpy as jnp
from jax import lax
from jax.experimental import pallas as pl
from jax.experimental.pallas import tpu as pltpu

F32 = jnp.float32
BF16 = jnp.bfloat16

D_MODEL = 1024
BATCH = 8
SEQ = 2048
DEPTH = 4
DEC_BATCH = 128
DEC_SEQ = 8
PAST_LEN = 8192
LAYER_KINDS = tuple(i % 3 for i in range(DEPTH))
NORM_EPS = 1e-6
ROPE_THETA = 10000.0
NEG_INF = -1e30
D_FF = 4 * D_MODEL

RET_HEADS = 4
RET_DK = D_MODEL // RET_HEADS
RET_DV = 2 * D_MODEL // RET_HEADS
RET_CHUNK = 128
RET_PROMPT_CHUNK = 256
RET_NQ = RET_HEADS * RET_DK
RET_NV = RET_HEADS * RET_DV

SWA_HEADS = 16
SWA_KV_HEADS = 4
SWA_GROUP = SWA_HEADS // SWA_KV_HEADS
SWA_HD = D_MODEL // SWA_HEADS
WINDOW = 128
SWA_BLOCK = 128
SWA_NQ = SWA_HEADS * SWA_HD
SWA_NK = SWA_KV_HEADS * SWA_HD

RWKV_HD = 64
RWKV_HEADS = D_MODEL // RWKV_HD
RWKV_GN_EPS = 64e-5
RWKV_CHUNK = 32

NP = BATCH * SEQ
NS = DEC_BATCH * DEC_SEQ
NT = NP + NS

VMEM_LIMIT_V7X = 48 * 1024 * 1024
STREAMED_F32_VMEM_LIMIT_V7X = 56 * 1024 * 1024
LANES = 128
SUBLANES = 8
MXU_DIM_V7X = 256

ROW_TILE = 1024


def _cparams(*sem, vmem=VMEM_LIMIT_V7X):
    return pltpu.CompilerParams(dimension_semantics=sem, vmem_limit_bytes=vmem)


def _dot(a, b):
    return jnp.dot(a.astype(BF16), b.astype(BF16), preferred_element_type=F32)


def _dot_nt(a, b):
    return lax.dot_general(a.astype(BF16), b.astype(BF16), (((1,), (1,)), ((), ())),
                           preferred_element_type=F32)


def _dot_tn(a, b):
    return lax.dot_general(a.astype(BF16), b.astype(BF16), (((0,), (0,)), ((), ())),
                           preferred_element_type=F32)


def _div_pow2(x, n):
    assert n & (n - 1) == 0
    return jnp.right_shift(x, n.bit_length() - 1)


def _mod_pow2(x, n):
    assert n & (n - 1) == 0
    return jnp.bitwise_and(x, n - 1)


def _sigmoid(x):
    return 0.5 * jnp.tanh(0.5 * x) + 0.5


def _rms(x, g):
    ms = jnp.mean(x * x, axis=-1, keepdims=True)
    return x * lax.rsqrt(ms + NORM_EPS) * g


def _split3(x):
    hi = x.astype(BF16)
    r1 = x - hi.astype(F32)
    mid = r1.astype(BF16)
    lo = (r1 - mid.astype(F32)).astype(BF16)
    return hi, mid, lo


def _seg_sum(x, bd):
    xb = x.astype(BF16)
    return jnp.concatenate(
        [jnp.dot(xb[:, j * MXU_DIM_V7X:(j + 1) * MXU_DIM_V7X], bd, preferred_element_type=F32)
         for j in range(D_MODEL // MXU_DIM_V7X)], axis=-1)


def _rope_tables(pos, half):
    inv_freq = jnp.power(ROPE_THETA, -jnp.arange(half, dtype=F32) / half)
    ang = pos.astype(F32)[:, None] * inv_freq[None, :]
    return jnp.cos(ang), jnp.sin(ang)


def _group_specs(tile, block_of):
    n_prompt = NP // tile
    return (block_of(lambda i: jnp.minimum(i, n_prompt - 1)), block_of(lambda i: jnp.maximum(i - n_prompt, 0)))


def _pick_group(tile, p_val, s_val):
    return jnp.where(pl.program_id(0) < NP // tile, p_val, s_val)


def _token_rows(x, tile):
    if isinstance(x, tuple):
        per_seq = SEQ // tile
        f_p, f_s = _group_specs(tile, lambda f: f)
        return list(x), [pl.BlockSpec((None, tile, D_MODEL), lambda *g: (f_p(g[0]) // per_seq, f_p(g[0]) % per_seq, 0)),
                         pl.BlockSpec((tile, D_MODEL), lambda *g: (f_s(g[0]), 0))]
    return [x], [pl.BlockSpec((tile, D_MODEL), lambda *g: (g[0], 0))]


def _read_rows(refs, tile):
    return refs[0][...] if len(refs) == 1 else _pick_group(tile, refs[0][...], refs[1][...])


def _rope_table_inputs(half, widen, tile):
    tp = widen(*_rope_tables(jnp.arange(SEQ, dtype=jnp.int32), half))
    ts = [jnp.tile(t, (DEC_BATCH, 1))
          for t in widen(*_rope_tables(PAST_LEN + jnp.arange(DEC_SEQ, dtype=jnp.int32), half))]
    n_prompt = NP // tile
    per_seq = SEQ // tile
    p_spec = pl.BlockSpec((tile, LANES), lambda *g: (g[0] % per_seq, 0))
    s_spec = pl.BlockSpec((tile, LANES), lambda *g: (jnp.maximum(g[0] - n_prompt, 0), 0))
    return [tp[0], tp[1], ts[0], ts[1]], [p_spec, p_spec, s_spec, s_spec]


RET_PROJ_TILE = 2048


def _ret_proj_body(n_x, *refs):
    x_refs, (g_ref, w_ref, o_ref, xn_ref) = refs[:n_x], refs[n_x:]

    @pl.when(pl.program_id(1) == 0)
    def _():
        xn_ref[...] = _rms(_read_rows(x_refs, ROW_TILE), g_ref[...]).astype(BF16)

    o_ref[...] = jnp.dot(xn_ref[...], w_ref[...], preferred_element_type=F32).astype(BF16)


def ret_proj(x, g, w):
    n_out = w.shape[1]
    x_arrays, x_specs = _token_rows(x, ROW_TILE)
    return pl.pallas_call(
        functools.partial(_ret_proj_body, len(x_arrays)),
        grid=(NT // ROW_TILE, n_out // RET_PROJ_TILE),
        in_specs=x_specs + [
            pl.BlockSpec((1, D_MODEL), lambda i, j: (0, 0)),
            pl.BlockSpec((D_MODEL, RET_PROJ_TILE), lambda i, j: (0, j)),
        ],
        out_specs=pl.BlockSpec((ROW_TILE, RET_PROJ_TILE), lambda i, j: (i, j)),
        out_shape=jax.ShapeDtypeStruct((NT, n_out), BF16),
        scratch_shapes=[pltpu.VMEM((ROW_TILE, D_MODEL), BF16)],
        compiler_params=_cparams("parallel", "arbitrary"),
        name="ret_proj",
    )(*x_arrays, g, w)


SWA_PROJ_ROW_TILE = 512


def _swa_rope(x, cos, sin):
    n = x.shape[1]
    half = SWA_HD // 2
    lane = lax.broadcasted_iota(jnp.int32, x.shape, 1)
    partner = jnp.where(_mod_pow2(lane, SWA_HD) < half, pltpu.roll(x, n - half, 1), pltpu.roll(x, half, 1))
    reps = n // LANES
    return x * jnp.concatenate([cos] * reps, axis=-1) + partner * jnp.concatenate([sin] * reps, axis=-1)


def _swa_proj_body(x_ref, g_ref, w_ref, b_ref, cosp_ref, sinp_ref, coss_ref, sins_ref, o_ref, kr_ref):
    xn = _rms(x_ref[...], g_ref[...]).astype(BF16)
    acc = jnp.dot(xn, w_ref[...], preferred_element_type=F32) + b_ref[...]
    n_qk = SWA_NQ + SWA_NK
    cos = _pick_group(SWA_PROJ_ROW_TILE, cosp_ref[...], coss_ref[...])
    sin = _pick_group(SWA_PROJ_ROW_TILE, sinp_ref[...], sins_ref[...])
    qk = _swa_rope(acc[:, :n_qk], cos, sin)
    assert math.log2(SWA_HD ** 0.5).is_integer()
    o_ref[:, :SWA_NQ] = (qk[:, :SWA_NQ] * (SWA_HD ** -0.5)).astype(BF16)
    o_ref[:, SWA_NQ:n_qk] = qk[:, SWA_NQ:].astype(BF16)
    o_ref[:, n_qk:] = acc[:, n_qk:].astype(BF16)
    kr_ref[...] = qk[:, SWA_NQ:]


def swa_proj(x, g, w, b):
    n_out = w.shape[1]

    def widen(c, s):
        rep = LANES // SWA_HD
        return jnp.tile(jnp.concatenate([c, c], axis=-1), (1, rep)), jnp.tile(jnp.concatenate([-s, s], axis=-1), (1, rep))

    tile = SWA_PROJ_ROW_TILE
    tables, table_specs = _rope_table_inputs(SWA_HD // 2, widen, tile)
    return pl.pallas_call(
        _swa_proj_body,
        grid=(NT // tile,),
        in_specs=[
            pl.BlockSpec((tile, D_MODEL), lambda i: (i, 0)),
            pl.BlockSpec((1, D_MODEL), lambda i: (0, 0)),
            pl.BlockSpec((D_MODEL, n_out), lambda i: (0, 0)),
            pl.BlockSpec((1, n_out), lambda i: (0, 0)),
        ] + table_specs,
        out_specs=[pl.BlockSpec((tile, n_out), lambda i: (i, 0)), pl.BlockSpec((tile, SWA_NK), lambda i: (i, 0))],
        out_shape=[jax.ShapeDtypeStruct((NT, n_out), BF16), jax.ShapeDtypeStruct((NT, SWA_NK), F32)],
        compiler_params=_cparams("parallel"),
        name="swa_proj",
    )(x, g, w, b, *tables)


def _matmul_residual_body(n_x, *refs):
    x_refs, (ap_ref, as_ref, w_ref, o_ref) = refs[:n_x], refs[n_x:]
    a = _pick_group(ROW_TILE, ap_ref[...], as_ref[...])
    o_ref[...] = _read_rows(x_refs, ROW_TILE) + jnp.dot(a, w_ref[...], preferred_element_type=F32)


def matmul_residual(x, a_p, a_s, w):
    k = a_p.shape[1]
    x_arrays, x_specs = _token_rows(x, ROW_TILE)
    return pl.pallas_call(
        functools.partial(_matmul_residual_body, len(x_arrays)),
        grid=(NT // ROW_TILE,),
        in_specs=x_specs + [
            *_group_specs(ROW_TILE, lambda f: pl.BlockSpec((ROW_TILE, k), lambda i: (f(i), 0))),
            pl.BlockSpec((k, D_MODEL), lambda i: (0, 0)),
        ],
        out_specs=pl.BlockSpec((ROW_TILE, D_MODEL), lambda i: (i, 0)),
        out_shape=jax.ShapeDtypeStruct((NT, D_MODEL), F32),
        compiler_params=_cparams("parallel"),
        name="matmul_residual",
    )(*x_arrays, a_p, a_s, w)


MLP_FF_TILE = 1024


def _mlp_body(has_final_norm, x_ref, g_ref, wu_ref, wd_ref, *rest):
    if has_final_norm:
        gf_ref, yp_ref, ys_ref, xn_ref, acc_ref = rest
    else:
        acc_ref, xn_ref = rest
    f = pl.program_id(1)

    @pl.when(f == 0)
    def _():
        x = x_ref[...]
        xn_ref[...] = _rms(x, g_ref[...]).astype(BF16)
        acc_ref[...] = x

    a = jnp.maximum(jnp.dot(xn_ref[...], wu_ref[...].astype(BF16), preferred_element_type=F32), 0.0)
    acc_ref[...] += jnp.dot((a * a).astype(BF16), wd_ref[...].astype(BF16), preferred_element_type=F32)

    if has_final_norm:
        @pl.when(f == pl.num_programs(1) - 1)
        def _():
            y = _rms(acc_ref[...], gf_ref[...])
            is_prompt = pl.program_id(0) < NP // ROW_TILE

            @pl.when(is_prompt)
            def _():
                yp_ref[...] = y

            @pl.when(jnp.logical_not(is_prompt))
            def _():
                ys_ref[...] = y


def mlp(x, g, w_up, w_down, layer, final_g=None):
    row = pl.BlockSpec((ROW_TILE, D_MODEL), lambda i, f: (i, 0))
    vec = pl.BlockSpec((1, D_MODEL), lambda i, f: (0, 0))
    if final_g is None:
        extra_in, extra_specs = [], []
        out_specs = row
        out_shape = jax.ShapeDtypeStruct((NT, D_MODEL), F32)
    else:
        extra_in, extra_specs = [final_g], [vec]
        out_specs = list(_group_specs(ROW_TILE, lambda g_: pl.BlockSpec((ROW_TILE, D_MODEL), lambda i, f: (g_(i), 0))))
        out_shape = [jax.ShapeDtypeStruct((NP, D_MODEL), F32), jax.ShapeDtypeStruct((NS, D_MODEL), F32)]
    return pl.pallas_call(
        functools.partial(_mlp_body, final_g is not None),
        grid=(NT // ROW_TILE, D_FF // MLP_FF_TILE),
        in_specs=[
            row, vec,
            pl.BlockSpec((None, D_MODEL, MLP_FF_TILE), lambda i, f: (layer, 0, f)),
            pl.BlockSpec((None, MLP_FF_TILE, D_MODEL), lambda i, f: (layer, f, 0)),
        ] + extra_specs,
        out_specs=out_specs,
        out_shape=out_shape,
        scratch_shapes=[pltpu.VMEM((ROW_TILE, D_MODEL), BF16)]
        + ([] if final_g is None else [pltpu.VMEM((ROW_TILE, D_MODEL), F32)]),
        compiler_params=_cparams("parallel" if final_g is None else "arbitrary", "arbitrary", vmem=STREAMED_F32_VMEM_LIMIT_V7X),
        name="mlp",
    )(x, g, w_up, w_down, *extra_in)


def _ret_tables(c):
    log_gamma = jnp.log1p(-jnp.exp2(-5.0 - jnp.arange(RET_HEADS, dtype=F32)))
    idx = jnp.arange(c, dtype=F32)
    diff = idx[:, None] - idx[None, :]
    inner = jnp.where(diff >= 0, jnp.exp(log_gamma[:, None, None] * jnp.maximum(diff, 0.0)), 0.0)
    q_decay = jnp.exp(log_gamma[:, None] * (idx[None, :] + 1.0))
    k_decay = jnp.exp(log_gamma[:, None] * (c - 1.0 - idx[None, :]))
    chunk_decay = jnp.exp(log_gamma * c)
    rep = lambda t: jnp.broadcast_to(t[:, :, None], (RET_HEADS, c, LANES))
    return inner, rep(q_decay), rep(k_decay), chunk_decay


def _ret_heads(rows, lo, width):
    return jnp.stack([rows[:, lo + h * width:lo + (h + 1) * width] for h in range(RET_HEADS)])


def _ret_chunk(rows_list, s0, inner, qd, kd, cd_ref, cos, sin):
    n_seq = len(rows_list)
    cat = lambda f: jnp.concatenate([f(r) for r in rows_list], axis=0)

    def rope(x):
        x1, x2 = x[..., :RET_DK // 2], x[..., RET_DK // 2:]
        return jnp.concatenate([x1 * cos - x2 * sin, x2 * cos + x1 * sin], axis=-1)

    q = rope(cat(lambda r: _ret_heads(r, 0, RET_DK)).astype(F32)).astype(BF16)
    k = rope(cat(lambda r: _ret_heads(r, RET_NQ, RET_DK)).astype(F32)) * (RET_DK ** -0.5)
    v = cat(lambda r: _ret_heads(r, 2 * RET_NQ, RET_DV)).astype(BF16)
    g = cat(lambda r: _ret_heads(r, 2 * RET_NQ + RET_NV, RET_DV)).astype(F32)
    per_seq = lambda t: jnp.concatenate([t] * n_seq, axis=0)
    inner, qd, kd = per_seq(inner), per_seq(qd), per_seq(kd)
    cd = per_seq(jnp.stack([jnp.full((1, 1), cd_ref[h], F32) for h in range(RET_HEADS)]))
    qd = jnp.concatenate([qd] * (RET_DV // LANES), axis=-1)
    kd = jnp.concatenate([kd] * (RET_DK // LANES), axis=-1)
    scores = jnp.einsum('bqd,bkd->bqk', q, k.astype(BF16), preferred_element_type=F32) * inner
    o = (jnp.einsum('bqk,bke->bqe', scores.astype(BF16), v, preferred_element_type=F32)
         + jnp.einsum('bqd,bde->bqe', q, s0.astype(BF16), preferred_element_type=F32) * qd)
    s_new = s0 * cd + jnp.einsum('bkd,bke->bde', (k * kd).astype(BF16), v, preferred_element_type=F32)
    o = o * lax.rsqrt(jnp.mean(o * o, axis=-1, keepdims=True) + NORM_EPS)
    return o * (g * _sigmoid(g)), s_new


RET_SAMPLE_SEQS = 2


RET_STATE_RING = 3


def _retention_body(layer, cdp_ref, cds_ref, projp_ref, innerp_ref, qdp_ref, kdp_ref, cosp_ref, sinp_ref,
                    projs_ref, inners_ref, qds_ref, kds_ref, coss_ref, sins_ref, s0_hbm, *rest):
    op_ref, sp_ref, os_ref, ss_ref, s_scr, s0_ring, s0_sem = rest[-7:]
    step, n_steps = pl.program_id(0), pl.num_programs(0)
    c = step % (SEQ // RET_PROMPT_CHUNK)

    def fetch(s):
        slot = s % RET_STATE_RING
        return pltpu.make_async_copy(s0_hbm.at[layer, pl.ds(s * RET_SAMPLE_SEQS, RET_SAMPLE_SEQS)],
                                     s0_ring.at[slot], s0_sem.at[slot])

    @pl.when(step == 0)
    def _():
        for s in range(RET_STATE_RING - 1):
            fetch(s).start()

    @pl.when(step + RET_STATE_RING - 1 < n_steps)
    def _():
        fetch(step + RET_STATE_RING - 1).start()

    @pl.when(c == 0)
    def _():
        s_scr[...] = jnp.zeros_like(s_scr)

    o, s_new = _ret_chunk([projp_ref[...]], s_scr[...], innerp_ref[...], qdp_ref[...], kdp_ref[...], cdp_ref,
                          cosp_ref[...], sinp_ref[...])
    s_scr[...] = s_new
    op_ref[...] = jnp.concatenate([o[h] for h in range(RET_HEADS)], axis=-1).astype(BF16)

    @pl.when(c == SEQ // RET_PROMPT_CHUNK - 1)
    def _():
        sp_ref[0, 0] = s_scr[...]

    rows = projs_ref[...].astype(F32)
    batch = RET_SAMPLE_SEQS * RET_HEADS
    fetch(step).wait()
    s0 = s0_ring[step % RET_STATE_RING]
    o, s_new = _ret_chunk([rows[seq * DEC_SEQ:(seq + 1) * DEC_SEQ] for seq in range(RET_SAMPLE_SEQS)],
                          s0.reshape(batch, RET_DK, RET_DV), inners_ref[...], qds_ref[...], kds_ref[...],
                          cds_ref, coss_ref[...], sins_ref[...])
    ss_ref[0] = s_new.reshape(RET_SAMPLE_SEQS, RET_HEADS, RET_DK, RET_DV)
    os_ref[...] = jnp.concatenate(
        [jnp.concatenate([o[seq * RET_HEADS + h] for h in range(RET_HEADS)], axis=-1)
         for seq in range(RET_SAMPLE_SEQS)], axis=0).astype(BF16)


def retention_core(proj, state_ret, j, stacks):
    n_in = proj.shape[1]
    nc = SEQ // RET_PROMPT_CHUNK
    n_steps = BATCH * nc
    assert n_steps == DEC_BATCH // RET_SAMPLE_SEQS
    assert math.gcd(DEC_SEQ, RET_CHUNK) == DEC_SEQ
    smem = pl.BlockSpec(memory_space=pltpu.SMEM)
    full3 = lambda t: pl.BlockSpec(t.shape, lambda i: (0, 0, 0))
    inner_p, qd_p, kd_p, cd_p = _ret_tables(RET_PROMPT_CHUNK)
    inner_s, qd_s, kd_s, cd_s = _ret_tables(DEC_SEQ)
    cos_p, sin_p = _rope_tables(jnp.arange(SEQ, dtype=jnp.int32), RET_DK // 2)
    cos_s, sin_s = _rope_tables(PAST_LEN + jnp.arange(DEC_SEQ, dtype=jnp.int32), RET_DK // 2)
    rows = RET_SAMPLE_SEQS * DEC_SEQ
    row0 = NP // rows
    st = pl.BlockSpec((1, RET_SAMPLE_SEQS, RET_HEADS, RET_DK, RET_DV), lambda i: (j, i, 0, 0, 0))
    stacked = [] if stacks is None else list(stacks)
    o_p, s_p, o_s, s_s = pl.pallas_call(
        functools.partial(_retention_body, j),
        grid=(n_steps,),
        in_specs=[
            smem, smem,
            pl.BlockSpec((RET_PROMPT_CHUNK, n_in), lambda i: (i, 0)),
            full3(inner_p), full3(qd_p), full3(kd_p),
            pl.BlockSpec((RET_PROMPT_CHUNK, LANES), lambda i: (i % nc, 0)),
            pl.BlockSpec((RET_PROMPT_CHUNK, LANES), lambda i: (i % nc, 0)),
            pl.BlockSpec((rows, n_in), lambda i: (row0 + i, 0)),
            full3(inner_s), full3(qd_s), full3(kd_s),
            pl.BlockSpec((DEC_SEQ, LANES), lambda i: (0, 0)),
            pl.BlockSpec((DEC_SEQ, LANES), lambda i: (0, 0)),
            pl.BlockSpec(memory_space=pl.ANY),
        ] + [pl.BlockSpec(memory_space=pl.ANY)] * len(stacked),
        out_specs=[
            pl.BlockSpec((RET_PROMPT_CHUNK, RET_NV), lambda i: (i, 0)),
            pl.BlockSpec((1, 1, RET_HEADS, RET_DK, RET_DV), lambda i: (j, i // nc, 0, 0, 0)),
            pl.BlockSpec((rows, RET_NV), lambda i: (i, 0)),
            st,
        ],
        out_shape=[jax.ShapeDtypeStruct((NP, RET_NV), BF16),
                   jax.ShapeDtypeStruct((state_ret.shape[0], BATCH, RET_HEADS, RET_DK, RET_DV), F32),
                   jax.ShapeDtypeStruct((NS, RET_NV), BF16),
                   jax.ShapeDtypeStruct(state_ret.shape, F32)],
        input_output_aliases={15: 1, 16: 3} if stacked else {},
        scratch_shapes=[pltpu.VMEM((RET_HEADS, RET_DK, RET_DV), F32),
                        pltpu.VMEM((RET_STATE_RING, RET_SAMPLE_SEQS, RET_HEADS, RET_DK, RET_DV), F32),
                        pltpu.SemaphoreType.DMA((RET_STATE_RING,))],
        compiler_params=_cparams("arbitrary"),
        name="retention",
    )(cd_p, cd_s, proj, inner_p, qd_p, kd_p, cos_p, sin_p, proj, inner_s, qd_s, kd_s, cos_s, sin_s, state_ret,
      *stacked)
    return o_p, o_s, s_p, s_s


def _sink_attention(q, k, v, valid_t, sink_ref):
    t = q.shape[0]
    head_cols = lambda x, h: x[:, h * SWA_HD:(h + 1) * SWA_HD]
    qs = jnp.stack([jnp.concatenate([head_cols(q, kh * SWA_GROUP + g) for g in range(SWA_GROUP)], axis=0)
                    for kh in range(SWA_KV_HEADS)]).astype(BF16)
    ks = jnp.stack([head_cols(k, kh) for kh in range(SWA_KV_HEADS)]).astype(BF16)
    vs = jnp.stack([head_cols(v, kh) for kh in range(SWA_KV_HEADS)]).astype(BF16)
    sink = jnp.stack([jnp.concatenate([jnp.full((1, t), sink_ref[kh * SWA_GROUP + g], F32)
                                       for g in range(SWA_GROUP)], axis=1) for kh in range(SWA_KV_HEADS)])
    s = jnp.einsum('hkd,hqd->hkq', ks, qs, preferred_element_type=F32)
    s = jnp.where(valid_t[None], s, NEG_INF)
    m = jnp.maximum(jnp.max(s, axis=1, keepdims=True), sink)
    p = jnp.exp(s - m)
    denom = jnp.sum(p, axis=1, keepdims=True) + jnp.exp(sink - m)
    o_t = jnp.einsum('hkd,hkq->hdq', vs, p.astype(BF16), preferred_element_type=F32) / denom
    o = [o_t[kh].T for kh in range(SWA_KV_HEADS)]
    return jnp.concatenate([o[h // SWA_GROUP][(h % SWA_GROUP) * t:(h % SWA_GROUP + 1) * t]
                            for h in range(SWA_HEADS)], axis=-1)


def _swa_prompt_body(sink_ref, q_ref, kc_ref, vc_ref, kp_ref, vp_ref, o_ref):
    blk = pl.program_id(1)
    q = q_ref[...]
    k_all = jnp.concatenate([kp_ref[...], kc_ref[...]], axis=0)
    v_all = jnp.concatenate([vp_ref[...], vc_ref[...]], axis=0)
    shape = (2 * SWA_BLOCK, SWA_GROUP * SWA_BLOCK)
    i = _mod_pow2(lax.broadcasted_iota(jnp.int32, shape, 1), SWA_BLOCK)
    j = lax.broadcasted_iota(jnp.int32, shape, 0)
    rel = i + SWA_BLOCK - j
    valid = (rel >= 0) & (rel <= WINDOW) & (j >= jnp.where(blk > 0, 0, SWA_BLOCK))
    o_ref[...] = _sink_attention(q, k_all, v_all, valid, sink_ref).astype(BF16)


SWA_SAMPLE_SEQS = 8


def _swa_sample_body(sink_ref, q_ref, kn_ref, vn_ref, kc_ref, vc_ref, o_ref, ko_ref, vo_ref):
    rows = SWA_SAMPLE_SEQS * DEC_SEQ
    n_cache = SWA_SAMPLE_SEQS * WINDOW
    q = q_ref[...]
    kn = kn_ref[...]
    vn = vn_ref[...].astype(F32)
    k_all = jnp.concatenate([kc_ref[...].reshape(n_cache, SWA_NK), kn], axis=0)
    v_all = jnp.concatenate([vc_ref[...].reshape(n_cache, SWA_NK), vn], axis=0)
    for b in range(SWA_SAMPLE_SEQS):
        new = slice(b * DEC_SEQ, (b + 1) * DEC_SEQ)
        ko_ref[b, :WINDOW - DEC_SEQ, :] = kc_ref[b, DEC_SEQ:, :]
        ko_ref[b, WINDOW - DEC_SEQ:, :] = kn[new]
        vo_ref[b, :WINDOW - DEC_SEQ, :] = vc_ref[b, DEC_SEQ:, :]
        vo_ref[b, WINDOW - DEC_SEQ:, :] = vn[new]
    shape = (n_cache + rows, SWA_GROUP * rows)
    r = _mod_pow2(lax.broadcasted_iota(jnp.int32, shape, 1), rows)
    c = lax.broadcasted_iota(jnp.int32, shape, 0)
    q_seq, q_t = _div_pow2(r, DEC_SEQ), _mod_pow2(r, DEC_SEQ)
    is_new = c >= n_cache
    k_seq = jnp.where(is_new, _div_pow2(c - n_cache, DEC_SEQ), _div_pow2(c, WINDOW))
    k_slot = jnp.where(is_new, WINDOW + _mod_pow2(c - n_cache, DEC_SEQ), _mod_pow2(c, WINDOW))
    rel = q_t + WINDOW - k_slot
    valid = (q_seq == k_seq) & (rel >= 0) & (rel <= WINDOW)
    o_ref[...] = _sink_attention(q, k_all, v_all, valid, sink_ref).astype(BF16)


def swa_core(proj, k_rot, sinks, cache_k, cache_v):
    nb = SEQ // SWA_BLOCK
    smem = pl.BlockSpec(memory_space=pltpu.SMEM)
    kcol = SWA_NQ // SWA_NK
    cur = lambda b, i: b * nb + i
    prev = lambda b, i: b * nb + jnp.maximum(i - 1, 0)
    o = pl.pallas_call(
        _swa_prompt_body,
        grid=(BATCH, nb),
        in_specs=[
            smem,
            pl.BlockSpec((SWA_BLOCK, SWA_NQ), lambda b, i: (cur(b, i), 0)),
            pl.BlockSpec((SWA_BLOCK, SWA_NK), lambda b, i: (cur(b, i), kcol)),
            pl.BlockSpec((SWA_BLOCK, SWA_NK), lambda b, i: (cur(b, i), kcol + 1)),
            pl.BlockSpec((SWA_BLOCK, SWA_NK), lambda b, i: (prev(b, i), kcol)),
            pl.BlockSpec((SWA_BLOCK, SWA_NK), lambda b, i: (prev(b, i), kcol + 1)),
        ],
        out_specs=pl.BlockSpec((SWA_BLOCK, SWA_NQ), lambda b, i: (cur(b, i), 0)),
        out_shape=jax.ShapeDtypeStruct((NP, SWA_NQ), BF16),
        compiler_params=_cparams("parallel", "arbitrary"),
        name="swa_prompt",
    )(sinks, proj, proj, proj, proj, proj)

    rows = SWA_SAMPLE_SEQS * DEC_SEQ
    row0 = NP // rows
    o_s, k_s, v_s = pl.pallas_call(
        _swa_sample_body,
        grid=(DEC_BATCH // SWA_SAMPLE_SEQS,),
        in_specs=[
            smem,
            pl.BlockSpec((rows, SWA_NQ), lambda g: (row0 + g, 0)),
            pl.BlockSpec((rows, SWA_NK), lambda g: (row0 + g, 0)),
            pl.BlockSpec((rows, SWA_NK), lambda g: (row0 + g, kcol + 1)),
            pl.BlockSpec((SWA_SAMPLE_SEQS, WINDOW, SWA_NK), lambda g: (g, 0, 0)),
            pl.BlockSpec((SWA_SAMPLE_SEQS, WINDOW, SWA_NK), lambda g: (g, 0, 0)),
        ],
        out_specs=[
            pl.BlockSpec((rows, SWA_NQ), lambda g: (g, 0)),
            pl.BlockSpec((SWA_SAMPLE_SEQS, WINDOW, SWA_NK), lambda g: (g, 0, 0)),
            pl.BlockSpec((SWA_SAMPLE_SEQS, WINDOW, SWA_NK), lambda g: (g, 0, 0)),
        ],
        out_shape=[jax.ShapeDtypeStruct((NS, SWA_NQ), BF16),
                   jax.ShapeDtypeStruct((DEC_BATCH, WINDOW, SWA_NK), F32),
                   jax.ShapeDtypeStruct((DEC_BATCH, WINDOW, SWA_NK), F32)],
        compiler_params=_cparams("parallel"),
        name="swa_sample",
    )(sinks, proj, k_rot, proj, cache_k, cache_v)
    return o, o_s, k_s, v_s


RWKV_ROW_TILE = 256
RWKV_PAIRS = RWKV_HEADS // 2
RWKV_STACK = MXU_DIM_V7X
RWKV_SAMPLE_SEQS = RWKV_STACK // (RWKV_PAIRS * DEC_SEQ)


def _block_diag_ones():
    r = jnp.arange(MXU_DIM_V7X) // RWKV_HD
    return (r[:, None] == r[None, :]).astype(BF16)


def _store_pairs(ref, val):
    for p in range(RWKV_PAIRS):
        ref[p] = val[:, p * LANES:(p + 1) * LANES].astype(ref.dtype)


def _load_pairs(ref):
    return jnp.concatenate([ref[p] for p in range(RWKV_PAIRS)], axis=-1)


def _rwkv_proj_body(x_ref, xprev_ref, first_ref, gn_ref, mu_ref, wr_ref, wk_ref, wv_ref, w0_ref, w1_ref, w2_ref,
                    a0_ref, a1_ref, a2_ref, g1_ref, g2_ref, kk_ref, ka_ref, bd_ref,
                    r_out, lw_out, k_out, v_out, kk_out, b_out, g_out, hlast_out, h_scr):
    i = pl.program_id(0)
    h = _rms(x_ref[...], gn_ref[...])
    prev_tile_last = _rms(xprev_ref[...], gn_ref[...])[SUBLANES - 1:SUBLANES, :]
    rowid = lax.broadcasted_iota(jnp.int32, h.shape, 0)
    xp = jnp.where(rowid == 0, prev_tile_last, pltpu.roll(h, 1, 0))
    seq_len_mask = jnp.where(i >= NP // RWKV_ROW_TILE, DEC_SEQ - 1, SEQ - 1)
    is_first = jnp.bitwise_and(i * RWKV_ROW_TILE + rowid, seq_len_mask) == 0
    xp = jnp.where(is_first, first_ref[...], xp)
    _store_pairs(h_scr, h)
    every8th = pl.ds(SUBLANES - 1, RWKV_ROW_TILE // SUBLANES, stride=SUBLANES)
    hlast_out[...] = jnp.concatenate([h_scr[p, every8th, :] for p in range(RWKV_PAIRS)], axis=-1)
    d = xp - h
    xs = [h + d * mu_ref[i:i + 1, :] for i in range(6)]
    _store_pairs(r_out, _dot(xs[0], wr_ref[...]))
    k = _dot(xs[1], wk_ref[...])
    _store_pairs(v_out, _dot(xs[2], wv_ref[...]))
    z = w0_ref[...] + _dot(jnp.tanh(_dot(xs[3], w1_ref[...])), w2_ref[...])
    _store_pairs(lw_out, -math.exp(-0.5) * _sigmoid(z))
    a = _sigmoid(a0_ref[...] + _dot(_dot(xs[4], a1_ref[...]), a2_ref[...]))
    g_out[...] = _dot(_sigmoid(_dot(xs[5], g1_ref[...])), g2_ref[...]).astype(BF16)
    kk = k * kk_ref[...]
    kk = kk * jnp.minimum(lax.rsqrt(_seg_sum(kk * kk, bd_ref[...])), 1e12)
    _store_pairs(kk_out, kk)
    _store_pairs(b_out, kk * a)
    _store_pairs(k_out, k * (1.0 + (a - 1.0) * ka_ref[...]))


def rwkv_proj(x, shift_s, p):
    first = jnp.concatenate([
        jnp.zeros((RWKV_ROW_TILE, D_MODEL), F32),
        jnp.pad(shift_s[:, None, :], ((0, 0), (0, DEC_SEQ - 1), (0, 0))).reshape(NS, D_MODEL)], axis=0)
    prompt_tiles = NP // RWKV_ROW_TILE
    row = pl.BlockSpec((RWKV_ROW_TILE, D_MODEL), lambda i: (i, 0))
    prev = pl.BlockSpec((SUBLANES, D_MODEL), lambda i: (jnp.maximum(i * (RWKV_ROW_TILE // SUBLANES) - 1, 0), 0))
    first_spec = pl.BlockSpec((RWKV_ROW_TILE, D_MODEL), lambda i: (jnp.maximum(i - prompt_tiles + 1, 0), 0))
    pair = pl.BlockSpec((RWKV_PAIRS, RWKV_ROW_TILE, LANES), lambda i: (0, i, 0))
    full = lambda a: pl.BlockSpec(a.shape, lambda i: (0,) * a.ndim)
    consts = [p["g_mix"], p["mu"], p["wr"], p["wk"], p["wv"], p["w0"], p["w1"], p["w2"], p["a0"], p["a1"],
              p["a2"], p["g1"], p["g2"], p["k_k"], p["k_a"], p["bd"]]
    pair_shape = lambda dt: jax.ShapeDtypeStruct((RWKV_PAIRS, NT, LANES), dt)
    return pl.pallas_call(
        _rwkv_proj_body,
        grid=(NT // RWKV_ROW_TILE,),
        in_specs=[row, prev, first_spec] + [full(c) for c in consts],
        out_specs=[pair] * 6 + [row, pl.BlockSpec((RWKV_ROW_TILE // SUBLANES, D_MODEL), lambda i: (i, 0))],
        out_shape=[pair_shape(BF16), pair_shape(F32)] + [pair_shape(BF16)] * 4
        + [jax.ShapeDtypeStruct((NT, D_MODEL), BF16), jax.ShapeDtypeStruct((NT // SUBLANES, D_MODEL), F32)],
        scratch_shapes=[pltpu.VMEM((RWKV_PAIRS, RWKV_ROW_TILE, LANES), F32)],
        compiler_params=_cparams("parallel"),
        name="rwkv_proj",
    )(x, x, first, *consts)


def _rwkv_chunks(c_len, problems):
    n_grp = RWKV_STACK // c_len
    shift = int(math.log2(c_len))
    assert 2 ** shift == c_len
    row = lax.broadcasted_iota(jnp.int32, (RWKV_STACK, RWKV_STACK), 0)
    col = lax.broadcasted_iota(jnp.int32, (RWKV_STACK, RWKV_STACK), 1)
    same = jnp.right_shift(row, shift) == jnp.right_shift(col, shift)
    strict_t = same & (row < col)
    incl_t = same & (row <= col)
    tri_t = jnp.where(incl_t, 1.0, 0.0).astype(BF16)
    each = lambda f, *cols: [f(*xs) for xs in zip(*cols)]
    r, lw, k, v, kk, b, state, store = zip(*problems)
    lam = each(lambda x: sum(jnp.dot(part, tri_t, preferred_element_type=F32) for part in _split3(x.T)).T, lw)
    lam3 = each(lambda x: x.reshape(n_grp, c_len, LANES), lam)
    lam_end3 = each(lambda x: x[:, c_len - 1:c_len, :], lam3)
    e_end = each(lambda e, x: jnp.exp(e - x).reshape(RWKV_STACK, LANES), lam_end3, lam3)
    e_tot3 = each(jnp.exp, lam_end3)
    e_neg = each(lambda x: jnp.exp(-x), lam)
    at = each(lambda kk_, l, w: -kk_ * jnp.exp(l - w), kk, lam, lw)
    rt = each(lambda r_, l: r_ * jnp.exp(l), r, lam)
    mul = lambda x, y: x * y
    bh, kh, bb, kb = each(mul, b, e_neg), each(mul, k, e_neg), each(mul, b, e_end), each(mul, k, e_end)

    lane = lax.broadcasted_iota(jnp.int32, (RWKV_STACK, LANES), 1)
    halves = (lane < RWKV_HD, lane >= RWKV_HD)
    rhs = each(lambda x, y: jnp.concatenate([x, y], axis=0), bh, kh)
    at_pow, bt_up, rt_up = [], [], []
    for sel in halves:
        lhs = each(lambda x, y: jnp.concatenate([jnp.where(sel, x, 0.0), jnp.where(sel, y, 0.0)], axis=0), at, rt)
        mt = each(_dot_nt, rhs, lhs)
        at_pow.append(each(lambda x: jnp.where(strict_t, x[:RWKV_STACK, :RWKV_STACK], 0.0), mt))
        bt_up.append(each(lambda x: jnp.where(strict_t, x[RWKV_STACK:, :RWKV_STACK], 0.0), mt))
        rt_up.append(each(lambda x: jnp.concatenate([jnp.where(incl_t, x[:RWKV_STACK, RWKV_STACK:], 0.0),
                                                     jnp.where(incl_t, x[RWKV_STACK:, RWKV_STACK:], 0.0)], axis=0), mt))

    rows = lambda t, g: t[g * c_len:(g + 1) * c_len]
    states = each(lambda st: [st(g) for g in range(n_grp)], state)
    p = [each(lambda x, y, st: _dot_nt(jnp.concatenate([rows(x, g), rows(y, g)], axis=0), st[g]), at, rt, states)
         for g in range(n_grp)]
    n_prob = len(problems)
    a_s_t = [jnp.concatenate([p[g][i][:c_len] for g in range(n_grp)], axis=0).T for i in range(n_prob)]
    r_s_t = [jnp.concatenate([p[g][i][c_len:] for g in range(n_grp)], axis=0).T for i in range(n_prob)]
    v_t = each(lambda x: x.T, v)
    half_rows = lambda x, e: x[e * RWKV_HD:(e + 1) * RWKV_HD]
    n_half = len(halves)
    u_t = [each(lambda a_, v_, b_: half_rows(a_, e) + _dot(half_rows(v_, e), b_), a_s_t, v_t, bt_up[e])
           for e in range(n_half)]
    assert shift >= 2
    for it in range(shift - 1):
        u_t = [each(lambda u_, a_: u_ + _dot(u_, a_), u_t[e], at_pow[e]) for e in range(n_half)]
        if it + 2 < shift:
            at_pow = [each(lambda a_: _dot(a_, a_), at_pow[e]) for e in range(n_half)]
    z_t = [each(lambda u_, a_: _dot(u_, a_), u_t[e], at_pow[e]) for e in range(n_half)]
    u_t = [each(lambda u_, z_, a_: u_ + _dot(z_, a_), u_t[e], z_t[e], at_pow[e]) for e in range(n_half)]
    y_t = [each(lambda r_, u_, v_, m_: half_rows(r_, e) + _dot(jnp.concatenate([u_, half_rows(v_, e)], axis=1), m_),
                r_s_t, u_t[e], v_t, rt_up[e]) for e in range(n_half)]
    stack_halves = lambda parts: [jnp.concatenate([parts[e][i] for e in range(n_half)], axis=0).T
                                  for i in range(n_prob)]
    u = stack_halves(u_t)
    y = stack_halves(y_t)

    r128 = lax.broadcasted_iota(jnp.int32, (LANES, LANES), 0)
    c128 = lax.broadcasted_iota(jnp.int32, (LANES, LANES), 1)
    diag = (r128 < RWKV_HD) == (c128 < RWKV_HD)
    for g in range(n_grp):
        upd = each(lambda u_, v_, bb_, kb_: _dot_tn(jnp.concatenate([rows(u_, g), rows(v_, g)], axis=0),
                                                    jnp.concatenate([rows(bb_, g), rows(kb_, g)], axis=0)),
                   u, v, bb, kb)
        for i in range(n_prob):
            store[i](g, states[i][g] * e_tot3[i][g] + jnp.where(diag, upd[i], 0.0))
    return y


RWKV_PROMPT_SEQS = 4
RWKV_STEP_TOKENS = 128


def _rwkv_prompt_body(*refs):
    ins, (y_ref, s_ref, s_scr) = refs[:6 * RWKV_PROMPT_SEQS], refs[6 * RWKV_PROMPT_SEQS:]
    c = pl.program_id(1)

    @pl.when(c == 0)
    def _():
        s_scr[...] = jnp.zeros_like(s_scr)

    def one_chunk(ci, carry):
        rows = pl.ds(pl.multiple_of(ci * RWKV_CHUNK, RWKV_CHUNK), RWKV_CHUNK)
        def problem(seq):
            def store(g, s):
                s_scr[seq, g] = s

            args = [ins[a * RWKV_PROMPT_SEQS + seq][:, rows, :].reshape(RWKV_STACK, LANES).astype(F32)
                    for a in range(6)]
            return (*args, lambda g: s_scr[seq, g], store)

        ys = _rwkv_chunks(RWKV_CHUNK, [problem(seq) for seq in range(RWKV_PROMPT_SEQS)])
        for seq, y in enumerate(ys):
            y_ref[:, seq, rows, :] = y.reshape(RWKV_PAIRS, RWKV_CHUNK, LANES)
        return carry

    lax.fori_loop(0, RWKV_STEP_TOKENS // RWKV_CHUNK, one_chunk, 0)

    @pl.when(c == pl.num_programs(1) - 1)
    def _():
        for seq in range(RWKV_PROMPT_SEQS):
            for p in range(RWKV_PAIRS):
                s_ref[seq, 2 * p] = s_scr[seq, p, :RWKV_HD, :RWKV_HD]
                s_ref[seq, 2 * p + 1] = s_scr[seq, p, RWKV_HD:, RWKV_HD:]


RWKV_SAMPLE_PROBLEMS = 2


def _rwkv_sample_body(r_ref, lw_ref, k_ref, v_ref, kk_ref, b_ref, s0_ref, y_ref, s_ref):
    zero = jnp.zeros((RWKV_HD, RWKV_HD), F32)
    rows_per = RWKV_SAMPLE_SEQS * DEC_SEQ

    def problem(q):
        def state(g):
            p, seq = divmod(g, RWKV_SAMPLE_SEQS)
            seq += q * RWKV_SAMPLE_SEQS
            return jnp.concatenate([jnp.concatenate([s0_ref[seq, 2 * p], zero], axis=1),
                                    jnp.concatenate([zero, s0_ref[seq, 2 * p + 1]], axis=1)], axis=0)

        def store(g, s):
            p, seq = divmod(g, RWKV_SAMPLE_SEQS)
            seq += q * RWKV_SAMPLE_SEQS
            s_ref[seq, 2 * p] = s[:RWKV_HD, :RWKV_HD]
            s_ref[seq, 2 * p + 1] = s[RWKV_HD:, RWKV_HD:]

        args = [ref[:, q * rows_per:(q + 1) * rows_per, :].reshape(RWKV_STACK, LANES).astype(F32)
                for ref in (r_ref, lw_ref, k_ref, v_ref, kk_ref, b_ref)]
        return (*args, state, store)

    ys = _rwkv_chunks(DEC_SEQ, [problem(q) for q in range(RWKV_SAMPLE_PROBLEMS)])
    for q, y in enumerate(ys):
        y_ref[:, q * rows_per:(q + 1) * rows_per, :] = y.reshape(RWKV_PAIRS, rows_per, LANES)


def rwkv_core(r, lw, k, v, kk, b, state_s, j):
    assert RWKV_PAIRS * RWKV_CHUNK == RWKV_STACK
    nc = SEQ // RWKV_STEP_TOKENS
    st_shape = (RWKV_HEADS, RWKV_HD, RWKV_HD)
    seq_blk = lambda seq: pl.BlockSpec((RWKV_PAIRS, RWKV_STEP_TOKENS, LANES),
                                       lambda bi, c: (0, (bi * RWKV_PROMPT_SEQS + seq) * nc + c, 0))
    inputs = (r, lw, k, v, kk, b)
    y_p, s_p = pl.pallas_call(
        _rwkv_prompt_body,
        grid=(BATCH // RWKV_PROMPT_SEQS, nc),
        in_specs=[seq_blk(seq) for _ in inputs for seq in range(RWKV_PROMPT_SEQS)],
        out_specs=[pl.BlockSpec((RWKV_PAIRS, RWKV_PROMPT_SEQS, RWKV_STEP_TOKENS, LANES), lambda bi, c: (0, bi, c, 0)),
                   pl.BlockSpec((RWKV_PROMPT_SEQS,) + st_shape, lambda bi, c: (bi, 0, 0, 0))],
        out_shape=[jax.ShapeDtypeStruct((RWKV_PAIRS, BATCH, SEQ, LANES), F32),
                   jax.ShapeDtypeStruct((BATCH,) + st_shape, F32)],
        scratch_shapes=[pltpu.VMEM((RWKV_PROMPT_SEQS, RWKV_PAIRS, LANES, LANES), F32)],
        compiler_params=_cparams("parallel", "arbitrary"),
        name="rwkv_prompt",
    )(*[a for a in inputs for _ in range(RWKV_PROMPT_SEQS)])

    step_seqs = RWKV_SAMPLE_PROBLEMS * RWKV_SAMPLE_SEQS
    rows = step_seqs * DEC_SEQ
    row0 = NP // rows
    blk = pl.BlockSpec((RWKV_PAIRS, rows, LANES), lambda i: (0, row0 + i, 0))
    st = pl.BlockSpec((step_seqs,) + st_shape, lambda i: (i, 0, 0, 0))
    st_in = pl.BlockSpec((None, step_seqs) + st_shape, lambda i: (j, i, 0, 0, 0))
    y_s, s_s = pl.pallas_call(
        _rwkv_sample_body,
        grid=(DEC_BATCH // step_seqs,),
        in_specs=[blk] * 6 + [st_in],
        out_specs=[pl.BlockSpec((RWKV_PAIRS, rows, LANES), lambda i: (0, i, 0)), st],
        out_shape=[jax.ShapeDtypeStruct((RWKV_PAIRS, NS, LANES), F32),
                   jax.ShapeDtypeStruct((DEC_BATCH,) + st_shape, F32)],
        compiler_params=_cparams("parallel"),
        name="rwkv_sample",
    )(*inputs, state_s)
    return y_p.reshape(RWKV_PAIRS, NP, LANES), y_s, s_p, s_s


def _rwkv_out_body(x_ref, yp_ref, ys_ref, r_ref, k_ref, v_ref, g_ref, rk_ref, lnw_ref, lnb_ref, bd_ref, wo_ref,
                   o_ref):
    bd = bd_ref[...]
    y = _pick_group(RWKV_ROW_TILE, _load_pairs(yp_ref), _load_pairs(ys_ref))
    yc = y - _seg_sum(y, bd) * (1.0 / RWKV_HD)
    var = _seg_sum(yc * yc, bd) * (1.0 / RWKV_HD)
    yn = yc * lax.rsqrt(var + RWKV_GN_EPS) * lnw_ref[...] + lnb_ref[...]
    rk = _load_pairs(r_ref).astype(F32) * _load_pairs(k_ref).astype(F32)
    bonus = _seg_sum(rk * rk_ref[...], bd) * _load_pairs(v_ref).astype(F32)
    z = (yn + bonus) * g_ref[...].astype(F32)
    o_ref[...] = x_ref[...] + _dot(z, wo_ref[...])


def rwkv_out(x, y_p, y_s, r, k, v, g, p):
    row = pl.BlockSpec((RWKV_ROW_TILE, D_MODEL), lambda i: (i, 0))
    pair_of = lambda f: pl.BlockSpec((RWKV_PAIRS, RWKV_ROW_TILE, LANES), lambda i: (0, f(i), 0))
    full = lambda a: pl.BlockSpec(a.shape, lambda i: (0,) * a.ndim)
    consts = [p["r_k"], p["ln_w"], p["ln_b"], p["bd"], p["w_out"]]
    return pl.pallas_call(
        _rwkv_out_body,
        grid=(NT // RWKV_ROW_TILE,),
        in_specs=[row, *_group_specs(RWKV_ROW_TILE, pair_of)] + [pair_of(lambda i: i)] * 3 + [row]
        + [full(c) for c in consts],
        out_specs=row,
        out_shape=jax.ShapeDtypeStruct((NT, D_MODEL), F32),
        compiler_params=_cparams("parallel"),
        name="rwkv_out",
    )(x, y_p, y_s, r, k, v, g, *consts)


def _pad_cols(w, n):
    return jnp.pad(w, ((0, 0), (0, n - w.shape[1])))


def _pad_rows(w, n):
    return jnp.pad(w, ((0, n - w.shape[0]), (0, 0)))


def kernel(x_prompt, x_sample, state_ret, cache_swa_k, cache_swa_v, state_rwkv_wkv, state_rwkv_shift,
           norm_mix, norm_mlp, norm_final, ret_w_in, ret_w_out,
           swa_w_in, swa_b_in, swa_sinks, swa_w_out,
           rwkv_mu, rwkv_w_rkv, rwkv_w0, rwkv_w1, rwkv_w2, rwkv_a0, rwkv_a1, rwkv_a2,
           rwkv_g1, rwkv_g2, rwkv_k_k, rwkv_k_a, rwkv_r_k, rwkv_ln_w, rwkv_ln_b, rwkv_w_out,
           mlp_w_up, mlp_w_down):
    x = (x_prompt, x_sample.reshape(NS, D_MODEL))
    if LAYER_KINDS[0] != 0:
        x = jnp.concatenate([x_prompt.reshape(NP, D_MODEL), x[1]], axis=0)
    row = lambda t: t.reshape(1, -1)
    k_p, k_s, v_p, v_s, wkv_p, wkv_s, sh_p, sh_s = ([] for _ in range(8))
    ret_states = None
    for layer in range(DEPTH):
        kind = LAYER_KINDS[layer]
        j = LAYER_KINDS[:layer].count(kind)
        g_mix = row(norm_mix[layer])
        if kind == 0:
            proj = ret_proj(x, g_mix, ret_w_in[j].astype(BF16))
            o_p, o_s, *ret_states = retention_core(proj, state_ret, j, ret_states)
            x = matmul_residual(x, o_p, o_s, ret_w_out[j].astype(BF16))
        elif kind == 1:
            proj, k_rot = swa_proj(x, g_mix, swa_w_in[j].astype(BF16), row(swa_b_in[j]))
            o_p, o_s, kc, vc = swa_core(proj, k_rot, swa_sinks[j],
                                        cache_swa_k[j].reshape(DEC_BATCH, WINDOW, SWA_NK),
                                        cache_swa_v[j].reshape(DEC_BATCH, WINDOW, SWA_NK))
            x = matmul_residual(x, o_p, o_s, swa_w_out[j].astype(BF16))
            kv_shape = (-1, WINDOW, SWA_KV_HEADS, SWA_HD)
            last = slice(SEQ // WINDOW - 1, NP // WINDOW, SEQ // WINDOW)
            k_p.append(k_rot.reshape(NT // WINDOW, WINDOW, SWA_NK)[last].reshape(kv_shape))
            v_p.append(proj.reshape(NT // WINDOW, WINDOW, -1)[last, :, SWA_NQ + SWA_NK:]
                       .astype(F32).reshape(kv_shape))
            k_s.append(kc.reshape(kv_shape))
            v_s.append(vc.reshape(kv_shape))
        else:
            lora_w = LANES
            lora_g = 2 * LANES
            p = dict(
                mu=rwkv_mu[j], wr=rwkv_w_rkv[j, 0].astype(BF16), wk=rwkv_w_rkv[j, 1].astype(BF16),
                wv=rwkv_w_rkv[j, 2].astype(BF16), w0=row(rwkv_w0[j]),
                w1=_pad_cols(rwkv_w1[j], lora_w).astype(BF16), w2=_pad_rows(rwkv_w2[j], lora_w).astype(BF16),
                a0=row(rwkv_a0[j]),
                a1=_pad_cols(rwkv_a1[j], lora_w).astype(BF16), a2=_pad_rows(rwkv_a2[j], lora_w).astype(BF16),
                g1=_pad_cols(rwkv_g1[j], lora_g).astype(BF16), g2=_pad_rows(rwkv_g2[j], lora_g).astype(BF16),
                k_k=row(rwkv_k_k[j]), k_a=row(rwkv_k_a[j]), r_k=row(rwkv_r_k[j]),
                ln_w=row(rwkv_ln_w[j]), ln_b=row(rwkv_ln_b[j]), bd=_block_diag_ones(),
                w_out=rwkv_w_out[j].astype(BF16), g_mix=g_mix)
            r, lw, k, v, kk, b, g, h_last = rwkv_proj(x, state_rwkv_shift[j], p)
            y_p, y_s, s_p, s_s = rwkv_core(r, lw, k, v, kk, b, state_rwkv_wkv, j)
            x = rwkv_out(x, y_p, y_s, r, k, v, g, p)
            wkv_p.append(s_p)
            wkv_s.append(s_s)
            sh_p.append(h_last[SEQ // SUBLANES - 1:NP // SUBLANES:SEQ // SUBLANES])
            sh_s.append(h_last[NP // SUBLANES:])
        x = mlp(x, row(norm_mlp[layer]), mlp_w_up, mlp_w_down, layer,
                final_g=row(norm_final) if layer == DEPTH - 1 else None)
    y_p, y_s = x
    return (y_p.reshape(BATCH, SEQ, D_MODEL), y_s.reshape(DEC_BATCH, DEC_SEQ, D_MODEL),
            ret_states[0], ret_states[1], jnp.stack(k_p), jnp.stack(k_s), jnp.stack(v_p), jnp.stack(v_s),
            jnp.stack(wkv_p), jnp.stack(wkv_s), jnp.stack(sh_p), jnp.stack(sh_s))
```
